```python
import math
import jax, jax.numpy as jnp
from jax import lax
import numpy as np

D_MODEL = 1024
BATCH = 4
SEQ = 4096
DEPTH = 1

SB_HEADS = 8
SB_HEAD_DIM = 64
SB_BLOCK = 128
GLA_HEADS = 4
GLA_KEY_DIM = 64
GLA_VAL_DIM = 128
GLA_GATE_RANK = 16
GLA_GATE_TEMP = 16.0
GLA_CHUNK = 64
D_FF = 2816
CONV_WIDTH = 3
N_MOD = 6
DEEPNORM_ALPHA = (2 * DEPTH) ** 0.25
DEEPNORM_BETA = (8 * DEPTH) ** -0.25
LN_EPS = 1e-5
RMS_EPS = 1e-6

SB_WIDTH = SB_HEADS * SB_HEAD_DIM
GLA_QK_WIDTH = GLA_HEADS * GLA_KEY_DIM
GLA_V_WIDTH = GLA_HEADS * GLA_VAL_DIM
MIX_WIDTH = SB_WIDTH + GLA_V_WIDTH
IN_SPLITS = (SB_WIDTH, SB_WIDTH, SB_WIDTH, GLA_QK_WIDTH, GLA_QK_WIDTH, GLA_V_WIDTH, GLA_GATE_RANK, GLA_V_WIDTH)
IN_WIDTH = 3 * SB_WIDTH + 2 * GLA_QK_WIDTH + 2 * GLA_V_WIDTH + GLA_GATE_RANK

kernel_name = "hymba_sb_gla_deepnorm_adaln"


def layer_norm(x, g, b):
    xf = x.astype(jnp.float32)
    mu = jnp.mean(xf, axis=-1, keepdims=True)
    xc = xf - mu
    var = jnp.mean(xc * xc, axis=-1, keepdims=True)
    return (xc * lax.rsqrt(var + LN_EPS) * g.astype(jnp.float32) + b.astype(jnp.float32)).astype(x.dtype)


def head_rms_norm(o):
    of = o.astype(jnp.float32)
    return of * lax.rsqrt(jnp.mean(of * of, axis=-1, keepdims=True) + RMS_EPS)


def stick_breaking_attention(q, k, v):
    B, H, S, d = q.shape
    nb = S // SB_BLOCK
    scale = d ** -0.5
    kf = k.astype(jnp.float32)
    vf = v.astype(jnp.float32)
    qb = q.astype(jnp.float32).reshape(B, H, nb, SB_BLOCK, d).transpose(2, 0, 1, 3, 4)
    key_pos = jnp.arange(S)

    def one_block(args):
        qi, i = args
        z = jnp.einsum('bhqd,bhkd->bhqk', qi, kf) * scale
        q_pos = i * SB_BLOCK + jnp.arange(SB_BLOCK)
        causal = key_pos[None, :] < q_pos[:, None]
        log_beta = jax.nn.log_sigmoid(z)
        log_one_minus = jnp.where(causal, jax.nn.log_sigmoid(-z), 0.0)
        tail = lax.cumsum(log_one_minus, axis=3, reverse=True) - log_one_minus
        w = jnp.where(causal, jnp.exp(log_beta + tail), 0.0)
        return jnp.einsum('bhqk,bhkd->bhqd', w, vf)

    out = lax.map(one_block, (qb, jnp.arange(nb)))
    return out.transpose(1, 2, 0, 3, 4).reshape(B, H, S, d)


def gla_chunked(q, k, v, log_a):
    B, H, S, dk = q.shape
    dv = v.shape[-1]
    C = GLA_CHUNK
    N = S // C
    qf = q.astype(jnp.float32).reshape(B, H, N, C, dk) * dk ** -0.5
    kf = k.astype(jnp.float32).reshape(B, H, N, C, dk)
    vf = v.astype(jnp.float32).reshape(B, H, N, C, dv)
    b = jnp.cumsum(log_a.astype(jnp.float32).reshape(B, H, N, C, dk), axis=3)
    b_last = b[:, :, :, -1:, :]
    q_dec = qf * jnp.exp(b)
    k_inv = kf * jnp.exp(-b)
    k_end = kf * jnp.exp(b_last - b)
    tril = jnp.tril(jnp.ones((C, C), dtype=bool))
    a_intra = jnp.where(tril, jnp.einsum('bhnqd,bhnkd->bhnqk', q_dec, k_inv), 0.0)
    o_intra = jnp.einsum('bhnqk,bhnkd->bhnqd', a_intra, vf)
    kv_chunk = jnp.einsum('bhnkd,bhnke->bhnde', k_end, vf)
    decay = jnp.exp(b_last[:, :, :, 0, :])

    def step(state, inp):
        dec, kv = inp
        return dec[..., None] * state + kv, state

    init = jnp.zeros((B, H, dk, dv), jnp.float32)
    _, states = lax.scan(step, init, (decay.transpose(2, 0, 1, 3), kv_chunk.transpose(2, 0, 1, 3, 4)))
    states = states.transpose(1, 2, 0, 3, 4)
    o_inter = jnp.einsum('bhnqd,bhnde->bhnqe', q_dec, states)
    return (o_intra + o_inter).reshape(B, H, S, dv)


def hybrid_mixer(h, w_in, gla_w_gate, gla_b_gate, sb_norm_g, gla_norm_g, w_out):
    B, S, _ = h.shape
    proj = h @ w_in
    split_at = [int(i) for i in np.cumsum(IN_SPLITS)[:-1]]
    sb_q, sb_k, sb_v, g_q, g_k, g_v, g_lr, g_r = jnp.split(proj, split_at, axis=-1)

    def heads(t, n):
        return t.reshape(B, S, n, -1).transpose(0, 2, 1, 3)

    o_sb = stick_breaking_attention(heads(sb_q, SB_HEADS), heads(sb_k, SB_HEADS), heads(sb_v, SB_HEADS))
    o_sb = head_rms_norm(o_sb.transpose(0, 2, 1, 3)).reshape(B, S, SB_WIDTH) * sb_norm_g.astype(jnp.float32)

    log_a = jax.nn.log_sigmoid((g_lr @ gla_w_gate + gla_b_gate).astype(jnp.float32)) / GLA_GATE_TEMP
    o_gla = gla_chunked(heads(g_q, GLA_HEADS), heads(g_k, GLA_HEADS), heads(g_v, GLA_HEADS), heads(log_a, GLA_HEADS))
    o_gla = head_rms_norm(o_gla.transpose(0, 2, 1, 3)).reshape(B, S, GLA_V_WIDTH) * gla_norm_g.astype(jnp.float32)
    o_gla = o_gla * jax.nn.silu(g_r.astype(jnp.float32))

    o = jnp.concatenate([o_sb, o_gla], axis=-1).astype(h.dtype)
    return o @ w_out


def conv_glu_ffn(h, w_ff_gate, w_ff_up, conv_w, conv_b, w_down):
    S = h.shape[1]
    gate = h @ w_ff_gate
    up = h @ w_ff_up
    gp = jnp.pad(gate, ((0, 0), (CONV_WIDTH - 1, 0), (0, 0)))
    conv = sum(gp[:, i:i + S, :] * conv_w[i] for i in range(CONV_WIDTH)) + conv_b
    return (jax.nn.silu(conv) * up) @ w_down


def setup_inputs(seed: int = 0) -> dict:
    key = jax.random.key(seed)
    ks = jax.random.split(key, 20)
    n = jax.random.normal
    f32 = jnp.float32
    return {
        "x": n(ks[0], (BATCH, SEQ, D_MODEL), f32),
        "c": n(ks[1], (BATCH, D_MODEL), f32),
        "w_ada": n(ks[2], (DEPTH, D_MODEL, N_MOD * D_MODEL), f32) * (0.5 * D_MODEL ** -0.5),
        "b_ada": 0.01 * n(ks[3], (DEPTH, N_MOD * D_MODEL), f32),
        "w_in": n(ks[4], (DEPTH, D_MODEL, IN_WIDTH), f32) * D_MODEL ** -0.5,
        "gla_w_gate": n(ks[5], (DEPTH, GLA_GATE_RANK, GLA_QK_WIDTH), f32) * GLA_GATE_RANK ** -0.5,
        "gla_b_gate": 0.1 * n(ks[6], (DEPTH, GLA_QK_WIDTH), f32),
        "sb_norm_g": 1.0 + 0.02 * n(ks[7], (DEPTH, SB_WIDTH), f32),
        "gla_norm_g": 1.0 + 0.02 * n(ks[8], (DEPTH, GLA_V_WIDTH), f32),
        "w_out": n(ks[9], (DEPTH, MIX_WIDTH, D_MODEL), f32) * (MIX_WIDTH ** -0.5 * DEEPNORM_BETA),
        "ln1_g": 1.0 + 0.02 * n(ks[10], (DEPTH, D_MODEL), f32),
        "ln1_b": 0.01 * n(ks[11], (DEPTH, D_MODEL), f32),
        "w_ff_gate": n(ks[12], (DEPTH, D_MODEL, D_FF), f32) * D_MODEL ** -0.5,
        "w_ff_up": n(ks[13], (DEPTH, D_MODEL, D_FF), f32) * D_MODEL ** -0.5,
        "conv_w": n(ks[14], (DEPTH, CONV_WIDTH, D_FF), f32) * CONV_WIDTH ** -0.5,
        "conv_b": 0.01 * n(ks[15], (DEPTH, D_FF), f32),
        "w_down": n(ks[16], (DEPTH, D_FF, D_MODEL), f32) * (D_FF ** -0.5 * DEEPNORM_BETA),
        "ln2_g": 1.0 + 0.02 * n(ks[17], (DEPTH, D_MODEL), f32),
        "ln2_b": 0.01 * n(ks[18], (DEPTH, D_MODEL), f32),
    }


def reference(x, c, w_ada, b_ada, w_in, gla_w_gate, gla_b_gate, sb_norm_g, gla_norm_g, w_out,
              ln1_g, ln1_b, w_ff_gate, w_ff_up, conv_w, conv_b, w_down, ln2_g, ln2_b):
    c_act = jax.nn.silu(c)
    for l in range(DEPTH):
        mod = c_act @ w_ada[l] + b_ada[l]
        shift_a, scale_a, gate_a, shift_f, scale_f, gate_f = jnp.split(mod[:, None, :], N_MOD, axis=-1)
        h = x * (1.0 + scale_a) + shift_a
        y = hybrid_mixer(h, w_in[l], gla_w_gate[l], gla_b_gate[l], sb_norm_g[l], gla_norm_g[l], w_out[l])
        x = layer_norm(DEEPNORM_ALPHA * x + (1.0 + gate_a) * y, ln1_g[l], ln1_b[l])
        h = x * (1.0 + scale_f) + shift_f
        y = conv_glu_ffn(h, w_ff_gate[l], w_ff_up[l], conv_w[l], conv_b[l], w_down[l])
        x = layer_norm(DEEPNORM_ALPHA * x + (1.0 + gate_f) * y, ln2_g[l], ln2_b[l])
    return x
```

```python
import functools

import jax
import jax.numpy as jnp
from jax import lax
from jax.experimental import pallas as pl
from jax.experimental.pallas import tpu as pltpu

F32 = jnp.float32
BF16 = jnp.bfloat16

LANES = 128
SUBLANES = 8
VMEM_LIMIT_BYTES = 56 * 1024 * 1024

SB_HEADS = 8
SB_HEAD_DIM = 64
GLA_HEADS = 4
GLA_KEY_DIM = 64
GLA_VAL_DIM = 128
GLA_GATE_RANK = 16
GLA_GATE_TEMP = 16.0
GLA_CHUNK = 64
CONV_WIDTH = 3
N_MOD = 6
LN_EPS = 1e-5
RMS_EPS = 1e-6

SB_WIDTH = SB_HEADS * SB_HEAD_DIM
GLA_QK_WIDTH = GLA_HEADS * GLA_KEY_DIM
GLA_V_WIDTH = GLA_HEADS * GLA_VAL_DIM

SB_EXIT = 104.0
SB_TQ = 128
SB_TK = 128

GLA_STEP = 512
ROW_TILE = 512
FF_TILE = 256


def _softplus(z):
    return jnp.maximum(z, 0.0) + jnp.log(1.0 + jnp.exp(-jnp.abs(z)))


def _split_bf16(a):
    hi = a.astype(BF16)
    lo = (a - hi.astype(F32)).astype(BF16)
    return hi, lo


def _dot(a, b):
    return jnp.dot(a, b, preferred_element_type=F32)


def _dot_nt(a, b):
    return lax.dot_general(a, b, (((1,), (1,)), ((), ())), preferred_element_type=F32)


def _dot_tn(a, b):
    return lax.dot_general(a, b, (((0,), (0,)), ((), ())), preferred_element_type=F32)


def _layer_norm(v, g, b):
    mu = jnp.mean(v, axis=-1, keepdims=True)
    vc = v - mu
    var = jnp.mean(vc * vc, axis=-1, keepdims=True)
    return vc * lax.rsqrt(var + LN_EPS) * g + b


def _ada_kernel(c_ref, w_ref, b_ref, o_ref):
    c = c_ref[...]
    ca = c / (1.0 + jnp.exp(-c))
    ch, cl = _split_bf16(ca)
    wh, wl = _split_bf16(w_ref[...])
    o_ref[...] = _dot(ch, wh) + _dot(cl, wh) + _dot(ch, wl) + b_ref[...]


def _ada(c_pad, w, b):
    d, n = w.shape
    tn = 1024
    return pl.pallas_call(
        _ada_kernel,
        out_shape=jax.ShapeDtypeStruct((c_pad.shape[0], n), F32),
        grid=(n // tn,),
        in_specs=[
            pl.BlockSpec((c_pad.shape[0], d), lambda j: (0, 0)),
            pl.BlockSpec((d, tn), lambda j: (0, j)),
            pl.BlockSpec((1, tn), lambda j: (0, j)),
        ],
        out_specs=pl.BlockSpec((c_pad.shape[0], tn), lambda j: (0, j)),
        compiler_params=pltpu.CompilerParams(
            dimension_semantics=("arbitrary",), vmem_limit_bytes=VMEM_LIMIT_BYTES),
        name="ada",
    )(c_pad, w, b)


_PROJ_GROUPS = (
    (SB_WIDTH, SB_HEAD_DIM ** -0.5),
    (SB_WIDTH, 1.0),
    (SB_WIDTH, 1.0),
    (GLA_QK_WIDTH, 1.0),
    (GLA_QK_WIDTH, 1.0),
    (GLA_V_WIDTH, 1.0),
    (GLA_V_WIDTH, 1.0),
    (LANES, 1.0),
)


def _inproj_kernel(x_ref, mod_ref, w_ref, *out_refs):
    shift = mod_ref[0:1, :]
    scale = mod_ref[1:2, :]
    h = (x_ref[...] * (1.0 + scale) + shift).astype(BF16)
    c0 = 0
    for (width, mul), o_ref in zip(_PROJ_GROUPS, out_refs):
        acc = _dot(h, w_ref[:, c0:c0 + width])
        if mul != 1.0:
            acc = acc * mul
        o_ref[...] = acc.astype(o_ref.dtype)
        c0 += width


def _inproj(x2d, mod, w_perm, seq):
    t, d = x2d.shape
    tm = ROW_TILE
    tiles_per_seq = seq // tm
    n = w_perm.shape[1]
    return pl.pallas_call(
        _inproj_kernel,
        out_shape=[jax.ShapeDtypeStruct((t, width), BF16) for width, _ in _PROJ_GROUPS],
        grid=(t // tm,),
        in_specs=[
            pl.BlockSpec((tm, d), lambda i: (i, 0)),
            pl.BlockSpec((None, SUBLANES, d), lambda i: (i // tiles_per_seq, 0, 0)),
            pl.BlockSpec((d, n), lambda i: (0, 0)),
        ],
        out_specs=[pl.BlockSpec((tm, width), lambda i: (i, 0)) for width, _ in _PROJ_GROUPS],
        compiler_params=pltpu.CompilerParams(
            dimension_semantics=("arbitrary",), vmem_limit_bytes=VMEM_LIMIT_BYTES),
        name="inproj",
    )(x2d, mod, w_perm)


def _sb_block(qm, kj, vj, carry, acc, umat, mask):
    z = _dot_nt(qm, kj)
    sp = _softplus(z)
    if mask is not None:
        sp = jnp.where(mask, sp, 0.0)
    sp_hi, sp_lo = _split_bf16(sp)
    ct = _dot(sp_hi, umat) + _dot(sp_lo, umat)
    arg = z - ct[:, :SB_TK] - carry
    if mask is not None:
        arg = jnp.where(mask, arg, -1e30)
    w = jnp.exp(arg).astype(BF16)
    return carry + ct[:, SB_TK:], acc + _dot(w, vj)


def _sb_kernel(q_ref, k_ref, v_ref, g_ref, o_ref):
    i = pl.program_id(2)
    q = q_ref[...]
    lane = lax.broadcasted_iota(jnp.int32, (SB_TQ, LANES), 1)
    head0 = lane < SB_HEAD_DIM
    qms = (jnp.where(head0, q, jnp.zeros_like(q)), jnp.where(head0, jnp.zeros_like(q), q))

    uj = lax.broadcasted_iota(jnp.int32, (SB_TK, 2 * SB_TK), 0)
    us = lax.broadcasted_iota(jnp.int32, (SB_TK, 2 * SB_TK), 1)
    umat = jnp.where((uj >= us) | (us >= SB_TK), 1.0, 0.0).astype(BF16)

    row = lax.broadcasted_iota(jnp.int32, (SB_TQ, SB_TK), 0)
    col = lax.broadcasted_iota(jnp.int32, (SB_TQ, SB_TK), 1)
    causal = col < row

    def load_kv(j):
        start = pl.multiple_of(j * SB_TK, SB_TK)
        return k_ref[pl.ds(start, SB_TK), :], v_ref[pl.ds(start, SB_TK), :]

    zero = jnp.zeros((SB_TQ, LANES), F32)
    kd, vd = load_kv(i)
    c0, a0 = _sb_block(qms[0], kd, vd, zero, zero, umat, causal)
    c1, a1 = _sb_block(qms[1], kd, vd, zero, zero, umat, causal)

    def not_done(c0, c1):
        return (jnp.minimum(jnp.min(c0), jnp.min(c1)) < SB_EXIT).astype(jnp.int32)

    def cond(s):
        return jnp.logical_and(s[0] >= 0, s[1] > 0)

    def body(s):
        j, _, c0, a0, c1, a1 = s
        kj, vj = load_kv(j)
        c0, a0 = _sb_block(qms[0], kj, vj, c0, a0, umat, None)
        c1, a1 = _sb_block(qms[1], kj, vj, c1, a1, umat, None)
        return j - 1, not_done(c0, c1), c0, a0, c1, a1

    _, _, _, a0, _, a1 = lax.while_loop(cond, body, (i - 1, not_done(c0, c1), c0, a0, c1, a1))

    o = jnp.where(head0, a0, a1)
    sq = o * o
    ss0 = jnp.sum(jnp.where(head0, sq, 0.0), axis=-1, keepdims=True)
    ss1 = jnp.sum(jnp.where(head0, 0.0, sq), axis=-1, keepdims=True)
    ms = jnp.where(head0, ss0, ss1) * (1.0 / SB_HEAD_DIM)
    o_ref[...] = (o * lax.rsqrt(ms + RMS_EPS) * g_ref[...]).astype(o_ref.dtype)


def _sb_attention(q, k, v, g):
    b, s, wdt = q.shape
    pairs = wdt // LANES
    return pl.pallas_call(
        _sb_kernel,
        out_shape=jax.ShapeDtypeStruct((b, s, wdt), BF16),
        grid=(b, pairs, s // SB_TQ),
        in_specs=[
            pl.BlockSpec((None, SB_TQ, LANES), lambda bi, p, i: (bi, i, p)),
            pl.BlockSpec((None, s, LANES), lambda bi, p, i: (bi, 0, p)),
            pl.BlockSpec((None, s, LANES), lambda bi, p, i: (bi, 0, p)),
            pl.BlockSpec((1, LANES), lambda bi, p, i: (0, p)),
        ],
        out_specs=pl.BlockSpec((None, SB_TQ, LANES), lambda bi, p, i: (bi, i, p)),
        compiler_params=pltpu.CompilerParams(
            dimension_semantics=("arbitrary", "arbitrary", "arbitrary"),
            vmem_limit_bytes=VMEM_LIMIT_BYTES),
        name="sb_attention",
    )(q, k, v, g)


def _gla_kernel(q_ref, k_ref, v_ref, r_ref, lr_ref, wg_ref, bg_ref, g_ref, o_ref, st_ref):
    c = GLA_CHUNK
    dv2 = 2 * GLA_VAL_DIM

    @pl.when(pl.program_id(2) == 0)
    def _():
        st_ref[...] = jnp.zeros_like(st_ref)

    u = _dot(lr_ref[...], wg_ref[...]) + bg_ref[...]
    log_a = (jnp.minimum(u, 0.0) - jnp.log(1.0 + jnp.exp(-jnp.abs(u)))) * (1.0 / GLA_GATE_TEMP)

    ti = lax.broadcasted_iota(jnp.int32, (c, c), 0)
    tj = lax.broadcasted_iota(jnp.int32, (c, c), 1)
    lower = ti >= tj
    tril = jnp.where(lower, 1.0, 0.0).astype(BF16)
    lane = lax.broadcasted_iota(jnp.int32, (c, LANES), 1)
    head0 = lane < GLA_KEY_DIM
    se = lax.broadcasted_iota(jnp.int32, (dv2, LANES), 0)
    sd = lax.broadcasted_iota(jnp.int32, (dv2, LANES), 1)
    st_mask = (se < GLA_VAL_DIM) == (sd < GLA_KEY_DIM)

    st = st_ref[...]
    for n in range(GLA_STEP // c):
        rows = slice(n * c, (n + 1) * c)
        la_hi, la_lo = _split_bf16(log_a[rows])
        bcum = _dot(tril, la_hi) + _dot(tril, la_lo)
        blast = bcum[c - 1:c, :]
        qf = q_ref[rows, :].astype(F32)
        kf = k_ref[rows, :].astype(F32)
        vv = v_ref[rows, :]
        q_dec = (qf * jnp.exp(bcum) * (GLA_KEY_DIM ** -0.5)).astype(BF16)
        k_inv = (kf * jnp.exp(-bcum)).astype(BF16)
        k_end = (kf * jnp.exp(blast - bcum)).astype(BF16)
        zq = jnp.zeros_like(q_dec)
        a0 = jnp.where(lower, _dot_nt(jnp.where(head0, q_dec, zq), k_inv), 0.0).astype(BF16)
        a1 = jnp.where(lower, _dot_nt(jnp.where(head0, zq, q_dec), k_inv), 0.0).astype(BF16)
        o_intra = jnp.concatenate(
            [_dot(a0, vv[:, :GLA_VAL_DIM]), _dot(a1, vv[:, GLA_VAL_DIM:])], axis=1)
        o_inter = _dot_nt(q_dec, st.astype(BF16))
        st = st * jnp.exp(blast) + jnp.where(st_mask, _dot_tn(vv, k_end), 0.0)
        o = o_intra + o_inter
        o0 = o[:, :GLA_VAL_DIM]
        o1 = o[:, GLA_VAL_DIM:]
        n0 = o0 * lax.rsqrt(jnp.mean(o0 * o0, axis=-1, keepdims=True) + RMS_EPS)
        n1 = o1 * lax.rsqrt(jnp.mean(o1 * o1, axis=-1, keepdims=True) + RMS_EPS)
        rf = r_ref[rows, :].astype(F32)
        gate = rf / (1.0 + jnp.exp(-rf))
        o_ref[rows, :] = (jnp.concatenate([n0, n1], axis=1) * g_ref[...] * gate).astype(o_ref.dtype)
    st_ref[...] = st


def _gla(q, k, v, r, lr, wg, bg, g):
    b, s, _ = q.shape
    pairs = GLA_HEADS // 2
    step = GLA_STEP
    dv2 = 2 * GLA_VAL_DIM
    seq_map = lambda bi, p, t: (bi, t, p)
    return pl.pallas_call(
        _gla_kernel,
        out_shape=jax.ShapeDtypeStruct((b, s, GLA_V_WIDTH), BF16),
        grid=(b, pairs, s // step),
        in_specs=[
            pl.BlockSpec((None, step, LANES), seq_map),
            pl.BlockSpec((None, step, LANES), seq_map),
            pl.BlockSpec((None, step, dv2), seq_map),
            pl.BlockSpec((None, step, dv2), seq_map),
            pl.BlockSpec((None, step, LANES), lambda bi, p, t: (bi, t, 0)),
            pl.BlockSpec((LANES, LANES), lambda bi, p, t: (0, p)),
            pl.BlockSpec((1, LANES), lambda bi, p, t: (0, p)),
            pl.BlockSpec((1, dv2), lambda bi, p, t: (0, p)),
        ],
        out_specs=pl.BlockSpec((None, step, dv2), seq_map),
        scratch_shapes=[pltpu.VMEM((dv2, LANES), F32)],
        compiler_params=pltpu.CompilerParams(
            dimension_semantics=("arbitrary", "arbitrary", "arbitrary"),
            vmem_limit_bytes=VMEM_LIMIT_BYTES),
        name="gla",
    )(q, k, v, r, lr, wg, bg, g)


def _outproj_kernel(alpha, osb_ref, ogla_ref, x_ref, mod_ref, wa_ref, wb_ref, g_ref, b_ref,
                    x1_ref, h2_ref):
    y = _dot(osb_ref[...], wa_ref[...]) + _dot(ogla_ref[...], wb_ref[...])
    gate_a = mod_ref[2:3, :]
    x1 = _layer_norm(alpha * x_ref[...] + (1.0 + gate_a) * y, g_ref[...], b_ref[...])
    x1_ref[...] = x1
    h2_ref[...] = (x1 * (1.0 + mod_ref[4:5, :]) + mod_ref[3:4, :]).astype(h2_ref.dtype)


def _outproj(o_sb, o_gla, x2d, mod, w_a, w_b, ln_g, ln_b, seq, alpha):
    t, d = x2d.shape
    tm = ROW_TILE
    tiles_per_seq = seq // tm
    row_map = lambda i: (i, 0)
    fixed = lambda i: (0, 0)
    return pl.pallas_call(
        functools.partial(_outproj_kernel, alpha),
        out_shape=[jax.ShapeDtypeStruct((t, d), F32), jax.ShapeDtypeStruct((t, d), BF16)],
        grid=(t // tm,),
        in_specs=[
            pl.BlockSpec((tm, o_sb.shape[1]), row_map),
            pl.BlockSpec((tm, o_gla.shape[1]), row_map),
            pl.BlockSpec((tm, d), row_map),
            pl.BlockSpec((None, SUBLANES, d), lambda i: (i // tiles_per_seq, 0, 0)),
            pl.BlockSpec(w_a.shape, fixed),
            pl.BlockSpec(w_b.shape, fixed),
            pl.BlockSpec((1, d), fixed),
            pl.BlockSpec((1, d), fixed),
        ],
        out_specs=[pl.BlockSpec((tm, d), row_map), pl.BlockSpec((tm, d), row_map)],
        compiler_params=pltpu.CompilerParams(
            dimension_semantics=("arbitrary",), vmem_limit_bytes=VMEM_LIMIT_BYTES),
        name="outproj",
    )(o_sb, o_gla, x2d, mod, w_a, w_b, ln_g, ln_b)


def _ffn_kernel(alpha, tiles_per_seq, h_ref, x1_ref, mod_ref, wg_ref, wu_ref, cw_ref, cb_ref,
                wd_ref, g_ref, b_ref, o_ref, halo_ref, acc_ref):
    tm = h_ref.shape[0]
    d_ff = wg_ref.shape[1]
    first = (pl.program_id(0) % tiles_per_seq) == 0
    h = h_ref[...]
    top = lax.broadcasted_iota(jnp.int32, (SUBLANES, FF_TILE), 0)

    @pl.when(first)
    def _():
        halo_ref[...] = jnp.zeros_like(halo_ref)

    for f in range(d_ff // FF_TILE):
        cols = slice(f * FF_TILE, (f + 1) * FF_TILE)
        gt = _dot(h, wg_ref[:, cols])
        up = _dot(h, wu_ref[:, cols])
        prev = halo_ref[:, cols]
        halo_ref[:, cols] = gt[tm - SUBLANES:, :]
        r1 = pltpu.roll(gt, 1, 0)
        r2 = pltpu.roll(gt, 2, 0)
        t1 = jnp.where(top < 1, pltpu.roll(prev, 1, 0), r1[:SUBLANES])
        t2 = jnp.where(top < 2, pltpu.roll(prev, 2, 0), r2[:SUBLANES])
        g1 = jnp.concatenate([t1, r1[SUBLANES:]], axis=0)
        g2 = jnp.concatenate([t2, r2[SUBLANES:]], axis=0)
        conv = g2 * cw_ref[0:1, cols] + g1 * cw_ref[1:2, cols] + gt * cw_ref[2:3, cols] + cb_ref[:, cols]
        act = (conv / (1.0 + jnp.exp(-conv)) * up).astype(BF16)
        part = _dot(act, wd_ref[cols, :])
        if f == 0:
            acc_ref[...] = part
        else:
            acc_ref[...] += part
    gate_f = mod_ref[5:6, :]
    o_ref[...] = _layer_norm(alpha * x1_ref[...] + (1.0 + gate_f) * acc_ref[...], g_ref[...], b_ref[...])


def _ffn(h2, x1, mod, wg, wu, cw, cb, wd, ln_g, ln_b, seq, alpha):
    t, d = x1.shape
    d_ff = wg.shape[1]
    tm = ROW_TILE
    tiles_per_seq = seq // tm
    row_map = lambda i: (i, 0)
    fixed = lambda i: (0, 0)
    return pl.pallas_call(
        functools.partial(_ffn_kernel, alpha, tiles_per_seq),
        out_shape=jax.ShapeDtypeStruct((t, d), F32),
        grid=(t // tm,),
        in_specs=[
            pl.BlockSpec((tm, d), row_map),
            pl.BlockSpec((tm, d), row_map),
            pl.BlockSpec((None, SUBLANES, d), lambda i: (i // tiles_per_seq, 0, 0)),
            pl.BlockSpec(wg.shape, fixed),
            pl.BlockSpec(wu.shape, fixed),
            pl.BlockSpec(cw.shape, fixed),
            pl.BlockSpec(cb.shape, fixed),
            pl.BlockSpec(wd.shape, fixed),
            pl.BlockSpec((1, d), fixed),
            pl.BlockSpec((1, d), fixed),
        ],
        out_specs=pl.BlockSpec((tm, d), row_map),
        scratch_shapes=[pltpu.VMEM((SUBLANES, d_ff), F32), pltpu.VMEM((tm, d), F32)],
        compiler_params=pltpu.CompilerParams(
            dimension_semantics=("arbitrary",), vmem_limit_bytes=VMEM_LIMIT_BYTES),
        name="ffn",
    )(h2, x1, mod, wg, wu, cw, cb, wd, ln_g, ln_b)


def _permute_w_in(w_in):
    d = w_in.shape[0]
    splits = (SB_WIDTH, SB_WIDTH, SB_WIDTH, GLA_QK_WIDTH, GLA_QK_WIDTH, GLA_V_WIDTH,
              GLA_GATE_RANK, GLA_V_WIDTH)
    parts = []
    c0 = 0
    for wdt in splits:
        parts.append(w_in[:, c0:c0 + wdt])
        c0 += wdt
    sb_q, sb_k, sb_v, g_q, g_k, g_v, g_lr, g_r = parts
    g_lr = jnp.concatenate([g_lr, jnp.zeros((d, LANES - GLA_GATE_RANK), w_in.dtype)], axis=1)
    return jnp.concatenate([sb_q, sb_k, sb_v, g_q, g_k, g_v, g_r, g_lr], axis=1).astype(BF16)


def kernel(x, c, w_ada, b_ada, w_in, gla_w_gate, gla_b_gate, sb_norm_g, gla_norm_g, w_out,
           ln1_g, ln1_b, w_ff_gate, w_ff_up, conv_w, conv_b, w_down, ln2_g, ln2_b):
    bsz, seq, d = x.shape
    depth = w_ada.shape[0]
    alpha = float((2 * depth) ** 0.25)
    t = bsz * seq

    c_pad = jnp.concatenate([c, jnp.zeros((SUBLANES - bsz, d), c.dtype)], axis=0)
    xcur = x.reshape(t, d)
    for l in range(depth):
        mod = _ada(c_pad, w_ada[l], b_ada[l][None, :])
        mod = mod[:bsz].reshape(bsz, N_MOD, d)
        mod = jnp.concatenate([mod, jnp.zeros((bsz, SUBLANES - N_MOD, d), F32)], axis=1)

        sb_q, sb_k, sb_v, g_q, g_k, g_v, g_r, g_lr = _inproj(xcur, mod, _permute_w_in(w_in[l]), seq)
        as_seq = lambda a: a.reshape(bsz, seq, a.shape[-1])

        o_sb = _sb_attention(as_seq(sb_q), as_seq(sb_k), as_seq(sb_v), sb_norm_g[l][None, :])
        wg_pad = jnp.concatenate(
            [gla_w_gate[l], jnp.zeros((LANES - GLA_GATE_RANK, GLA_QK_WIDTH), F32)], axis=0).astype(BF16)
        o_gla = _gla(as_seq(g_q), as_seq(g_k), as_seq(g_v), as_seq(g_r), as_seq(g_lr),
                     wg_pad, gla_b_gate[l][None, :], gla_norm_g[l][None, :])

        w_o = w_out[l].astype(BF16)
        x1, h2 = _outproj(o_sb.reshape(t, SB_WIDTH), o_gla.reshape(t, GLA_V_WIDTH), xcur, mod,
                          w_o[:SB_WIDTH], w_o[SB_WIDTH:], ln1_g[l][None, :], ln1_b[l][None, :],
                          seq, alpha)
        xcur = _ffn(h2, x1, mod, w_ff_gate[l].astype(BF16), w_ff_up[l].astype(BF16),
                    conv_w[l], conv_b[l][None, :], w_down[l].astype(BF16),
                    ln2_g[l][None, :], ln2_b[l][None, :], seq, alpha)
    return xcur.reshape(bsz, seq, d)
```

```python
import functools

import jax
import jax.numpy as jnp
from jax import lax
from jax.experimental import pallas as pl
from jax.experimental.pallas import tpu as pltpu

F32 = jnp.float32
BF16 = jnp.bfloat16

LANES = 128
SUBLANES = 8
VMEM_LIMIT_BYTES = 56 * 1024 * 1024

SB_HEADS = 8
SB_HEAD_DIM = 64
GLA_HEADS = 4
GLA_KEY_DIM = 64
GLA_VAL_DIM = 128
GLA_GATE_RANK = 16
GLA_GATE_TEMP = 16.0
GLA_CHUNK = 64
CONV_WIDTH = 3
N_MOD = 6
LN_EPS = 1e-5
RMS_EPS = 1e-6

SB_WIDTH = SB_HEADS * SB_HEAD_DIM
GLA_QK_WIDTH = GLA_HEADS * GLA_KEY_DIM
GLA_V_WIDTH = GLA_HEADS * GLA_VAL_DIM

SB_EXIT = 104.0
SB_TQ = 128
SB_TK = 128
SB_WIN_BLOCKS = 3
SB_WIN = SB_WIN_BLOCKS * SB_TK

GLA_STEP = 512
ROW_TILE = 512
FF_TILE = 256


def _softplus(z):
    return jnp.maximum(z, 0.0) + jnp.log(1.0 + jnp.exp(-jnp.abs(z)))


def _split_bf16(a):
    hi = a.astype(BF16)
    lo = (a - hi.astype(F32)).astype(BF16)
    return hi, lo


def _dot(a, b):
    return jnp.dot(a, b, preferred_element_type=F32)


def _dot_nt(a, b):
    return lax.dot_general(a, b, (((1,), (1,)), ((), ())), preferred_element_type=F32)


def _dot_tn(a, b):
    return lax.dot_general(a, b, (((0,), (0,)), ((), ())), preferred_element_type=F32)


def _layer_norm(v, g, b):
    mu = jnp.mean(v, axis=-1, keepdims=True)
    vc = v - mu
    var = jnp.mean(vc * vc, axis=-1, keepdims=True)
    return vc * lax.rsqrt(var + LN_EPS) * g + b


def _ada_kernel(c_ref, w_ref, b_ref, o_ref):
    c = c_ref[...]
    ca = c / (1.0 + jnp.exp(-c))
    ch, cl = _split_bf16(ca)
    wh, wl = _split_bf16(w_ref[...])
    o_ref[...] = _dot(ch, wh) + _dot(cl, wh) + _dot(ch, wl) + b_ref[...]


def _ada(c_pad, w, b):
    d, n = w.shape
    tn = 1024
    return pl.pallas_call(
        _ada_kernel,
        out_shape=jax.ShapeDtypeStruct((c_pad.shape[0], n), F32),
        grid=(n // tn,),
        in_specs=[
            pl.BlockSpec((c_pad.shape[0], d), lambda j: (0, 0)),
            pl.BlockSpec((d, tn), lambda j: (0, j)),
            pl.BlockSpec((1, tn), lambda j: (0, j)),
        ],
        out_specs=pl.BlockSpec((c_pad.shape[0], tn), lambda j: (0, j)),
        compiler_params=pltpu.CompilerParams(
            dimension_semantics=("arbitrary",), vmem_limit_bytes=VMEM_LIMIT_BYTES),
        name="ada",
    )(c_pad, w, b)


_PROJ_GROUPS = (
    (SB_WIDTH, SB_HEAD_DIM ** -0.5),
    (SB_WIDTH, 1.0),
    (SB_WIDTH, 1.0),
    (GLA_QK_WIDTH, 1.0),
    (GLA_QK_WIDTH, 1.0),
    (GLA_V_WIDTH, 1.0),
    (GLA_V_WIDTH, 1.0),
    (LANES, 1.0),
)


def _inproj_kernel(x_ref, mod_ref, w_ref, *out_refs):
    shift = mod_ref[0:1, :]
    scale = mod_ref[1:2, :]
    h = (x_ref[...] * (1.0 + scale) + shift).astype(BF16)
    c0 = 0
    for (width, mul), o_ref in zip(_PROJ_GROUPS, out_refs):
        acc = _dot(h, w_ref[:, c0:c0 + width])
        if mul != 1.0:
            acc = acc * mul
        o_ref[...] = acc.astype(o_ref.dtype)
        c0 += width


def _inproj(x2d, mod, w_perm, seq):
    t, d = x2d.shape
    tm = ROW_TILE
    tiles_per_seq = seq // tm
    n = w_perm.shape[1]
    return pl.pallas_call(
        _inproj_kernel,
        out_shape=[jax.ShapeDtypeStruct((t, width), BF16) for width, _ in _PROJ_GROUPS],
        grid=(t // tm,),
        in_specs=[
            pl.BlockSpec((tm, d), lambda i: (i, 0)),
            pl.BlockSpec((None, SUBLANES, d), lambda i: (i // tiles_per_seq, 0, 0)),
            pl.BlockSpec((d, n), lambda i: (0, 0)),
        ],
        out_specs=[pl.BlockSpec((tm, width), lambda i: (i, 0)) for width, _ in _PROJ_GROUPS],
        compiler_params=pltpu.CompilerParams(
            dimension_semantics=("arbitrary",), vmem_limit_bytes=VMEM_LIMIT_BYTES),
        name="inproj",
    )(x2d, mod, w_perm)


def _sb_suffix_sums(sp_blocks, umat):
    m = sp_blocks[0].shape[0]
    hi, lo = _split_bf16(jnp.concatenate(sp_blocks, axis=0))
    res = _dot(hi, umat) + _dot(lo, umat)
    cs, nearer = [], None
    for c in reversed(range(len(sp_blocks))):
        part = res[c * m:(c + 1) * m]
        loc, tot = part[:, :SB_TK], part[:, SB_TK:]
        cs.append(loc if nearer is None else loc + nearer)
        nearer = tot if nearer is None else nearer + tot
    return cs[::-1], nearer


def _sb_kernel(q_ref, k_ref, v_ref, g_ref, o_ref):
    i = pl.program_id(1)
    pairs = q_ref.shape[1] // LANES
    lane = lax.broadcasted_iota(jnp.int32, (SB_TQ, LANES), 1)
    head0 = lane < SB_HEAD_DIM

    uj = lax.broadcasted_iota(jnp.int32, (SB_TK, 2 * SB_TK), 0)
    us = lax.broadcasted_iota(jnp.int32, (SB_TK, 2 * SB_TK), 1)
    umat = jnp.where((uj >= us) | (us >= SB_TK), 1.0, 0.0).astype(BF16)

    start_blk = jnp.maximum(i - (SB_WIN_BLOCKS - 1), 0)
    start = pl.multiple_of(start_blk * SB_TK, SB_TK)
    row = lax.broadcasted_iota(jnp.int32, (2 * SB_TQ, SB_WIN), 0) & (SB_TQ - 1)
    col = lax.broadcasted_iota(jnp.int32, (2 * SB_TQ, SB_WIN), 1)
    valid = (start + col) < (i * SB_TQ + row)

    qs, carries, accs = [], [], []
    for p in range(pairs):
        lanes = slice(p * LANES, (p + 1) * LANES)
        q = q_ref[:, lanes]
        zq = jnp.zeros_like(q)
        qs.append(jnp.concatenate([jnp.where(head0, q, zq), jnp.where(head0, zq, q)], axis=0))
        z = _dot_nt(qs[p], k_ref[pl.ds(start, SB_WIN), lanes])
        sp = jnp.where(valid, _softplus(z), 0.0)
        cs, tot = _sb_suffix_sums(
            [sp[:, c * SB_TK:(c + 1) * SB_TK] for c in range(SB_WIN_BLOCKS)], umat)
        arg = jnp.where(valid, z - jnp.concatenate(cs, axis=1), -1e30)
        w = jnp.exp(arg).astype(BF16)
        accs.append(_dot(w, v_ref[pl.ds(start, SB_WIN), lanes]))
        carries.append(tot)

    def not_done(cs_):
        m = cs_[0]
        for c in cs_[1:]:
            m = jnp.minimum(m, c)
        return (jnp.min(m) < SB_EXIT).astype(jnp.int32)

    def cond(s):
        return jnp.logical_and(s[0] >= 0, s[1] > 0)

    def body(s):
        j, _, cs_, as_ = s
        rows = pl.ds(pl.multiple_of(j * SB_TK, SB_TK), SB_TK)
        new_c, new_a = [], []
        for p in range(pairs):
            lanes = slice(p * LANES, (p + 1) * LANES)
            z = _dot_nt(qs[p], k_ref[rows, lanes])
            (loc,), tot = _sb_suffix_sums([_softplus(z)], umat)
            w = jnp.exp(z - loc - cs_[p]).astype(BF16)
            new_a.append(as_[p] + _dot(w, v_ref[rows, lanes]))
            new_c.append(cs_[p] + tot)
        return j - 1, not_done(new_c), tuple(new_c), tuple(new_a)

    _, _, _, accs = lax.while_loop(
        cond, body, (start_blk - 1, not_done(carries), tuple(carries), tuple(accs)))

    outs = []
    for p in range(pairs):
        o = jnp.where(head0, accs[p][:SB_TQ], accs[p][SB_TQ:])
        sq = o * o
        ss0 = jnp.sum(jnp.where(head0, sq, 0.0), axis=-1, keepdims=True)
        ss1 = jnp.sum(jnp.where(head0, 0.0, sq), axis=-1, keepdims=True)
        ms = jnp.where(head0, ss0, ss1) * (1.0 / SB_HEAD_DIM)
        outs.append(o * lax.rsqrt(ms + RMS_EPS))
    o_ref[...] = (jnp.concatenate(outs, axis=1) * g_ref[...]).astype(o_ref.dtype)


def _sb_attention(q, k, v, g):
    b, s, wdt = q.shape
    return pl.pallas_call(
        _sb_kernel,
        out_shape=jax.ShapeDtypeStruct((b, s, wdt), BF16),
        grid=(b, s // SB_TQ),
        in_specs=[
            pl.BlockSpec((None, SB_TQ, wdt), lambda bi, i: (bi, i, 0)),
            pl.BlockSpec((None, s, wdt), lambda bi, i: (bi, 0, 0)),
            pl.BlockSpec((None, s, wdt), lambda bi, i: (bi, 0, 0)),
            pl.BlockSpec((1, wdt), lambda bi, i: (0, 0)),
        ],
        out_specs=pl.BlockSpec((None, SB_TQ, wdt), lambda bi, i: (bi, i, 0)),
        compiler_params=pltpu.CompilerParams(
            dimension_semantics=("arbitrary", "arbitrary"),
            vmem_limit_bytes=VMEM_LIMIT_BYTES),
        name="sb_attention",
    )(q, k, v, g)


def _gla_kernel(q_ref, k_ref, v_ref, r_ref, lr_ref, wg_ref, bg_ref, g_ref, o_ref, st_ref):
    c = GLA_CHUNK
    dv2 = 2 * GLA_VAL_DIM

    @pl.when(pl.program_id(2) == 0)
    def _():
        st_ref[...] = jnp.zeros_like(st_ref)

    u = _dot(lr_ref[...], wg_ref[...]) + bg_ref[...]
    log_a = (jnp.minimum(u, 0.0) - jnp.log(1.0 + jnp.exp(-jnp.abs(u)))) * (1.0 / GLA_GATE_TEMP)

    ti = lax.broadcasted_iota(jnp.int32, (c, c), 0)
    tj = lax.broadcasted_iota(jnp.int32, (c, c), 1)
    lower = ti >= tj
    tril = jnp.where(lower, 1.0, 0.0).astype(BF16)
    lane = lax.broadcasted_iota(jnp.int32, (c, LANES), 1)
    head0 = lane < GLA_KEY_DIM
    se = lax.broadcasted_iota(jnp.int32, (dv2, LANES), 0)
    sd = lax.broadcasted_iota(jnp.int32, (dv2, LANES), 1)
    st_mask = (se < GLA_VAL_DIM) == (sd < GLA_KEY_DIM)

    st = st_ref[...]
    for n in range(GLA_STEP // c):
        rows = slice(n * c, (n + 1) * c)
        la_hi, la_lo = _split_bf16(log_a[rows])
        bcum = _dot(tril, la_hi) + _dot(tril, la_lo)
        blast = bcum[c - 1:c, :]
        qf = q_ref[rows, :].astype(F32)
        kf = k_ref[rows, :].astype(F32)
        vv = v_ref[rows, :]
        q_dec = (qf * jnp.exp(bcum) * (GLA_KEY_DIM ** -0.5)).astype(BF16)
        k_inv = (kf * jnp.exp(-bcum)).astype(BF16)
        k_end = (kf * jnp.exp(blast - bcum)).astype(BF16)
        zq = jnp.zeros_like(q_dec)
        a0 = jnp.where(lower, _dot_nt(jnp.where(head0, q_dec, zq), k_inv), 0.0).astype(BF16)
        a1 = jnp.where(lower, _dot_nt(jnp.where(head0, zq, q_dec), k_inv), 0.0).astype(BF16)
        o_intra = jnp.concatenate(
            [_dot(a0, vv[:, :GLA_VAL_DIM]), _dot(a1, vv[:, GLA_VAL_DIM:])], axis=1)
        o_inter = _dot_nt(q_dec, st.astype(BF16))
        st = st * jnp.exp(blast) + jnp.where(st_mask, _dot_tn(vv, k_end), 0.0)
        o = o_intra + o_inter
        o0 = o[:, :GLA_VAL_DIM]
        o1 = o[:, GLA_VAL_DIM:]
        n0 = o0 * lax.rsqrt(jnp.mean(o0 * o0, axis=-1, keepdims=True) + RMS_EPS)
        n1 = o1 * lax.rsqrt(jnp.mean(o1 * o1, axis=-1, keepdims=True) + RMS_EPS)
        rf = r_ref[rows, :].astype(F32)
        gate = rf / (1.0 + jnp.exp(-rf))
        o_ref[rows, :] = (jnp.concatenate([n0, n1], axis=1) * g_ref[...] * gate).astype(o_ref.dtype)
    st_ref[...] = st


def _gla(q, k, v, r, lr, wg, bg, g):
    b, s, _ = q.shape
    pairs = GLA_HEADS // 2
    step = GLA_STEP
    dv2 = 2 * GLA_VAL_DIM
    seq_map = lambda bi, p, t: (bi, t, p)
    return pl.pallas_call(
        _gla_kernel,
        out_shape=jax.ShapeDtypeStruct((b, s, GLA_V_WIDTH), BF16),
        grid=(b, pairs, s // step),
        in_specs=[
            pl.BlockSpec((None, step, LANES), seq_map),
            pl.BlockSpec((None, step, LANES), seq_map),
            pl.BlockSpec((None, step, dv2), seq_map),
            pl.BlockSpec((None, step, dv2), seq_map),
            pl.BlockSpec((None, step, LANES), lambda bi, p, t: (bi, t, 0)),
            pl.BlockSpec((LANES, LANES), lambda bi, p, t: (0, p)),
            pl.BlockSpec((1, LANES), lambda bi, p, t: (0, p)),
            pl.BlockSpec((1, dv2), lambda bi, p, t: (0, p)),
        ],
        out_specs=pl.BlockSpec((None, step, dv2), seq_map),
        scratch_shapes=[pltpu.VMEM((dv2, LANES), F32)],
        compiler_params=pltpu.CompilerParams(
            dimension_semantics=("arbitrary", "arbitrary", "arbitrary"),
            vmem_limit_bytes=VMEM_LIMIT_BYTES),
        name="gla",
    )(q, k, v, r, lr, wg, bg, g)


def _outproj_kernel(alpha, osb_ref, ogla_ref, x_ref, mod_ref, wa_ref, wb_ref, g_ref, b_ref,
                    x1_ref, h2_ref):
    y = _dot(osb_ref[...], wa_ref[...]) + _dot(ogla_ref[...], wb_ref[...])
    gate_a = mod_ref[2:3, :]
    x1 = _layer_norm(alpha * x_ref[...] + (1.0 + gate_a) * y, g_ref[...], b_ref[...])
    x1_ref[...] = x1
    h2_ref[...] = (x1 * (1.0 + mod_ref[4:5, :]) + mod_ref[3:4, :]).astype(h2_ref.dtype)


def _outproj(o_sb, o_gla, x2d, mod, w_a, w_b, ln_g, ln_b, seq, alpha):
    t, d = x2d.shape
    tm = ROW_TILE
    tiles_per_seq = seq // tm
    row_map = lambda i: (i, 0)
    fixed = lambda i: (0, 0)
    return pl.pallas_call(
        functools.partial(_outproj_kernel, alpha),
        out_shape=[jax.ShapeDtypeStruct((t, d), F32), jax.ShapeDtypeStruct((t, d), BF16)],
        grid=(t // tm,),
        in_specs=[
            pl.BlockSpec((tm, o_sb.shape[1]), row_map),
            pl.BlockSpec((tm, o_gla.shape[1]), row_map),
            pl.BlockSpec((tm, d), row_map),
            pl.BlockSpec((None, SUBLANES, d), lambda i: (i // tiles_per_seq, 0, 0)),
            pl.BlockSpec(w_a.shape, fixed),
            pl.BlockSpec(w_b.shape, fixed),
            pl.BlockSpec((1, d), fixed),
            pl.BlockSpec((1, d), fixed),
        ],
        out_specs=[pl.BlockSpec((tm, d), row_map), pl.BlockSpec((tm, d), row_map)],
        compiler_params=pltpu.CompilerParams(
            dimension_semantics=("arbitrary",), vmem_limit_bytes=VMEM_LIMIT_BYTES),
        name="outproj",
    )(o_sb, o_gla, x2d, mod, w_a, w_b, ln_g, ln_b)


def _ffn_kernel(alpha, tiles_per_seq, h_ref, x1_ref, mod_ref, wg_ref, wu_ref, cw_ref, cb_ref,
                wd_ref, g_ref, b_ref, o_ref, halo_ref, acc_ref):
    tm = h_ref.shape[0]
    d_ff = wg_ref.shape[1]
    first = (pl.program_id(0) % tiles_per_seq) == 0
    h = h_ref[...]
    top = lax.broadcasted_iota(jnp.int32, (SUBLANES, FF_TILE), 0)

    @pl.when(first)
    def _():
        halo_ref[...] = jnp.zeros_like(halo_ref)

    for f in range(d_ff // FF_TILE):
        cols = slice(f * FF_TILE, (f + 1) * FF_TILE)
        gt = _dot(h, wg_ref[:, cols])
        up = _dot(h, wu_ref[:, cols])
        prev = halo_ref[:, cols]
        halo_ref[:, cols] = gt[tm - SUBLANES:, :]
        r1 = pltpu.roll(gt, 1, 0)
        r2 = pltpu.roll(gt, 2, 0)
        t1 = jnp.where(top < 1, pltpu.roll(prev, 1, 0), r1[:SUBLANES])
        t2 = jnp.where(top < 2, pltpu.roll(prev, 2, 0), r2[:SUBLANES])
        g1 = jnp.concatenate([t1, r1[SUBLANES:]], axis=0)
        g2 = jnp.concatenate([t2, r2[SUBLANES:]], axis=0)
        conv = g2 * cw_ref[0:1, cols] + g1 * cw_ref[1:2, cols] + gt * cw_ref[2:3, cols] + cb_ref[:, cols]
        act = (conv / (1.0 + jnp.exp(-conv)) * up).astype(BF16)
        part = _dot(act, wd_ref[cols, :])
        if f == 0:
            acc_ref[...] = part
        else:
            acc_ref[...] += part
    gate_f = mod_ref[5:6, :]
    o_ref[...] = _layer_norm(alpha * x1_ref[...] + (1.0 + gate_f) * acc_ref[...], g_ref[...], b_ref[...])


def _ffn(h2, x1, mod, wg, wu, cw, cb, wd, ln_g, ln_b, seq, alpha):
    t, d = x1.shape
    d_ff = wg.shape[1]
    tm = ROW_TILE
    tiles_per_seq = seq // tm
    row_map = lambda i: (i, 0)
    fixed = lambda i: (0, 0)
    return pl.pallas_call(
        functools.partial(_ffn_kernel, alpha, tiles_per_seq),
        out_shape=jax.ShapeDtypeStruct((t, d), F32),
        grid=(t // tm,),
        in_specs=[
            pl.BlockSpec((tm, d), row_map),
            pl.BlockSpec((tm, d), row_map),
            pl.BlockSpec((None, SUBLANES, d), lambda i: (i // tiles_per_seq, 0, 0)),
            pl.BlockSpec(wg.shape, fixed),
            pl.BlockSpec(wu.shape, fixed),
            pl.BlockSpec(cw.shape, fixed),
            pl.BlockSpec(cb.shape, fixed),
            pl.BlockSpec(wd.shape, fixed),
            pl.BlockSpec((1, d), fixed),
            pl.BlockSpec((1, d), fixed),
        ],
        out_specs=pl.BlockSpec((tm, d), row_map),
        scratch_shapes=[pltpu.VMEM((SUBLANES, d_ff), F32), pltpu.VMEM((tm, d), F32)],
        compiler_params=pltpu.CompilerParams(
            dimension_semantics=("arbitrary",), vmem_limit_bytes=VMEM_LIMIT_BYTES),
        name="ffn",
    )(h2, x1, mod, wg, wu, cw, cb, wd, ln_g, ln_b)


def _permute_w_in(w_in):
    d = w_in.shape[0]
    splits = (SB_WIDTH, SB_WIDTH, SB_WIDTH, GLA_QK_WIDTH, GLA_QK_WIDTH, GLA_V_WIDTH,
              GLA_GATE_RANK, GLA_V_WIDTH)
    parts = []
    c0 = 0
    for wdt in splits:
        parts.append(w_in[:, c0:c0 + wdt])
        c0 += wdt
    sb_q, sb_k, sb_v, g_q, g_k, g_v, g_lr, g_r = parts
    g_lr = jnp.concatenate([g_lr, jnp.zeros((d, LANES - GLA_GATE_RANK), w_in.dtype)], axis=1)
    return jnp.concatenate([sb_q, sb_k, sb_v, g_q, g_k, g_v, g_r, g_lr], axis=1).astype(BF16)


def kernel(x, c, w_ada, b_ada, w_in, gla_w_gate, gla_b_gate, sb_norm_g, gla_norm_g, w_out,
           ln1_g, ln1_b, w_ff_gate, w_ff_up, conv_w, conv_b, w_down, ln2_g, ln2_b):
    bsz, seq, d = x.shape
    depth = w_ada.shape[0]
    alpha = float((2 * depth) ** 0.25)
    t = bsz * seq

    c_pad = jnp.concatenate([c, jnp.zeros((SUBLANES - bsz, d), c.dtype)], axis=0)
    xcur = x.reshape(t, d)
    for l in range(depth):
        mod = _ada(c_pad, w_ada[l], b_ada[l][None, :])
        mod = mod[:bsz].reshape(bsz, N_MOD, d)
        mod = jnp.concatenate([mod, jnp.zeros((bsz, SUBLANES - N_MOD, d), F32)], axis=1)

        sb_q, sb_k, sb_v, g_q, g_k, g_v, g_r, g_lr = _inproj(xcur, mod, _permute_w_in(w_in[l]), seq)
        as_seq = lambda a: a.reshape(bsz, seq, a.shape[-1])

        o_sb = _sb_attention(as_seq(sb_q), as_seq(sb_k), as_seq(sb_v), sb_norm_g[l][None, :])
        wg_pad = jnp.concatenate(
            [gla_w_gate[l], jnp.zeros((LANES - GLA_GATE_RANK, GLA_QK_WIDTH), F32)], axis=0).astype(BF16)
        o_gla = _gla(as_seq(g_q), as_seq(g_k), as_seq(g_v), as_seq(g_r), as_seq(g_lr),
                     wg_pad, gla_b_gate[l][None, :], gla_norm_g[l][None, :])

        w_o = w_out[l].astype(BF16)
        x1, h2 = _outproj(o_sb.reshape(t, SB_WIDTH), o_gla.reshape(t, GLA_V_WIDTH), xcur, mod,
                          w_o[:SB_WIDTH], w_o[SB_WIDTH:], ln1_g[l][None, :], ln1_b[l][None, :],
                          seq, alpha)
        xcur = _ffn(h2, x1, mod, w_ff_gate[l].astype(BF16), w_ff_up[l].astype(BF16),
                    conv_w[l], conv_b[l][None, :], w_down[l].astype(BF16),
                    ln2_g[l][None, :], ln2_b[l][None, :], seq, alpha)
    return xcur.reshape(bsz, seq, d)
```

```python
import functools

import jax
import jax.numpy as jnp
from jax import lax
from jax.experimental import pallas as pl
from jax.experimental.pallas import tpu as pltpu

F32 = jnp.float32
BF16 = jnp.bfloat16

LANES = 128
SUBLANES = 8
VMEM_LIMIT_BYTES = 56 * 1024 * 1024

SB_HEADS = 8
SB_HEAD_DIM = 64
GLA_HEADS = 4
GLA_KEY_DIM = 64
GLA_VAL_DIM = 128
GLA_GATE_RANK = 16
GLA_GATE_TEMP = 16.0
GLA_CHUNK = 64
CONV_WIDTH = 3
N_MOD = 6
LN_EPS = 1e-5
RMS_EPS = 1e-6

SB_WIDTH = SB_HEADS * SB_HEAD_DIM
GLA_QK_WIDTH = GLA_HEADS * GLA_KEY_DIM
GLA_V_WIDTH = GLA_HEADS * GLA_VAL_DIM

SB_EXIT = 104.0
LOG2E = 1.4426950408889634
SB_EXIT_BITS = SB_EXIT * LOG2E
SB_TQ = 128
SB_TK = 128
SB_WIN_BLOCKS = 3
SB_WIN = SB_WIN_BLOCKS * SB_TK

GLA_STEP = 512
ROW_TILE = 512
FF_TILE = 256


def _softplus2(z):
    return jnp.maximum(jnp.log2(1.0 + jnp.exp2(jnp.minimum(z, 126.0))), z)


def _split_bf16(a):
    hi = a.astype(BF16)
    lo = (a - hi.astype(F32)).astype(BF16)
    return hi, lo


def _dot(a, b):
    return jnp.dot(a, b, preferred_element_type=F32)


def _dot_nt(a, b):
    return lax.dot_general(a, b, (((1,), (1,)), ((), ())), preferred_element_type=F32)


def _dot_tn(a, b):
    return lax.dot_general(a, b, (((0,), (0,)), ((), ())), preferred_element_type=F32)


def _layer_norm(v, g, b):
    mu = jnp.mean(v, axis=-1, keepdims=True)
    vc = v - mu
    var = jnp.mean(vc * vc, axis=-1, keepdims=True)
    return vc * lax.rsqrt(var + LN_EPS) * g + b


def _ada_kernel(c_ref, w_ref, b_ref, o_ref):
    c = c_ref[...]
    ca = c / (1.0 + jnp.exp(-c))
    ch, cl = _split_bf16(ca)
    wh, wl = _split_bf16(w_ref[...])
    o_ref[...] = _dot(ch, wh) + _dot(cl, wh) + _dot(ch, wl) + b_ref[...]


def _ada(c_pad, w, b):
    d, n = w.shape
    tn = 1024
    return pl.pallas_call(
        _ada_kernel,
        out_shape=jax.ShapeDtypeStruct((c_pad.shape[0], n), F32),
        grid=(n // tn,),
        in_specs=[
            pl.BlockSpec((c_pad.shape[0], d), lambda j: (0, 0)),
            pl.BlockSpec((d, tn), lambda j: (0, j)),
            pl.BlockSpec((1, tn), lambda j: (0, j)),
        ],
        out_specs=pl.BlockSpec((c_pad.shape[0], tn), lambda j: (0, j)),
        compiler_params=pltpu.CompilerParams(
            dimension_semantics=("arbitrary",), vmem_limit_bytes=VMEM_LIMIT_BYTES),
        name="ada",
    )(c_pad, w, b)


_PROJ_GROUPS = (
    (SB_WIDTH, SB_HEAD_DIM ** -0.5 * LOG2E),
    (SB_WIDTH, 1.0),
    (SB_WIDTH, 1.0),
    (GLA_QK_WIDTH, 1.0),
    (GLA_QK_WIDTH, 1.0),
    (GLA_V_WIDTH, 1.0),
    (GLA_V_WIDTH, 1.0),
    (LANES, 1.0),
)


def _inproj_kernel(x_ref, mod_ref, w_ref, *out_refs):
    shift = mod_ref[0:1, :]
    scale = mod_ref[1:2, :]
    h = (x_ref[...] * (1.0 + scale) + shift).astype(BF16)
    c0 = 0
    for (width, mul), o_ref in zip(_PROJ_GROUPS, out_refs):
        acc = _dot(h, w_ref[:, c0:c0 + width])
        if mul != 1.0:
            acc = acc * mul
        o_ref[...] = acc.astype(o_ref.dtype)
        c0 += width


def _inproj(x2d, mod, w_perm, seq):
    t, d = x2d.shape
    tm = ROW_TILE
    tiles_per_seq = seq // tm
    n = w_perm.shape[1]
    return pl.pallas_call(
        _inproj_kernel,
        out_shape=[jax.ShapeDtypeStruct((t, width), BF16) for width, _ in _PROJ_GROUPS],
        grid=(t // tm,),
        in_specs=[
            pl.BlockSpec((tm, d), lambda i: (i, 0)),
            pl.BlockSpec((None, SUBLANES, d), lambda i: (i // tiles_per_seq, 0, 0)),
            pl.BlockSpec((d, n), lambda i: (0, 0)),
        ],
        out_specs=[pl.BlockSpec((tm, width), lambda i: (i, 0)) for width, _ in _PROJ_GROUPS],
        compiler_params=pltpu.CompilerParams(
            dimension_semantics=("arbitrary",), vmem_limit_bytes=VMEM_LIMIT_BYTES),
        name="inproj",
    )(x2d, mod, w_perm)


def _sb_suffix_sums(sp_blocks, umat):
    m = sp_blocks[0].shape[0]
    res = _dot(jnp.concatenate(sp_blocks, axis=0).astype(BF16), umat)
    cs, nearer = [], None
    for c in reversed(range(len(sp_blocks))):
        part = res[c * m:(c + 1) * m]
        loc, tot = part[:, :SB_TK], part[:, SB_TK:]
        cs.append(loc if nearer is None else loc + nearer)
        nearer = tot if nearer is None else nearer + tot
    return cs[::-1], nearer


def _sb_kernel(q_ref, k_ref, v_ref, g_ref, o_ref):
    i = pl.program_id(1)
    pairs = q_ref.shape[1] // LANES
    lane = lax.broadcasted_iota(jnp.int32, (SB_TQ, LANES), 1)
    head0 = lane < SB_HEAD_DIM

    uj = lax.broadcasted_iota(jnp.int32, (SB_TK, 2 * SB_TK), 0)
    us = lax.broadcasted_iota(jnp.int32, (SB_TK, 2 * SB_TK), 1)
    umat = jnp.where((uj >= us) | (us >= SB_TK), 1.0, 0.0).astype(BF16)

    nb = SB_WIN_BLOCKS
    blk_rows, blk_exists = [], []
    for c in range(nb):
        jb = i - (nb - 1 - c)
        blk_exists.append(jb >= 0)
        blk_rows.append(pl.ds(pl.multiple_of(jnp.maximum(jb, 0) * SB_TK, SB_TK), SB_TK))
    row = lax.broadcasted_iota(jnp.int32, (2 * SB_TQ, SB_TK), 0) & (SB_TQ - 1)
    col = lax.broadcasted_iota(jnp.int32, (2 * SB_TQ, SB_TK), 1)
    causal = col < row

    lanes = [slice(p * LANES, (p + 1) * LANES) for p in range(pairs)]
    qs, zs = [], []
    for p in range(pairs):
        q = q_ref[:, lanes[p]]
        zq = jnp.zeros_like(q)
        qs.append(jnp.concatenate([jnp.where(head0, q, zq), jnp.where(head0, zq, q)], axis=0))
        kw = jnp.concatenate([k_ref[r, lanes[p]] for r in blk_rows], axis=0)
        zs.append(_dot_nt(qs[p], kw))
    sps = []
    for p in range(pairs):
        blocks = [_softplus2(zs[p][:, c * SB_TK:(c + 1) * SB_TK]) for c in range(nb)]
        blocks[-1] = jnp.where(causal, blocks[-1], 0.0)
        sps.append(blocks)
    sums = [_sb_suffix_sums(sps[p], umat) for p in range(pairs)]
    ws = []
    for p in range(pairs):
        cs = sums[p][0]
        args = [zs[p][:, c * SB_TK:(c + 1) * SB_TK] - cs[c] for c in range(nb)]
        args[-1] = jnp.where(causal, args[-1], -1e30)
        ws.append(jnp.exp2(jnp.concatenate(args, axis=1)).astype(BF16))
    accs = []
    for p in range(pairs):
        vw = [v_ref[r, lanes[p]] for r in blk_rows]
        vw = [jnp.where(e, v, jnp.zeros_like(v)) for e, v in zip(blk_exists[:-1], vw[:-1])] + vw[-1:]
        accs.append(_dot(ws[p], jnp.concatenate(vw, axis=0)))
    carries = [sums[p][1] for p in range(pairs)]

    def not_done(cs_):
        m = cs_[0]
        for c in cs_[1:]:
            m = jnp.minimum(m, c)
        return (jnp.min(m) < SB_EXIT_BITS).astype(jnp.int32)

    def cond(s):
        return jnp.logical_and(s[0] >= 0, s[1] > 0)

    def body(s):
        j, _, cs_, as_ = s
        rows = pl.ds(pl.multiple_of(j * SB_TK, SB_TK), SB_TK)
        zb = [_dot_nt(qs[p], k_ref[rows, lanes[p]]) for p in range(pairs)]
        sb = [_sb_suffix_sums([_softplus2(zb[p])], umat) for p in range(pairs)]
        wb = [jnp.exp2(zb[p] - sb[p][0][0] - cs_[p]).astype(BF16) for p in range(pairs)]
        new_a = [as_[p] + _dot(wb[p], v_ref[rows, lanes[p]]) for p in range(pairs)]
        new_c = [cs_[p] + sb[p][1] for p in range(pairs)]
        return j - 1, not_done(new_c), tuple(new_c), tuple(new_a)

    _, _, _, accs = lax.while_loop(
        cond, body, (i - nb, not_done(carries), tuple(carries), tuple(accs)))

    outs = []
    for p in range(pairs):
        o = jnp.where(head0, accs[p][:SB_TQ], accs[p][SB_TQ:])
        sq = o * o
        ss0 = jnp.sum(jnp.where(head0, sq, 0.0), axis=-1, keepdims=True)
        ss1 = jnp.sum(jnp.where(head0, 0.0, sq), axis=-1, keepdims=True)
        ms = jnp.where(head0, ss0, ss1) * (1.0 / SB_HEAD_DIM)
        outs.append(o * lax.rsqrt(ms + RMS_EPS))
    o_ref[...] = (jnp.concatenate(outs, axis=1) * g_ref[...]).astype(o_ref.dtype)


def _sb_attention(q, k, v, g):
    b, s, wdt = q.shape
    return pl.pallas_call(
        _sb_kernel,
        out_shape=jax.ShapeDtypeStruct((b, s, wdt), BF16),
        grid=(b, s // SB_TQ),
        in_specs=[
            pl.BlockSpec((None, SB_TQ, wdt), lambda bi, i: (bi, i, 0)),
            pl.BlockSpec((None, s, wdt), lambda bi, i: (bi, 0, 0)),
            pl.BlockSpec((None, s, wdt), lambda bi, i: (bi, 0, 0)),
            pl.BlockSpec((1, wdt), lambda bi, i: (0, 0)),
        ],
        out_specs=pl.BlockSpec((None, SB_TQ, wdt), lambda bi, i: (bi, i, 0)),
        compiler_params=pltpu.CompilerParams(
            dimension_semantics=("arbitrary", "arbitrary"),
            vmem_limit_bytes=VMEM_LIMIT_BYTES),
        name="sb_attention",
    )(q, k, v, g)


def _gla_kernel(q_ref, k_ref, v_ref, r_ref, lr_ref, wg_ref, bg_ref, g_ref, o_ref, st_ref):
    c = GLA_CHUNK
    dv2 = 2 * GLA_VAL_DIM

    @pl.when(pl.program_id(2) == 0)
    def _():
        st_ref[...] = jnp.zeros_like(st_ref)

    u = _dot(lr_ref[...], wg_ref[...]) + bg_ref[...]
    log_a = (jnp.minimum(u, 0.0) - jnp.log(1.0 + jnp.exp(-jnp.abs(u)))) * (1.0 / GLA_GATE_TEMP)

    ti = lax.broadcasted_iota(jnp.int32, (c, c), 0)
    tj = lax.broadcasted_iota(jnp.int32, (c, c), 1)
    lower = ti >= tj
    tril = jnp.where(lower, 1.0, 0.0).astype(BF16)
    lane = lax.broadcasted_iota(jnp.int32, (c, LANES), 1)
    head0 = lane < GLA_KEY_DIM
    se = lax.broadcasted_iota(jnp.int32, (dv2, LANES), 0)
    sd = lax.broadcasted_iota(jnp.int32, (dv2, LANES), 1)
    st_mask = (se < GLA_VAL_DIM) == (sd < GLA_KEY_DIM)

    st = st_ref[...]
    for n in range(GLA_STEP // c):
        rows = slice(n * c, (n + 1) * c)
        la_hi, la_lo = _split_bf16(log_a[rows])
        bcum = _dot(tril, la_hi) + _dot(tril, la_lo)
        blast = bcum[c - 1:c, :]
        qf = q_ref[rows, :].astype(F32)
        kf = k_ref[rows, :].astype(F32)
        vv = v_ref[rows, :]
        q_dec = (qf * jnp.exp(bcum) * (GLA_KEY_DIM ** -0.5)).astype(BF16)
        k_inv = (kf * jnp.exp(-bcum)).astype(BF16)
        k_end = (kf * jnp.exp(blast - bcum)).astype(BF16)
        zq = jnp.zeros_like(q_dec)
        a0 = jnp.where(lower, _dot_nt(jnp.where(head0, q_dec, zq), k_inv), 0.0).astype(BF16)
        a1 = jnp.where(lower, _dot_nt(jnp.where(head0, zq, q_dec), k_inv), 0.0).astype(BF16)
        o_intra = jnp.concatenate(
            [_dot(a0, vv[:, :GLA_VAL_DIM]), _dot(a1, vv[:, GLA_VAL_DIM:])], axis=1)
        o_inter = _dot_nt(q_dec, st.astype(BF16))
        st = st * jnp.exp(blast) + jnp.where(st_mask, _dot_tn(vv, k_end), 0.0)
        o = o_intra + o_inter
        o0 = o[:, :GLA_VAL_DIM]
        o1 = o[:, GLA_VAL_DIM:]
        n0 = o0 * lax.rsqrt(jnp.mean(o0 * o0, axis=-1, keepdims=True) + RMS_EPS)
        n1 = o1 * lax.rsqrt(jnp.mean(o1 * o1, axis=-1, keepdims=True) + RMS_EPS)
        rf = r_ref[rows, :].astype(F32)
        gate = rf / (1.0 + jnp.exp(-rf))
        o_ref[rows, :] = (jnp.concatenate([n0, n1], axis=1) * g_ref[...] * gate).astype(o_ref.dtype)
    st_ref[...] = st


def _gla(q, k, v, r, lr, wg, bg, g):
    b, s, _ = q.shape
    pairs = GLA_HEADS // 2
    step = GLA_STEP
    dv2 = 2 * GLA_VAL_DIM
    seq_map = lambda bi, p, t: (bi, t, p)
    return pl.pallas_call(
        _gla_kernel,
        out_shape=jax.ShapeDtypeStruct((b, s, GLA_V_WIDTH), BF16),
        grid=(b, pairs, s // step),
        in_specs=[
            pl.BlockSpec((None, step, LANES), seq_map),
            pl.BlockSpec((None, step, LANES), seq_map),
            pl.BlockSpec((None, step, dv2), seq_map),
            pl.BlockSpec((None, step, dv2), seq_map),
            pl.BlockSpec((None, step, LANES), lambda bi, p, t: (bi, t, 0)),
            pl.BlockSpec((LANES, LANES), lambda bi, p, t: (0, p)),
            pl.BlockSpec((1, LANES), lambda bi, p, t: (0, p)),
            pl.BlockSpec((1, dv2), lambda bi, p, t: (0, p)),
        ],
        out_specs=pl.BlockSpec((None, step, dv2), seq_map),
        scratch_shapes=[pltpu.VMEM((dv2, LANES), F32)],
        compiler_params=pltpu.CompilerParams(
            dimension_semantics=("arbitrary", "arbitrary", "arbitrary"),
            vmem_limit_bytes=VMEM_LIMIT_BYTES),
        name="gla",
    )(q, k, v, r, lr, wg, bg, g)


def _outproj_kernel(alpha, osb_ref, ogla_ref, x_ref, mod_ref, wa_ref, wb_ref, g_ref, b_ref,
                    x1_ref, h2_ref):
    y = _dot(osb_ref[...], wa_ref[...]) + _dot(ogla_ref[...], wb_ref[...])
    gate_a = mod_ref[2:3, :]
    x1 = _layer_norm(alpha * x_ref[...] + (1.0 + gate_a) * y, g_ref[...], b_ref[...])
    x1_ref[...] = x1
    h2_ref[...] = (x1 * (1.0 + mod_ref[4:5, :]) + mod_ref[3:4, :]).astype(h2_ref.dtype)


def _outproj(o_sb, o_gla, x2d, mod, w_a, w_b, ln_g, ln_b, seq, alpha):
    t, d = x2d.shape
    tm = ROW_TILE
    tiles_per_seq = seq // tm
    row_map = lambda i: (i, 0)
    fixed = lambda i: (0, 0)
    return pl.pallas_call(
        functools.partial(_outproj_kernel, alpha),
        out_shape=[jax.ShapeDtypeStruct((t, d), F32), jax.ShapeDtypeStruct((t, d), BF16)],
        grid=(t // tm,),
        in_specs=[
            pl.BlockSpec((tm, o_sb.shape[1]), row_map),
            pl.BlockSpec((tm, o_gla.shape[1]), row_map),
            pl.BlockSpec((tm, d), row_map),
            pl.BlockSpec((None, SUBLANES, d), lambda i: (i // tiles_per_seq, 0, 0)),
            pl.BlockSpec(w_a.shape, fixed),
            pl.BlockSpec(w_b.shape, fixed),
            pl.BlockSpec((1, d), fixed),
            pl.BlockSpec((1, d), fixed),
        ],
        out_specs=[pl.BlockSpec((tm, d), row_map), pl.BlockSpec((tm, d), row_map)],
        compiler_params=pltpu.CompilerParams(
            dimension_semantics=("arbitrary",), vmem_limit_bytes=VMEM_LIMIT_BYTES),
        name="outproj",
    )(o_sb, o_gla, x2d, mod, w_a, w_b, ln_g, ln_b)


def _ffn_kernel(alpha, tiles_per_seq, h_ref, x1_ref, mod_ref, wg_ref, wu_ref, cw_ref, cb_ref,
                wd_ref, g_ref, b_ref, o_ref, halo_ref, acc_ref):
    tm = h_ref.shape[0]
    d_ff = wg_ref.shape[1]
    first = (pl.program_id(0) % tiles_per_seq) == 0
    h = h_ref[...]
    top = lax.broadcasted_iota(jnp.int32, (SUBLANES, FF_TILE), 0)

    @pl.when(first)
    def _():
        halo_ref[...] = jnp.zeros_like(halo_ref)

    for f in range(d_ff // FF_TILE):
        cols = slice(f * FF_TILE, (f + 1) * FF_TILE)
        gt = _dot(h, wg_ref[:, cols])
        up = _dot(h, wu_ref[:, cols])
        prev = halo_ref[:, cols]
        halo_ref[:, cols] = gt[tm - SUBLANES:, :]
        r1 = pltpu.roll(gt, 1, 0)
        r2 = pltpu.roll(gt, 2, 0)
        t1 = jnp.where(top < 1, pltpu.roll(prev, 1, 0), r1[:SUBLANES])
        t2 = jnp.where(top < 2, pltpu.roll(prev, 2, 0), r2[:SUBLANES])
        g1 = jnp.concatenate([t1, r1[SUBLANES:]], axis=0)
        g2 = jnp.concatenate([t2, r2[SUBLANES:]], axis=0)
        conv = g2 * cw_ref[0:1, cols] + g1 * cw_ref[1:2, cols] + gt * cw_ref[2:3, cols] + cb_ref[:, cols]
        act = (conv / (1.0 + jnp.exp(-conv)) * up).astype(BF16)
        part = _dot(act, wd_ref[cols, :])
        if f == 0:
            acc_ref[...] = part
        else:
            acc_ref[...] += part
    gate_f = mod_ref[5:6, :]
    o_ref[...] = _layer_norm(alpha * x1_ref[...] + (1.0 + gate_f) * acc_ref[...], g_ref[...], b_ref[...])


def _ffn(h2, x1, mod, wg, wu, cw, cb, wd, ln_g, ln_b, seq, alpha):
    t, d = x1.shape
    d_ff = wg.shape[1]
    tm = ROW_TILE
    tiles_per_seq = seq // tm
    row_map = lambda i: (i, 0)
    fixed = lambda i: (0, 0)
    return pl.pallas_call(
        functools.partial(_ffn_kernel, alpha, tiles_per_seq),
        out_shape=jax.ShapeDtypeStruct((t, d), F32),
        grid=(t // tm,),
        in_specs=[
            pl.BlockSpec((tm, d), row_map),
            pl.BlockSpec((tm, d), row_map),
            pl.BlockSpec((None, SUBLANES, d), lambda i: (i // tiles_per_seq, 0, 0)),
            pl.BlockSpec(wg.shape, fixed),
            pl.BlockSpec(wu.shape, fixed),
            pl.BlockSpec(cw.shape, fixed),
            pl.BlockSpec(cb.shape, fixed),
            pl.BlockSpec(wd.shape, fixed),
            pl.BlockSpec((1, d), fixed),
            pl.BlockSpec((1, d), fixed),
        ],
        out_specs=pl.BlockSpec((tm, d), row_map),
        scratch_shapes=[pltpu.VMEM((SUBLANES, d_ff), F32), pltpu.VMEM((tm, d), F32)],
        compiler_params=pltpu.CompilerParams(
            dimension_semantics=("arbitrary",), vmem_limit_bytes=VMEM_LIMIT_BYTES),
        name="ffn",
    )(h2, x1, mod, wg, wu, cw, cb, wd, ln_g, ln_b)


def _permute_w_in(w_in):
    d = w_in.shape[0]
    splits = (SB_WIDTH, SB_WIDTH, SB_WIDTH, GLA_QK_WIDTH, GLA_QK_WIDTH, GLA_V_WIDTH,
              GLA_GATE_RANK, GLA_V_WIDTH)
    parts = []
    c0 = 0
    for wdt in splits:
        parts.append(w_in[:, c0:c0 + wdt])
        c0 += wdt
    sb_q, sb_k, sb_v, g_q, g_k, g_v, g_lr, g_r = parts
    g_lr = jnp.concatenate([g_lr, jnp.zeros((d, LANES - GLA_GATE_RANK), w_in.dtype)], axis=1)
    return jnp.concatenate([sb_q, sb_k, sb_v, g_q, g_k, g_v, g_r, g_lr], axis=1).astype(BF16)


def kernel(x, c, w_ada, b_ada, w_in, gla_w_gate, gla_b_gate, sb_norm_g, gla_norm_g, w_out,
           ln1_g, ln1_b, w_ff_gate, w_ff_up, conv_w, conv_b, w_down, ln2_g, ln2_b):
    bsz, seq, d = x.shape
    depth = w_ada.shape[0]
    alpha = float((2 * depth) ** 0.25)
    t = bsz * seq

    c_pad = jnp.concatenate([c, jnp.zeros((SUBLANES - bsz, d), c.dtype)], axis=0)
    xcur = x.reshape(t, d)
    for l in range(depth):
        mod = _ada(c_pad, w_ada[l], b_ada[l][None, :])
        mod = mod[:bsz].reshape(bsz, N_MOD, d)
        mod = jnp.concatenate([mod, jnp.zeros((bsz, SUBLANES - N_MOD, d), F32)], axis=1)

        sb_q, sb_k, sb_v, g_q, g_k, g_v, g_r, g_lr = _inproj(xcur, mod, _permute_w_in(w_in[l]), seq)
        as_seq = lambda a: a.reshape(bsz, seq, a.shape[-1])

        o_sb = _sb_attention(as_seq(sb_q), as_seq(sb_k), as_seq(sb_v), sb_norm_g[l][None, :])
        wg_pad = jnp.concatenate(
            [gla_w_gate[l], jnp.zeros((LANES - GLA_GATE_RANK, GLA_QK_WIDTH), F32)], axis=0).astype(BF16)
        o_gla = _gla(as_seq(g_q), as_seq(g_k), as_seq(g_v), as_seq(g_r), as_seq(g_lr),
                     wg_pad, gla_b_gate[l][None, :], gla_norm_g[l][None, :])

        w_o = w_out[l].astype(BF16)
        x1, h2 = _outproj(o_sb.reshape(t, SB_WIDTH), o_gla.reshape(t, GLA_V_WIDTH), xcur, mod,
                          w_o[:SB_WIDTH], w_o[SB_WIDTH:], ln1_g[l][None, :], ln1_b[l][None, :],
                          seq, alpha)
        xcur = _ffn(h2, x1, mod, w_ff_gate[l].astype(BF16), w_ff_up[l].astype(BF16),
                    conv_w[l], conv_b[l][None, :], w_down[l].astype(BF16),
                    ln2_g[l][None, :], ln2_b[l][None, :], seq, alpha)
    return xcur.reshape(bsz, seq, d)
```

```python
import functools

import jax
import jax.numpy as jnp
from jax import lax
from jax.experimental import pallas as pl
from jax.experimental.pallas import tpu as pltpu

F32 = jnp.float32
BF16 = jnp.bfloat16

LANES = 128
SUBLANES = 8
VMEM_LIMIT_BYTES = 56 * 1024 * 1024

SB_HEADS = 8
SB_HEAD_DIM = 64
GLA_HEADS = 4
GLA_KEY_DIM = 64
GLA_VAL_DIM = 128
GLA_GATE_RANK = 16
GLA_GATE_TEMP = 16.0
GLA_CHUNK = 64
CONV_WIDTH = 3
N_MOD = 6
LN_EPS = 1e-5
RMS_EPS = 1e-6

SB_WIDTH = SB_HEADS * SB_HEAD_DIM
GLA_QK_WIDTH = GLA_HEADS * GLA_KEY_DIM
GLA_V_WIDTH = GLA_HEADS * GLA_VAL_DIM

SB_EXIT = 104.0
LOG2E = 1.4426950408889634
SB_EXIT_BITS = SB_EXIT * LOG2E
SB_TQ = 128
SB_TK = 128
SB_WIN_BLOCKS = 3
SB_WIN = SB_WIN_BLOCKS * SB_TK

GLA_STEP = 512
ROW_TILE = 512
FF_TILE = 256


def _softplus2(z):
    return jnp.maximum(jnp.log2(1.0 + jnp.exp2(jnp.minimum(z, 126.0))), z)


def _split_bf16(a):
    hi = a.astype(BF16)
    lo = (a - hi.astype(F32)).astype(BF16)
    return hi, lo


def _dot(a, b):
    return jnp.dot(a, b, preferred_element_type=F32)


def _dot_nt(a, b):
    return lax.dot_general(a, b, (((1,), (1,)), ((), ())), preferred_element_type=F32)


def _dot_tn(a, b):
    return lax.dot_general(a, b, (((0,), (0,)), ((), ())), preferred_element_type=F32)


def _layer_norm(v, g, b):
    mu = jnp.mean(v, axis=-1, keepdims=True)
    vc = v - mu
    var = jnp.mean(vc * vc, axis=-1, keepdims=True)
    return vc * lax.rsqrt(var + LN_EPS) * g + b


def _ada_kernel(c_ref, w_ref, b_ref, o_ref):
    c = c_ref[...]
    ca = c / (1.0 + jnp.exp(-c))
    ch, cl = _split_bf16(ca)
    wh, wl = _split_bf16(w_ref[...])
    o_ref[...] = _dot(ch, wh) + _dot(cl, wh) + _dot(ch, wl) + b_ref[...]


def _ada(c_pad, w, b):
    d, n = w.shape
    tn = 1024
    return pl.pallas_call(
        _ada_kernel,
        out_shape=jax.ShapeDtypeStruct((c_pad.shape[0], n), F32),
        grid=(n // tn,),
        in_specs=[
            pl.BlockSpec((c_pad.shape[0], d), lambda j: (0, 0)),
            pl.BlockSpec((d, tn), lambda j: (0, j)),
            pl.BlockSpec((1, tn), lambda j: (0, j)),
        ],
        out_specs=pl.BlockSpec((c_pad.shape[0], tn), lambda j: (0, j)),
        compiler_params=pltpu.CompilerParams(
            dimension_semantics=("arbitrary",), vmem_limit_bytes=VMEM_LIMIT_BYTES),
        name="ada",
    )(c_pad, w, b)


_PROJ_GROUPS = (
    (SB_WIDTH, SB_HEAD_DIM ** -0.5 * LOG2E),
    (SB_WIDTH, 1.0),
    (SB_WIDTH, 1.0),
    (GLA_QK_WIDTH, 1.0),
    (GLA_QK_WIDTH, 1.0),
    (GLA_V_WIDTH, 1.0),
    (GLA_V_WIDTH, 1.0),
    (LANES, 1.0),
)


def _inproj_kernel(x_ref, mod_ref, w_ref, *out_refs):
    shift = mod_ref[0:1, :]
    scale = mod_ref[1:2, :]
    h = (x_ref[...] * (1.0 + scale) + shift).astype(BF16)
    c0 = 0
    for (width, mul), o_ref in zip(_PROJ_GROUPS, out_refs):
        acc = _dot(h, w_ref[:, c0:c0 + width])
        if mul != 1.0:
            acc = acc * mul
        o_ref[...] = acc.astype(o_ref.dtype)
        c0 += width


def _inproj(x2d, mod, w_perm, seq):
    t, d = x2d.shape
    tm = ROW_TILE
    tiles_per_seq = seq // tm
    n = w_perm.shape[1]
    return pl.pallas_call(
        _inproj_kernel,
        out_shape=[jax.ShapeDtypeStruct((t, width), BF16) for width, _ in _PROJ_GROUPS],
        grid=(t // tm,),
        in_specs=[
            pl.BlockSpec((tm, d), lambda i: (i, 0)),
            pl.BlockSpec((None, SUBLANES, d), lambda i: (i // tiles_per_seq, 0, 0)),
            pl.BlockSpec((d, n), lambda i: (0, 0)),
        ],
        out_specs=[pl.BlockSpec((tm, width), lambda i: (i, 0)) for width, _ in _PROJ_GROUPS],
        compiler_params=pltpu.CompilerParams(
            dimension_semantics=("arbitrary",), vmem_limit_bytes=VMEM_LIMIT_BYTES),
        name="inproj",
    )(x2d, mod, w_perm)


def _sb_suffix_sums(sp_blocks, umat):
    m = sp_blocks[0].shape[0]
    res = _dot(jnp.concatenate(sp_blocks, axis=0).astype(BF16), umat)
    cs, nearer = [], None
    for c in reversed(range(len(sp_blocks))):
        part = res[c * m:(c + 1) * m]
        loc, tot = part[:, :SB_TK], part[:, SB_TK:]
        cs.append(loc if nearer is None else loc + nearer)
        nearer = tot if nearer is None else nearer + tot
    return cs[::-1], nearer


def _sb_kernel(q_ref, k_ref, v_ref, g_ref, o_ref):
    i = pl.program_id(1)
    pairs = q_ref.shape[1] // LANES
    lane = lax.broadcasted_iota(jnp.int32, (SB_TQ, LANES), 1)
    head0 = lane < SB_HEAD_DIM

    uj = lax.broadcasted_iota(jnp.int32, (SB_TK, 2 * SB_TK), 0)
    us = lax.broadcasted_iota(jnp.int32, (SB_TK, 2 * SB_TK), 1)
    umat = jnp.where((uj >= us) | (us >= SB_TK), 1.0, 0.0).astype(BF16)

    nb = SB_WIN_BLOCKS
    blk_rows, blk_exists = [], []
    for c in range(nb):
        jb = i - (nb - 1 - c)
        blk_exists.append(jb >= 0)
        blk_rows.append(pl.ds(pl.multiple_of(jnp.maximum(jb, 0) * SB_TK, SB_TK), SB_TK))
    row = lax.broadcasted_iota(jnp.int32, (2 * SB_TQ, SB_TK), 0) & (SB_TQ - 1)
    col = lax.broadcasted_iota(jnp.int32, (2 * SB_TQ, SB_TK), 1)
    causal = col < row

    lanes = [slice(p * LANES, (p + 1) * LANES) for p in range(pairs)]
    qs, zs = [], []
    for p in range(pairs):
        q = q_ref[:, lanes[p]]
        zq = jnp.zeros_like(q)
        qs.append(jnp.concatenate([jnp.where(head0, q, zq), jnp.where(head0, zq, q)], axis=0))
        kw = jnp.concatenate([k_ref[r, lanes[p]] for r in blk_rows], axis=0)
        zs.append(_dot_nt(qs[p], kw))
    sps = []
    for p in range(pairs):
        blocks = [_softplus2(zs[p][:, c * SB_TK:(c + 1) * SB_TK]) for c in range(nb)]
        blocks[-1] = jnp.where(causal, blocks[-1], 0.0)
        sps.append(blocks)
    sums = [_sb_suffix_sums(sps[p], umat) for p in range(pairs)]
    ws = []
    for p in range(pairs):
        cs = sums[p][0]
        args = [zs[p][:, c * SB_TK:(c + 1) * SB_TK] - cs[c] for c in range(nb)]
        args[-1] = jnp.where(causal, args[-1], -1e30)
        ws.append(jnp.exp2(jnp.concatenate(args, axis=1)).astype(BF16))
    accs = []
    for p in range(pairs):
        vw = [v_ref[r, lanes[p]] for r in blk_rows]
        vw = [jnp.where(e, v, jnp.zeros_like(v)) for e, v in zip(blk_exists[:-1], vw[:-1])] + vw[-1:]
        accs.append(_dot(ws[p], jnp.concatenate(vw, axis=0)))
    carries = [sums[p][1] for p in range(pairs)]

    def not_done(cs_):
        m = cs_[0]
        for c in cs_[1:]:
            m = jnp.minimum(m, c)
        return (jnp.min(m) < SB_EXIT_BITS).astype(jnp.int32)

    def cond(s):
        return jnp.logical_and(s[0] >= 0, s[1] > 0)

    def body(s):
        j, _, cs_, as_ = s
        rows = pl.ds(pl.multiple_of(j * SB_TK, SB_TK), SB_TK)
        zb = [_dot_nt(qs[p], k_ref[rows, lanes[p]]) for p in range(pairs)]
        sb = [_sb_suffix_sums([_softplus2(zb[p])], umat) for p in range(pairs)]
        wb = [jnp.exp2(zb[p] - sb[p][0][0] - cs_[p]).astype(BF16) for p in range(pairs)]
        new_a = [as_[p] + _dot(wb[p], v_ref[rows, lanes[p]]) for p in range(pairs)]
        new_c = [cs_[p] + sb[p][1] for p in range(pairs)]
        return j - 1, not_done(new_c), tuple(new_c), tuple(new_a)

    def write_out(accs_):
        outs = []
        for p in range(pairs):
            o = jnp.where(head0, accs_[p][:SB_TQ], accs_[p][SB_TQ:])
            sq = o * o
            ss0 = jnp.sum(jnp.where(head0, sq, 0.0), axis=-1, keepdims=True)
            ss1 = jnp.sum(jnp.where(head0, 0.0, sq), axis=-1, keepdims=True)
            ms = jnp.where(head0, ss0, ss1) * (1.0 / SB_HEAD_DIM)
            outs.append(o * lax.rsqrt(ms + RMS_EPS))
        o_ref[...] = (jnp.concatenate(outs, axis=1) * g_ref[...]).astype(o_ref.dtype)

    write_out(accs)
    more = not_done(carries)

    @pl.when(jnp.logical_and(i - nb >= 0, more > 0))
    def _():
        _, _, _, accs_ = lax.while_loop(cond, body, (i - nb, more, tuple(carries), tuple(accs)))
        write_out(accs_)


def _sb_attention(q, k, v, g):
    b, s, wdt = q.shape
    return pl.pallas_call(
        _sb_kernel,
        out_shape=jax.ShapeDtypeStruct((b, s, wdt), BF16),
        grid=(b, s // SB_TQ),
        in_specs=[
            pl.BlockSpec((None, SB_TQ, wdt), lambda bi, i: (bi, i, 0)),
            pl.BlockSpec((None, s, wdt), lambda bi, i: (bi, 0, 0)),
            pl.BlockSpec((None, s, wdt), lambda bi, i: (bi, 0, 0)),
            pl.BlockSpec((1, wdt), lambda bi, i: (0, 0)),
        ],
        out_specs=pl.BlockSpec((None, SB_TQ, wdt), lambda bi, i: (bi, i, 0)),
        compiler_params=pltpu.CompilerParams(
            dimension_semantics=("arbitrary", "arbitrary"),
            vmem_limit_bytes=VMEM_LIMIT_BYTES),
        name="sb_attention",
    )(q, k, v, g)


def _gla_kernel(q_ref, k_ref, v_ref, r_ref, lr_ref, wg_ref, bg_ref, g_ref, o_ref, st_ref):
    c = GLA_CHUNK
    dv2 = 2 * GLA_VAL_DIM

    @pl.when(pl.program_id(2) == 0)
    def _():
        st_ref[...] = jnp.zeros_like(st_ref)

    u = _dot(lr_ref[...], wg_ref[...]) + bg_ref[...]
    log_a = (jnp.minimum(u, 0.0) - jnp.log(1.0 + jnp.exp(-jnp.abs(u)))) * (1.0 / GLA_GATE_TEMP)

    ti = lax.broadcasted_iota(jnp.int32, (c, c), 0)
    tj = lax.broadcasted_iota(jnp.int32, (c, c), 1)
    lower = ti >= tj
    tril = jnp.where(lower, 1.0, 0.0).astype(BF16)
    lane = lax.broadcasted_iota(jnp.int32, (c, LANES), 1)
    head0 = lane < GLA_KEY_DIM
    se = lax.broadcasted_iota(jnp.int32, (dv2, LANES), 0)
    sd = lax.broadcasted_iota(jnp.int32, (dv2, LANES), 1)
    st_mask = (se < GLA_VAL_DIM) == (sd < GLA_KEY_DIM)

    chunks = [slice(n * c, (n + 1) * c) for n in range(GLA_STEP // c)]
    la_hi, la_lo = _split_bf16(log_a)
    la_parts = jnp.concatenate([la_hi, la_lo], axis=1)
    bcum, dec = [], []
    for rows in chunks:
        res = _dot(tril, la_parts[rows])
        bcum.append(res[:, :LANES] + res[:, LANES:])
        dec.append(jnp.exp(bcum[-1][c - 1:c, :]))
    q_dec, k_inv, k_end = [], [], []
    for n, rows in enumerate(chunks):
        kf = k_ref[rows, :].astype(F32) * jnp.exp(-bcum[n])
        q_dec.append((q_ref[rows, :].astype(F32) * jnp.exp(bcum[n]) * (GLA_KEY_DIM ** -0.5)).astype(BF16))
        k_inv.append(kf.astype(BF16))
        k_end.append((kf * dec[n]).astype(BF16))
    lower2 = jnp.concatenate([lower, lower], axis=0)
    o_intra = []
    for n, rows in enumerate(chunks):
        zq = jnp.zeros_like(q_dec[n])
        q2 = jnp.concatenate([jnp.where(head0, q_dec[n], zq), jnp.where(head0, zq, q_dec[n])], axis=0)
        a = jnp.where(lower2, _dot_nt(q2, k_inv[n]), 0.0).astype(BF16)
        o_intra.append(jnp.concatenate(
            [_dot(a[:c], v_ref[rows, :GLA_VAL_DIM]), _dot(a[c:], v_ref[rows, GLA_VAL_DIM:])], axis=1))
    kv = [jnp.where(st_mask, _dot_tn(v_ref[rows, :], k_end[n]), 0.0) for n, rows in enumerate(chunks)]
    st = st_ref[...]
    starts = []
    for n in range(len(chunks)):
        starts.append(st.astype(BF16))
        st = st * dec[n] + kv[n]
    st_ref[...] = st
    for n, rows in enumerate(chunks):
        o = o_intra[n] + _dot_nt(q_dec[n], starts[n])
        o0 = o[:, :GLA_VAL_DIM]
        o1 = o[:, GLA_VAL_DIM:]
        n0 = o0 * lax.rsqrt(jnp.mean(o0 * o0, axis=-1, keepdims=True) + RMS_EPS)
        n1 = o1 * lax.rsqrt(jnp.mean(o1 * o1, axis=-1, keepdims=True) + RMS_EPS)
        rf = r_ref[rows, :].astype(F32)
        gate = rf / (1.0 + jnp.exp(-rf))
        o_ref[rows, :] = (jnp.concatenate([n0, n1], axis=1) * g_ref[...] * gate).astype(o_ref.dtype)


def _gla(q, k, v, r, lr, wg, bg, g):
    b, s, _ = q.shape
    pairs = GLA_HEADS // 2
    step = GLA_STEP
    dv2 = 2 * GLA_VAL_DIM
    seq_map = lambda bi, p, t: (bi, t, p)
    return pl.pallas_call(
        _gla_kernel,
        out_shape=jax.ShapeDtypeStruct((b, s, GLA_V_WIDTH), BF16),
        grid=(b, pairs, s // step),
        in_specs=[
            pl.BlockSpec((None, step, LANES), seq_map),
            pl.BlockSpec((None, step, LANES), seq_map),
            pl.BlockSpec((None, step, dv2), seq_map),
            pl.BlockSpec((None, step, dv2), seq_map),
            pl.BlockSpec((None, step, LANES), lambda bi, p, t: (bi, t, 0)),
            pl.BlockSpec((LANES, LANES), lambda bi, p, t: (0, p)),
            pl.BlockSpec((1, LANES), lambda bi, p, t: (0, p)),
            pl.BlockSpec((1, dv2), lambda bi, p, t: (0, p)),
        ],
        out_specs=pl.BlockSpec((None, step, dv2), seq_map),
        scratch_shapes=[pltpu.VMEM((dv2, LANES), F32)],
        compiler_params=pltpu.CompilerParams(
            dimension_semantics=("arbitrary", "arbitrary", "arbitrary"),
            vmem_limit_bytes=VMEM_LIMIT_BYTES),
        name="gla",
    )(q, k, v, r, lr, wg, bg, g)


def _outproj_kernel(alpha, osb_ref, ogla_ref, x_ref, mod_ref, wa_ref, wb_ref, g_ref, b_ref,
                    x1_ref, h2_ref):
    y = _dot(osb_ref[...], wa_ref[...]) + _dot(ogla_ref[...], wb_ref[...])
    gate_a = mod_ref[2:3, :]
    x1 = _layer_norm(alpha * x_ref[...] + (1.0 + gate_a) * y, g_ref[...], b_ref[...])
    x1_ref[...] = x1
    h2_ref[...] = (x1 * (1.0 + mod_ref[4:5, :]) + mod_ref[3:4, :]).astype(h2_ref.dtype)


def _outproj(o_sb, o_gla, x2d, mod, w_a, w_b, ln_g, ln_b, seq, alpha):
    t, d = x2d.shape
    tm = ROW_TILE
    tiles_per_seq = seq // tm
    row_map = lambda i: (i, 0)
    fixed = lambda i: (0, 0)
    return pl.pallas_call(
        functools.partial(_outproj_kernel, alpha),
        out_shape=[jax.ShapeDtypeStruct((t, d), F32), jax.ShapeDtypeStruct((t, d), BF16)],
        grid=(t // tm,),
        in_specs=[
            pl.BlockSpec((tm, o_sb.shape[1]), row_map),
            pl.BlockSpec((tm, o_gla.shape[1]), row_map),
            pl.BlockSpec((tm, d), row_map),
            pl.BlockSpec((None, SUBLANES, d), lambda i: (i // tiles_per_seq, 0, 0)),
            pl.BlockSpec(w_a.shape, fixed),
            pl.BlockSpec(w_b.shape, fixed),
            pl.BlockSpec((1, d), fixed),
            pl.BlockSpec((1, d), fixed),
        ],
        out_specs=[pl.BlockSpec((tm, d), row_map), pl.BlockSpec((tm, d), row_map)],
        compiler_params=pltpu.CompilerParams(
            dimension_semantics=("arbitrary",), vmem_limit_bytes=VMEM_LIMIT_BYTES),
        name="outproj",
    )(o_sb, o_gla, x2d, mod, w_a, w_b, ln_g, ln_b)


def _ffn_kernel(alpha, tiles_per_seq, h_ref, x1_ref, mod_ref, wg_ref, wu_ref, cw_ref, cb_ref,
                wd_ref, g_ref, b_ref, o_ref, halo_ref, act_ref):
    tm = h_ref.shape[0]
    d_ff = wg_ref.shape[1]
    first = (pl.program_id(0) % tiles_per_seq) == 0
    h = h_ref[...]
    top = lax.broadcasted_iota(jnp.int32, (SUBLANES, FF_TILE), 0)

    @pl.when(first)
    def _():
        halo_ref[...] = jnp.zeros_like(halo_ref)

    for f in range(d_ff // FF_TILE):
        cols = slice(f * FF_TILE, (f + 1) * FF_TILE)
        gt = _dot(h, wg_ref[:, cols])
        up = _dot(h, wu_ref[:, cols])
        prev = halo_ref[:, cols]
        halo_ref[:, cols] = gt[tm - SUBLANES:, :]
        r1 = pltpu.roll(gt, 1, 0)
        r2 = pltpu.roll(gt, 2, 0)
        t1 = jnp.where(top < 1, pltpu.roll(prev, 1, 0), r1[:SUBLANES])
        t2 = jnp.where(top < 2, pltpu.roll(prev, 2, 0), r2[:SUBLANES])
        g1 = jnp.concatenate([t1, r1[SUBLANES:]], axis=0)
        g2 = jnp.concatenate([t2, r2[SUBLANES:]], axis=0)
        conv = g2 * cw_ref[0:1, cols] + g1 * cw_ref[1:2, cols] + gt * cw_ref[2:3, cols] + cb_ref[:, cols]
        act_ref[:, cols] = (conv / (1.0 + jnp.exp(-conv)) * up).astype(BF16)
    y = _dot(act_ref[...], wd_ref[...])
    gate_f = mod_ref[5:6, :]
    o_ref[...] = _layer_norm(alpha * x1_ref[...] + (1.0 + gate_f) * y, g_ref[...], b_ref[...])


def _ffn(h2, x1, mod, wg, wu, cw, cb, wd, ln_g, ln_b, seq, alpha):
    t, d = x1.shape
    d_ff = wg.shape[1]
    tm = ROW_TILE
    tiles_per_seq = seq // tm
    row_map = lambda i: (i, 0)
    fixed = lambda i: (0, 0)
    return pl.pallas_call(
        functools.partial(_ffn_kernel, alpha, tiles_per_seq),
        out_shape=jax.ShapeDtypeStruct((t, d), F32),
        grid=(t // tm,),
        in_specs=[
            pl.BlockSpec((tm, d), row_map),
            pl.BlockSpec((tm, d), row_map),
            pl.BlockSpec((None, SUBLANES, d), lambda i: (i // tiles_per_seq, 0, 0)),
            pl.BlockSpec(wg.shape, fixed),
            pl.BlockSpec(wu.shape, fixed),
            pl.BlockSpec(cw.shape, fixed),
            pl.BlockSpec(cb.shape, fixed),
            pl.BlockSpec(wd.shape, fixed),
            pl.BlockSpec((1, d), fixed),
            pl.BlockSpec((1, d), fixed),
        ],
        out_specs=pl.BlockSpec((tm, d), row_map),
        scratch_shapes=[pltpu.VMEM((SUBLANES, d_ff), F32), pltpu.VMEM((tm, d_ff), BF16)],
        compiler_params=pltpu.CompilerParams(
            dimension_semantics=("arbitrary",), vmem_limit_bytes=VMEM_LIMIT_BYTES),
        name="ffn",
    )(h2, x1, mod, wg, wu, cw, cb, wd, ln_g, ln_b)


def _permute_w_in(w_in):
    d = w_in.shape[0]
    splits = (SB_WIDTH, SB_WIDTH, SB_WIDTH, GLA_QK_WIDTH, GLA_QK_WIDTH, GLA_V_WIDTH,
              GLA_GATE_RANK, GLA_V_WIDTH)
    parts = []
    c0 = 0
    for wdt in splits:
        parts.append(w_in[:, c0:c0 + wdt])
        c0 += wdt
    sb_q, sb_k, sb_v, g_q, g_k, g_v, g_lr, g_r = parts
    g_lr = jnp.concatenate([g_lr, jnp.zeros((d, LANES - GLA_GATE_RANK), w_in.dtype)], axis=1)
    return jnp.concatenate([sb_q, sb_k, sb_v, g_q, g_k, g_v, g_r, g_lr], axis=1).astype(BF16)


def kernel(x, c, w_ada, b_ada, w_in, gla_w_gate, gla_b_gate, sb_norm_g, gla_norm_g, w_out,
           ln1_g, ln1_b, w_ff_gate, w_ff_up, conv_w, conv_b, w_down, ln2_g, ln2_b):
    bsz, seq, d = x.shape
    depth = w_ada.shape[0]
    alpha = float((2 * depth) ** 0.25)
    t = bsz * seq

    c_pad = jnp.concatenate([c, jnp.zeros((SUBLANES - bsz, d), c.dtype)], axis=0)
    xcur = x.reshape(t, d)
    for l in range(depth):
        mod = _ada(c_pad, w_ada[l], b_ada[l][None, :])
        mod = mod[:bsz].reshape(bsz, N_MOD, d)
        mod = jnp.concatenate([mod, jnp.zeros((bsz, SUBLANES - N_MOD, d), F32)], axis=1)

        sb_q, sb_k, sb_v, g_q, g_k, g_v, g_r, g_lr = _inproj(xcur, mod, _permute_w_in(w_in[l]), seq)
        as_seq = lambda a: a.reshape(bsz, seq, a.shape[-1])

        o_sb = _sb_attention(as_seq(sb_q), as_seq(sb_k), as_seq(sb_v), sb_norm_g[l][None, :])
        wg_pad = jnp.concatenate(
            [gla_w_gate[l], jnp.zeros((LANES - GLA_GATE_RANK, GLA_QK_WIDTH), F32)], axis=0).astype(BF16)
        o_gla = _gla(as_seq(g_q), as_seq(g_k), as_seq(g_v), as_seq(g_r), as_seq(g_lr),
                     wg_pad, gla_b_gate[l][None, :], gla_norm_g[l][None, :])

        w_o = w_out[l].astype(BF16)
        x1, h2 = _outproj(o_sb.reshape(t, SB_WIDTH), o_gla.reshape(t, GLA_V_WIDTH), xcur, mod,
                          w_o[:SB_WIDTH], w_o[SB_WIDTH:], ln1_g[l][None, :], ln1_b[l][None, :],
                          seq, alpha)
        xcur = _ffn(h2, x1, mod, w_ff_gate[l].astype(BF16), w_ff_up[l].astype(BF16),
                    conv_w[l], conv_b[l][None, :], w_down[l].astype(BF16),
                    ln2_g[l][None, :], ln2_b[l][None, :], seq, alpha)
    return xcur.reshape(bsz, seq, d)
```

```python
import functools

import jax
import jax.numpy as jnp
from jax import lax
from jax.experimental import pallas as pl
from jax.experimental.pallas import tpu as pltpu

F32 = jnp.float32
BF16 = jnp.bfloat16

LANES = 128
SUBLANES = 8
VMEM_LIMIT_BYTES = 56 * 1024 * 1024

SB_HEADS = 8
SB_HEAD_DIM = 64
GLA_HEADS = 4
GLA_KEY_DIM = 64
GLA_VAL_DIM = 128
GLA_GATE_RANK = 16
GLA_GATE_TEMP = 16.0
GLA_CHUNK = 64
CONV_WIDTH = 3
N_MOD = 6
LN_EPS = 1e-5
RMS_EPS = 1e-6

SB_WIDTH = SB_HEADS * SB_HEAD_DIM
GLA_QK_WIDTH = GLA_HEADS * GLA_KEY_DIM
GLA_V_WIDTH = GLA_HEADS * GLA_VAL_DIM

SB_EXIT = 104.0
LOG2E = 1.4426950408889634
SB_EXIT_BITS = SB_EXIT * LOG2E
SB_TQ = 128
SB_TK = 128
SB_WIN_BLOCKS = 3
SB_WIN = SB_WIN_BLOCKS * SB_TK

GLA_STEP = 1024
ROW_TILE = 512
FF_TILE = 256


def _softplus2(z):
    return jnp.maximum(jnp.log2(1.0 + jnp.exp2(jnp.minimum(z, 126.0))), z)


def _split_bf16(a):
    hi = a.astype(BF16)
    lo = (a - hi.astype(F32)).astype(BF16)
    return hi, lo


def _dot(a, b):
    return jnp.dot(a, b, preferred_element_type=F32)


def _dot_nt(a, b):
    return lax.dot_general(a, b, (((1,), (1,)), ((), ())), preferred_element_type=F32)


def _dot_tn(a, b):
    return lax.dot_general(a, b, (((0,), (0,)), ((), ())), preferred_element_type=F32)


def _layer_norm(v, g, b):
    mu = jnp.mean(v, axis=-1, keepdims=True)
    vc = v - mu
    var = jnp.mean(vc * vc, axis=-1, keepdims=True)
    return vc * lax.rsqrt(var + LN_EPS) * g + b


def _ada_kernel(c_ref, w_ref, b_ref, o_ref):
    c = c_ref[...]
    ca = c / (1.0 + jnp.exp(-c))
    ch, cl = _split_bf16(ca)
    wh, wl = _split_bf16(w_ref[...])
    o_ref[...] = _dot(ch, wh) + _dot(cl, wh) + _dot(ch, wl) + b_ref[...]


def _ada(c_pad, w, b):
    d, n = w.shape
    tn = 1024
    return pl.pallas_call(
        _ada_kernel,
        out_shape=jax.ShapeDtypeStruct((c_pad.shape[0], n), F32),
        grid=(n // tn,),
        in_specs=[
            pl.BlockSpec((c_pad.shape[0], d), lambda j: (0, 0)),
            pl.BlockSpec((d, tn), lambda j: (0, j)),
            pl.BlockSpec((1, tn), lambda j: (0, j)),
        ],
        out_specs=pl.BlockSpec((c_pad.shape[0], tn), lambda j: (0, j)),
        compiler_params=pltpu.CompilerParams(
            dimension_semantics=("arbitrary",), vmem_limit_bytes=VMEM_LIMIT_BYTES),
        name="ada",
    )(c_pad, w, b)


_MAIN_WIDTH = 3 * SB_WIDTH + 2 * GLA_QK_WIDTH + GLA_V_WIDTH
_PROJ_GROUPS = (
    (0, 0, SB_WIDTH, SB_HEAD_DIM ** -0.5 * LOG2E),
    (0, SB_WIDTH, SB_WIDTH, 1.0),
    (0, 2 * SB_WIDTH, SB_WIDTH, 1.0),
    (0, 3 * SB_WIDTH, GLA_QK_WIDTH, 1.0),
    (0, 3 * SB_WIDTH + GLA_QK_WIDTH, GLA_QK_WIDTH, 1.0),
    (0, 3 * SB_WIDTH + 2 * GLA_QK_WIDTH, GLA_V_WIDTH, 1.0),
    (1, 0, GLA_V_WIDTH, 1.0),
    (2, 0, LANES, 1.0),
)


def _inproj_kernel(x_ref, mod_ref, w_main_ref, w_r_ref, w_lr_ref, *out_refs):
    shift = mod_ref[0:1, :]
    scale = mod_ref[1:2, :]
    h = (x_ref[...] * (1.0 + scale) + shift).astype(BF16)
    w_refs = (w_main_ref, w_r_ref, w_lr_ref)
    for (src, c0, width, mul), o_ref in zip(_PROJ_GROUPS, out_refs):
        acc = _dot(h, w_refs[src][:, c0:c0 + width])
        if mul != 1.0:
            acc = acc * mul
        o_ref[...] = acc.astype(o_ref.dtype)


def _inproj(x2d, mod, w_main, w_r, w_lr, seq):
    t, d = x2d.shape
    tm = ROW_TILE
    tiles_per_seq = seq // tm
    fixed = lambda i: (0, 0)
    widths = [g[2] for g in _PROJ_GROUPS]
    return pl.pallas_call(
        _inproj_kernel,
        out_shape=[jax.ShapeDtypeStruct((t, width), BF16) for width in widths],
        grid=(t // tm,),
        in_specs=[
            pl.BlockSpec((tm, d), lambda i: (i, 0)),
            pl.BlockSpec((None, SUBLANES, d), lambda i: (i // tiles_per_seq, 0, 0)),
            pl.BlockSpec(w_main.shape, fixed),
            pl.BlockSpec(w_r.shape, fixed),
            pl.BlockSpec(w_lr.shape, fixed),
        ],
        out_specs=[pl.BlockSpec((tm, width), lambda i: (i, 0)) for width in widths],
        compiler_params=pltpu.CompilerParams(
            dimension_semantics=("arbitrary",), vmem_limit_bytes=VMEM_LIMIT_BYTES),
        name="inproj",
    )(x2d, mod, w_main, w_r, w_lr)


def _sb_suffix_sums(sp_blocks, umat):
    m = sp_blocks[0].shape[0]
    res = _dot(jnp.concatenate(sp_blocks, axis=0).astype(BF16), umat)
    cs, nearer = [], None
    for c in reversed(range(len(sp_blocks))):
        part = res[c * m:(c + 1) * m]
        loc, tot = part[:, :SB_TK], part[:, SB_TK:]
        cs.append(loc if nearer is None else loc + nearer)
        nearer = tot if nearer is None else nearer + tot
    return cs[::-1], nearer


def _sb_kernel(q_ref, k_ref, v_ref, g_ref, o_ref):
    i = pl.program_id(1)
    pairs = q_ref.shape[1] // LANES
    lane = lax.broadcasted_iota(jnp.int32, (SB_TQ, LANES), 1)
    head0 = lane < SB_HEAD_DIM

    uj = lax.broadcasted_iota(jnp.int32, (SB_TK, 2 * SB_TK), 0)
    us = lax.broadcasted_iota(jnp.int32, (SB_TK, 2 * SB_TK), 1)
    umat = jnp.where((uj >= us) | (us >= SB_TK), 1.0, 0.0).astype(BF16)

    nb = SB_WIN_BLOCKS
    blk_rows, blk_exists = [], []
    for c in range(nb):
        jb = i - (nb - 1 - c)
        blk_exists.append(jb >= 0)
        blk_rows.append(pl.ds(pl.multiple_of(jnp.maximum(jb, 0) * SB_TK, SB_TK), SB_TK))
    row = lax.broadcasted_iota(jnp.int32, (2 * SB_TQ, SB_TK), 0) & (SB_TQ - 1)
    col = lax.broadcasted_iota(jnp.int32, (2 * SB_TQ, SB_TK), 1)
    causal = col < row

    lanes = [slice(p * LANES, (p + 1) * LANES) for p in range(pairs)]
    qs, zs = [], []
    for p in range(pairs):
        q = q_ref[:, lanes[p]]
        zq = jnp.zeros_like(q)
        qs.append(jnp.concatenate([jnp.where(head0, q, zq), jnp.where(head0, zq, q)], axis=0))
        kw = jnp.concatenate([k_ref[r, lanes[p]] for r in blk_rows], axis=0)
        zs.append(_dot_nt(qs[p], kw))
    sps = []
    for p in range(pairs):
        blocks = [_softplus2(zs[p][:, c * SB_TK:(c + 1) * SB_TK]) for c in range(nb)]
        blocks[-1] = jnp.where(causal, blocks[-1], 0.0)
        sps.append(blocks)
    sums = [_sb_suffix_sums(sps[p], umat) for p in range(pairs)]
    ws = []
    for p in range(pairs):
        cs = sums[p][0]
        args = [zs[p][:, c * SB_TK:(c + 1) * SB_TK] - cs[c] for c in range(nb)]
        args[-1] = jnp.where(causal, args[-1], -1e30)
        ws.append(jnp.exp2(jnp.concatenate(args, axis=1)).astype(BF16))
    accs = []
    for p in range(pairs):
        vw = [v_ref[r, lanes[p]] for r in blk_rows]
        vw = [jnp.where(e, v, jnp.zeros_like(v)) for e, v in zip(blk_exists[:-1], vw[:-1])] + vw[-1:]
        accs.append(_dot(ws[p], jnp.concatenate(vw, axis=0)))
    carries = [sums[p][1] for p in range(pairs)]

    def not_done(cs_):
        m = cs_[0]
        for c in cs_[1:]:
            m = jnp.minimum(m, c)
        return (jnp.min(m) < SB_EXIT_BITS).astype(jnp.int32)

    def cond(s):
        return jnp.logical_and(s[0] >= 0, s[1] > 0)

    def body(s):
        j, _, cs_, as_ = s
        rows = pl.ds(pl.multiple_of(j * SB_TK, SB_TK), SB_TK)
        zb = [_dot_nt(qs[p], k_ref[rows, lanes[p]]) for p in range(pairs)]
        sb = [_sb_suffix_sums([_softplus2(zb[p])], umat) for p in range(pairs)]
        wb = [jnp.exp2(zb[p] - sb[p][0][0] - cs_[p]).astype(BF16) for p in range(pairs)]
        new_a = [as_[p] + _dot(wb[p], v_ref[rows, lanes[p]]) for p in range(pairs)]
        new_c = [cs_[p] + sb[p][1] for p in range(pairs)]
        return j - 1, not_done(new_c), tuple(new_c), tuple(new_a)

    def write_out(accs_):
        outs = []
        for p in range(pairs):
            o = jnp.where(head0, accs_[p][:SB_TQ], accs_[p][SB_TQ:])
            sq = o * o
            ss0 = jnp.sum(jnp.where(head0, sq, 0.0), axis=-1, keepdims=True)
            ss1 = jnp.sum(jnp.where(head0, 0.0, sq), axis=-1, keepdims=True)
            ms = jnp.where(head0, ss0, ss1) * (1.0 / SB_HEAD_DIM)
            outs.append(o * lax.rsqrt(ms + RMS_EPS))
        o_ref[...] = (jnp.concatenate(outs, axis=1) * g_ref[...]).astype(o_ref.dtype)

    write_out(accs)
    more = not_done(carries)

    @pl.when(jnp.logical_and(i - nb >= 0, more > 0))
    def _():
        _, _, _, accs_ = lax.while_loop(cond, body, (i - nb, more, tuple(carries), tuple(accs)))
        write_out(accs_)


def _sb_attention(q, k, v, g):
    b, s, wdt = q.shape
    return pl.pallas_call(
        _sb_kernel,
        out_shape=jax.ShapeDtypeStruct((b, s, wdt), BF16),
        grid=(b, s // SB_TQ),
        in_specs=[
            pl.BlockSpec((None, SB_TQ, wdt), lambda bi, i: (bi, i, 0)),
            pl.BlockSpec((None, s, wdt), lambda bi, i: (bi, 0, 0)),
            pl.BlockSpec((None, s, wdt), lambda bi, i: (bi, 0, 0)),
            pl.BlockSpec((1, wdt), lambda bi, i: (0, 0)),
        ],
        out_specs=pl.BlockSpec((None, SB_TQ, wdt), lambda bi, i: (bi, i, 0)),
        compiler_params=pltpu.CompilerParams(
            dimension_semantics=("arbitrary", "arbitrary"),
            vmem_limit_bytes=VMEM_LIMIT_BYTES),
        name="sb_attention",
    )(q, k, v, g)


def _gla_kernel(q_ref, k_ref, v_ref, r_ref, lr_ref, wg_ref, bg_ref, g_ref, o_ref, st_ref):
    c = GLA_CHUNK
    dv2 = 2 * GLA_VAL_DIM

    @pl.when(pl.program_id(2) == 0)
    def _():
        st_ref[...] = jnp.zeros_like(st_ref)

    u = _dot(lr_ref[...], wg_ref[...]) + bg_ref[...]
    log_a = (jnp.minimum(u, 0.0) - jnp.log(1.0 + jnp.exp(-jnp.abs(u)))) * (1.0 / GLA_GATE_TEMP)

    ti = lax.broadcasted_iota(jnp.int32, (c, c), 0)
    tj = lax.broadcasted_iota(jnp.int32, (c, c), 1)
    lower = ti >= tj
    tril = jnp.where(lower, 1.0, 0.0).astype(BF16)
    lane = lax.broadcasted_iota(jnp.int32, (c, LANES), 1)
    head0 = lane < GLA_KEY_DIM
    se = lax.broadcasted_iota(jnp.int32, (dv2, LANES), 0)
    sd = lax.broadcasted_iota(jnp.int32, (dv2, LANES), 1)
    st_mask = (se < GLA_VAL_DIM) == (sd < GLA_KEY_DIM)

    chunks = [slice(n * c, (n + 1) * c) for n in range(GLA_STEP // c)]
    la_hi, la_lo = _split_bf16(log_a)
    la_parts = jnp.concatenate([la_hi, la_lo], axis=1)
    bcum, dec = [], []
    for rows in chunks:
        res = _dot(tril, la_parts[rows])
        bcum.append(res[:, :LANES] + res[:, LANES:])
        dec.append(jnp.exp(bcum[-1][c - 1:c, :]))
    q_dec, k_inv, k_end = [], [], []
    for n, rows in enumerate(chunks):
        kf = k_ref[rows, :].astype(F32) * jnp.exp(-bcum[n])
        q_dec.append((q_ref[rows, :].astype(F32) * jnp.exp(bcum[n]) * (GLA_KEY_DIM ** -0.5)).astype(BF16))
        k_inv.append(kf.astype(BF16))
        k_end.append((kf * dec[n]).astype(BF16))
    lower2 = jnp.concatenate([lower, lower], axis=0)
    o_intra = []
    for n, rows in enumerate(chunks):
        zq = jnp.zeros_like(q_dec[n])
        q2 = jnp.concatenate([jnp.where(head0, q_dec[n], zq), jnp.where(head0, zq, q_dec[n])], axis=0)
        a = jnp.where(lower2, _dot_nt(q2, k_inv[n]), 0.0).astype(BF16)
        o_intra.append(jnp.concatenate(
            [_dot(a[:c], v_ref[rows, :GLA_VAL_DIM]), _dot(a[c:], v_ref[rows, GLA_VAL_DIM:])], axis=1))
    kv = [jnp.where(st_mask, _dot_tn(v_ref[rows, :], k_end[n]), 0.0) for n, rows in enumerate(chunks)]
    st = st_ref[...]
    starts = []
    for n in range(len(chunks)):
        starts.append(st.astype(BF16))
        st = st * dec[n] + kv[n]
    st_ref[...] = st
    for n, rows in enumerate(chunks):
        o = o_intra[n] + _dot_nt(q_dec[n], starts[n])
        o0 = o[:, :GLA_VAL_DIM]
        o1 = o[:, GLA_VAL_DIM:]
        n0 = o0 * lax.rsqrt(jnp.mean(o0 * o0, axis=-1, keepdims=True) + RMS_EPS)
        n1 = o1 * lax.rsqrt(jnp.mean(o1 * o1, axis=-1, keepdims=True) + RMS_EPS)
        rf = r_ref[rows, :].astype(F32)
        gate = rf / (1.0 + jnp.exp(-rf))
        o_ref[rows, :] = (jnp.concatenate([n0, n1], axis=1) * g_ref[...] * gate).astype(o_ref.dtype)


def _gla(q, k, v, r, lr, wg, bg, g):
    b, s, _ = q.shape
    pairs = GLA_HEADS // 2
    step = GLA_STEP
    dv2 = 2 * GLA_VAL_DIM
    seq_map = lambda bi, p, t: (bi, t, p)
    return pl.pallas_call(
        _gla_kernel,
        out_shape=jax.ShapeDtypeStruct((b, s, GLA_V_WIDTH), BF16),
        grid=(b, pairs, s // step),
        in_specs=[
            pl.BlockSpec((None, step, LANES), seq_map),
            pl.BlockSpec((None, step, LANES), seq_map),
            pl.BlockSpec((None, step, dv2), seq_map),
            pl.BlockSpec((None, step, dv2), seq_map),
            pl.BlockSpec((None, step, LANES), lambda bi, p, t: (bi, t, 0)),
            pl.BlockSpec((LANES, LANES), lambda bi, p, t: (0, p)),
            pl.BlockSpec((1, LANES), lambda bi, p, t: (0, p)),
            pl.BlockSpec((1, dv2), lambda bi, p, t: (0, p)),
        ],
        out_specs=pl.BlockSpec((None, step, dv2), seq_map),
        scratch_shapes=[pltpu.VMEM((dv2, LANES), F32)],
        compiler_params=pltpu.CompilerParams(
            dimension_semantics=("arbitrary", "arbitrary", "arbitrary"),
            vmem_limit_bytes=VMEM_LIMIT_BYTES),
        name="gla",
    )(q, k, v, r, lr, wg, bg, g)


def _post_kernel(alpha, tiles_per_seq, osb_ref, ogla_ref, x_ref, mod_ref, wa_ref, wb_ref,
                 g1_ref, b1_ref, wg_ref, wu_ref, cw_ref, cb_ref, wd_ref, g_ref, b_ref,
                 o_ref, halo_ref, act_ref, x1_ref):
    tm = x_ref.shape[0]
    d_ff = wg_ref.shape[1]
    first = (pl.program_id(0) % tiles_per_seq) == 0
    mix = _dot(osb_ref[...], wa_ref[...]) + _dot(ogla_ref[...], wb_ref[...])
    x1_ref[...] = _layer_norm(alpha * x_ref[...] + (1.0 + mod_ref[2:3, :]) * mix,
                              g1_ref[...], b1_ref[...])
    h = (x1_ref[...] * (1.0 + mod_ref[4:5, :]) + mod_ref[3:4, :]).astype(BF16)
    top = lax.broadcasted_iota(jnp.int32, (SUBLANES, FF_TILE), 0)

    @pl.when(first)
    def _():
        halo_ref[...] = jnp.zeros_like(halo_ref)

    for f in range(d_ff // FF_TILE):
        cols = slice(f * FF_TILE, (f + 1) * FF_TILE)
        gt = _dot(h, wg_ref[:, cols])
        up = _dot(h, wu_ref[:, cols])
        prev = halo_ref[:, cols]
        halo_ref[:, cols] = gt[tm - SUBLANES:, :]
        r1 = pltpu.roll(gt, 1, 0)
        r2 = pltpu.roll(gt, 2, 0)
        t1 = jnp.where(top < 1, pltpu.roll(prev, 1, 0), r1[:SUBLANES])
        t2 = jnp.where(top < 2, pltpu.roll(prev, 2, 0), r2[:SUBLANES])
        g1 = jnp.concatenate([t1, r1[SUBLANES:]], axis=0)
        g2 = jnp.concatenate([t2, r2[SUBLANES:]], axis=0)
        conv = g2 * cw_ref[0:1, cols] + g1 * cw_ref[1:2, cols] + gt * cw_ref[2:3, cols] + cb_ref[:, cols]
        act_ref[:, cols] = (conv / (1.0 + jnp.exp(-conv)) * up).astype(BF16)
    y = _dot(act_ref[...], wd_ref[...])
    gate_f = mod_ref[5:6, :]
    o_ref[...] = _layer_norm(alpha * x1_ref[...] + (1.0 + gate_f) * y, g_ref[...], b_ref[...])


def _post(o_sb, o_gla, x2d, mod, w_a, w_b, ln1_g, ln1_b, wg, wu, cw, cb, wd, ln_g, ln_b, seq, alpha):
    t, d = x2d.shape
    d_ff = wg.shape[1]
    tm = ROW_TILE
    tiles_per_seq = seq // tm
    row_map = lambda i: (i, 0)
    fixed = lambda i: (0, 0)
    return pl.pallas_call(
        functools.partial(_post_kernel, alpha, tiles_per_seq),
        out_shape=jax.ShapeDtypeStruct((t, d), F32),
        grid=(t // tm,),
        in_specs=[
            pl.BlockSpec((tm, o_sb.shape[1]), row_map),
            pl.BlockSpec((tm, o_gla.shape[1]), row_map),
            pl.BlockSpec((tm, d), row_map),
            pl.BlockSpec((None, SUBLANES, d), lambda i: (i // tiles_per_seq, 0, 0)),
            pl.BlockSpec(w_a.shape, fixed),
            pl.BlockSpec(w_b.shape, fixed),
            pl.BlockSpec((1, d), fixed),
            pl.BlockSpec((1, d), fixed),
            pl.BlockSpec(wg.shape, fixed),
            pl.BlockSpec(wu.shape, fixed),
            pl.BlockSpec(cw.shape, fixed),
            pl.BlockSpec(cb.shape, fixed),
            pl.BlockSpec(wd.shape, fixed),
            pl.BlockSpec((1, d), fixed),
            pl.BlockSpec((1, d), fixed),
        ],
        out_specs=pl.BlockSpec((tm, d), row_map),
        scratch_shapes=[pltpu.VMEM((SUBLANES, d_ff), F32), pltpu.VMEM((tm, d_ff), BF16),
                        pltpu.VMEM((tm, d), F32)],
        compiler_params=pltpu.CompilerParams(
            dimension_semantics=("arbitrary",), vmem_limit_bytes=VMEM_LIMIT_BYTES),
        name="post",
    )(o_sb, o_gla, x2d, mod, w_a, w_b, ln1_g, ln1_b, wg, wu, cw, cb, wd, ln_g, ln_b)


def _split_w_in(w_in):
    lr0 = _MAIN_WIDTH
    r0 = lr0 + GLA_GATE_RANK
    w_main = w_in[:, :lr0].astype(BF16)
    w_r = w_in[:, r0:r0 + GLA_V_WIDTH].astype(BF16)
    w_lr = jnp.pad(w_in[:, lr0:r0], ((0, 0), (0, LANES - GLA_GATE_RANK))).astype(BF16)
    return w_main, w_r, w_lr


def kernel(x, c, w_ada, b_ada, w_in, gla_w_gate, gla_b_gate, sb_norm_g, gla_norm_g, w_out,
           ln1_g, ln1_b, w_ff_gate, w_ff_up, conv_w, conv_b, w_down, ln2_g, ln2_b):
    bsz, seq, d = x.shape
    depth = w_ada.shape[0]
    alpha = float((2 * depth) ** 0.25)
    t = bsz * seq

    c_pad = jnp.concatenate([c, jnp.zeros((SUBLANES - bsz, d), c.dtype)], axis=0)
    xcur = x.reshape(t, d)
    for l in range(depth):
        mod = _ada(c_pad, w_ada[l], b_ada[l][None, :])
        mod = mod[:bsz].reshape(bsz, N_MOD, d)
        mod = jnp.concatenate([mod, jnp.zeros((bsz, SUBLANES - N_MOD, d), F32)], axis=1)

        sb_q, sb_k, sb_v, g_q, g_k, g_v, g_r, g_lr = _inproj(xcur, mod, *_split_w_in(w_in[l]), seq)
        as_seq = lambda a: a.reshape(bsz, seq, a.shape[-1])

        o_sb = _sb_attention(as_seq(sb_q), as_seq(sb_k), as_seq(sb_v), sb_norm_g[l][None, :])
        wg_pad = jnp.concatenate(
            [gla_w_gate[l], jnp.zeros((LANES - GLA_GATE_RANK, GLA_QK_WIDTH), F32)], axis=0).astype(BF16)
        o_gla = _gla(as_seq(g_q), as_seq(g_k), as_seq(g_v), as_seq(g_r), as_seq(g_lr),
                     wg_pad, gla_b_gate[l][None, :], gla_norm_g[l][None, :])

        w_o = w_out[l].astype(BF16)
        xcur = _post(o_sb.reshape(t, SB_WIDTH), o_gla.reshape(t, GLA_V_WIDTH), xcur, mod,
                     w_o[:SB_WIDTH], w_o[SB_WIDTH:], ln1_g[l][None, :], ln1_b[l][None, :],
                     w_ff_gate[l].astype(BF16), w_ff_up[l].astype(BF16),
                     conv_w[l], conv_b[l][None, :], w_down[l].astype(BF16),
                     ln2_g[l][None, :], ln2_b[l][None, :], seq, alpha)
    return xcur.reshape(bsz, seq, d)
```

```python
import functools

import jax
import jax.numpy as jnp
from jax import lax
from jax.experimental import pallas as pl
from jax.experimental.pallas import tpu as pltpu

F32 = jnp.float32
BF16 = jnp.bfloat16

LANES = 128
SUBLANES = 8
VMEM_LIMIT_BYTES = 56 * 1024 * 1024

SB_HEADS = 8
SB_HEAD_DIM = 64
GLA_HEADS = 4
GLA_KEY_DIM = 64
GLA_VAL_DIM = 128
GLA_GATE_RANK = 16
GLA_GATE_TEMP = 16.0
GLA_CHUNK = 64
CONV_WIDTH = 3
N_MOD = 6
LN_EPS = 1e-5
RMS_EPS = 1e-6

SB_WIDTH = SB_HEADS * SB_HEAD_DIM
GLA_QK_WIDTH = GLA_HEADS * GLA_KEY_DIM
GLA_V_WIDTH = GLA_HEADS * GLA_VAL_DIM

SB_EXIT = 104.0
LOG2E = 1.4426950408889634
SB_EXIT_BITS = SB_EXIT * LOG2E
SB_TK = 128
SB_TQ = 2 * SB_TK
SB_WIN_BLOCKS = 3
SB_WIN = SB_WIN_BLOCKS * SB_TK

GLA_STEP = 1024
ROW_TILE = 512
FF_TILE = 256


def _softplus2(z):
    return jnp.maximum(jnp.log2(1.0 + jnp.exp2(jnp.minimum(z, 126.0))), z)


def _split_bf16(a):
    hi = a.astype(BF16)
    lo = (a - hi.astype(F32)).astype(BF16)
    return hi, lo


def _dot(a, b):
    return jnp.dot(a, b, preferred_element_type=F32)


def _dot_nt(a, b):
    return lax.dot_general(a, b, (((1,), (1,)), ((), ())), preferred_element_type=F32)


def _dot_tn(a, b):
    return lax.dot_general(a, b, (((0,), (0,)), ((), ())), preferred_element_type=F32)


def _layer_norm(v, g, b):
    mu = jnp.mean(v, axis=-1, keepdims=True)
    vc = v - mu
    var = jnp.mean(vc * vc, axis=-1, keepdims=True)
    return vc * lax.rsqrt(var + LN_EPS) * g + b


def _ada_kernel(c_ref, w_ref, b_ref, o_ref):
    c = c_ref[...]
    ca = c / (1.0 + jnp.exp(-c))
    ch, cl = _split_bf16(ca)
    wh, wl = _split_bf16(w_ref[...])
    o_ref[...] = _dot(ch, wh) + _dot(cl, wh) + _dot(ch, wl) + b_ref[...]


def _ada(c_pad, w, b):
    d, n = w.shape
    tn = 1024
    return pl.pallas_call(
        _ada_kernel,
        out_shape=jax.ShapeDtypeStruct((c_pad.shape[0], n), F32),
        grid=(n // tn,),
        in_specs=[
            pl.BlockSpec((c_pad.shape[0], d), lambda j: (0, 0)),
            pl.BlockSpec((d, tn), lambda j: (0, j)),
            pl.BlockSpec((1, tn), lambda j: (0, j)),
        ],
        out_specs=pl.BlockSpec((c_pad.shape[0], tn), lambda j: (0, j)),
        compiler_params=pltpu.CompilerParams(
            dimension_semantics=("arbitrary",), vmem_limit_bytes=VMEM_LIMIT_BYTES),
        name="ada",
    )(c_pad, w, b)


_MAIN_WIDTH = 3 * SB_WIDTH + 2 * GLA_QK_WIDTH + GLA_V_WIDTH
_PROJ_GROUPS = (
    (0, 0, SB_WIDTH, SB_HEAD_DIM ** -0.5 * LOG2E),
    (0, SB_WIDTH, SB_WIDTH, 1.0),
    (0, 2 * SB_WIDTH, SB_WIDTH, 1.0),
    (0, 3 * SB_WIDTH, GLA_QK_WIDTH, 1.0),
    (0, 3 * SB_WIDTH + GLA_QK_WIDTH, GLA_QK_WIDTH, 1.0),
    (0, 3 * SB_WIDTH + 2 * GLA_QK_WIDTH, GLA_V_WIDTH, 1.0),
    (1, 0, GLA_V_WIDTH, 1.0),
    (2, 0, LANES, 1.0),
)


def _inproj_kernel(x_ref, mod_ref, w_main_ref, w_r_ref, w_lr_ref, *out_refs):
    shift = mod_ref[0:1, :]
    scale = mod_ref[1:2, :]
    h = (x_ref[...] * (1.0 + scale) + shift).astype(BF16)
    w_refs = (w_main_ref, w_r_ref, w_lr_ref)
    for (src, c0, width, mul), o_ref in zip(_PROJ_GROUPS, out_refs):
        acc = _dot(h, w_refs[src][:, c0:c0 + width])
        if mul != 1.0:
            acc = acc * mul
        o_ref[...] = acc.astype(o_ref.dtype)


def _inproj(x2d, mod, w_main, w_r, w_lr, seq):
    t, d = x2d.shape
    tm = ROW_TILE
    tiles_per_seq = seq // tm
    fixed = lambda i: (0, 0)
    widths = [g[2] for g in _PROJ_GROUPS]
    return pl.pallas_call(
        _inproj_kernel,
        out_shape=[jax.ShapeDtypeStruct((t, width), BF16) for width in widths],
        grid=(t // tm,),
        in_specs=[
            pl.BlockSpec((tm, d), lambda i: (i, 0)),
            pl.BlockSpec((None, SUBLANES, d), lambda i: (i // tiles_per_seq, 0, 0)),
            pl.BlockSpec(w_main.shape, fixed),
            pl.BlockSpec(w_r.shape, fixed),
            pl.BlockSpec(w_lr.shape, fixed),
        ],
        out_specs=[pl.BlockSpec((tm, width), lambda i: (i, 0)) for width in widths],
        compiler_params=pltpu.CompilerParams(
            dimension_semantics=("arbitrary",), vmem_limit_bytes=VMEM_LIMIT_BYTES),
        name="inproj",
    )(x2d, mod, w_main, w_r, w_lr)


def _sb_suffix_sums(sp_blocks, umat):
    m = sp_blocks[0].shape[0]
    res = _dot(jnp.concatenate(sp_blocks, axis=0).astype(BF16), umat)
    cs, nearer = [], None
    for c in reversed(range(len(sp_blocks))):
        part = res[c * m:(c + 1) * m]
        loc, tot = part[:, :SB_TK], part[:, SB_TK:]
        cs.append(loc if nearer is None else loc + nearer)
        nearer = tot if nearer is None else nearer + tot
    return cs[::-1], nearer


def _sb_kernel(q_ref, k_ref, v_ref, g_ref, o_ref):
    tk = SB_TK
    pairs = q_ref.shape[1] // LANES
    nrb = SB_TQ // tk
    nb = SB_WIN_BLOCKS
    lane = lax.broadcasted_iota(jnp.int32, (tk, LANES), 1)
    head0 = lane < SB_HEAD_DIM

    uj = lax.broadcasted_iota(jnp.int32, (tk, 2 * tk), 0)
    us = lax.broadcasted_iota(jnp.int32, (tk, 2 * tk), 1)
    umat = jnp.where((uj >= us) | (us >= tk), 1.0, 0.0).astype(BF16)
    row = lax.broadcasted_iota(jnp.int32, (2 * tk, tk), 0) & (tk - 1)
    col = lax.broadcasted_iota(jnp.int32, (2 * tk, tk), 1)
    causal = col < row
    lanes = [slice(p * LANES, (p + 1) * LANES) for p in range(pairs)]

    first_blk = [pl.program_id(1) * nrb + r - (nb - 1) for r in range(nrb)]
    blk_rows = [[pl.ds(pl.multiple_of(jnp.maximum(first_blk[r] + c, 0) * tk, tk), tk)
                 for c in range(nb)] for r in range(nrb)]
    units = [(r, p) for r in range(nrb) for p in range(pairs)]

    qs, zs, sps, ws, accs = {}, {}, {}, {}, {}
    for r, p in units:
        q = q_ref[r * tk:(r + 1) * tk, lanes[p]]
        zq = jnp.zeros_like(q)
        qs[r, p] = jnp.concatenate([jnp.where(head0, q, zq), jnp.where(head0, zq, q)], axis=0)
        kw = jnp.concatenate([k_ref[rows, lanes[p]] for rows in blk_rows[r]], axis=0)
        zs[r, p] = _dot_nt(qs[r, p], kw)
    for u in units:
        blocks = [_softplus2(zs[u][:, c * tk:(c + 1) * tk]) for c in range(nb)]
        blocks[-1] = jnp.where(causal, blocks[-1], 0.0)
        sps[u] = blocks
    sums = {u: _sb_suffix_sums(sps[u], umat) for u in units}
    for u in units:
        cs = sums[u][0]
        args = [zs[u][:, c * tk:(c + 1) * tk] - cs[c] for c in range(nb)]
        args[-1] = jnp.where(causal, args[-1], -1e30)
        ws[u] = jnp.exp2(jnp.concatenate(args, axis=1)).astype(BF16)
    for r, p in units:
        vw = [v_ref[rows, lanes[p]] for rows in blk_rows[r]]
        vw = [jnp.where(first_blk[r] + c >= 0, vw[c], jnp.zeros_like(vw[c]))
              for c in range(nb - 1)] + vw[-1:]
        accs[r, p] = _dot(ws[r, p], jnp.concatenate(vw, axis=0))

    def not_done(cs_):
        m = cs_[0]
        for c in cs_[1:]:
            m = jnp.minimum(m, c)
        return (jnp.min(m) < SB_EXIT_BITS).astype(jnp.int32)

    def cond(s):
        return jnp.logical_and(s[0] >= 0, s[1] > 0)

    def write_out(r, accs_):
        outs = []
        for p in range(pairs):
            o = jnp.where(head0, accs_[p][:tk], accs_[p][tk:])
            sq = o * o
            ss0 = jnp.sum(jnp.where(head0, sq, 0.0), axis=-1, keepdims=True)
            ss1 = jnp.sum(jnp.where(head0, 0.0, sq), axis=-1, keepdims=True)
            ms = jnp.where(head0, ss0, ss1) * (1.0 / SB_HEAD_DIM)
            outs.append(o * lax.rsqrt(ms + RMS_EPS))
        o_ref[r * tk:(r + 1) * tk, :] = (jnp.concatenate(outs, axis=1) * g_ref[...]).astype(o_ref.dtype)

    for r in range(nrb):
        write_out(r, [accs[r, p] for p in range(pairs)])
    for r in range(nrb):
        carries = tuple(sums[r, p][1] for p in range(pairs))
        more = not_done(carries)

        def body(s, r=r):
            j, _, cs_, as_ = s
            rows = pl.ds(pl.multiple_of(j * tk, tk), tk)
            zb = [_dot_nt(qs[r, p], k_ref[rows, lanes[p]]) for p in range(pairs)]
            sb = [_sb_suffix_sums([_softplus2(zb[p])], umat) for p in range(pairs)]
            wb = [jnp.exp2(zb[p] - sb[p][0][0] - cs_[p]).astype(BF16) for p in range(pairs)]
            new_a = [as_[p] + _dot(wb[p], v_ref[rows, lanes[p]]) for p in range(pairs)]
            new_c = [cs_[p] + sb[p][1] for p in range(pairs)]
            return j - 1, not_done(new_c), tuple(new_c), tuple(new_a)

        @pl.when(jnp.logical_and(first_blk[r] - 1 >= 0, more > 0))
        def _(r=r, carries=carries, more=more, body=body):
            _, _, _, accs_ = lax.while_loop(
                cond, body,
                (first_blk[r] - 1, more, carries, tuple(accs[r, p] for p in range(pairs))))
            write_out(r, accs_)


def _sb_attention(q, k, v, g):
    b, s, wdt = q.shape
    return pl.pallas_call(
        _sb_kernel,
        out_shape=jax.ShapeDtypeStruct((b, s, wdt), BF16),
        grid=(b, s // SB_TQ),
        in_specs=[
            pl.BlockSpec((None, SB_TQ, wdt), lambda bi, i: (bi, i, 0)),
            pl.BlockSpec((None, s, wdt), lambda bi, i: (bi, 0, 0)),
            pl.BlockSpec((None, s, wdt), lambda bi, i: (bi, 0, 0)),
            pl.BlockSpec((1, wdt), lambda bi, i: (0, 0)),
        ],
        out_specs=pl.BlockSpec((None, SB_TQ, wdt), lambda bi, i: (bi, i, 0)),
        compiler_params=pltpu.CompilerParams(
            dimension_semantics=("arbitrary", "arbitrary"),
            vmem_limit_bytes=VMEM_LIMIT_BYTES),
        name="sb_attention",
    )(q, k, v, g)


def _gla_kernel(q_ref, k_ref, v_ref, r_ref, lr_ref, wg_ref, bg_ref, g_ref, o_ref, st_ref):
    c = GLA_CHUNK
    dv2 = 2 * GLA_VAL_DIM

    @pl.when(pl.program_id(2) == 0)
    def _():
        st_ref[...] = jnp.zeros_like(st_ref)

    u = _dot(lr_ref[...], wg_ref[...]) + bg_ref[...]
    log_a = (jnp.minimum(u, 0.0) - jnp.log(1.0 + jnp.exp(-jnp.abs(u)))) * (1.0 / GLA_GATE_TEMP)

    ti = lax.broadcasted_iota(jnp.int32, (c, c), 0)
    tj = lax.broadcasted_iota(jnp.int32, (c, c), 1)
    lower = ti >= tj
    tril = jnp.where(lower, 1.0, 0.0).astype(BF16)
    lane = lax.broadcasted_iota(jnp.int32, (c, LANES), 1)
    head0 = lane < GLA_KEY_DIM
    se = lax.broadcasted_iota(jnp.int32, (dv2, LANES), 0)
    sd = lax.broadcasted_iota(jnp.int32, (dv2, LANES), 1)
    st_mask = (se < GLA_VAL_DIM) == (sd < GLA_KEY_DIM)

    chunks = [slice(n * c, (n + 1) * c) for n in range(GLA_STEP // c)]
    la_hi, la_lo = _split_bf16(log_a)
    la_parts = jnp.concatenate([la_hi, la_lo], axis=1)
    bcum, dec = [], []
    for rows in chunks:
        res = _dot(tril, la_parts[rows])
        bcum.append(res[:, :LANES] + res[:, LANES:])
        dec.append(jnp.exp(bcum[-1][c - 1:c, :]))
    q_dec, k_inv, k_end = [], [], []
    for n, rows in enumerate(chunks):
        kf = k_ref[rows, :].astype(F32) * jnp.exp(-bcum[n])
        q_dec.append((q_ref[rows, :].astype(F32) * jnp.exp(bcum[n]) * (GLA_KEY_DIM ** -0.5)).astype(BF16))
        k_inv.append(kf.astype(BF16))
        k_end.append((kf * dec[n]).astype(BF16))
    lower2 = jnp.concatenate([lower, lower], axis=0)
    o_intra = []
    for n, rows in enumerate(chunks):
        zq = jnp.zeros_like(q_dec[n])
        q2 = jnp.concatenate([jnp.where(head0, q_dec[n], zq), jnp.where(head0, zq, q_dec[n])], axis=0)
        a = jnp.where(lower2, _dot_nt(q2, k_inv[n]), 0.0).astype(BF16)
        o_intra.append(jnp.concatenate(
            [_dot(a[:c], v_ref[rows, :GLA_VAL_DIM]), _dot(a[c:], v_ref[rows, GLA_VAL_DIM:])], axis=1))
    kv = [jnp.where(st_mask, _dot_tn(v_ref[rows, :], k_end[n]), 0.0) for n, rows in enumerate(chunks)]
    st = st_ref[...]
    starts = []
    for n in range(len(chunks)):
        starts.append(st.astype(BF16))
        st = st * dec[n] + kv[n]
    st_ref[...] = st
    for n, rows in enumerate(chunks):
        o = o_intra[n] + _dot_nt(q_dec[n], starts[n])
        o0 = o[:, :GLA_VAL_DIM]
        o1 = o[:, GLA_VAL_DIM:]
        n0 = o0 * lax.rsqrt(jnp.mean(o0 * o0, axis=-1, keepdims=True) + RMS_EPS)
        n1 = o1 * lax.rsqrt(jnp.mean(o1 * o1, axis=-1, keepdims=True) + RMS_EPS)
        rf = r_ref[rows, :].astype(F32)
        gate = rf / (1.0 + jnp.exp(-rf))
        o_ref[rows, :] = (jnp.concatenate([n0, n1], axis=1) * g_ref[...] * gate).astype(o_ref.dtype)


def _gla(q, k, v, r, lr, wg, bg, g):
    b, s, _ = q.shape
    pairs = GLA_HEADS // 2
    step = GLA_STEP
    dv2 = 2 * GLA_VAL_DIM
    seq_map = lambda bi, p, t: (bi, t, p)
    return pl.pallas_call(
        _gla_kernel,
        out_shape=jax.ShapeDtypeStruct((b, s, GLA_V_WIDTH), BF16),
        grid=(b, pairs, s // step),
        in_specs=[
            pl.BlockSpec((None, step, LANES), seq_map),
            pl.BlockSpec((None, step, LANES), seq_map),
            pl.BlockSpec((None, step, dv2), seq_map),
            pl.BlockSpec((None, step, dv2), seq_map),
            pl.BlockSpec((None, step, LANES), lambda bi, p, t: (bi, t, 0)),
            pl.BlockSpec((LANES, LANES), lambda bi, p, t: (0, p)),
            pl.BlockSpec((1, LANES), lambda bi, p, t: (0, p)),
            pl.BlockSpec((1, dv2), lambda bi, p, t: (0, p)),
        ],
        out_specs=pl.BlockSpec((None, step, dv2), seq_map),
        scratch_shapes=[pltpu.VMEM((dv2, LANES), F32)],
        compiler_params=pltpu.CompilerParams(
            dimension_semantics=("arbitrary", "arbitrary", "arbitrary"),
            vmem_limit_bytes=VMEM_LIMIT_BYTES),
        name="gla",
    )(q, k, v, r, lr, wg, bg, g)


def _post_kernel(alpha, tiles_per_seq, osb_ref, ogla_ref, x_ref, mod_ref, wa_ref, wb_ref,
                 g1_ref, b1_ref, wg_ref, wu_ref, cw_ref, cb_ref, wd_ref, g_ref, b_ref,
                 o_ref, halo_ref, act_ref, x1_ref):
    tm = x_ref.shape[0]
    d_ff = wg_ref.shape[1]
    first = (pl.program_id(0) % tiles_per_seq) == 0
    mix = _dot(osb_ref[...], wa_ref[...]) + _dot(ogla_ref[...], wb_ref[...])
    x1_ref[...] = _layer_norm(alpha * x_ref[...] + (1.0 + mod_ref[2:3, :]) * mix,
                              g1_ref[...], b1_ref[...])
    h = (x1_ref[...] * (1.0 + mod_ref[4:5, :]) + mod_ref[3:4, :]).astype(BF16)
    top = lax.broadcasted_iota(jnp.int32, (SUBLANES, FF_TILE), 0)

    @pl.when(first)
    def _():
        halo_ref[...] = jnp.zeros_like(halo_ref)

    for f in range(d_ff // FF_TILE):
        cols = slice(f * FF_TILE, (f + 1) * FF_TILE)
        gt = _dot(h, wg_ref[:, cols])
        up = _dot(h, wu_ref[:, cols])
        prev = halo_ref[:, cols]
        halo_ref[:, cols] = gt[tm - SUBLANES:, :]
        r1 = pltpu.roll(gt, 1, 0)
        r2 = pltpu.roll(gt, 2, 0)
        t1 = jnp.where(top < 1, pltpu.roll(prev, 1, 0), r1[:SUBLANES])
        t2 = jnp.where(top < 2, pltpu.roll(prev, 2, 0), r2[:SUBLANES])
        g1 = jnp.concatenate([t1, r1[SUBLANES:]], axis=0)
        g2 = jnp.concatenate([t2, r2[SUBLANES:]], axis=0)
        conv = g2 * cw_ref[0:1, cols] + g1 * cw_ref[1:2, cols] + gt * cw_ref[2:3, cols] + cb_ref[:, cols]
        act_ref[:, cols] = (conv / (1.0 + jnp.exp(-conv)) * up).astype(BF16)
    y = _dot(act_ref[...], wd_ref[...])
    gate_f = mod_ref[5:6, :]
    o_ref[...] = _layer_norm(alpha * x1_ref[...] + (1.0 + gate_f) * y, g_ref[...], b_ref[...])


def _post(o_sb, o_gla, x2d, mod, w_a, w_b, ln1_g, ln1_b, wg, wu, cw, cb, wd, ln_g, ln_b, seq, alpha):
    t, d = x2d.shape
    d_ff = wg.shape[1]
    tm = ROW_TILE
    tiles_per_seq = seq // tm
    row_map = lambda i: (i, 0)
    fixed = lambda i: (0, 0)
    return pl.pallas_call(
        functools.partial(_post_kernel, alpha, tiles_per_seq),
        out_shape=jax.ShapeDtypeStruct((t, d), F32),
        grid=(t // tm,),
        in_specs=[
            pl.BlockSpec((tm, o_sb.shape[1]), row_map),
            pl.BlockSpec((tm, o_gla.shape[1]), row_map),
            pl.BlockSpec((tm, d), row_map),
            pl.BlockSpec((None, SUBLANES, d), lambda i: (i // tiles_per_seq, 0, 0)),
            pl.BlockSpec(w_a.shape, fixed),
            pl.BlockSpec(w_b.shape, fixed),
            pl.BlockSpec((1, d), fixed),
            pl.BlockSpec((1, d), fixed),
            pl.BlockSpec(wg.shape, fixed),
            pl.BlockSpec(wu.shape, fixed),
            pl.BlockSpec(cw.shape, fixed),
            pl.BlockSpec(cb.shape, fixed),
            pl.BlockSpec(wd.shape, fixed),
            pl.BlockSpec((1, d), fixed),
            pl.BlockSpec((1, d), fixed),
        ],
        out_specs=pl.BlockSpec((tm, d), row_map),
        scratch_shapes=[pltpu.VMEM((SUBLANES, d_ff), F32), pltpu.VMEM((tm, d_ff), BF16),
                        pltpu.VMEM((tm, d), F32)],
        compiler_params=pltpu.CompilerParams(
            dimension_semantics=("arbitrary",), vmem_limit_bytes=VMEM_LIMIT_BYTES),
        name="post",
    )(o_sb, o_gla, x2d, mod, w_a, w_b, ln1_g, ln1_b, wg, wu, cw, cb, wd, ln_g, ln_b)


def _split_w_in(w_in):
    lr0 = _MAIN_WIDTH
    r0 = lr0 + GLA_GATE_RANK
    w_main = w_in[:, :lr0].astype(BF16)
    w_r = w_in[:, r0:r0 + GLA_V_WIDTH].astype(BF16)
    w_lr = jnp.pad(w_in[:, lr0:r0], ((0, 0), (0, LANES - GLA_GATE_RANK))).astype(BF16)
    return w_main, w_r, w_lr


def kernel(x, c, w_ada, b_ada, w_in, gla_w_gate, gla_b_gate, sb_norm_g, gla_norm_g, w_out,
           ln1_g, ln1_b, w_ff_gate, w_ff_up, conv_w, conv_b, w_down, ln2_g, ln2_b):
    bsz, seq, d = x.shape
    depth = w_ada.shape[0]
    alpha = float((2 * depth) ** 0.25)
    t = bsz * seq

    c_pad = jnp.concatenate([c, jnp.zeros((SUBLANES - bsz, d), c.dtype)], axis=0)
    xcur = x.reshape(t, d)
    for l in range(depth):
        mod = _ada(c_pad, w_ada[l], b_ada[l][None, :])
        mod = mod[:bsz].reshape(bsz, N_MOD, d)
        mod = jnp.concatenate([mod, jnp.zeros((bsz, SUBLANES - N_MOD, d), F32)], axis=1)

        sb_q, sb_k, sb_v, g_q, g_k, g_v, g_r, g_lr = _inproj(xcur, mod, *_split_w_in(w_in[l]), seq)
        as_seq = lambda a: a.reshape(bsz, seq, a.shape[-1])

        o_sb = _sb_attention(as_seq(sb_q), as_seq(sb_k), as_seq(sb_v), sb_norm_g[l][None, :])
        wg_pad = jnp.concatenate(
            [gla_w_gate[l], jnp.zeros((LANES - GLA_GATE_RANK, GLA_QK_WIDTH), F32)], axis=0).astype(BF16)
        o_gla = _gla(as_seq(g_q), as_seq(g_k), as_seq(g_v), as_seq(g_r), as_seq(g_lr),
                     wg_pad, gla_b_gate[l][None, :], gla_norm_g[l][None, :])

        w_o = w_out[l].astype(BF16)
        xcur = _post(o_sb.reshape(t, SB_WIDTH), o_gla.reshape(t, GLA_V_WIDTH), xcur, mod,
                     w_o[:SB_WIDTH], w_o[SB_WIDTH:], ln1_g[l][None, :], ln1_b[l][None, :],
                     w_ff_gate[l].astype(BF16), w_ff_up[l].astype(BF16),
                     conv_w[l], conv_b[l][None, :], w_down[l].astype(BF16),
                     ln2_g[l][None, :], ln2_b[l][None, :], seq, alpha)
    return xcur.reshape(bsz, seq, d)
```

```python
import functools

import jax
import jax.numpy as jnp
from jax import lax
from jax.experimental import pallas as pl
from jax.experimental.pallas import tpu as pltpu

F32 = jnp.float32
BF16 = jnp.bfloat16

LANES = 128
SUBLANES = 8
VMEM_LIMIT_BYTES = 56 * 1024 * 1024

SB_HEADS = 8
SB_HEAD_DIM = 64
GLA_HEADS = 4
GLA_KEY_DIM = 64
GLA_VAL_DIM = 128
GLA_GATE_RANK = 16
GLA_GATE_TEMP = 16.0
GLA_CHUNK = 64
CONV_WIDTH = 3
N_MOD = 6
LN_EPS = 1e-5
RMS_EPS = 1e-6

SB_WIDTH = SB_HEADS * SB_HEAD_DIM
GLA_QK_WIDTH = GLA_HEADS * GLA_KEY_DIM
GLA_V_WIDTH = GLA_HEADS * GLA_VAL_DIM

SB_EXIT = 104.0
LOG2E = 1.4426950408889634
SB_EXIT_BITS = SB_EXIT * LOG2E
SB_MAX_BITS = 126.0
SB_MASKED_BITS = -1e30
SB_TK = 128
SB_TQ = 4 * SB_TK
SB_WIN_BLOCKS = 3
SB_WIN = SB_WIN_BLOCKS * SB_TK

GLA_STEP = 1024
INPROJ_TILE = 1024
ROW_TILE = 512
FF_TILE = 256


def _sb_logits(q, k):
    return jnp.minimum(_dot_nt(q, k), SB_MAX_BITS)


def _softplus2(z):
    return jnp.log2(1.0 + jnp.exp2(z))


def _split_bf16(a):
    hi = a.astype(BF16)
    lo = (a - hi.astype(F32)).astype(BF16)
    return hi, lo


def _dot(a, b):
    return jnp.dot(a, b, preferred_element_type=F32)


def _dot_nt(a, b):
    return lax.dot_general(a, b, (((1,), (1,)), ((), ())), preferred_element_type=F32)


def _dot_tn(a, b):
    return lax.dot_general(a, b, (((0,), (0,)), ((), ())), preferred_element_type=F32)


def _layer_norm(v, g, b):
    mu = jnp.mean(v, axis=-1, keepdims=True)
    vc = v - mu
    var = jnp.mean(vc * vc, axis=-1, keepdims=True)
    return vc * lax.rsqrt(var + LN_EPS) * g + b


def _ada_kernel(c_ref, w_ref, b_ref, o_ref):
    c = c_ref[...]
    ca = c / (1.0 + jnp.exp(-c))
    ch, cl = _split_bf16(ca)
    wh, wl = _split_bf16(w_ref[...])
    o_ref[...] = _dot(ch, wh) + _dot(cl, wh) + _dot(ch, wl) + b_ref[...]


def _ada(c_pad, w, b):
    d, n = w.shape
    tn = 1024
    return pl.pallas_call(
        _ada_kernel,
        out_shape=jax.ShapeDtypeStruct((c_pad.shape[0], n), F32),
        grid=(n // tn,),
        in_specs=[
            pl.BlockSpec((c_pad.shape[0], d), lambda j: (0, 0)),
            pl.BlockSpec((d, tn), lambda j: (0, j)),
            pl.BlockSpec((1, tn), lambda j: (0, j)),
        ],
        out_specs=pl.BlockSpec((c_pad.shape[0], tn), lambda j: (0, j)),
        compiler_params=pltpu.CompilerParams(
            dimension_semantics=("arbitrary",), vmem_limit_bytes=VMEM_LIMIT_BYTES),
        name="ada",
    )(c_pad, w, b)


_MAIN_WIDTH = 3 * SB_WIDTH + 2 * GLA_QK_WIDTH + GLA_V_WIDTH
_PROJ_GROUPS = (
    (0, 0, SB_WIDTH, SB_HEAD_DIM ** -0.5 * LOG2E),
    (0, SB_WIDTH, SB_WIDTH, 1.0),
    (0, 2 * SB_WIDTH, SB_WIDTH, 1.0),
    (0, 3 * SB_WIDTH, GLA_QK_WIDTH, 1.0),
    (0, 3 * SB_WIDTH + GLA_QK_WIDTH, GLA_QK_WIDTH, 1.0),
    (0, 3 * SB_WIDTH + 2 * GLA_QK_WIDTH, GLA_V_WIDTH, 1.0),
    (1, 0, GLA_V_WIDTH, 1.0),
    (2, 0, LANES, 1.0),
)


def _inproj_kernel(x_ref, mod_ref, w_main_ref, w_r_ref, w_lr_ref, *out_refs):
    shift = mod_ref[0:1, :]
    scale = mod_ref[1:2, :]
    h = (x_ref[...] * (1.0 + scale) + shift).astype(BF16)
    w_refs = (w_main_ref, w_r_ref, w_lr_ref)
    for (src, c0, width, mul), o_ref in zip(_PROJ_GROUPS, out_refs):
        acc = _dot(h, w_refs[src][:, c0:c0 + width])
        if mul != 1.0:
            acc = acc * mul
        o_ref[...] = acc.astype(o_ref.dtype)


def _inproj(x2d, mod, w_main, w_r, w_lr, seq):
    t, d = x2d.shape
    tm = INPROJ_TILE
    tiles_per_seq = seq // tm
    fixed = lambda i: (0, 0)
    widths = [g[2] for g in _PROJ_GROUPS]
    return pl.pallas_call(
        _inproj_kernel,
        out_shape=[jax.ShapeDtypeStruct((t, width), BF16) for width in widths],
        grid=(t // tm,),
        in_specs=[
            pl.BlockSpec((tm, d), lambda i: (i, 0)),
            pl.BlockSpec((None, SUBLANES, d), lambda i: (i // tiles_per_seq, 0, 0)),
            pl.BlockSpec(w_main.shape, fixed),
            pl.BlockSpec(w_r.shape, fixed),
            pl.BlockSpec(w_lr.shape, fixed),
        ],
        out_specs=[pl.BlockSpec((tm, width), lambda i: (i, 0)) for width in widths],
        compiler_params=pltpu.CompilerParams(
            dimension_semantics=("arbitrary",), vmem_limit_bytes=VMEM_LIMIT_BYTES),
        name="inproj",
    )(x2d, mod, w_main, w_r, w_lr)


def _sb_suffix_sums(sp_blocks, umat):
    m = sp_blocks[0].shape[0]
    res = _dot(jnp.concatenate(sp_blocks, axis=0).astype(BF16), umat)
    cs, nearer = [], None
    for c in reversed(range(len(sp_blocks))):
        part = res[c * m:(c + 1) * m]
        loc, tot = part[:, :SB_TK], part[:, SB_TK:]
        cs.append(loc if nearer is None else loc + nearer)
        nearer = tot if nearer is None else nearer + tot
    return cs[::-1], nearer


def _sb_kernel(q_ref, k_ref, v_ref, g_ref, o_ref):
    tk = SB_TK
    pairs = q_ref.shape[1] // LANES
    nrb = SB_TQ // tk
    nb = SB_WIN_BLOCKS
    lane = lax.broadcasted_iota(jnp.int32, (tk, LANES), 1)
    head0 = lane < SB_HEAD_DIM

    uj = lax.broadcasted_iota(jnp.int32, (tk, 2 * tk), 0)
    us = lax.broadcasted_iota(jnp.int32, (tk, 2 * tk), 1)
    umat = jnp.where((uj >= us) | (us >= tk), 1.0, 0.0).astype(BF16)
    row = lax.broadcasted_iota(jnp.int32, (2 * tk, tk), 0) & (tk - 1)
    col = lax.broadcasted_iota(jnp.int32, (2 * tk, tk), 1)
    causal = col < row
    lanes = [slice(p * LANES, (p + 1) * LANES) for p in range(pairs)]

    first_blk = [pl.program_id(1) * nrb + r - (nb - 1) for r in range(nrb)]
    blk_rows = [[pl.ds(pl.multiple_of(jnp.maximum(first_blk[r] + c, 0) * tk, tk), tk)
                 for c in range(nb)] for r in range(nrb)]
    units = [(r, p) for r in range(nrb) for p in range(pairs)]

    qs, zs, sums, ws, accs = {}, {}, {}, {}, {}

    def stage_logits(u):
        r, p = u
        q = q_ref[r * tk:(r + 1) * tk, lanes[p]]
        zq = jnp.zeros_like(q)
        qs[u] = jnp.concatenate([jnp.where(head0, q, zq), jnp.where(head0, zq, q)], axis=0)
        kw = jnp.concatenate([k_ref[rows, lanes[p]] for rows in blk_rows[r]], axis=0)
        z = _sb_logits(qs[u], kw)
        zs[u] = [z[:, c * tk:(c + 1) * tk] for c in range(nb)]
        zs[u][-1] = jnp.where(causal, zs[u][-1], SB_MASKED_BITS)

    def stage_sums(u):
        sums[u] = _sb_suffix_sums([_softplus2(z) for z in zs[u]], umat)

    def stage_weights(u):
        args = [z - cs for z, cs in zip(zs[u], sums[u][0])]
        ws[u] = jnp.exp2(jnp.concatenate(args, axis=1)).astype(BF16)

    def stage_values(u):
        r, p = u
        vw = [v_ref[rows, lanes[p]] for rows in blk_rows[r]]
        vw = [jnp.where(first_blk[r] + c >= 0, vw[c], jnp.zeros_like(vw[c]))
              for c in range(nb - 1)] + vw[-1:]
        accs[u] = _dot(ws[u], jnp.concatenate(vw, axis=0))

    stages = (stage_logits, stage_sums, stage_weights, stage_values)
    for t in range(len(stages) + len(units) - 1):
        for n, u in enumerate(units):
            if 0 <= t - n < len(stages):
                stages[t - n](u)

    def not_done(cs_):
        m = cs_[0]
        for c in cs_[1:]:
            m = jnp.minimum(m, c)
        return (jnp.min(m) < SB_EXIT_BITS).astype(jnp.int32)

    def cond(s):
        return jnp.logical_and(s[0] >= 0, s[1] > 0)

    def write_out(r, accs_):
        outs = []
        for p in range(pairs):
            o = jnp.where(head0, accs_[p][:tk], accs_[p][tk:])
            sq = o * o
            ss0 = jnp.sum(jnp.where(head0, sq, 0.0), axis=-1, keepdims=True)
            ss1 = jnp.sum(jnp.where(head0, 0.0, sq), axis=-1, keepdims=True)
            ms = jnp.where(head0, ss0, ss1) * (1.0 / SB_HEAD_DIM)
            outs.append(o * lax.rsqrt(ms + RMS_EPS))
        o_ref[r * tk:(r + 1) * tk, :] = (jnp.concatenate(outs, axis=1) * g_ref[...]).astype(o_ref.dtype)

    for r in range(nrb):
        write_out(r, [accs[r, p] for p in range(pairs)])
    for r in range(nrb):
        carries = tuple(sums[r, p][1] for p in range(pairs))
        more = not_done(carries)

        def body(s, r=r):
            j, _, cs_, as_ = s
            rows = pl.ds(pl.multiple_of(j * tk, tk), tk)
            zb = [_sb_logits(qs[r, p], k_ref[rows, lanes[p]]) for p in range(pairs)]
            sb = [_sb_suffix_sums([_softplus2(zb[p])], umat) for p in range(pairs)]
            wb = [jnp.exp2(zb[p] - sb[p][0][0] - cs_[p]).astype(BF16) for p in range(pairs)]
            new_a = [as_[p] + _dot(wb[p], v_ref[rows, lanes[p]]) for p in range(pairs)]
            new_c = [cs_[p] + sb[p][1] for p in range(pairs)]
            return j - 1, not_done(new_c), tuple(new_c), tuple(new_a)

        @pl.when(jnp.logical_and(first_blk[r] - 1 >= 0, more > 0))
        def _(r=r, carries=carries, more=more, body=body):
            _, _, _, accs_ = lax.while_loop(
                cond, body,
                (first_blk[r] - 1, more, carries, tuple(accs[r, p] for p in range(pairs))))
            write_out(r, accs_)


def _sb_attention(q, k, v, g):
    b, s, wdt = q.shape
    return pl.pallas_call(
        _sb_kernel,
        out_shape=jax.ShapeDtypeStruct((b, s, wdt), BF16),
        grid=(b, s // SB_TQ),
        in_specs=[
            pl.BlockSpec((None, SB_TQ, wdt), lambda bi, i: (bi, i, 0)),
            pl.BlockSpec((None, s, wdt), lambda bi, i: (bi, 0, 0)),
            pl.BlockSpec((None, s, wdt), lambda bi, i: (bi, 0, 0)),
            pl.BlockSpec((1, wdt), lambda bi, i: (0, 0)),
        ],
        out_specs=pl.BlockSpec((None, SB_TQ, wdt), lambda bi, i: (bi, i, 0)),
        compiler_params=pltpu.CompilerParams(
            dimension_semantics=("arbitrary", "arbitrary"),
            vmem_limit_bytes=VMEM_LIMIT_BYTES),
        name="sb_attention",
    )(q, k, v, g)


def _gla_kernel(q_ref, k_ref, v_ref, r_ref, lr_ref, wg_ref, bg_ref, g_ref, o_ref, st_ref):
    c = GLA_CHUNK
    dv2 = 2 * GLA_VAL_DIM

    @pl.when(pl.program_id(2) == 0)
    def _():
        st_ref[...] = jnp.zeros_like(st_ref)

    u = _dot(lr_ref[...], wg_ref[...]) + bg_ref[...]
    log_a = (jnp.minimum(u, 0.0) - jnp.log(1.0 + jnp.exp(-jnp.abs(u)))) * (1.0 / GLA_GATE_TEMP)

    ti = lax.broadcasted_iota(jnp.int32, (c, c), 0)
    tj = lax.broadcasted_iota(jnp.int32, (c, c), 1)
    lower = ti >= tj
    tril = jnp.where(lower, 1.0, 0.0).astype(BF16)
    lane = lax.broadcasted_iota(jnp.int32, (c, LANES), 1)
    head0 = lane < GLA_KEY_DIM
    se = lax.broadcasted_iota(jnp.int32, (dv2, LANES), 0)
    sd = lax.broadcasted_iota(jnp.int32, (dv2, LANES), 1)
    st_mask = (se < GLA_VAL_DIM) == (sd < GLA_KEY_DIM)

    chunks = [slice(n * c, (n + 1) * c) for n in range(GLA_STEP // c)]
    la_hi, la_lo = _split_bf16(log_a)
    la_parts = jnp.concatenate([la_hi, la_lo], axis=1)
    bcum, dec = [], []
    for rows in chunks:
        res = _dot(tril, la_parts[rows])
        bcum.append(res[:, :LANES] + res[:, LANES:])
        dec.append(jnp.exp(bcum[-1][c - 1:c, :]))
    q_dec, k_inv, k_end = [], [], []
    for n, rows in enumerate(chunks):
        kf = k_ref[rows, :].astype(F32) * jnp.exp(-bcum[n])
        q_dec.append((q_ref[rows, :].astype(F32) * jnp.exp(bcum[n]) * (GLA_KEY_DIM ** -0.5)).astype(BF16))
        k_inv.append(kf.astype(BF16))
        k_end.append((kf * dec[n]).astype(BF16))
    lower2 = jnp.concatenate([lower, lower], axis=0)
    o_intra = []
    for n, rows in enumerate(chunks):
        zq = jnp.zeros_like(q_dec[n])
        q2 = jnp.concatenate([jnp.where(head0, q_dec[n], zq), jnp.where(head0, zq, q_dec[n])], axis=0)
        a = jnp.where(lower2, _dot_nt(q2, k_inv[n]), 0.0).astype(BF16)
        o_intra.append(jnp.concatenate(
            [_dot(a[:c], v_ref[rows, :GLA_VAL_DIM]), _dot(a[c:], v_ref[rows, GLA_VAL_DIM:])], axis=1))
    kv = [jnp.where(st_mask, _dot_tn(v_ref[rows, :], k_end[n]), 0.0) for n, rows in enumerate(chunks)]
    st = st_ref[...]
    starts = []
    for n in range(len(chunks)):
        starts.append(st.astype(BF16))
        st = st * dec[n] + kv[n]
    st_ref[...] = st
    for n, rows in enumerate(chunks):
        o = o_intra[n] + _dot_nt(q_dec[n], starts[n])
        o0 = o[:, :GLA_VAL_DIM]
        o1 = o[:, GLA_VAL_DIM:]
        n0 = o0 * lax.rsqrt(jnp.mean(o0 * o0, axis=-1, keepdims=True) + RMS_EPS)
        n1 = o1 * lax.rsqrt(jnp.mean(o1 * o1, axis=-1, keepdims=True) + RMS_EPS)
        rf = r_ref[rows, :].astype(F32)
        gate = rf / (1.0 + jnp.exp(-rf))
        o_ref[rows, :] = (jnp.concatenate([n0, n1], axis=1) * g_ref[...] * gate).astype(o_ref.dtype)


def _gla(q, k, v, r, lr, wg, bg, g):
    b, s, _ = q.shape
    pairs = GLA_HEADS // 2
    step = GLA_STEP
    dv2 = 2 * GLA_VAL_DIM
    seq_map = lambda bi, p, t: (bi, t, p)
    return pl.pallas_call(
        _gla_kernel,
        out_shape=jax.ShapeDtypeStruct((b, s, GLA_V_WIDTH), BF16),
        grid=(b, pairs, s // step),
        in_specs=[
            pl.BlockSpec((None, step, LANES), seq_map),
            pl.BlockSpec((None, step, LANES), seq_map),
            pl.BlockSpec((None, step, dv2), seq_map),
            pl.BlockSpec((None, step, dv2), seq_map),
            pl.BlockSpec((None, step, LANES), lambda bi, p, t: (bi, t, 0)),
            pl.BlockSpec((LANES, LANES), lambda bi, p, t: (0, p)),
            pl.BlockSpec((1, LANES), lambda bi, p, t: (0, p)),
            pl.BlockSpec((1, dv2), lambda bi, p, t: (0, p)),
        ],
        out_specs=pl.BlockSpec((None, step, dv2), seq_map),
        scratch_shapes=[pltpu.VMEM((dv2, LANES), F32)],
        compiler_params=pltpu.CompilerParams(
            dimension_semantics=("arbitrary", "arbitrary", "arbitrary"),
            vmem_limit_bytes=VMEM_LIMIT_BYTES),
        name="gla",
    )(q, k, v, r, lr, wg, bg, g)


def _post_kernel(alpha, tiles_per_seq, osb_ref, ogla_ref, x_ref, mod_ref, wa_ref, wb_ref,
                 g1_ref, b1_ref, wg_ref, wu_ref, cw_ref, cb_ref, wd_ref, g_ref, b_ref,
                 o_ref, halo_ref, act_ref, x1_ref):
    tm = x_ref.shape[0]
    d_ff = wg_ref.shape[1]
    first = (pl.program_id(0) % tiles_per_seq) == 0
    mix = _dot(osb_ref[...], wa_ref[...]) + _dot(ogla_ref[...], wb_ref[...])
    x1_ref[...] = _layer_norm(alpha * x_ref[...] + (1.0 + mod_ref[2:3, :]) * mix,
                              g1_ref[...], b1_ref[...])
    h = (x1_ref[...] * (1.0 + mod_ref[4:5, :]) + mod_ref[3:4, :]).astype(BF16)
    top = lax.broadcasted_iota(jnp.int32, (SUBLANES, FF_TILE), 0)

    @pl.when(first)
    def _():
        halo_ref[...] = jnp.zeros_like(halo_ref)

    for f in range(d_ff // FF_TILE):
        cols = slice(f * FF_TILE, (f + 1) * FF_TILE)
        gt = _dot(h, wg_ref[:, cols])
        up = _dot(h, wu_ref[:, cols])
        prev = halo_ref[:, cols]
        halo_ref[:, cols] = gt[tm - SUBLANES:, :]
        r1 = pltpu.roll(gt, 1, 0)
        r2 = pltpu.roll(gt, 2, 0)
        t1 = jnp.where(top < 1, pltpu.roll(prev, 1, 0), r1[:SUBLANES])
        t2 = jnp.where(top < 2, pltpu.roll(prev, 2, 0), r2[:SUBLANES])
        g1 = jnp.concatenate([t1, r1[SUBLANES:]], axis=0)
        g2 = jnp.concatenate([t2, r2[SUBLANES:]], axis=0)
        conv = g2 * cw_ref[0:1, cols] + g1 * cw_ref[1:2, cols] + gt * cw_ref[2:3, cols] + cb_ref[:, cols]
        act_ref[:, cols] = (conv / (1.0 + jnp.exp(-conv)) * up).astype(BF16)
    y = _dot(act_ref[...], wd_ref[...])
    gate_f = mod_ref[5:6, :]
    o_ref[...] = _layer_norm(alpha * x1_ref[...] + (1.0 + gate_f) * y, g_ref[...], b_ref[...])


def _post(o_sb, o_gla, x2d, mod, w_a, w_b, ln1_g, ln1_b, wg, wu, cw, cb, wd, ln_g, ln_b, seq, alpha):
    t, d = x2d.shape
    d_ff = wg.shape[1]
    tm = ROW_TILE
    tiles_per_seq = seq // tm
    row_map = lambda i: (i, 0)
    fixed = lambda i: (0, 0)
    return pl.pallas_call(
        functools.partial(_post_kernel, alpha, tiles_per_seq),
        out_shape=jax.ShapeDtypeStruct((t, d), F32),
        grid=(t // tm,),
        in_specs=[
            pl.BlockSpec((tm, o_sb.shape[1]), row_map),
            pl.BlockSpec((tm, o_gla.shape[1]), row_map),
            pl.BlockSpec((tm, d), row_map),
            pl.BlockSpec((None, SUBLANES, d), lambda i: (i // tiles_per_seq, 0, 0)),
            pl.BlockSpec(w_a.shape, fixed),
            pl.BlockSpec(w_b.shape, fixed),
            pl.BlockSpec((1, d), fixed),
            pl.BlockSpec((1, d), fixed),
            pl.BlockSpec(wg.shape, fixed),
            pl.BlockSpec(wu.shape, fixed),
            pl.BlockSpec(cw.shape, fixed),
            pl.BlockSpec(cb.shape, fixed),
            pl.BlockSpec(wd.shape, fixed),
            pl.BlockSpec((1, d), fixed),
            pl.BlockSpec((1, d), fixed),
        ],
        out_specs=pl.BlockSpec((tm, d), row_map),
        scratch_shapes=[pltpu.VMEM((SUBLANES, d_ff), F32), pltpu.VMEM((tm, d_ff), BF16),
                        pltpu.VMEM((tm, d), F32)],
        compiler_params=pltpu.CompilerParams(
            dimension_semantics=("arbitrary",), vmem_limit_bytes=VMEM_LIMIT_BYTES),
        name="post",
    )(o_sb, o_gla, x2d, mod, w_a, w_b, ln1_g, ln1_b, wg, wu, cw, cb, wd, ln_g, ln_b)


def _split_w_in(w_in):
    lr0 = _MAIN_WIDTH
    r0 = lr0 + GLA_GATE_RANK
    w_main = w_in[:, :lr0].astype(BF16)
    w_r = w_in[:, r0:r0 + GLA_V_WIDTH].astype(BF16)
    w_lr = jnp.pad(w_in[:, lr0:r0], ((0, 0), (0, LANES - GLA_GATE_RANK))).astype(BF16)
    return w_main, w_r, w_lr


def kernel(x, c, w_ada, b_ada, w_in, gla_w_gate, gla_b_gate, sb_norm_g, gla_norm_g, w_out,
           ln1_g, ln1_b, w_ff_gate, w_ff_up, conv_w, conv_b, w_down, ln2_g, ln2_b):
    bsz, seq, d = x.shape
    depth = w_ada.shape[0]
    alpha = float((2 * depth) ** 0.25)
    t = bsz * seq

    c_pad = jnp.concatenate([c, jnp.zeros((SUBLANES - bsz, d), c.dtype)], axis=0)
    xcur = x.reshape(t, d)
    for l in range(depth):
        mod = _ada(c_pad, w_ada[l], b_ada[l][None, :])
        mod = mod[:bsz].reshape(bsz, N_MOD, d)
        mod = jnp.concatenate([mod, jnp.zeros((bsz, SUBLANES - N_MOD, d), F32)], axis=1)

        sb_q, sb_k, sb_v, g_q, g_k, g_v, g_r, g_lr = _inproj(xcur, mod, *_split_w_in(w_in[l]), seq)
        as_seq = lambda a: a.reshape(bsz, seq, a.shape[-1])

        o_sb = _sb_attention(as_seq(sb_q), as_seq(sb_k), as_seq(sb_v), sb_norm_g[l][None, :])
        wg_pad = jnp.concatenate(
            [gla_w_gate[l], jnp.zeros((LANES - GLA_GATE_RANK, GLA_QK_WIDTH), F32)], axis=0).astype(BF16)
        o_gla = _gla(as_seq(g_q), as_seq(g_k), as_seq(g_v), as_seq(g_r), as_seq(g_lr),
                     wg_pad, gla_b_gate[l][None, :], gla_norm_g[l][None, :])

        w_o = w_out[l].astype(BF16)
        xcur = _post(o_sb.reshape(t, SB_WIDTH), o_gla.reshape(t, GLA_V_WIDTH), xcur, mod,
                     w_o[:SB_WIDTH], w_o[SB_WIDTH:], ln1_g[l][None, :], ln1_b[l][None, :],
                     w_ff_gate[l].astype(BF16), w_ff_up[l].astype(BF16),
                     conv_w[l], conv_b[l][None, :], w_down[l].astype(BF16),
                     ln2_g[l][None, :], ln2_b[l][None, :], seq, alpha)
    return xcur.reshape(bsz, seq, d)
```

```python
import functools

import jax
import jax.numpy as jnp
from jax import lax
from jax.experimental import pallas as pl
from jax.experimental.pallas import tpu as pltpu

F32 = jnp.float32
BF16 = jnp.bfloat16

LANES = 128
SUBLANES = 8
VMEM_LIMIT_BYTES = 56 * 1024 * 1024

SB_HEADS = 8
SB_HEAD_DIM = 64
GLA_HEADS = 4
GLA_KEY_DIM = 64
GLA_VAL_DIM = 128
GLA_GATE_RANK = 16
GLA_GATE_TEMP = 16.0
GLA_CHUNK = 64
CONV_WIDTH = 3
N_MOD = 6
LN_EPS = 1e-5
RMS_EPS = 1e-6

SB_WIDTH = SB_HEADS * SB_HEAD_DIM
GLA_QK_WIDTH = GLA_HEADS * GLA_KEY_DIM
GLA_V_WIDTH = GLA_HEADS * GLA_VAL_DIM

SB_EXIT = 104.0
LOG2E = 1.4426950408889634
SB_EXIT_BITS = SB_EXIT * LOG2E
SB_MAX_BITS = 126.0
SB_MASKED_BITS = -1e30
SB_TK = 128
SB_TQ = 4 * SB_TK
SB_WIN_BLOCKS = 3
SB_WIN = SB_WIN_BLOCKS * SB_TK

GLA_STEP = 1024
INPROJ_TILE = 1024
ROW_TILE = 512
FF_TILE = 256
STAGE_ROWS = 128


def _sb_logits(q, k):
    return jnp.minimum(_dot_nt(q, k), SB_MAX_BITS)


def _softplus2(z):
    return jnp.log2(1.0 + jnp.exp2(z))


def _split_bf16(a):
    hi = a.astype(BF16)
    lo = (a - hi.astype(F32)).astype(BF16)
    return hi, lo


def _dot(a, b):
    return jnp.dot(a, b, preferred_element_type=F32)


def _dot_nt(a, b):
    return lax.dot_general(a, b, (((1,), (1,)), ((), ())), preferred_element_type=F32)


def _dot_tn(a, b):
    return lax.dot_general(a, b, (((0,), (0,)), ((), ())), preferred_element_type=F32)


def _stage_weights_bf16(pairs, stage_ref, sem):
    rows = stage_ref.shape[1]
    jobs = [(src, dst, r0) for src, dst in pairs for r0 in range(0, src.shape[0], rows)]

    def window(n):
        return stage_ref.at[n % 2, :, pl.ds(0, jobs[n][0].shape[1])]

    def copy(n):
        src, _, r0 = jobs[n]
        return pltpu.make_async_copy(src.at[pl.ds(r0, rows), :], window(n), sem.at[n % 2])

    copy(0).start()
    for n, (_, dst, r0) in enumerate(jobs):
        if n + 1 < len(jobs):
            copy(n + 1).start()
        copy(n).wait()
        dst[pl.ds(r0, rows), :] = window(n)[...].astype(BF16)


def _layer_norm(v, g, b):
    mu = jnp.mean(v, axis=-1, keepdims=True)
    vc = v - mu
    var = jnp.mean(vc * vc, axis=-1, keepdims=True)
    return vc * lax.rsqrt(var + LN_EPS) * g + b


def _ada_kernel(c_ref, w_ref, b_ref, o_ref):
    c = c_ref[...]
    ca = c / (1.0 + jnp.exp(-c))
    ch, cl = _split_bf16(ca)
    wh, wl = _split_bf16(w_ref[...])
    o_ref[...] = _dot(ch, wh) + _dot(cl, wh) + _dot(ch, wl) + b_ref[...]


def _ada(c_pad, w, b):
    d, n = w.shape
    tn = 1024
    return pl.pallas_call(
        _ada_kernel,
        out_shape=jax.ShapeDtypeStruct((c_pad.shape[0], n), F32),
        grid=(n // tn,),
        in_specs=[
            pl.BlockSpec((c_pad.shape[0], d), lambda j: (0, 0)),
            pl.BlockSpec((d, tn), lambda j: (0, j)),
            pl.BlockSpec((1, tn), lambda j: (0, j)),
        ],
        out_specs=pl.BlockSpec((c_pad.shape[0], tn), lambda j: (0, j)),
        compiler_params=pltpu.CompilerParams(
            dimension_semantics=("arbitrary",), vmem_limit_bytes=VMEM_LIMIT_BYTES),
        name="ada",
    )(c_pad, w, b)


_MAIN_WIDTH = 3 * SB_WIDTH + 2 * GLA_QK_WIDTH + GLA_V_WIDTH
_PROJ_GROUPS = (
    (0, 0, SB_WIDTH, SB_HEAD_DIM ** -0.5 * LOG2E),
    (0, SB_WIDTH, SB_WIDTH, 1.0),
    (0, 2 * SB_WIDTH, SB_WIDTH, 1.0),
    (0, 3 * SB_WIDTH, GLA_QK_WIDTH, 1.0),
    (0, 3 * SB_WIDTH + GLA_QK_WIDTH, GLA_QK_WIDTH, 1.0),
    (0, 3 * SB_WIDTH + 2 * GLA_QK_WIDTH, GLA_V_WIDTH, 1.0),
    (1, 0, GLA_V_WIDTH, 1.0),
    (2, 0, LANES, 1.0),
)


def _stage_w_in(w_hbm, w_main_ref, w_r_ref, w_lr_ref, stage_ref, sem):
    n_in = w_hbm.shape[0]
    lr0 = _MAIN_WIDTH
    r0 = lr0 + GLA_GATE_RANK
    copy = pltpu.make_async_copy(w_hbm, stage_ref, sem.at[0])
    copy.start()
    copy.wait()
    lane = lax.broadcasted_iota(jnp.int32, (STAGE_ROWS, LANES), 1)
    for n in range(n_in // STAGE_ROWS):
        rows = pl.ds(n * STAGE_ROWS, STAGE_ROWS)
        w_main_ref[rows, :] = stage_ref[rows, :lr0].astype(BF16)
        w_r_ref[rows, :] = stage_ref[rows, r0:r0 + GLA_V_WIDTH].astype(BF16)
        lr = stage_ref[rows, lr0:lr0 + LANES]
        w_lr_ref[rows, :] = jnp.where(lane < GLA_GATE_RANK, lr, 0.0).astype(BF16)


def _inproj_kernel(x_ref, mod_ref, w_hbm, *refs):
    n_out = len(_PROJ_GROUPS)
    out_refs = refs[:n_out]
    w_main_ref, w_r_ref, w_lr_ref, stage_ref, sem = refs[n_out:]

    @pl.when(pl.program_id(0) == 0)
    def _():
        _stage_w_in(w_hbm, w_main_ref, w_r_ref, w_lr_ref, stage_ref, sem)

    shift = mod_ref[0:1, :]
    scale = mod_ref[1:2, :]
    h = (x_ref[...] * (1.0 + scale) + shift).astype(BF16)
    w_refs = (w_main_ref, w_r_ref, w_lr_ref)
    for (src, c0, width, mul), o_ref in zip(_PROJ_GROUPS, out_refs):
        acc = _dot(h, w_refs[src][:, c0:c0 + width])
        if mul != 1.0:
            acc = acc * mul
        o_ref[...] = acc.astype(o_ref.dtype)


def _inproj(x2d, mod, w_in, seq):
    t, d = x2d.shape
    tm = INPROJ_TILE
    tiles_per_seq = seq // tm
    widths = [g[2] for g in _PROJ_GROUPS]
    return pl.pallas_call(
        _inproj_kernel,
        out_shape=[jax.ShapeDtypeStruct((t, width), BF16) for width in widths],
        grid=(t // tm,),
        in_specs=[
            pl.BlockSpec((tm, d), lambda i: (i, 0)),
            pl.BlockSpec((None, SUBLANES, d), lambda i: (i // tiles_per_seq, 0, 0)),
            pl.BlockSpec(memory_space=pl.ANY),
        ],
        out_specs=[pl.BlockSpec((tm, width), lambda i: (i, 0)) for width in widths],
        scratch_shapes=[pltpu.VMEM((d, _MAIN_WIDTH), BF16), pltpu.VMEM((d, GLA_V_WIDTH), BF16),
                        pltpu.VMEM((d, LANES), BF16),
                        pltpu.VMEM(w_in.shape, F32),
                        pltpu.SemaphoreType.DMA((1,))],
        compiler_params=pltpu.CompilerParams(
            dimension_semantics=("arbitrary",), vmem_limit_bytes=VMEM_LIMIT_BYTES),
        name="inproj",
    )(x2d, mod, w_in)


def _sb_suffix_sums(sp_blocks, umat):
    m = sp_blocks[0].shape[0]
    res = _dot(jnp.concatenate(sp_blocks, axis=0).astype(BF16), umat)
    cs, nearer = [], None
    for c in reversed(range(len(sp_blocks))):
        part = res[c * m:(c + 1) * m]
        loc, tot = part[:, :SB_TK], part[:, SB_TK:]
        cs.append(loc if nearer is None else loc + nearer)
        nearer = tot if nearer is None else nearer + tot
    return cs[::-1], nearer


def _sb_kernel(q_ref, k_ref, v_ref, g_ref, o_ref):
    tk = SB_TK
    pairs = q_ref.shape[1] // LANES
    nrb = SB_TQ // tk
    nb = SB_WIN_BLOCKS
    lane = lax.broadcasted_iota(jnp.int32, (tk, LANES), 1)
    head0 = lane < SB_HEAD_DIM

    uj = lax.broadcasted_iota(jnp.int32, (tk, 2 * tk), 0)
    us = lax.broadcasted_iota(jnp.int32, (tk, 2 * tk), 1)
    umat = jnp.where((uj >= us) | (us >= tk), 1.0, 0.0).astype(BF16)
    row = lax.broadcasted_iota(jnp.int32, (2 * tk, tk), 0) & (tk - 1)
    col = lax.broadcasted_iota(jnp.int32, (2 * tk, tk), 1)
    causal = col < row
    lanes = [slice(p * LANES, (p + 1) * LANES) for p in range(pairs)]

    first_blk = [pl.program_id(1) * nrb + r - (nb - 1) for r in range(nrb)]
    blk_rows = [[pl.ds(pl.multiple_of(jnp.maximum(first_blk[r] + c, 0) * tk, tk), tk)
                 for c in range(nb)] for r in range(nrb)]
    units = [(r, p) for r in range(nrb) for p in range(pairs)]

    qs, zs, sums, ws, accs = {}, {}, {}, {}, {}

    def stage_logits(u):
        r, p = u
        q = q_ref[r * tk:(r + 1) * tk, lanes[p]]
        zq = jnp.zeros_like(q)
        qs[u] = jnp.concatenate([jnp.where(head0, q, zq), jnp.where(head0, zq, q)], axis=0)
        kw = jnp.concatenate([k_ref[rows, lanes[p]] for rows in blk_rows[r]], axis=0)
        z = _sb_logits(qs[u], kw)
        zs[u] = [z[:, c * tk:(c + 1) * tk] for c in range(nb)]
        zs[u][-1] = jnp.where(causal, zs[u][-1], SB_MASKED_BITS)

    def stage_sums(u):
        sums[u] = _sb_suffix_sums([_softplus2(z) for z in zs[u]], umat)

    def stage_weights(u):
        args = [z - cs for z, cs in zip(zs[u], sums[u][0])]
        ws[u] = jnp.exp2(jnp.concatenate(args, axis=1)).astype(BF16)

    def stage_values(u):
        r, p = u
        vw = [v_ref[rows, lanes[p]] for rows in blk_rows[r]]
        vw = [jnp.where(first_blk[r] + c >= 0, vw[c], jnp.zeros_like(vw[c]))
              for c in range(nb - 1)] + vw[-1:]
        accs[u] = _dot(ws[u], jnp.concatenate(vw, axis=0))

    stages = (stage_logits, stage_sums, stage_weights, stage_values)
    for t in range(len(stages) + len(units) - 1):
        for n, u in enumerate(units):
            if 0 <= t - n < len(stages):
                stages[t - n](u)

    def not_done(cs_):
        m = cs_[0]
        for c in cs_[1:]:
            m = jnp.minimum(m, c)
        return (jnp.min(m) < SB_EXIT_BITS).astype(jnp.int32)

    def cond(s):
        return jnp.logical_and(s[0] >= 0, s[1] > 0)

    def write_out(r, accs_):
        outs = []
        for p in range(pairs):
            o = jnp.where(head0, accs_[p][:tk], accs_[p][tk:])
            sq = o * o
            ss0 = jnp.sum(jnp.where(head0, sq, 0.0), axis=-1, keepdims=True)
            ss1 = jnp.sum(jnp.where(head0, 0.0, sq), axis=-1, keepdims=True)
            ms = jnp.where(head0, ss0, ss1) * (1.0 / SB_HEAD_DIM)
            outs.append(o * lax.rsqrt(ms + RMS_EPS))
        o_ref[r * tk:(r + 1) * tk, :] = (jnp.concatenate(outs, axis=1) * g_ref[...]).astype(o_ref.dtype)

    for r in range(nrb):
        write_out(r, [accs[r, p] for p in range(pairs)])
    for r in range(nrb):
        carries = tuple(sums[r, p][1] for p in range(pairs))
        more = not_done(carries)

        def body(s, r=r):
            j, _, cs_, as_ = s
            rows = pl.ds(pl.multiple_of(j * tk, tk), tk)
            zb = [_sb_logits(qs[r, p], k_ref[rows, lanes[p]]) for p in range(pairs)]
            sb = [_sb_suffix_sums([_softplus2(zb[p])], umat) for p in range(pairs)]
            wb = [jnp.exp2(zb[p] - sb[p][0][0] - cs_[p]).astype(BF16) for p in range(pairs)]
            new_a = [as_[p] + _dot(wb[p], v_ref[rows, lanes[p]]) for p in range(pairs)]
            new_c = [cs_[p] + sb[p][1] for p in range(pairs)]
            return j - 1, not_done(new_c), tuple(new_c), tuple(new_a)

        @pl.when(jnp.logical_and(first_blk[r] - 1 >= 0, more > 0))
        def _(r=r, carries=carries, more=more, body=body):
            _, _, _, accs_ = lax.while_loop(
                cond, body,
                (first_blk[r] - 1, more, carries, tuple(accs[r, p] for p in range(pairs))))
            write_out(r, accs_)


def _sb_attention(q, k, v, g):
    b, s, wdt = q.shape
    return pl.pallas_call(
        _sb_kernel,
        out_shape=jax.ShapeDtypeStruct((b, s, wdt), BF16),
        grid=(b, s // SB_TQ),
        in_specs=[
            pl.BlockSpec((None, SB_TQ, wdt), lambda bi, i: (bi, i, 0)),
            pl.BlockSpec((None, s, wdt), lambda bi, i: (bi, 0, 0)),
            pl.BlockSpec((None, s, wdt), lambda bi, i: (bi, 0, 0)),
            pl.BlockSpec((1, wdt), lambda bi, i: (0, 0)),
        ],
        out_specs=pl.BlockSpec((None, SB_TQ, wdt), lambda bi, i: (bi, i, 0)),
        compiler_params=pltpu.CompilerParams(
            dimension_semantics=("arbitrary", "arbitrary"),
            vmem_limit_bytes=VMEM_LIMIT_BYTES),
        name="sb_attention",
    )(q, k, v, g)


def _gla_kernel(q_ref, k_ref, v_ref, r_ref, lr_ref, wg_ref, bg_ref, g_ref, o_ref, st_ref):
    c = GLA_CHUNK
    dv2 = 2 * GLA_VAL_DIM

    @pl.when(pl.program_id(2) == 0)
    def _():
        st_ref[...] = jnp.zeros_like(st_ref)

    u = _dot(lr_ref[...], wg_ref[...]) + bg_ref[...]
    log_a = (jnp.minimum(u, 0.0) - jnp.log(1.0 + jnp.exp(-jnp.abs(u)))) * (1.0 / GLA_GATE_TEMP)

    ti = lax.broadcasted_iota(jnp.int32, (c, c), 0)
    tj = lax.broadcasted_iota(jnp.int32, (c, c), 1)
    lower = ti >= tj
    tril = jnp.where(lower, 1.0, 0.0).astype(BF16)
    lane = lax.broadcasted_iota(jnp.int32, (c, LANES), 1)
    head0 = lane < GLA_KEY_DIM
    se = lax.broadcasted_iota(jnp.int32, (dv2, LANES), 0)
    sd = lax.broadcasted_iota(jnp.int32, (dv2, LANES), 1)
    st_mask = (se < GLA_VAL_DIM) == (sd < GLA_KEY_DIM)

    chunks = [slice(n * c, (n + 1) * c) for n in range(GLA_STEP // c)]
    la_hi, la_lo = _split_bf16(log_a)
    la_parts = jnp.concatenate([la_hi, la_lo], axis=1)
    bcum, dec = [], []
    for rows in chunks:
        res = _dot(tril, la_parts[rows])
        bcum.append(res[:, :LANES] + res[:, LANES:])
        dec.append(jnp.exp(bcum[-1][c - 1:c, :]))
    q_dec, k_inv, k_end = [], [], []
    for n, rows in enumerate(chunks):
        kf = k_ref[rows, :].astype(F32) * jnp.exp(-bcum[n])
        q_dec.append((q_ref[rows, :].astype(F32) * jnp.exp(bcum[n]) * (GLA_KEY_DIM ** -0.5)).astype(BF16))
        k_inv.append(kf.astype(BF16))
        k_end.append((kf * dec[n]).astype(BF16))
    lower2 = jnp.concatenate([lower, lower], axis=0)
    o_intra = []
    for n, rows in enumerate(chunks):
        zq = jnp.zeros_like(q_dec[n])
        q2 = jnp.concatenate([jnp.where(head0, q_dec[n], zq), jnp.where(head0, zq, q_dec[n])], axis=0)
        a = jnp.where(lower2, _dot_nt(q2, k_inv[n]), 0.0).astype(BF16)
        o_intra.append(jnp.concatenate(
            [_dot(a[:c], v_ref[rows, :GLA_VAL_DIM]), _dot(a[c:], v_ref[rows, GLA_VAL_DIM:])], axis=1))
    kv = [jnp.where(st_mask, _dot_tn(v_ref[rows, :], k_end[n]), 0.0) for n, rows in enumerate(chunks)]
    st = st_ref[...]
    starts = []
    for n in range(len(chunks)):
        starts.append(st.astype(BF16))
        st = st * dec[n] + kv[n]
    st_ref[...] = st
    for n, rows in enumerate(chunks):
        o = o_intra[n] + _dot_nt(q_dec[n], starts[n])
        o0 = o[:, :GLA_VAL_DIM]
        o1 = o[:, GLA_VAL_DIM:]
        n0 = o0 * lax.rsqrt(jnp.mean(o0 * o0, axis=-1, keepdims=True) + RMS_EPS)
        n1 = o1 * lax.rsqrt(jnp.mean(o1 * o1, axis=-1, keepdims=True) + RMS_EPS)
        rf = r_ref[rows, :].astype(F32)
        gate = rf / (1.0 + jnp.exp(-rf))
        o_ref[rows, :] = (jnp.concatenate([n0, n1], axis=1) * g_ref[...] * gate).astype(o_ref.dtype)


def _gla(q, k, v, r, lr, wg, bg, g):
    b, s, _ = q.shape
    pairs = GLA_HEADS // 2
    step = GLA_STEP
    dv2 = 2 * GLA_VAL_DIM
    seq_map = lambda bi, p, t: (bi, t, p)
    return pl.pallas_call(
        _gla_kernel,
        out_shape=jax.ShapeDtypeStruct((b, s, GLA_V_WIDTH), BF16),
        grid=(b, pairs, s // step),
        in_specs=[
            pl.BlockSpec((None, step, LANES), seq_map),
            pl.BlockSpec((None, step, LANES), seq_map),
            pl.BlockSpec((None, step, dv2), seq_map),
            pl.BlockSpec((None, step, dv2), seq_map),
            pl.BlockSpec((None, step, LANES), lambda bi, p, t: (bi, t, 0)),
            pl.BlockSpec((LANES, LANES), lambda bi, p, t: (0, p)),
            pl.BlockSpec((1, LANES), lambda bi, p, t: (0, p)),
            pl.BlockSpec((1, dv2), lambda bi, p, t: (0, p)),
        ],
        out_specs=pl.BlockSpec((None, step, dv2), seq_map),
        scratch_shapes=[pltpu.VMEM((dv2, LANES), F32)],
        compiler_params=pltpu.CompilerParams(
            dimension_semantics=("arbitrary", "arbitrary", "arbitrary"),
            vmem_limit_bytes=VMEM_LIMIT_BYTES),
        name="gla",
    )(q, k, v, r, lr, wg, bg, g)


def _post_kernel(alpha, tiles_per_seq, osb_ref, ogla_ref, x_ref, mod_ref, wo_hbm, g1_ref, b1_ref,
                 wg_hbm, wu_hbm, cw_ref, cb_ref, wd_hbm, g_ref, b_ref,
                 o_ref, halo_ref, act_ref, x1_ref, wo_ref, wg_ref, wu_ref, wd_ref, stage_ref, sem):
    tm = x_ref.shape[0]
    d_ff = wg_ref.shape[1]
    n_sb = osb_ref.shape[1]
    first = (pl.program_id(0) % tiles_per_seq) == 0

    @pl.when(pl.program_id(0) == 0)
    def _():
        _stage_weights_bf16(((wo_hbm, wo_ref), (wg_hbm, wg_ref), (wu_hbm, wu_ref), (wd_hbm, wd_ref)),
                            stage_ref, sem)

    mix = _dot(osb_ref[...], wo_ref[:n_sb, :]) + _dot(ogla_ref[...], wo_ref[n_sb:, :])
    x1_ref[...] = _layer_norm(alpha * x_ref[...] + (1.0 + mod_ref[2:3, :]) * mix,
                              g1_ref[...], b1_ref[...])
    h = (x1_ref[...] * (1.0 + mod_ref[4:5, :]) + mod_ref[3:4, :]).astype(BF16)
    top = lax.broadcasted_iota(jnp.int32, (SUBLANES, FF_TILE), 0)

    @pl.when(first)
    def _():
        halo_ref[...] = jnp.zeros_like(halo_ref)

    for f in range(d_ff // FF_TILE):
        cols = slice(f * FF_TILE, (f + 1) * FF_TILE)
        gt = _dot(h, wg_ref[:, cols])
        up = _dot(h, wu_ref[:, cols])
        prev = halo_ref[:, cols]
        halo_ref[:, cols] = gt[tm - SUBLANES:, :]
        r1 = pltpu.roll(gt, 1, 0)
        r2 = pltpu.roll(gt, 2, 0)
        t1 = jnp.where(top < 1, pltpu.roll(prev, 1, 0), r1[:SUBLANES])
        t2 = jnp.where(top < 2, pltpu.roll(prev, 2, 0), r2[:SUBLANES])
        g1 = jnp.concatenate([t1, r1[SUBLANES:]], axis=0)
        g2 = jnp.concatenate([t2, r2[SUBLANES:]], axis=0)
        conv = g2 * cw_ref[0:1, cols] + g1 * cw_ref[1:2, cols] + gt * cw_ref[2:3, cols] + cb_ref[:, cols]
        act_ref[:, cols] = (conv / (1.0 + jnp.exp(-conv)) * up).astype(BF16)
    y = _dot(act_ref[...], wd_ref[...])
    gate_f = mod_ref[5:6, :]
    o_ref[...] = _layer_norm(alpha * x1_ref[...] + (1.0 + gate_f) * y, g_ref[...], b_ref[...])


def _post(o_sb, o_gla, x2d, mod, w_o, ln1_g, ln1_b, wg, wu, cw, cb, wd, ln_g, ln_b, seq, alpha):
    t, d = x2d.shape
    d_ff = wg.shape[1]
    tm = ROW_TILE
    tiles_per_seq = seq // tm
    row_map = lambda i: (i, 0)
    fixed = lambda i: (0, 0)
    hbm = pl.BlockSpec(memory_space=pl.ANY)
    stage_width = max(w.shape[1] for w in (w_o, wg, wu, wd))
    return pl.pallas_call(
        functools.partial(_post_kernel, alpha, tiles_per_seq),
        out_shape=jax.ShapeDtypeStruct((t, d), F32),
        grid=(t // tm,),
        in_specs=[
            pl.BlockSpec((tm, o_sb.shape[1]), row_map),
            pl.BlockSpec((tm, o_gla.shape[1]), row_map),
            pl.BlockSpec((tm, d), row_map),
            pl.BlockSpec((None, SUBLANES, d), lambda i: (i // tiles_per_seq, 0, 0)),
            hbm,
            pl.BlockSpec((1, d), fixed),
            pl.BlockSpec((1, d), fixed),
            hbm,
            hbm,
            pl.BlockSpec(cw.shape, fixed),
            pl.BlockSpec(cb.shape, fixed),
            hbm,
            pl.BlockSpec((1, d), fixed),
            pl.BlockSpec((1, d), fixed),
        ],
        out_specs=pl.BlockSpec((tm, d), row_map),
        scratch_shapes=[pltpu.VMEM((SUBLANES, d_ff), F32), pltpu.VMEM((tm, d_ff), BF16),
                        pltpu.VMEM((tm, d), F32),
                        pltpu.VMEM(w_o.shape, BF16), pltpu.VMEM(wg.shape, BF16),
                        pltpu.VMEM(wu.shape, BF16), pltpu.VMEM(wd.shape, BF16),
                        pltpu.VMEM((2, STAGE_ROWS, stage_width), F32),
                        pltpu.SemaphoreType.DMA((2,))],
        compiler_params=pltpu.CompilerParams(
            dimension_semantics=("arbitrary",), vmem_limit_bytes=VMEM_LIMIT_BYTES),
        name="post",
    )(o_sb, o_gla, x2d, mod, w_o, ln1_g, ln1_b, wg, wu, cw, cb, wd, ln_g, ln_b)


def kernel(x, c, w_ada, b_ada, w_in, gla_w_gate, gla_b_gate, sb_norm_g, gla_norm_g, w_out,
           ln1_g, ln1_b, w_ff_gate, w_ff_up, conv_w, conv_b, w_down, ln2_g, ln2_b):
    bsz, seq, d = x.shape
    depth = w_ada.shape[0]
    alpha = float((2 * depth) ** 0.25)
    t = bsz * seq

    c_pad = jnp.concatenate([c, jnp.zeros((SUBLANES - bsz, d), c.dtype)], axis=0)
    xcur = x.reshape(t, d)
    for l in range(depth):
        mod = _ada(c_pad, w_ada[l], b_ada[l][None, :])
        mod = mod[:bsz].reshape(bsz, N_MOD, d)
        mod = jnp.concatenate([mod, jnp.zeros((bsz, SUBLANES - N_MOD, d), F32)], axis=1)

        sb_q, sb_k, sb_v, g_q, g_k, g_v, g_r, g_lr = _inproj(xcur, mod, w_in[l], seq)
        as_seq = lambda a: a.reshape(bsz, seq, a.shape[-1])

        o_sb = _sb_attention(as_seq(sb_q), as_seq(sb_k), as_seq(sb_v), sb_norm_g[l][None, :])
        wg_pad = jnp.concatenate(
            [gla_w_gate[l], jnp.zeros((LANES - GLA_GATE_RANK, GLA_QK_WIDTH), F32)], axis=0).astype(BF16)
        o_gla = _gla(as_seq(g_q), as_seq(g_k), as_seq(g_v), as_seq(g_r), as_seq(g_lr),
                     wg_pad, gla_b_gate[l][None, :], gla_norm_g[l][None, :])

        xcur = _post(o_sb.reshape(t, SB_WIDTH), o_gla.reshape(t, GLA_V_WIDTH), xcur, mod,
                     w_out[l], ln1_g[l][None, :], ln1_b[l][None, :],
                     w_ff_gate[l], w_ff_up[l], conv_w[l], conv_b[l][None, :], w_down[l],
                     ln2_g[l][None, :], ln2_b[l][None, :], seq, alpha)
    return xcur.reshape(bsz, seq, d)
```

```python
import functools

import jax
import jax.numpy as jnp
from jax import lax
from jax.experimental import pallas as pl
from jax.experimental.pallas import tpu as pltpu

F32 = jnp.float32
BF16 = jnp.bfloat16

LANES = 128
SUBLANES = 8
VMEM_LIMIT_BYTES = 56 * 1024 * 1024

SB_HEADS = 8
SB_HEAD_DIM = 64
GLA_HEADS = 4
GLA_KEY_DIM = 64
GLA_VAL_DIM = 128
GLA_GATE_RANK = 16
GLA_GATE_TEMP = 16.0
GLA_CHUNK = 64
CONV_WIDTH = 3
N_MOD = 6
LN_EPS = 1e-5
RMS_EPS = 1e-6

SB_WIDTH = SB_HEADS * SB_HEAD_DIM
GLA_QK_WIDTH = GLA_HEADS * GLA_KEY_DIM
GLA_V_WIDTH = GLA_HEADS * GLA_VAL_DIM

SB_EXIT = 104.0
LOG2E = 1.4426950408889634
SB_EXIT_BITS = SB_EXIT * LOG2E
SB_MAX_BITS = 126.0
SB_MASKED_BITS = -1e30
SB_TK = 128
SB_TQ = 4 * SB_TK
SB_WIN_BLOCKS = 3
SB_WIN = SB_WIN_BLOCKS * SB_TK

GLA_STEP = 1024
INPROJ_TILE = 1024
ROW_TILE = 512
FF_TILE = 256
STAGE_ROWS = 128
STAGE_SLOTS = 4


def _sb_logits(q, k):
    return jnp.minimum(_dot_nt(q, k), SB_MAX_BITS)


def _softplus2(z):
    return jnp.log2(1.0 + jnp.exp2(z))


def _split_bf16(a):
    hi = a.astype(BF16)
    lo = (a - hi.astype(F32)).astype(BF16)
    return hi, lo


def _dot(a, b):
    return jnp.dot(a, b, preferred_element_type=F32)


def _dot_nt(a, b):
    return lax.dot_general(a, b, (((1,), (1,)), ((), ())), preferred_element_type=F32)


def _dot_tn(a, b):
    return lax.dot_general(a, b, (((0,), (0,)), ((), ())), preferred_element_type=F32)


def _stage_weights_bf16(pairs, stage_ref, sem):
    slots, rows = stage_ref.shape[0], stage_ref.shape[1]
    jobs = [(src, dst, r0) for src, dst in pairs for r0 in range(0, src.shape[0], rows)]

    def window(n):
        return stage_ref.at[n % slots, :, pl.ds(0, jobs[n][0].shape[1])]

    def copy(n):
        src, _, r0 = jobs[n]
        return pltpu.make_async_copy(src.at[pl.ds(r0, rows), :], window(n), sem.at[n % slots])

    for n in range(min(slots - 1, len(jobs))):
        copy(n).start()
    for n, (_, dst, r0) in enumerate(jobs):
        if n + slots - 1 < len(jobs):
            copy(n + slots - 1).start()
        copy(n).wait()
        dst[pl.ds(r0, rows), :] = window(n)[...].astype(BF16)


def _layer_norm(v, g, b):
    mu = jnp.mean(v, axis=-1, keepdims=True)
    vc = v - mu
    var = jnp.mean(vc * vc, axis=-1, keepdims=True)
    return vc * lax.rsqrt(var + LN_EPS) * g + b


def _ada_kernel(c_ref, w_ref, b_ref, o_ref):
    c = c_ref[...]
    ca = c / (1.0 + jnp.exp(-c))
    ch, cl = _split_bf16(ca)
    wh, wl = _split_bf16(w_ref[...])
    o_ref[...] = _dot(ch, wh) + _dot(cl, wh) + _dot(ch, wl) + b_ref[...]


def _ada(c_pad, w, b):
    d, n = w.shape
    tn = 1024
    return pl.pallas_call(
        _ada_kernel,
        out_shape=jax.ShapeDtypeStruct((c_pad.shape[0], n), F32),
        grid=(n // tn,),
        in_specs=[
            pl.BlockSpec((c_pad.shape[0], d), lambda j: (0, 0)),
            pl.BlockSpec((d, tn), lambda j: (0, j)),
            pl.BlockSpec((1, tn), lambda j: (0, j)),
        ],
        out_specs=pl.BlockSpec((c_pad.shape[0], tn), lambda j: (0, j)),
        compiler_params=pltpu.CompilerParams(
            dimension_semantics=("arbitrary",), vmem_limit_bytes=VMEM_LIMIT_BYTES),
        name="ada",
    )(c_pad, w, b)


_MAIN_WIDTH = 3 * SB_WIDTH + 2 * GLA_QK_WIDTH + GLA_V_WIDTH
_PROJ_GROUPS = (
    (0, 0, SB_WIDTH, SB_HEAD_DIM ** -0.5 * LOG2E),
    (0, SB_WIDTH, SB_WIDTH, 1.0),
    (0, 2 * SB_WIDTH, SB_WIDTH, 1.0),
    (0, 3 * SB_WIDTH, GLA_QK_WIDTH, 1.0),
    (0, 3 * SB_WIDTH + GLA_QK_WIDTH, GLA_QK_WIDTH, 1.0),
    (0, 3 * SB_WIDTH + 2 * GLA_QK_WIDTH, GLA_V_WIDTH, 1.0),
    (1, 0, GLA_V_WIDTH, 1.0),
    (2, 0, LANES, 1.0),
)


def _stage_w_in(wt_hbm, w_main_ref, w_r_ref, w_lr_ref, stage_ref, sem):
    slots, rows, d = stage_ref.shape
    lr0 = _MAIN_WIDTH
    r0 = lr0 + GLA_GATE_RANK
    jobs = [(c, w_main_ref, c, rows) for c in range(0, lr0, rows)]
    jobs += [(lr0, w_lr_ref, 0, GLA_GATE_RANK)]
    jobs += [(r0 + c, w_r_ref, c, rows) for c in range(0, GLA_V_WIDTH, rows)]
    lane = lax.broadcasted_iota(jnp.int32, (d, rows), 1)

    def copy(n):
        return pltpu.make_async_copy(wt_hbm.at[pl.ds(jobs[n][0], rows), :],
                                     stage_ref.at[n % slots], sem.at[n % slots])

    for n in range(min(slots - 1, len(jobs))):
        copy(n).start()
    for n, (_, dst, col, valid) in enumerate(jobs):
        if n + slots - 1 < len(jobs):
            copy(n + slots - 1).start()
        copy(n).wait()
        blk = stage_ref[n % slots].T
        if valid < rows:
            blk = jnp.where(lane < valid, blk, 0.0)
        dst[:, pl.ds(col, rows)] = blk.astype(BF16)


def _inproj_kernel(layer, x_ref, mod_ref, w_hbm, *refs):
    n_out = len(_PROJ_GROUPS)
    out_refs = refs[:n_out]
    w_main_ref, w_r_ref, w_lr_ref, stage_ref, sem = refs[n_out:]

    @pl.when(pl.program_id(0) == 0)
    def _():
        _stage_w_in(w_hbm.at[layer], w_main_ref, w_r_ref, w_lr_ref, stage_ref, sem)

    shift = mod_ref[0:1, :]
    scale = mod_ref[1:2, :]
    h = (x_ref[...] * (1.0 + scale) + shift).astype(BF16)
    w_refs = (w_main_ref, w_r_ref, w_lr_ref)
    for (src, c0, width, mul), o_ref in zip(_PROJ_GROUPS, out_refs):
        acc = _dot(h, w_refs[src][:, c0:c0 + width])
        if mul != 1.0:
            acc = acc * mul
        o_ref[...] = acc.astype(o_ref.dtype)


def _inproj(x2d, mod, w_in_t, seq, layer):
    t, d = x2d.shape
    tm = INPROJ_TILE
    tiles_per_seq = seq // tm
    widths = [g[2] for g in _PROJ_GROUPS]
    return pl.pallas_call(
        functools.partial(_inproj_kernel, layer),
        out_shape=[jax.ShapeDtypeStruct((t, width), BF16) for width in widths],
        grid=(t // tm,),
        in_specs=[
            pl.BlockSpec((tm, d), lambda i: (i, 0)),
            pl.BlockSpec((None, SUBLANES, d), lambda i: (i // tiles_per_seq, 0, 0)),
            pl.BlockSpec(memory_space=pl.ANY),
        ],
        out_specs=[pl.BlockSpec((tm, width), lambda i: (i, 0)) for width in widths],
        scratch_shapes=[pltpu.VMEM((d, _MAIN_WIDTH), BF16), pltpu.VMEM((d, GLA_V_WIDTH), BF16),
                        pltpu.VMEM((d, LANES), BF16),
                        pltpu.VMEM((STAGE_SLOTS, LANES, d), F32),
                        pltpu.SemaphoreType.DMA((STAGE_SLOTS,))],
        compiler_params=pltpu.CompilerParams(
            dimension_semantics=("arbitrary",), vmem_limit_bytes=VMEM_LIMIT_BYTES),
        name="inproj",
    )(x2d, mod, w_in_t)


def _sb_suffix_sums(sp_blocks, umat):
    m = sp_blocks[0].shape[0]
    res = _dot(jnp.concatenate(sp_blocks, axis=0).astype(BF16), umat)
    cs, nearer = [], None
    for c in reversed(range(len(sp_blocks))):
        part = res[c * m:(c + 1) * m]
        loc, tot = part[:, :SB_TK], part[:, SB_TK:]
        cs.append(loc if nearer is None else loc + nearer)
        nearer = tot if nearer is None else nearer + tot
    return cs[::-1], nearer


def _sb_kernel(q_ref, k_ref, v_ref, g_ref, o_ref):
    tk = SB_TK
    pairs = q_ref.shape[1] // LANES
    nrb = SB_TQ // tk
    nb = SB_WIN_BLOCKS
    lane = lax.broadcasted_iota(jnp.int32, (tk, LANES), 1)
    head0 = lane < SB_HEAD_DIM

    uj = lax.broadcasted_iota(jnp.int32, (tk, 2 * tk), 0)
    us = lax.broadcasted_iota(jnp.int32, (tk, 2 * tk), 1)
    umat = jnp.where((uj >= us) | (us >= tk), 1.0, 0.0).astype(BF16)
    row = lax.broadcasted_iota(jnp.int32, (2 * tk, tk), 0) & (tk - 1)
    col = lax.broadcasted_iota(jnp.int32, (2 * tk, tk), 1)
    causal = col < row
    lanes = [slice(p * LANES, (p + 1) * LANES) for p in range(pairs)]

    first_blk = [pl.program_id(1) * nrb + r - (nb - 1) for r in range(nrb)]
    blk_rows = [[pl.ds(pl.multiple_of(jnp.maximum(first_blk[r] + c, 0) * tk, tk), tk)
                 for c in range(nb)] for r in range(nrb)]
    units = [(r, p) for r in range(nrb) for p in range(pairs)]

    qs, zs, sums, ws, accs = {}, {}, {}, {}, {}

    def stage_logits(u):
        r, p = u
        q = q_ref[r * tk:(r + 1) * tk, lanes[p]]
        zq = jnp.zeros_like(q)
        qs[u] = jnp.concatenate([jnp.where(head0, q, zq), jnp.where(head0, zq, q)], axis=0)
        kw = jnp.concatenate([k_ref[rows, lanes[p]] for rows in blk_rows[r]], axis=0)
        z = _sb_logits(qs[u], kw)
        zs[u] = [z[:, c * tk:(c + 1) * tk] for c in range(nb)]
        zs[u][-1] = jnp.where(causal, zs[u][-1], SB_MASKED_BITS)

    def stage_sums(u):
        sums[u] = _sb_suffix_sums([_softplus2(z) for z in zs[u]], umat)

    def stage_weights(u):
        args = [z - cs for z, cs in zip(zs[u], sums[u][0])]
        ws[u] = jnp.exp2(jnp.concatenate(args, axis=1)).astype(BF16)

    def stage_values(u):
        r, p = u
        vw = [v_ref[rows, lanes[p]] for rows in blk_rows[r]]
        vw = [jnp.where(first_blk[r] + c >= 0, vw[c], jnp.zeros_like(vw[c]))
              for c in range(nb - 1)] + vw[-1:]
        accs[u] = _dot(ws[u], jnp.concatenate(vw, axis=0))

    stages = (stage_logits, stage_sums, stage_weights, stage_values)
    for t in range(len(stages) + len(units) - 1):
        for n, u in enumerate(units):
            if 0 <= t - n < len(stages):
                stages[t - n](u)

    def not_done(cs_):
        m = cs_[0]
        for c in cs_[1:]:
            m = jnp.minimum(m, c)
        return (jnp.min(m) < SB_EXIT_BITS).astype(jnp.int32)

    def cond(s):
        return jnp.logical_and(s[0] >= 0, s[1] > 0)

    def write_out(r, accs_):
        outs = []
        for p in range(pairs):
            o = jnp.where(head0, accs_[p][:tk], accs_[p][tk:])
            sq = o * o
            ss0 = jnp.sum(jnp.where(head0, sq, 0.0), axis=-1, keepdims=True)
            ss1 = jnp.sum(jnp.where(head0, 0.0, sq), axis=-1, keepdims=True)
            ms = jnp.where(head0, ss0, ss1) * (1.0 / SB_HEAD_DIM)
            outs.append(o * lax.rsqrt(ms + RMS_EPS))
        o_ref[r * tk:(r + 1) * tk, :] = (jnp.concatenate(outs, axis=1) * g_ref[...]).astype(o_ref.dtype)

    for r in range(nrb):
        write_out(r, [accs[r, p] for p in range(pairs)])
    for r in range(nrb):
        carries = tuple(sums[r, p][1] for p in range(pairs))
        more = not_done(carries)

        def body(s, r=r):
            j, _, cs_, as_ = s
            rows = pl.ds(pl.multiple_of(j * tk, tk), tk)
            zb = [_sb_logits(qs[r, p], k_ref[rows, lanes[p]]) for p in range(pairs)]
            sb = [_sb_suffix_sums([_softplus2(zb[p])], umat) for p in range(pairs)]
            wb = [jnp.exp2(zb[p] - sb[p][0][0] - cs_[p]).astype(BF16) for p in range(pairs)]
            new_a = [as_[p] + _dot(wb[p], v_ref[rows, lanes[p]]) for p in range(pairs)]
            new_c = [cs_[p] + sb[p][1] for p in range(pairs)]
            return j - 1, not_done(new_c), tuple(new_c), tuple(new_a)

        @pl.when(jnp.logical_and(first_blk[r] - 1 >= 0, more > 0))
        def _(r=r, carries=carries, more=more, body=body):
            _, _, _, accs_ = lax.while_loop(
                cond, body,
                (first_blk[r] - 1, more, carries, tuple(accs[r, p] for p in range(pairs))))
            write_out(r, accs_)


def _sb_attention(q, k, v, g):
    b, s, wdt = q.shape
    return pl.pallas_call(
        _sb_kernel,
        out_shape=jax.ShapeDtypeStruct((b, s, wdt), BF16),
        grid=(b, s // SB_TQ),
        in_specs=[
            pl.BlockSpec((None, SB_TQ, wdt), lambda bi, i: (bi, i, 0)),
            pl.BlockSpec((None, s, wdt), lambda bi, i: (bi, 0, 0)),
            pl.BlockSpec((None, s, wdt), lambda bi, i: (bi, 0, 0)),
            pl.BlockSpec((1, wdt), lambda bi, i: (0, 0)),
        ],
        out_specs=pl.BlockSpec((None, SB_TQ, wdt), lambda bi, i: (bi, i, 0)),
        compiler_params=pltpu.CompilerParams(
            dimension_semantics=("arbitrary", "arbitrary"),
            vmem_limit_bytes=VMEM_LIMIT_BYTES),
        name="sb_attention",
    )(q, k, v, g)


def _gla_kernel(q_ref, k_ref, v_ref, r_ref, lr_ref, wg_ref, bg_ref, g_ref, o_ref, st_ref):
    c = GLA_CHUNK
    dv2 = 2 * GLA_VAL_DIM

    @pl.when(pl.program_id(2) == 0)
    def _():
        st_ref[...] = jnp.zeros_like(st_ref)

    u = _dot(lr_ref[...], wg_ref[...]) + bg_ref[...]
    log_a = (jnp.minimum(u, 0.0) - jnp.log(1.0 + jnp.exp(-jnp.abs(u)))) * (1.0 / GLA_GATE_TEMP)

    ti = lax.broadcasted_iota(jnp.int32, (c, c), 0)
    tj = lax.broadcasted_iota(jnp.int32, (c, c), 1)
    lower = ti >= tj
    tril = jnp.where(lower, 1.0, 0.0).astype(BF16)
    lane = lax.broadcasted_iota(jnp.int32, (c, LANES), 1)
    head0 = lane < GLA_KEY_DIM
    se = lax.broadcasted_iota(jnp.int32, (dv2, LANES), 0)
    sd = lax.broadcasted_iota(jnp.int32, (dv2, LANES), 1)
    st_mask = (se < GLA_VAL_DIM) == (sd < GLA_KEY_DIM)

    chunks = [slice(n * c, (n + 1) * c) for n in range(GLA_STEP // c)]
    la_hi, la_lo = _split_bf16(log_a)
    la_parts = jnp.concatenate([la_hi, la_lo], axis=1)
    bcum, dec = [], []
    for rows in chunks:
        res = _dot(tril, la_parts[rows])
        bcum.append(res[:, :LANES] + res[:, LANES:])
        dec.append(jnp.exp(bcum[-1][c - 1:c, :]))
    q_dec, k_inv, k_end = [], [], []
    for n, rows in enumerate(chunks):
        kf = k_ref[rows, :].astype(F32) * jnp.exp(-bcum[n])
        q_dec.append((q_ref[rows, :].astype(F32) * jnp.exp(bcum[n]) * (GLA_KEY_DIM ** -0.5)).astype(BF16))
        k_inv.append(kf.astype(BF16))
        k_end.append((kf * dec[n]).astype(BF16))
    lower2 = jnp.concatenate([lower, lower], axis=0)
    o_intra = []
    for n, rows in enumerate(chunks):
        zq = jnp.zeros_like(q_dec[n])
        q2 = jnp.concatenate([jnp.where(head0, q_dec[n], zq), jnp.where(head0, zq, q_dec[n])], axis=0)
        a = jnp.where(lower2, _dot_nt(q2, k_inv[n]), 0.0).astype(BF16)
        o_intra.append(jnp.concatenate(
            [_dot(a[:c], v_ref[rows, :GLA_VAL_DIM]), _dot(a[c:], v_ref[rows, GLA_VAL_DIM:])], axis=1))
    kv = [jnp.where(st_mask, _dot_tn(v_ref[rows, :], k_end[n]), 0.0) for n, rows in enumerate(chunks)]
    st = st_ref[...]
    starts = []
    for n in range(len(chunks)):
        starts.append(st.astype(BF16))
        st = st * dec[n] + kv[n]
    st_ref[...] = st
    for n, rows in enumerate(chunks):
        o = o_intra[n] + _dot_nt(q_dec[n], starts[n])
        o0 = o[:, :GLA_VAL_DIM]
        o1 = o[:, GLA_VAL_DIM:]
        n0 = o0 * lax.rsqrt(jnp.mean(o0 * o0, axis=-1, keepdims=True) + RMS_EPS)
        n1 = o1 * lax.rsqrt(jnp.mean(o1 * o1, axis=-1, keepdims=True) + RMS_EPS)
        rf = r_ref[rows, :].astype(F32)
        gate = rf / (1.0 + jnp.exp(-rf))
        o_ref[rows, :] = (jnp.concatenate([n0, n1], axis=1) * g_ref[...] * gate).astype(o_ref.dtype)


def _gla(q, k, v, r, lr, wg, bg, g):
    b, s, _ = q.shape
    pairs = GLA_HEADS // 2
    step = GLA_STEP
    dv2 = 2 * GLA_VAL_DIM
    seq_map = lambda bi, p, t: (bi, t, p)
    return pl.pallas_call(
        _gla_kernel,
        out_shape=jax.ShapeDtypeStruct((b, s, GLA_V_WIDTH), BF16),
        grid=(b, pairs, s // step),
        in_specs=[
            pl.BlockSpec((None, step, LANES), seq_map),
            pl.BlockSpec((None, step, LANES), seq_map),
            pl.BlockSpec((None, step, dv2), seq_map),
            pl.BlockSpec((None, step, dv2), seq_map),
            pl.BlockSpec((None, step, LANES), lambda bi, p, t: (bi, t, 0)),
            pl.BlockSpec((LANES, LANES), lambda bi, p, t: (0, p)),
            pl.BlockSpec((1, LANES), lambda bi, p, t: (0, p)),
            pl.BlockSpec((1, dv2), lambda bi, p, t: (0, p)),
        ],
        out_specs=pl.BlockSpec((None, step, dv2), seq_map),
        scratch_shapes=[pltpu.VMEM((dv2, LANES), F32)],
        compiler_params=pltpu.CompilerParams(
            dimension_semantics=("arbitrary", "arbitrary", "arbitrary"),
            vmem_limit_bytes=VMEM_LIMIT_BYTES),
        name="gla",
    )(q, k, v, r, lr, wg, bg, g)


def _post_kernel(alpha, tiles_per_seq, layer, osb_ref, ogla_ref, x_ref, mod_ref, wo_hbm, g1_ref, b1_ref,
                 wg_hbm, wu_hbm, cw_ref, cb_ref, wd_hbm, g_ref, b_ref,
                 o_ref, halo_ref, act_ref, x1_ref, wo_ref, wg_ref, wu_ref, wd_ref, stage_ref, sem):
    tm = x_ref.shape[0]
    d_ff = wg_ref.shape[1]
    n_sb = osb_ref.shape[1]
    first = (pl.program_id(0) % tiles_per_seq) == 0

    @pl.when(pl.program_id(0) == 0)
    def _():
        _stage_weights_bf16(((wo_hbm.at[layer], wo_ref), (wg_hbm.at[layer], wg_ref),
                             (wu_hbm.at[layer], wu_ref), (wd_hbm.at[layer], wd_ref)), stage_ref, sem)

    mix = _dot(osb_ref[...], wo_ref[:n_sb, :]) + _dot(ogla_ref[...], wo_ref[n_sb:, :])
    x1_ref[...] = _layer_norm(alpha * x_ref[...] + (1.0 + mod_ref[2:3, :]) * mix,
                              g1_ref[...], b1_ref[...])
    h = (x1_ref[...] * (1.0 + mod_ref[4:5, :]) + mod_ref[3:4, :]).astype(BF16)
    top = lax.broadcasted_iota(jnp.int32, (SUBLANES, FF_TILE), 0)

    @pl.when(first)
    def _():
        halo_ref[...] = jnp.zeros_like(halo_ref)

    for f in range(d_ff // FF_TILE):
        cols = slice(f * FF_TILE, (f + 1) * FF_TILE)
        gt = _dot(h, wg_ref[:, cols])
        up = _dot(h, wu_ref[:, cols])
        prev = halo_ref[:, cols]
        halo_ref[:, cols] = gt[tm - SUBLANES:, :]
        r1 = pltpu.roll(gt, 1, 0)
        r2 = pltpu.roll(gt, 2, 0)
        t1 = jnp.where(top < 1, pltpu.roll(prev, 1, 0), r1[:SUBLANES])
        t2 = jnp.where(top < 2, pltpu.roll(prev, 2, 0), r2[:SUBLANES])
        g1 = jnp.concatenate([t1, r1[SUBLANES:]], axis=0)
        g2 = jnp.concatenate([t2, r2[SUBLANES:]], axis=0)
        conv = g2 * cw_ref[0:1, cols] + g1 * cw_ref[1:2, cols] + gt * cw_ref[2:3, cols] + cb_ref[:, cols]
        act_ref[:, cols] = (conv / (1.0 + jnp.exp(-conv)) * up).astype(BF16)
    y = _dot(act_ref[...], wd_ref[...])
    gate_f = mod_ref[5:6, :]
    o_ref[...] = _layer_norm(alpha * x1_ref[...] + (1.0 + gate_f) * y, g_ref[...], b_ref[...])


def _post(o_sb, o_gla, x2d, mod, w_o, ln1_g, ln1_b, wg, wu, cw, cb, wd, ln_g, ln_b, seq, alpha, layer):
    t, d = x2d.shape
    d_ff = wg.shape[2]
    tm = ROW_TILE
    tiles_per_seq = seq // tm
    row_map = lambda i: (i, 0)
    fixed = lambda i: (0, 0)
    hbm = pl.BlockSpec(memory_space=pl.ANY)
    stage_width = max(w.shape[2] for w in (w_o, wg, wu, wd))
    return pl.pallas_call(
        functools.partial(_post_kernel, alpha, tiles_per_seq, layer),
        out_shape=jax.ShapeDtypeStruct((t, d), F32),
        grid=(t // tm,),
        in_specs=[
            pl.BlockSpec((tm, o_sb.shape[1]), row_map),
            pl.BlockSpec((tm, o_gla.shape[1]), row_map),
            pl.BlockSpec((tm, d), row_map),
            pl.BlockSpec((None, SUBLANES, d), lambda i: (i // tiles_per_seq, 0, 0)),
            hbm,
            pl.BlockSpec((1, d), fixed),
            pl.BlockSpec((1, d), fixed),
            hbm,
            hbm,
            pl.BlockSpec(cw.shape, fixed),
            pl.BlockSpec(cb.shape, fixed),
            hbm,
            pl.BlockSpec((1, d), fixed),
            pl.BlockSpec((1, d), fixed),
        ],
        out_specs=pl.BlockSpec((tm, d), row_map),
        scratch_shapes=[pltpu.VMEM((SUBLANES, d_ff), F32), pltpu.VMEM((tm, d_ff), BF16),
                        pltpu.VMEM((tm, d), F32),
                        pltpu.VMEM(w_o.shape[1:], BF16), pltpu.VMEM(wg.shape[1:], BF16),
                        pltpu.VMEM(wu.shape[1:], BF16), pltpu.VMEM(wd.shape[1:], BF16),
                        pltpu.VMEM((STAGE_SLOTS, STAGE_ROWS, stage_width), F32),
                        pltpu.SemaphoreType.DMA((STAGE_SLOTS,))],
        compiler_params=pltpu.CompilerParams(
            dimension_semantics=("arbitrary",), vmem_limit_bytes=VMEM_LIMIT_BYTES),
        name="post",
    )(o_sb, o_gla, x2d, mod, w_o, ln1_g, ln1_b, wg, wu, cw, cb, wd, ln_g, ln_b)


def kernel(x, c, w_ada, b_ada, w_in, gla_w_gate, gla_b_gate, sb_norm_g, gla_norm_g, w_out,
           ln1_g, ln1_b, w_ff_gate, w_ff_up, conv_w, conv_b, w_down, ln2_g, ln2_b):
    bsz, seq, d = x.shape
    depth = w_ada.shape[0]
    alpha = float((2 * depth) ** 0.25)
    t = bsz * seq

    c_pad = jnp.concatenate([c, jnp.zeros((SUBLANES - bsz, d), c.dtype)], axis=0)
    w_in_t = jnp.swapaxes(w_in, 1, 2)
    xcur = x.reshape(t, d)
    for l in range(depth):
        mod = _ada(c_pad, w_ada[l], b_ada[l][None, :])
        mod = mod[:bsz].reshape(bsz, N_MOD, d)
        mod = jnp.concatenate([mod, jnp.zeros((bsz, SUBLANES - N_MOD, d), F32)], axis=1)

        sb_q, sb_k, sb_v, g_q, g_k, g_v, g_r, g_lr = _inproj(xcur, mod, w_in_t, seq, l)
        as_seq = lambda a: a.reshape(bsz, seq, a.shape[-1])

        o_sb = _sb_attention(as_seq(sb_q), as_seq(sb_k), as_seq(sb_v), sb_norm_g[l][None, :])
        wg_pad = jnp.concatenate(
            [gla_w_gate[l], jnp.zeros((LANES - GLA_GATE_RANK, GLA_QK_WIDTH), F32)], axis=0).astype(BF16)
        o_gla = _gla(as_seq(g_q), as_seq(g_k), as_seq(g_v), as_seq(g_r), as_seq(g_lr),
                     wg_pad, gla_b_gate[l][None, :], gla_norm_g[l][None, :])

        xcur = _post(o_sb.reshape(t, SB_WIDTH), o_gla.reshape(t, GLA_V_WIDTH), xcur, mod,
                     w_out, ln1_g[l][None, :], ln1_b[l][None, :],
                     w_ff_gate, w_ff_up, conv_w[l], conv_b[l][None, :], w_down,
                     ln2_g[l][None, :], ln2_b[l][None, :], seq, alpha, l)
    return xcur.reshape(bsz, seq, d)
```

```python
import functools

import jax
import jax.numpy as jnp
from jax import lax
from jax.experimental import pallas as pl
from jax.experimental.pallas import tpu as pltpu

F32 = jnp.float32
BF16 = jnp.bfloat16

LANES = 128
SUBLANES = 8
VMEM_LIMIT_BYTES = 60 * 1024 * 1024

SB_HEADS = 8
SB_HEAD_DIM = 64
GLA_HEADS = 4
GLA_KEY_DIM = 64
GLA_VAL_DIM = 128
GLA_GATE_RANK = 16
GLA_GATE_TEMP = 16.0
GLA_CHUNK = 64
CONV_WIDTH = 3
N_MOD = 6
LN_EPS = 1e-5
RMS_EPS = 1e-6

SB_WIDTH = SB_HEADS * SB_HEAD_DIM
GLA_QK_WIDTH = GLA_HEADS * GLA_KEY_DIM
GLA_V_WIDTH = GLA_HEADS * GLA_VAL_DIM

SB_EXIT = 104.0
LOG2E = 1.4426950408889634
SB_EXIT_BITS = SB_EXIT * LOG2E
SB_MAX_BITS = 126.0
SB_MASKED_BITS = -1e30
SB_TK = 128
SB_TQ = 4 * SB_TK
SB_WIN_BLOCKS = 3
SB_WIN = SB_WIN_BLOCKS * SB_TK

GLA_STEP = 1024
INPROJ_TILE = 1024
ROW_TILE = 1024
POST_SUB_TILE = 512
FF_TILE = 256
STAGE_ROWS = 128
STAGE_SLOTS = 3


def _sb_logits(q, k):
    return jnp.minimum(_dot_nt(q, k), SB_MAX_BITS)


def _softplus2(z):
    return jnp.log2(1.0 + jnp.exp2(z))


def _split_bf16(a):
    hi = a.astype(BF16)
    lo = (a - hi.astype(F32)).astype(BF16)
    return hi, lo


def _dot(a, b):
    return jnp.dot(a, b, preferred_element_type=F32)


def _dot_nt(a, b):
    return lax.dot_general(a, b, (((1,), (1,)), ((), ())), preferred_element_type=F32)


def _dot_tn(a, b):
    return lax.dot_general(a, b, (((0,), (0,)), ((), ())), preferred_element_type=F32)


def _stage_weights_bf16(pairs, stage_ref, sem):
    slots, rows = stage_ref.shape[0], stage_ref.shape[1]
    jobs = [(src, dst, r0) for src, dst in pairs for r0 in range(0, src.shape[0], rows)]

    def window(n):
        return stage_ref.at[n % slots, :, pl.ds(0, jobs[n][0].shape[1])]

    def copy(n):
        src, _, r0 = jobs[n]
        return pltpu.make_async_copy(src.at[pl.ds(r0, rows), :], window(n), sem.at[n % slots])

    for n in range(min(slots - 1, len(jobs))):
        copy(n).start()
    for n, (_, dst, r0) in enumerate(jobs):
        if n + slots - 1 < len(jobs):
            copy(n + slots - 1).start()
        copy(n).wait()
        dst[pl.ds(r0, rows), :] = window(n)[...].astype(BF16)


def _layer_norm(v, g, b):
    mu = jnp.mean(v, axis=-1, keepdims=True)
    vc = v - mu
    var = jnp.mean(vc * vc, axis=-1, keepdims=True)
    return vc * lax.rsqrt(var + LN_EPS) * g + b


def _ada_kernel(c_ref, w_ref, b_ref, o_ref):
    c = c_ref[...]
    ca = c / (1.0 + jnp.exp(-c))
    ch, cl = _split_bf16(ca)
    wh, wl = _split_bf16(w_ref[...])
    o_ref[...] = _dot(ch, wh) + _dot(cl, wh) + _dot(ch, wl) + b_ref[...]


def _ada(c_pad, w, b):
    d, n = w.shape
    tn = 1024
    return pl.pallas_call(
        _ada_kernel,
        out_shape=jax.ShapeDtypeStruct((c_pad.shape[0], n), F32),
        grid=(n // tn,),
        in_specs=[
            pl.BlockSpec((c_pad.shape[0], d), lambda j: (0, 0)),
            pl.BlockSpec((d, tn), lambda j: (0, j)),
            pl.BlockSpec((1, tn), lambda j: (0, j)),
        ],
        out_specs=pl.BlockSpec((c_pad.shape[0], tn), lambda j: (0, j)),
        compiler_params=pltpu.CompilerParams(
            dimension_semantics=("arbitrary",), vmem_limit_bytes=VMEM_LIMIT_BYTES),
        name="ada",
    )(c_pad, w, b)


_MAIN_WIDTH = 3 * SB_WIDTH + 2 * GLA_QK_WIDTH + GLA_V_WIDTH
_PROJ_GROUPS = (
    (0, 0, SB_WIDTH, SB_HEAD_DIM ** -0.5 * LOG2E),
    (0, SB_WIDTH, SB_WIDTH, 1.0),
    (0, 2 * SB_WIDTH, SB_WIDTH, 1.0),
    (0, 3 * SB_WIDTH, GLA_QK_WIDTH, 1.0),
    (0, 3 * SB_WIDTH + GLA_QK_WIDTH, GLA_QK_WIDTH, 1.0),
    (0, 3 * SB_WIDTH + 2 * GLA_QK_WIDTH, GLA_V_WIDTH, 1.0),
    (1, 0, GLA_V_WIDTH, 1.0),
    (2, 0, LANES, 1.0),
)


def _stage_w_in(wt_hbm, w_main_ref, w_r_ref, w_lr_ref, stage_ref, sem):
    slots, rows, d = stage_ref.shape
    lr0 = _MAIN_WIDTH
    r0 = lr0 + GLA_GATE_RANK
    jobs = [(c, w_main_ref, c, rows) for c in range(0, lr0, rows)]
    jobs += [(lr0, w_lr_ref, 0, GLA_GATE_RANK)]
    jobs += [(r0 + c, w_r_ref, c, rows) for c in range(0, GLA_V_WIDTH, rows)]
    lane = lax.broadcasted_iota(jnp.int32, (d, rows), 1)

    def copy(n):
        return pltpu.make_async_copy(wt_hbm.at[pl.ds(jobs[n][0], rows), :],
                                     stage_ref.at[n % slots], sem.at[n % slots])

    for n in range(min(slots - 1, len(jobs))):
        copy(n).start()
    for n, (_, dst, col, valid) in enumerate(jobs):
        if n + slots - 1 < len(jobs):
            copy(n + slots - 1).start()
        copy(n).wait()
        blk = stage_ref[n % slots].T
        if valid < rows:
            blk = jnp.where(lane < valid, blk, 0.0)
        dst[:, pl.ds(col, rows)] = blk.astype(BF16)


def _inproj_kernel(layer, x_ref, mod_ref, w_hbm, *refs):
    n_out = len(_PROJ_GROUPS)
    out_refs = refs[:n_out]
    w_main_ref, w_r_ref, w_lr_ref, stage_ref, sem = refs[n_out:]

    @pl.when(pl.program_id(0) == 0)
    def _():
        _stage_w_in(w_hbm.at[layer], w_main_ref, w_r_ref, w_lr_ref, stage_ref, sem)

    shift = mod_ref[0:1, :]
    scale = mod_ref[1:2, :]
    h = (x_ref[...] * (1.0 + scale) + shift).astype(BF16)
    w_refs = (w_main_ref, w_r_ref, w_lr_ref)
    for (src, c0, width, mul), o_ref in zip(_PROJ_GROUPS, out_refs):
        acc = _dot(h, w_refs[src][:, c0:c0 + width])
        if mul != 1.0:
            acc = acc * mul
        o_ref[...] = acc.astype(o_ref.dtype)


def _inproj(x2d, mod, w_in_t, seq, layer):
    t, d = x2d.shape
    tm = INPROJ_TILE
    tiles_per_seq = seq // tm
    widths = [g[2] for g in _PROJ_GROUPS]
    return pl.pallas_call(
        functools.partial(_inproj_kernel, layer),
        out_shape=[jax.ShapeDtypeStruct((t, width), BF16) for width in widths],
        grid=(t // tm,),
        in_specs=[
            pl.BlockSpec((tm, d), lambda i: (i, 0)),
            pl.BlockSpec((None, SUBLANES, d), lambda i: (i // tiles_per_seq, 0, 0)),
            pl.BlockSpec(memory_space=pl.ANY),
        ],
        out_specs=[pl.BlockSpec((tm, width), lambda i: (i, 0)) for width in widths],
        scratch_shapes=[pltpu.VMEM((d, _MAIN_WIDTH), BF16), pltpu.VMEM((d, GLA_V_WIDTH), BF16),
                        pltpu.VMEM((d, LANES), BF16),
                        pltpu.VMEM((STAGE_SLOTS, LANES, d), F32),
                        pltpu.SemaphoreType.DMA((STAGE_SLOTS,))],
        compiler_params=pltpu.CompilerParams(
            dimension_semantics=("arbitrary",), vmem_limit_bytes=VMEM_LIMIT_BYTES),
        name="inproj",
    )(x2d, mod, w_in_t)


def _sb_suffix_sums(sp_blocks, umat):
    m = sp_blocks[0].shape[0]
    res = _dot(jnp.concatenate(sp_blocks, axis=0).astype(BF16), umat)
    cs, nearer = [], None
    for c in reversed(range(len(sp_blocks))):
        part = res[c * m:(c + 1) * m]
        loc, tot = part[:, :SB_TK], part[:, SB_TK:]
        cs.append(loc if nearer is None else loc + nearer)
        nearer = tot if nearer is None else nearer + tot
    return cs[::-1], nearer


def _sb_kernel(q_ref, k_ref, v_ref, g_ref, o_ref):
    tk = SB_TK
    pairs = q_ref.shape[1] // LANES
    nrb = SB_TQ // tk
    nb = SB_WIN_BLOCKS
    lane = lax.broadcasted_iota(jnp.int32, (tk, LANES), 1)
    head0 = lane < SB_HEAD_DIM

    uj = lax.broadcasted_iota(jnp.int32, (tk, 2 * tk), 0)
    us = lax.broadcasted_iota(jnp.int32, (tk, 2 * tk), 1)
    umat = jnp.where((uj >= us) | (us >= tk), 1.0, 0.0).astype(BF16)
    row = lax.broadcasted_iota(jnp.int32, (2 * tk, tk), 0) & (tk - 1)
    col = lax.broadcasted_iota(jnp.int32, (2 * tk, tk), 1)
    causal = col < row
    lanes = [slice(p * LANES, (p + 1) * LANES) for p in range(pairs)]

    first_blk = [pl.program_id(1) * nrb + r - (nb - 1) for r in range(nrb)]
    blk_rows = [[pl.ds(pl.multiple_of(jnp.maximum(first_blk[r] + c, 0) * tk, tk), tk)
                 for c in range(nb)] for r in range(nrb)]
    units = [(r, p) for r in range(nrb) for p in range(pairs)]

    qs, zs, sums, ws, accs = {}, {}, {}, {}, {}

    def stage_logits(u):
        r, p = u
        q = q_ref[r * tk:(r + 1) * tk, lanes[p]]
        zq = jnp.zeros_like(q)
        qs[u] = jnp.concatenate([jnp.where(head0, q, zq), jnp.where(head0, zq, q)], axis=0)
        kw = jnp.concatenate([k_ref[rows, lanes[p]] for rows in blk_rows[r]], axis=0)
        z = _sb_logits(qs[u], kw)
        zs[u] = [z[:, c * tk:(c + 1) * tk] for c in range(nb)]
        zs[u][-1] = jnp.where(causal, zs[u][-1], SB_MASKED_BITS)

    def stage_sums(u):
        sums[u] = _sb_suffix_sums([_softplus2(z) for z in zs[u]], umat)

    def stage_weights(u):
        args = [z - cs for z, cs in zip(zs[u], sums[u][0])]
        ws[u] = jnp.exp2(jnp.concatenate(args, axis=1)).astype(BF16)

    def stage_values(u):
        r, p = u
        vw = [v_ref[rows, lanes[p]] for rows in blk_rows[r]]
        vw = [jnp.where(first_blk[r] + c >= 0, vw[c], jnp.zeros_like(vw[c]))
              for c in range(nb - 1)] + vw[-1:]
        accs[u] = _dot(ws[u], jnp.concatenate(vw, axis=0))

    stages = (stage_logits, stage_sums, stage_weights, stage_values)
    for t in range(len(stages) + len(units) - 1):
        for n, u in enumerate(units):
            if 0 <= t - n < len(stages):
                stages[t - n](u)

    def not_done(cs_):
        m = cs_[0]
        for c in cs_[1:]:
            m = jnp.minimum(m, c)
        return (jnp.min(m) < SB_EXIT_BITS).astype(jnp.int32)

    def cond(s):
        return jnp.logical_and(s[0] >= 0, s[1] > 0)

    def write_out(r, accs_):
        outs = []
        for p in range(pairs):
            o = jnp.where(head0, accs_[p][:tk], accs_[p][tk:])
            sq = o * o
            ss0 = jnp.sum(jnp.where(head0, sq, 0.0), axis=-1, keepdims=True)
            ss1 = jnp.sum(jnp.where(head0, 0.0, sq), axis=-1, keepdims=True)
            ms = jnp.where(head0, ss0, ss1) * (1.0 / SB_HEAD_DIM)
            outs.append(o * lax.rsqrt(ms + RMS_EPS))
        o_ref[r * tk:(r + 1) * tk, :] = (jnp.concatenate(outs, axis=1) * g_ref[...]).astype(o_ref.dtype)

    for r in range(nrb):
        write_out(r, [accs[r, p] for p in range(pairs)])
    for r in range(nrb):
        carries = tuple(sums[r, p][1] for p in range(pairs))
        more = not_done(carries)

        def body(s, r=r):
            j, _, cs_, as_ = s
            rows = pl.ds(pl.multiple_of(j * tk, tk), tk)
            zb = [_sb_logits(qs[r, p], k_ref[rows, lanes[p]]) for p in range(pairs)]
            sb = [_sb_suffix_sums([_softplus2(zb[p])], umat) for p in range(pairs)]
            wb = [jnp.exp2(zb[p] - sb[p][0][0] - cs_[p]).astype(BF16) for p in range(pairs)]
            new_a = [as_[p] + _dot(wb[p], v_ref[rows, lanes[p]]) for p in range(pairs)]
            new_c = [cs_[p] + sb[p][1] for p in range(pairs)]
            return j - 1, not_done(new_c), tuple(new_c), tuple(new_a)

        @pl.when(jnp.logical_and(first_blk[r] - 1 >= 0, more > 0))
        def _(r=r, carries=carries, more=more, body=body):
            _, _, _, accs_ = lax.while_loop(
                cond, body,
                (first_blk[r] - 1, more, carries, tuple(accs[r, p] for p in range(pairs))))
            write_out(r, accs_)


def _sb_attention(q, k, v, g):
    b, s, wdt = q.shape
    return pl.pallas_call(
        _sb_kernel,
        out_shape=jax.ShapeDtypeStruct((b, s, wdt), BF16),
        grid=(b, s // SB_TQ),
        in_specs=[
            pl.BlockSpec((None, SB_TQ, wdt), lambda bi, i: (bi, i, 0)),
            pl.BlockSpec((None, s, wdt), lambda bi, i: (bi, 0, 0)),
            pl.BlockSpec((None, s, wdt), lambda bi, i: (bi, 0, 0)),
            pl.BlockSpec((1, wdt), lambda bi, i: (0, 0)),
        ],
        out_specs=pl.BlockSpec((None, SB_TQ, wdt), lambda bi, i: (bi, i, 0)),
        compiler_params=pltpu.CompilerParams(
            dimension_semantics=("arbitrary", "arbitrary"),
            vmem_limit_bytes=VMEM_LIMIT_BYTES),
        name="sb_attention",
    )(q, k, v, g)


def _gla_kernel(q_ref, k_ref, v_ref, r_ref, lr_ref, wg_ref, bg_ref, g_ref, o_ref, st_ref):
    c = GLA_CHUNK
    dv2 = 2 * GLA_VAL_DIM

    @pl.when(pl.program_id(2) == 0)
    def _():
        st_ref[...] = jnp.zeros_like(st_ref)

    u = _dot(lr_ref[...], wg_ref[...]) + bg_ref[...]
    log_a = (jnp.minimum(u, 0.0) - jnp.log(1.0 + jnp.exp(-jnp.abs(u)))) * (1.0 / GLA_GATE_TEMP)

    ti = lax.broadcasted_iota(jnp.int32, (c, c), 0)
    tj = lax.broadcasted_iota(jnp.int32, (c, c), 1)
    lower = ti >= tj
    tril = jnp.where(lower, 1.0, 0.0).astype(BF16)
    lane = lax.broadcasted_iota(jnp.int32, (c, LANES), 1)
    head0 = lane < GLA_KEY_DIM
    se = lax.broadcasted_iota(jnp.int32, (dv2, LANES), 0)
    sd = lax.broadcasted_iota(jnp.int32, (dv2, LANES), 1)
    st_mask = (se < GLA_VAL_DIM) == (sd < GLA_KEY_DIM)

    chunks = [slice(n * c, (n + 1) * c) for n in range(GLA_STEP // c)]
    la_hi, la_lo = _split_bf16(log_a)
    la_parts = jnp.concatenate([la_hi, la_lo], axis=1)
    bcum, dec = [], []
    for rows in chunks:
        res = _dot(tril, la_parts[rows])
        bcum.append(res[:, :LANES] + res[:, LANES:])
        dec.append(jnp.exp(bcum[-1][c - 1:c, :]))
    q_dec, k_inv, k_end = [], [], []
    for n, rows in enumerate(chunks):
        kf = k_ref[rows, :].astype(F32) * jnp.exp(-bcum[n])
        q_dec.append((q_ref[rows, :].astype(F32) * jnp.exp(bcum[n]) * (GLA_KEY_DIM ** -0.5)).astype(BF16))
        k_inv.append(kf.astype(BF16))
        k_end.append((kf * dec[n]).astype(BF16))
    lower2 = jnp.concatenate([lower, lower], axis=0)
    o_intra = []
    for n, rows in enumerate(chunks):
        zq = jnp.zeros_like(q_dec[n])
        q2 = jnp.concatenate([jnp.where(head0, q_dec[n], zq), jnp.where(head0, zq, q_dec[n])], axis=0)
        a = jnp.where(lower2, _dot_nt(q2, k_inv[n]), 0.0).astype(BF16)
        o_intra.append(jnp.concatenate(
            [_dot(a[:c], v_ref[rows, :GLA_VAL_DIM]), _dot(a[c:], v_ref[rows, GLA_VAL_DIM:])], axis=1))
    kv = [jnp.where(st_mask, _dot_tn(v_ref[rows, :], k_end[n]), 0.0) for n, rows in enumerate(chunks)]
    st = st_ref[...]
    starts = []
    for n in range(len(chunks)):
        starts.append(st.astype(BF16))
        st = st * dec[n] + kv[n]
    st_ref[...] = st
    for n, rows in enumerate(chunks):
        o = o_intra[n] + _dot_nt(q_dec[n], starts[n])
        o0 = o[:, :GLA_VAL_DIM]
        o1 = o[:, GLA_VAL_DIM:]
        n0 = o0 * lax.rsqrt(jnp.mean(o0 * o0, axis=-1, keepdims=True) + RMS_EPS)
        n1 = o1 * lax.rsqrt(jnp.mean(o1 * o1, axis=-1, keepdims=True) + RMS_EPS)
        rf = r_ref[rows, :].astype(F32)
        gate = rf / (1.0 + jnp.exp(-rf))
        o_ref[rows, :] = (jnp.concatenate([n0, n1], axis=1) * g_ref[...] * gate).astype(o_ref.dtype)


def _gla(q, k, v, r, lr, wg, bg, g):
    b, s, _ = q.shape
    pairs = GLA_HEADS // 2
    step = GLA_STEP
    dv2 = 2 * GLA_VAL_DIM
    seq_map = lambda bi, p, t: (bi, t, p)
    return pl.pallas_call(
        _gla_kernel,
        out_shape=jax.ShapeDtypeStruct((b, s, GLA_V_WIDTH), BF16),
        grid=(b, pairs, s // step),
        in_specs=[
            pl.BlockSpec((None, step, LANES), seq_map),
            pl.BlockSpec((None, step, LANES), seq_map),
            pl.BlockSpec((None, step, dv2), seq_map),
            pl.BlockSpec((None, step, dv2), seq_map),
            pl.BlockSpec((None, step, LANES), lambda bi, p, t: (bi, t, 0)),
            pl.BlockSpec((LANES, LANES), lambda bi, p, t: (0, p)),
            pl.BlockSpec((1, LANES), lambda bi, p, t: (0, p)),
            pl.BlockSpec((1, dv2), lambda bi, p, t: (0, p)),
        ],
        out_specs=pl.BlockSpec((None, step, dv2), seq_map),
        scratch_shapes=[pltpu.VMEM((dv2, LANES), F32)],
        compiler_params=pltpu.CompilerParams(
            dimension_semantics=("arbitrary", "arbitrary", "arbitrary"),
            vmem_limit_bytes=VMEM_LIMIT_BYTES),
        name="gla",
    )(q, k, v, r, lr, wg, bg, g)


def _post_kernel(alpha, tiles_per_seq, layer, osb_ref, ogla_ref, x_ref, mod_ref, wo_hbm, g1_ref, b1_ref,
                 wg_hbm, wu_hbm, cw_ref, cb_ref, wd_hbm, g_ref, b_ref,
                 o_ref, halo_ref, act_ref, x1_ref, wo_ref, wg_ref, wu_ref, wd_ref, stage_ref, sem):
    tm = POST_SUB_TILE
    nsub = x_ref.shape[0] // tm
    d_ff = wg_ref.shape[1]
    n_sb = osb_ref.shape[1]
    first = (pl.program_id(0) % tiles_per_seq) == 0

    @pl.when(pl.program_id(0) == 0)
    def _():
        _stage_weights_bf16(((wo_hbm.at[layer], wo_ref), (wg_hbm.at[layer], wg_ref),
                             (wu_hbm.at[layer], wu_ref), (wd_hbm.at[layer], wd_ref)), stage_ref, sem)

    @pl.when(first)
    def _():
        halo_ref[...] = jnp.zeros_like(halo_ref)

    top = lax.broadcasted_iota(jnp.int32, (SUBLANES, FF_TILE), 0)
    hs = []
    for j in range(nsub):
        rows = slice(j * tm, (j + 1) * tm)
        mix = _dot(osb_ref[rows, :], wo_ref[:n_sb, :]) + _dot(ogla_ref[rows, :], wo_ref[n_sb:, :])
        x1_ref[rows, :] = _layer_norm(alpha * x_ref[rows, :] + (1.0 + mod_ref[2:3, :]) * mix,
                                      g1_ref[...], b1_ref[...])
        hs.append((x1_ref[rows, :] * (1.0 + mod_ref[4:5, :]) + mod_ref[3:4, :]).astype(BF16))
    for j in range(nsub):
        rows = slice(j * tm, (j + 1) * tm)
        h = hs[j]
        for f in range(d_ff // FF_TILE):
            cols = slice(f * FF_TILE, (f + 1) * FF_TILE)
            gt = _dot(h, wg_ref[:, cols])
            up = _dot(h, wu_ref[:, cols])
            prev = halo_ref[:, cols]
            halo_ref[:, cols] = gt[tm - SUBLANES:, :]
            r1 = pltpu.roll(gt, 1, 0)
            r2 = pltpu.roll(gt, 2, 0)
            t1 = jnp.where(top < 1, pltpu.roll(prev, 1, 0), r1[:SUBLANES])
            t2 = jnp.where(top < 2, pltpu.roll(prev, 2, 0), r2[:SUBLANES])
            g1 = jnp.concatenate([t1, r1[SUBLANES:]], axis=0)
            g2 = jnp.concatenate([t2, r2[SUBLANES:]], axis=0)
            conv = (g2 * cw_ref[0:1, cols] + g1 * cw_ref[1:2, cols] + gt * cw_ref[2:3, cols]
                    + cb_ref[:, cols])
            act_ref[rows, cols] = (conv / (1.0 + jnp.exp(-conv)) * up).astype(BF16)
        y = _dot(act_ref[rows, :], wd_ref[...])
        o_ref[rows, :] = _layer_norm(alpha * x1_ref[rows, :] + (1.0 + mod_ref[5:6, :]) * y,
                                     g_ref[...], b_ref[...])


def _post(o_sb, o_gla, x2d, mod, w_o, ln1_g, ln1_b, wg, wu, cw, cb, wd, ln_g, ln_b, seq, alpha, layer):
    t, d = x2d.shape
    d_ff = wg.shape[2]
    tm = ROW_TILE
    tiles_per_seq = seq // tm
    row_map = lambda i: (i, 0)
    fixed = lambda i: (0, 0)
    hbm = pl.BlockSpec(memory_space=pl.ANY)
    stage_width = max(w.shape[2] for w in (w_o, wg, wu, wd))
    return pl.pallas_call(
        functools.partial(_post_kernel, alpha, tiles_per_seq, layer),
        out_shape=jax.ShapeDtypeStruct((t, d), F32),
        grid=(t // tm,),
        in_specs=[
            pl.BlockSpec((tm, o_sb.shape[1]), row_map),
            pl.BlockSpec((tm, o_gla.shape[1]), row_map),
            pl.BlockSpec((tm, d), row_map),
            pl.BlockSpec((None, SUBLANES, d), lambda i: (i // tiles_per_seq, 0, 0)),
            hbm,
            pl.BlockSpec((1, d), fixed),
            pl.BlockSpec((1, d), fixed),
            hbm,
            hbm,
            pl.BlockSpec(cw.shape, fixed),
            pl.BlockSpec(cb.shape, fixed),
            hbm,
            pl.BlockSpec((1, d), fixed),
            pl.BlockSpec((1, d), fixed),
        ],
        out_specs=pl.BlockSpec((tm, d), row_map),
        scratch_shapes=[pltpu.VMEM((SUBLANES, d_ff), F32), pltpu.VMEM((tm, d_ff), BF16),
                        pltpu.VMEM((tm, d), F32),
                        pltpu.VMEM(w_o.shape[1:], BF16), pltpu.VMEM(wg.shape[1:], BF16),
                        pltpu.VMEM(wu.shape[1:], BF16), pltpu.VMEM(wd.shape[1:], BF16),
                        pltpu.VMEM((STAGE_SLOTS, STAGE_ROWS, stage_width), F32),
                        pltpu.SemaphoreType.DMA((STAGE_SLOTS,))],
        compiler_params=pltpu.CompilerParams(
            dimension_semantics=("arbitrary",), vmem_limit_bytes=VMEM_LIMIT_BYTES),
        name="post",
    )(o_sb, o_gla, x2d, mod, w_o, ln1_g, ln1_b, wg, wu, cw, cb, wd, ln_g, ln_b)


def kernel(x, c, w_ada, b_ada, w_in, gla_w_gate, gla_b_gate, sb_norm_g, gla_norm_g, w_out,
           ln1_g, ln1_b, w_ff_gate, w_ff_up, conv_w, conv_b, w_down, ln2_g, ln2_b):
    bsz, seq, d = x.shape
    depth = w_ada.shape[0]
    alpha = float((2 * depth) ** 0.25)
    t = bsz * seq

    c_pad = jnp.concatenate([c, jnp.zeros((SUBLANES - bsz, d), c.dtype)], axis=0)
    w_in_t = jnp.swapaxes(w_in, 1, 2)
    xcur = x.reshape(t, d)
    for l in range(depth):
        mod = _ada(c_pad, w_ada[l], b_ada[l][None, :])
        mod = mod[:bsz].reshape(bsz, N_MOD, d)
        mod = jnp.concatenate([mod, jnp.zeros((bsz, SUBLANES - N_MOD, d), F32)], axis=1)

        sb_q, sb_k, sb_v, g_q, g_k, g_v, g_r, g_lr = _inproj(xcur, mod, w_in_t, seq, l)
        as_seq = lambda a: a.reshape(bsz, seq, a.shape[-1])

        o_sb = _sb_attention(as_seq(sb_q), as_seq(sb_k), as_seq(sb_v), sb_norm_g[l][None, :])
        wg_pad = jnp.concatenate(
            [gla_w_gate[l], jnp.zeros((LANES - GLA_GATE_RANK, GLA_QK_WIDTH), F32)], axis=0).astype(BF16)
        o_gla = _gla(as_seq(g_q), as_seq(g_k), as_seq(g_v), as_seq(g_r), as_seq(g_lr),
                     wg_pad, gla_b_gate[l][None, :], gla_norm_g[l][None, :])

        xcur = _post(o_sb.reshape(t, SB_WIDTH), o_gla.reshape(t, GLA_V_WIDTH), xcur, mod,
                     w_out, ln1_g[l][None, :], ln1_b[l][None, :],
                     w_ff_gate, w_ff_up, conv_w[l], conv_b[l][None, :], w_down,
                     ln2_g[l][None, :], ln2_b[l][None, :], seq, alpha, l)
    return xcur.reshape(bsz, seq, d)
```

```python
import functools

import jax
import jax.numpy as jnp
from jax import lax
from jax.experimental import pallas as pl
from jax.experimental.pallas import tpu as pltpu

F32 = jnp.float32
BF16 = jnp.bfloat16

LANES = 128
SUBLANES = 8
VMEM_LIMIT_BYTES = 60 * 1024 * 1024

SB_HEADS = 8
SB_HEAD_DIM = 64
GLA_HEADS = 4
GLA_KEY_DIM = 64
GLA_VAL_DIM = 128
GLA_GATE_RANK = 16
GLA_GATE_TEMP = 16.0
GLA_CHUNK = 64
CONV_WIDTH = 3
N_MOD = 6
LN_EPS = 1e-5
RMS_EPS = 1e-6

SB_WIDTH = SB_HEADS * SB_HEAD_DIM
GLA_QK_WIDTH = GLA_HEADS * GLA_KEY_DIM
GLA_V_WIDTH = GLA_HEADS * GLA_VAL_DIM

SB_EXIT = 88.0
LOG2E = 1.4426950408889634
SB_EXIT_BITS = SB_EXIT * LOG2E
SB_MAX_BITS = 126.0
SB_MASKED_BITS = -1e30
SB_TK = 128
SB_TQ = 4 * SB_TK
SB_WIN_BLOCKS = 3
SB_WIN = SB_WIN_BLOCKS * SB_TK

GLA_STEP = 1024
INPROJ_TILE = 1024
ROW_TILE = 1024
POST_SUB_TILE = 512
FF_TILE = 256
STAGE_ROWS = 128
STAGE_SLOTS = 3


def _sb_logits(q, k):
    return jnp.minimum(_dot_nt(q, k), SB_MAX_BITS)


def _softplus2(z):
    return jnp.log2(1.0 + jnp.exp2(z))


def _split_bf16(a):
    hi = a.astype(BF16)
    lo = (a - hi.astype(F32)).astype(BF16)
    return hi, lo


def _dot(a, b):
    return jnp.dot(a, b, preferred_element_type=F32)


def _dot_nt(a, b):
    return lax.dot_general(a, b, (((1,), (1,)), ((), ())), preferred_element_type=F32)


def _dot_tn(a, b):
    return lax.dot_general(a, b, (((0,), (0,)), ((), ())), preferred_element_type=F32)


def _stage_weights_bf16(pairs, stage_ref, sem):
    slots, rows = stage_ref.shape[0], stage_ref.shape[1]
    jobs = [(src, dst, r0) for src, dst in pairs for r0 in range(0, src.shape[0], rows)]

    def window(n):
        return stage_ref.at[n % slots, :, pl.ds(0, jobs[n][0].shape[1])]

    def copy(n):
        src, _, r0 = jobs[n]
        return pltpu.make_async_copy(src.at[pl.ds(r0, rows), :], window(n), sem.at[n % slots])

    for n in range(min(slots - 1, len(jobs))):
        copy(n).start()
    for n, (_, dst, r0) in enumerate(jobs):
        if n + slots - 1 < len(jobs):
            copy(n + slots - 1).start()
        copy(n).wait()
        dst[pl.ds(r0, rows), :] = window(n)[...].astype(BF16)


def _layer_norm(v, g, b):
    mu = jnp.mean(v, axis=-1, keepdims=True)
    vc = v - mu
    var = jnp.mean(vc * vc, axis=-1, keepdims=True)
    return vc * lax.rsqrt(var + LN_EPS) * g + b


def _ada_kernel(c_ref, w_ref, b_ref, o_ref):
    c = c_ref[...]
    ca = c / (1.0 + jnp.exp(-c))
    ch, cl = _split_bf16(ca)
    wh, wl = _split_bf16(w_ref[...])
    o_ref[...] = _dot(ch, wh) + _dot(cl, wh) + _dot(ch, wl) + b_ref[...]


def _ada(c_pad, w, b):
    d, n = w.shape
    tn = 1024
    return pl.pallas_call(
        _ada_kernel,
        out_shape=jax.ShapeDtypeStruct((c_pad.shape[0], n), F32),
        grid=(n // tn,),
        in_specs=[
            pl.BlockSpec((c_pad.shape[0], d), lambda j: (0, 0)),
            pl.BlockSpec((d, tn), lambda j: (0, j)),
            pl.BlockSpec((1, tn), lambda j: (0, j)),
        ],
        out_specs=pl.BlockSpec((c_pad.shape[0], tn), lambda j: (0, j)),
        compiler_params=pltpu.CompilerParams(
            dimension_semantics=("arbitrary",), vmem_limit_bytes=VMEM_LIMIT_BYTES),
        name="ada",
    )(c_pad, w, b)


_MAIN_WIDTH = 3 * SB_WIDTH + 2 * GLA_QK_WIDTH + GLA_V_WIDTH
_PROJ_GROUPS = (
    (0, 0, SB_WIDTH, SB_HEAD_DIM ** -0.5 * LOG2E),
    (0, SB_WIDTH, SB_WIDTH, 1.0),
    (0, 2 * SB_WIDTH, SB_WIDTH, 1.0),
    (0, 3 * SB_WIDTH, GLA_QK_WIDTH, 1.0),
    (0, 3 * SB_WIDTH + GLA_QK_WIDTH, GLA_QK_WIDTH, 1.0),
    (0, 3 * SB_WIDTH + 2 * GLA_QK_WIDTH, GLA_V_WIDTH, 1.0),
    (1, 0, GLA_V_WIDTH, 1.0),
    (2, 0, LANES, 1.0),
)


def _stage_w_in(wt_hbm, w_main_ref, w_r_ref, w_lr_ref, stage_ref, sem):
    slots, rows, d = stage_ref.shape
    lr0 = _MAIN_WIDTH
    r0 = lr0 + GLA_GATE_RANK
    jobs = [(c, w_main_ref, c, rows) for c in range(0, lr0, rows)]
    jobs += [(lr0, w_lr_ref, 0, GLA_GATE_RANK)]
    jobs += [(r0 + c, w_r_ref, c, rows) for c in range(0, GLA_V_WIDTH, rows)]
    lane = lax.broadcasted_iota(jnp.int32, (d, rows), 1)

    def copy(n):
        return pltpu.make_async_copy(wt_hbm.at[pl.ds(jobs[n][0], rows), :],
                                     stage_ref.at[n % slots], sem.at[n % slots])

    for n in range(min(slots - 1, len(jobs))):
        copy(n).start()
    for n, (_, dst, col, valid) in enumerate(jobs):
        if n + slots - 1 < len(jobs):
            copy(n + slots - 1).start()
        copy(n).wait()
        blk = stage_ref[n % slots].T
        if valid < rows:
            blk = jnp.where(lane < valid, blk, 0.0)
        dst[:, pl.ds(col, rows)] = blk.astype(BF16)


def _inproj_kernel(layer, x_ref, mod_ref, w_hbm, *refs):
    n_out = len(_PROJ_GROUPS)
    out_refs = refs[:n_out]
    w_main_ref, w_r_ref, w_lr_ref, stage_ref, sem = refs[n_out:]

    @pl.when(pl.program_id(0) == 0)
    def _():
        _stage_w_in(w_hbm.at[layer], w_main_ref, w_r_ref, w_lr_ref, stage_ref, sem)

    shift = mod_ref[0:1, :]
    scale = mod_ref[1:2, :]
    h = (x_ref[...] * (1.0 + scale) + shift).astype(BF16)
    w_refs = (w_main_ref, w_r_ref, w_lr_ref)
    for (src, c0, width, mul), o_ref in zip(_PROJ_GROUPS, out_refs):
        acc = _dot(h, w_refs[src][:, c0:c0 + width])
        if mul != 1.0:
            acc = acc * mul
        o_ref[...] = acc.astype(o_ref.dtype)


def _inproj(x2d, mod, w_in_t, seq, layer):
    t, d = x2d.shape
    tm = INPROJ_TILE
    tiles_per_seq = seq // tm
    widths = [g[2] for g in _PROJ_GROUPS]
    return pl.pallas_call(
        functools.partial(_inproj_kernel, layer),
        out_shape=[jax.ShapeDtypeStruct((t, width), BF16) for width in widths],
        grid=(t // tm,),
        in_specs=[
            pl.BlockSpec((tm, d), lambda i: (i, 0)),
            pl.BlockSpec((None, SUBLANES, d), lambda i: (i // tiles_per_seq, 0, 0)),
            pl.BlockSpec(memory_space=pl.ANY),
        ],
        out_specs=[pl.BlockSpec((tm, width), lambda i: (i, 0)) for width in widths],
        scratch_shapes=[pltpu.VMEM((d, _MAIN_WIDTH), BF16), pltpu.VMEM((d, GLA_V_WIDTH), BF16),
                        pltpu.VMEM((d, LANES), BF16),
                        pltpu.VMEM((STAGE_SLOTS, LANES, d), F32),
                        pltpu.SemaphoreType.DMA((STAGE_SLOTS,))],
        compiler_params=pltpu.CompilerParams(
            dimension_semantics=("arbitrary",), vmem_limit_bytes=VMEM_LIMIT_BYTES),
        name="inproj",
    )(x2d, mod, w_in_t)


def _sb_suffix_sums(sp_blocks, umat):
    m = sp_blocks[0].shape[0]
    res = _dot(jnp.concatenate(sp_blocks, axis=0).astype(BF16), umat)
    cs, nearer = [], None
    for c in reversed(range(len(sp_blocks))):
        part = res[c * m:(c + 1) * m]
        loc, tot = part[:, :SB_TK], part[:, SB_TK:]
        cs.append(loc if nearer is None else loc + nearer)
        nearer = tot if nearer is None else nearer + tot
    return cs[::-1], nearer


def _sb_kernel(q_ref, k_ref, v_ref, g_ref, o_ref):
    tk = SB_TK
    pairs = q_ref.shape[1] // LANES
    nrb = SB_TQ // tk
    nb = SB_WIN_BLOCKS
    lane = lax.broadcasted_iota(jnp.int32, (tk, LANES), 1)
    head0 = lane < SB_HEAD_DIM

    uj = lax.broadcasted_iota(jnp.int32, (tk, 2 * tk), 0)
    us = lax.broadcasted_iota(jnp.int32, (tk, 2 * tk), 1)
    umat = jnp.where((uj >= us) | (us >= tk), 1.0, 0.0).astype(BF16)
    row = lax.broadcasted_iota(jnp.int32, (2 * tk, tk), 0) & (tk - 1)
    col = lax.broadcasted_iota(jnp.int32, (2 * tk, tk), 1)
    causal = col < row
    lanes = [slice(p * LANES, (p + 1) * LANES) for p in range(pairs)]

    first_blk = [pl.program_id(1) * nrb + r - (nb - 1) for r in range(nrb)]
    blk_rows = [[pl.ds(pl.multiple_of(jnp.maximum(first_blk[r] + c, 0) * tk, tk), tk)
                 for c in range(nb)] for r in range(nrb)]
    units = [(r, p) for r in range(nrb) for p in range(pairs)]

    qs, zs, sums, ws, accs = {}, {}, {}, {}, {}

    def stage_logits(u):
        r, p = u
        q = q_ref[r * tk:(r + 1) * tk, lanes[p]]
        zq = jnp.zeros_like(q)
        qs[u] = jnp.concatenate([jnp.where(head0, q, zq), jnp.where(head0, zq, q)], axis=0)
        kw = jnp.concatenate([k_ref[rows, lanes[p]] for rows in blk_rows[r]], axis=0)
        z = _sb_logits(qs[u], kw)
        zs[u] = [z[:, c * tk:(c + 1) * tk] for c in range(nb)]
        zs[u][-1] = jnp.where(causal, zs[u][-1], SB_MASKED_BITS)

    def stage_sums(u):
        sums[u] = _sb_suffix_sums([_softplus2(z) for z in zs[u]], umat)

    def stage_weights(u):
        args = [z - cs for z, cs in zip(zs[u], sums[u][0])]
        ws[u] = jnp.exp2(jnp.concatenate(args, axis=1)).astype(BF16)

    def stage_values(u):
        r, p = u
        vw = [v_ref[rows, lanes[p]] for rows in blk_rows[r]]
        vw = [jnp.where(first_blk[r] + c >= 0, vw[c], jnp.zeros_like(vw[c]))
              for c in range(nb - 1)] + vw[-1:]
        accs[u] = _dot(ws[u], jnp.concatenate(vw, axis=0))

    stages = (stage_logits, stage_sums, stage_weights, stage_values)
    for t in range(len(stages) + len(units) - 1):
        for n, u in enumerate(units):
            if 0 <= t - n < len(stages):
                stages[t - n](u)

    def not_done(cs_):
        m = cs_[0]
        for c in cs_[1:]:
            m = jnp.minimum(m, c)
        return (jnp.min(m) < SB_EXIT_BITS).astype(jnp.int32)

    def cond(s):
        return jnp.logical_and(s[0] >= 0, s[1] > 0)

    def write_out(r, accs_):
        outs = []
        for p in range(pairs):
            o = jnp.where(head0, accs_[p][:tk], accs_[p][tk:])
            sq = o * o
            ss0 = jnp.sum(jnp.where(head0, sq, 0.0), axis=-1, keepdims=True)
            ss1 = jnp.sum(jnp.where(head0, 0.0, sq), axis=-1, keepdims=True)
            ms = jnp.where(head0, ss0, ss1) * (1.0 / SB_HEAD_DIM)
            outs.append(o * lax.rsqrt(ms + RMS_EPS))
        o_ref[r * tk:(r + 1) * tk, :] = (jnp.concatenate(outs, axis=1) * g_ref[...]).astype(o_ref.dtype)

    for r in range(nrb):
        write_out(r, [accs[r, p] for p in range(pairs)])
    for r in range(nrb):
        carries = tuple(sums[r, p][1] for p in range(pairs))
        more = not_done(carries)

        def body(s, r=r):
            j, _, cs_, as_ = s
            rows = pl.ds(pl.multiple_of(j * tk, tk), tk)
            zb = [_sb_logits(qs[r, p], k_ref[rows, lanes[p]]) for p in range(pairs)]
            sb = [_sb_suffix_sums([_softplus2(zb[p])], umat) for p in range(pairs)]
            wb = [jnp.exp2(zb[p] - sb[p][0][0] - cs_[p]).astype(BF16) for p in range(pairs)]
            new_a = [as_[p] + _dot(wb[p], v_ref[rows, lanes[p]]) for p in range(pairs)]
            new_c = [cs_[p] + sb[p][1] for p in range(pairs)]
            return j - 1, not_done(new_c), tuple(new_c), tuple(new_a)

        @pl.when(jnp.logical_and(first_blk[r] - 1 >= 0, more > 0))
        def _(r=r, carries=carries, more=more, body=body):
            _, _, _, accs_ = lax.while_loop(
                cond, body,
                (first_blk[r] - 1, more, carries, tuple(accs[r, p] for p in range(pairs))))
            write_out(r, accs_)


def _sb_attention(q, k, v, g):
    b, s, wdt = q.shape
    return pl.pallas_call(
        _sb_kernel,
        out_shape=jax.ShapeDtypeStruct((b, s, wdt), BF16),
        grid=(b, s // SB_TQ),
        in_specs=[
            pl.BlockSpec((None, SB_TQ, wdt), lambda bi, i: (bi, i, 0)),
            pl.BlockSpec((None, s, wdt), lambda bi, i: (bi, 0, 0)),
            pl.BlockSpec((None, s, wdt), lambda bi, i: (bi, 0, 0)),
            pl.BlockSpec((1, wdt), lambda bi, i: (0, 0)),
        ],
        out_specs=pl.BlockSpec((None, SB_TQ, wdt), lambda bi, i: (bi, i, 0)),
        compiler_params=pltpu.CompilerParams(
            dimension_semantics=("arbitrary", "arbitrary"),
            vmem_limit_bytes=VMEM_LIMIT_BYTES),
        name="sb_attention",
    )(q, k, v, g)


def _gla_kernel(q_ref, k_ref, v_ref, r_ref, lr_ref, wg_ref, bg_ref, g_ref, o_ref, st_ref):
    c = GLA_CHUNK
    dv2 = 2 * GLA_VAL_DIM

    @pl.when(pl.program_id(2) == 0)
    def _():
        st_ref[...] = jnp.zeros_like(st_ref)

    u = _dot(lr_ref[...], wg_ref[...]) + bg_ref[...]
    log_a = (jnp.minimum(u, 0.0) - jnp.log(1.0 + jnp.exp(-jnp.abs(u)))) * (1.0 / GLA_GATE_TEMP)

    ti = lax.broadcasted_iota(jnp.int32, (c, c), 0)
    tj = lax.broadcasted_iota(jnp.int32, (c, c), 1)
    lower = ti >= tj
    tril = jnp.where(lower, 1.0, 0.0).astype(BF16)
    lane = lax.broadcasted_iota(jnp.int32, (c, LANES), 1)
    head0 = lane < GLA_KEY_DIM
    se = lax.broadcasted_iota(jnp.int32, (dv2, LANES), 0)
    sd = lax.broadcasted_iota(jnp.int32, (dv2, LANES), 1)
    st_mask = (se < GLA_VAL_DIM) == (sd < GLA_KEY_DIM)

    chunks = [slice(n * c, (n + 1) * c) for n in range(GLA_STEP // c)]
    la_hi, la_lo = _split_bf16(log_a)
    la_parts = jnp.concatenate([la_hi, la_lo], axis=1)
    bcum, dec = [], []
    for rows in chunks:
        res = _dot(tril, la_parts[rows])
        bcum.append(res[:, :LANES] + res[:, LANES:])
        dec.append(jnp.exp(bcum[-1][c - 1:c, :]))
    q_dec, k_inv, k_end = [], [], []
    for n, rows in enumerate(chunks):
        kf = k_ref[rows, :].astype(F32) * jnp.exp(-bcum[n])
        q_dec.append((q_ref[rows, :].astype(F32) * jnp.exp(bcum[n]) * (GLA_KEY_DIM ** -0.5)).astype(BF16))
        k_inv.append(kf.astype(BF16))
        k_end.append((kf * dec[n]).astype(BF16))
    lower2 = jnp.concatenate([lower, lower], axis=0)
    o_intra = []
    for n, rows in enumerate(chunks):
        zq = jnp.zeros_like(q_dec[n])
        q2 = jnp.concatenate([jnp.where(head0, q_dec[n], zq), jnp.where(head0, zq, q_dec[n])], axis=0)
        a = jnp.where(lower2, _dot_nt(q2, k_inv[n]), 0.0).astype(BF16)
        o_intra.append(jnp.concatenate(
            [_dot(a[:c], v_ref[rows, :GLA_VAL_DIM]), _dot(a[c:], v_ref[rows, GLA_VAL_DIM:])], axis=1))
    kv = [jnp.where(st_mask, _dot_tn(v_ref[rows, :], k_end[n]), 0.0) for n, rows in enumerate(chunks)]
    st = st_ref[...]
    starts = []
    for n in range(len(chunks)):
        starts.append(st.astype(BF16))
        st = st * dec[n] + kv[n]
    st_ref[...] = st
    for n, rows in enumerate(chunks):
        o = o_intra[n] + _dot_nt(q_dec[n], starts[n])
        o0 = o[:, :GLA_VAL_DIM]
        o1 = o[:, GLA_VAL_DIM:]
        n0 = o0 * lax.rsqrt(jnp.mean(o0 * o0, axis=-1, keepdims=True) + RMS_EPS)
        n1 = o1 * lax.rsqrt(jnp.mean(o1 * o1, axis=-1, keepdims=True) + RMS_EPS)
        rf = r_ref[rows, :].astype(F32)
        gate = rf / (1.0 + jnp.exp(-rf))
        o_ref[rows, :] = (jnp.concatenate([n0, n1], axis=1) * g_ref[...] * gate).astype(o_ref.dtype)


def _gla(q, k, v, r, lr, wg, bg, g):
    b, s, _ = q.shape
    pairs = GLA_HEADS // 2
    step = GLA_STEP
    dv2 = 2 * GLA_VAL_DIM
    seq_map = lambda bi, p, t: (bi, t, p)
    return pl.pallas_call(
        _gla_kernel,
        out_shape=jax.ShapeDtypeStruct((b, s, GLA_V_WIDTH), BF16),
        grid=(b, pairs, s // step),
        in_specs=[
            pl.BlockSpec((None, step, LANES), seq_map),
            pl.BlockSpec((None, step, LANES), seq_map),
            pl.BlockSpec((None, step, dv2), seq_map),
            pl.BlockSpec((None, step, dv2), seq_map),
            pl.BlockSpec((None, step, LANES), lambda bi, p, t: (bi, t, 0)),
            pl.BlockSpec((LANES, LANES), lambda bi, p, t: (0, p)),
            pl.BlockSpec((1, LANES), lambda bi, p, t: (0, p)),
            pl.BlockSpec((1, dv2), lambda bi, p, t: (0, p)),
        ],
        out_specs=pl.BlockSpec((None, step, dv2), seq_map),
        scratch_shapes=[pltpu.VMEM((dv2, LANES), F32)],
        compiler_params=pltpu.CompilerParams(
            dimension_semantics=("arbitrary", "arbitrary", "arbitrary"),
            vmem_limit_bytes=VMEM_LIMIT_BYTES),
        name="gla",
    )(q, k, v, r, lr, wg, bg, g)


def _post_kernel(alpha, tiles_per_seq, layer, osb_ref, ogla_ref, x_ref, mod_ref, wo_hbm, g1_ref, b1_ref,
                 wg_hbm, wu_hbm, cw_ref, cb_ref, wd_hbm, g_ref, b_ref,
                 o_ref, halo_ref, act_ref, x1_ref, wo_ref, wg_ref, wu_ref, wd_ref, stage_ref, sem):
    tm = POST_SUB_TILE
    nsub = x_ref.shape[0] // tm
    d_ff = wg_ref.shape[1]
    n_sb = osb_ref.shape[1]
    first = (pl.program_id(0) % tiles_per_seq) == 0

    @pl.when(pl.program_id(0) == 0)
    def _():
        _stage_weights_bf16(((wo_hbm.at[layer], wo_ref), (wg_hbm.at[layer], wg_ref),
                             (wu_hbm.at[layer], wu_ref), (wd_hbm.at[layer], wd_ref)), stage_ref, sem)

    @pl.when(first)
    def _():
        halo_ref[...] = jnp.zeros_like(halo_ref)

    top = lax.broadcasted_iota(jnp.int32, (SUBLANES, FF_TILE), 0)
    hs = []
    for j in range(nsub):
        rows = slice(j * tm, (j + 1) * tm)
        mix = _dot(osb_ref[rows, :], wo_ref[:n_sb, :]) + _dot(ogla_ref[rows, :], wo_ref[n_sb:, :])
        x1_ref[rows, :] = _layer_norm(alpha * x_ref[rows, :] + (1.0 + mod_ref[2:3, :]) * mix,
                                      g1_ref[...], b1_ref[...])
        hs.append((x1_ref[rows, :] * (1.0 + mod_ref[4:5, :]) + mod_ref[3:4, :]).astype(BF16))
    for j in range(nsub):
        rows = slice(j * tm, (j + 1) * tm)
        h = hs[j]
        for f in range(d_ff // FF_TILE):
            cols = slice(f * FF_TILE, (f + 1) * FF_TILE)
            gt = _dot(h, wg_ref[:, cols])
            up = _dot(h, wu_ref[:, cols])
            prev = halo_ref[:, cols]
            halo_ref[:, cols] = gt[tm - SUBLANES:, :]
            r1 = pltpu.roll(gt, 1, 0)
            r2 = pltpu.roll(gt, 2, 0)
            t1 = jnp.where(top < 1, pltpu.roll(prev, 1, 0), r1[:SUBLANES])
            t2 = jnp.where(top < 2, pltpu.roll(prev, 2, 0), r2[:SUBLANES])
            g1 = jnp.concatenate([t1, r1[SUBLANES:]], axis=0)
            g2 = jnp.concatenate([t2, r2[SUBLANES:]], axis=0)
            conv = (g2 * cw_ref[0:1, cols] + g1 * cw_ref[1:2, cols] + gt * cw_ref[2:3, cols]
                    + cb_ref[:, cols])
            act_ref[rows, cols] = (conv / (1.0 + jnp.exp(-conv)) * up).astype(BF16)
        y = _dot(act_ref[rows, :], wd_ref[...])
        o_ref[rows, :] = _layer_norm(alpha * x1_ref[rows, :] + (1.0 + mod_ref[5:6, :]) * y,
                                     g_ref[...], b_ref[...])


def _post(o_sb, o_gla, x2d, mod, w_o, ln1_g, ln1_b, wg, wu, cw, cb, wd, ln_g, ln_b, seq, alpha, layer):
    t, d = x2d.shape
    d_ff = wg.shape[2]
    tm = ROW_TILE
    tiles_per_seq = seq // tm
    row_map = lambda i: (i, 0)
    fixed = lambda i: (0, 0)
    hbm = pl.BlockSpec(memory_space=pl.ANY)
    stage_width = max(w.shape[2] for w in (w_o, wg, wu, wd))
    return pl.pallas_call(
        functools.partial(_post_kernel, alpha, tiles_per_seq, layer),
        out_shape=jax.ShapeDtypeStruct((t, d), F32),
        grid=(t // tm,),
        in_specs=[
            pl.BlockSpec((tm, o_sb.shape[1]), row_map),
            pl.BlockSpec((tm, o_gla.shape[1]), row_map),
            pl.BlockSpec((tm, d), row_map),
            pl.BlockSpec((None, SUBLANES, d), lambda i: (i // tiles_per_seq, 0, 0)),
            hbm,
            pl.BlockSpec((1, d), fixed),
            pl.BlockSpec((1, d), fixed),
            hbm,
            hbm,
            pl.BlockSpec(cw.shape, fixed),
            pl.BlockSpec(cb.shape, fixed),
            hbm,
            pl.BlockSpec((1, d), fixed),
            pl.BlockSpec((1, d), fixed),
        ],
        out_specs=pl.BlockSpec((tm, d), row_map),
        scratch_shapes=[pltpu.VMEM((SUBLANES, d_ff), F32), pltpu.VMEM((tm, d_ff), BF16),
                        pltpu.VMEM((tm, d), F32),
                        pltpu.VMEM(w_o.shape[1:], BF16), pltpu.VMEM(wg.shape[1:], BF16),
                        pltpu.VMEM(wu.shape[1:], BF16), pltpu.VMEM(wd.shape[1:], BF16),
                        pltpu.VMEM((STAGE_SLOTS, STAGE_ROWS, stage_width), F32),
                        pltpu.SemaphoreType.DMA((STAGE_SLOTS,))],
        compiler_params=pltpu.CompilerParams(
            dimension_semantics=("arbitrary",), vmem_limit_bytes=VMEM_LIMIT_BYTES),
        name="post",
    )(o_sb, o_gla, x2d, mod, w_o, ln1_g, ln1_b, wg, wu, cw, cb, wd, ln_g, ln_b)


def kernel(x, c, w_ada, b_ada, w_in, gla_w_gate, gla_b_gate, sb_norm_g, gla_norm_g, w_out,
           ln1_g, ln1_b, w_ff_gate, w_ff_up, conv_w, conv_b, w_down, ln2_g, ln2_b):
    bsz, seq, d = x.shape
    depth = w_ada.shape[0]
    alpha = float((2 * depth) ** 0.25)
    t = bsz * seq

    c_pad = jnp.concatenate([c, jnp.zeros((SUBLANES - bsz, d), c.dtype)], axis=0)
    w_in_t = jnp.swapaxes(w_in, 1, 2)
    xcur = x.reshape(t, d)
    for l in range(depth):
        mod = _ada(c_pad, w_ada[l], b_ada[l][None, :])
        mod = mod[:bsz].reshape(bsz, N_MOD, d)
        mod = jnp.concatenate([mod, jnp.zeros((bsz, SUBLANES - N_MOD, d), F32)], axis=1)

        sb_q, sb_k, sb_v, g_q, g_k, g_v, g_r, g_lr = _inproj(xcur, mod, w_in_t, seq, l)
        as_seq = lambda a: a.reshape(bsz, seq, a.shape[-1])

        o_sb = _sb_attention(as_seq(sb_q), as_seq(sb_k), as_seq(sb_v), sb_norm_g[l][None, :])
        wg_pad = jnp.concatenate(
            [gla_w_gate[l], jnp.zeros((LANES - GLA_GATE_RANK, GLA_QK_WIDTH), F32)], axis=0).astype(BF16)
        o_gla = _gla(as_seq(g_q), as_seq(g_k), as_seq(g_v), as_seq(g_r), as_seq(g_lr),
                     wg_pad, gla_b_gate[l][None, :], gla_norm_g[l][None, :])

        xcur = _post(o_sb.reshape(t, SB_WIDTH), o_gla.reshape(t, GLA_V_WIDTH), xcur, mod,
                     w_out, ln1_g[l][None, :], ln1_b[l][None, :],
                     w_ff_gate, w_ff_up, conv_w[l], conv_b[l][None, :], w_down,
                     ln2_g[l][None, :], ln2_b[l][None, :], seq, alpha, l)
    return xcur.reshape(bsz, seq, d)
```

```python
import functools

import jax
import jax.numpy as jnp
from jax import lax
from jax.experimental import pallas as pl
from jax.experimental.pallas import tpu as pltpu

F32 = jnp.float32
BF16 = jnp.bfloat16

LANES = 128
SUBLANES = 8
VMEM_LIMIT_BYTES = 60 * 1024 * 1024

SB_HEADS = 8
SB_HEAD_DIM = 64
GLA_HEADS = 4
GLA_KEY_DIM = 64
GLA_VAL_DIM = 128
GLA_GATE_RANK = 16
GLA_GATE_TEMP = 16.0
GLA_CHUNK = 64
CONV_WIDTH = 3
N_MOD = 6
LN_EPS = 1e-5
RMS_EPS = 1e-6

SB_WIDTH = SB_HEADS * SB_HEAD_DIM
GLA_QK_WIDTH = GLA_HEADS * GLA_KEY_DIM
GLA_V_WIDTH = GLA_HEADS * GLA_VAL_DIM

SB_EXIT = 88.0
LOG2E = 1.4426950408889634
SB_EXIT_BITS = SB_EXIT * LOG2E
SB_MAX_BITS = 126.0
SB_MASKED_BITS = -1e30
SB_TK = 128
SB_TQ = 8 * SB_TK
SB_WIN_BLOCKS = 3
SB_WIN = SB_WIN_BLOCKS * SB_TK

GLA_STEP = 1024
INPROJ_TILE = 1024
ROW_TILE = 1024
POST_SUB_TILE = 512
FF_TILE = 256
STAGE_ROWS = 128
STAGE_SLOTS = 3
INPROJ_STAGE_SLOTS = 6


def _sb_logits(q, k):
    return jnp.minimum(_dot_nt(q, k), SB_MAX_BITS)


def _softplus2(z):
    return jnp.log2(1.0 + jnp.exp2(z))


def _split_bf16(a):
    hi = a.astype(BF16)
    lo = (a - hi.astype(F32)).astype(BF16)
    return hi, lo


def _dot(a, b):
    return jnp.dot(a, b, preferred_element_type=F32)


def _dot_nt(a, b):
    return lax.dot_general(a, b, (((1,), (1,)), ((), ())), preferred_element_type=F32)


def _dot_tn(a, b):
    return lax.dot_general(a, b, (((0,), (0,)), ((), ())), preferred_element_type=F32)


def _stage_weights_bf16(pairs, stage_ref, sem):
    slots, rows = stage_ref.shape[0], stage_ref.shape[1]
    jobs = [(src, dst, r0) for src, dst in pairs for r0 in range(0, src.shape[0], rows)]

    def window(n):
        return stage_ref.at[n % slots, :, pl.ds(0, jobs[n][0].shape[1])]

    def copy(n):
        src, _, r0 = jobs[n]
        return pltpu.make_async_copy(src.at[pl.ds(r0, rows), :], window(n), sem.at[n % slots])

    for n in range(min(slots - 1, len(jobs))):
        copy(n).start()
    for n, (_, dst, r0) in enumerate(jobs):
        if n + slots - 1 < len(jobs):
            copy(n + slots - 1).start()
        copy(n).wait()
        dst[pl.ds(r0, rows), :] = window(n)[...].astype(BF16)


def _layer_norm(v, g, b):
    mu = jnp.mean(v, axis=-1, keepdims=True)
    vc = v - mu
    var = jnp.mean(vc * vc, axis=-1, keepdims=True)
    return vc * lax.rsqrt(var + LN_EPS) * g + b


def _ada_kernel(c_ref, w_ref, b_ref, o_ref):
    c = c_ref[...]
    ca = c / (1.0 + jnp.exp(-c))
    ch, cl = _split_bf16(ca)
    wh, wl = _split_bf16(w_ref[...])
    o_ref[...] = _dot(ch, wh) + _dot(cl, wh) + _dot(ch, wl) + b_ref[...]


def _ada(c_pad, w, b):
    d, n = w.shape
    tn = 1024
    return pl.pallas_call(
        _ada_kernel,
        out_shape=jax.ShapeDtypeStruct((c_pad.shape[0], n), F32),
        grid=(n // tn,),
        in_specs=[
            pl.BlockSpec((c_pad.shape[0], d), lambda j: (0, 0)),
            pl.BlockSpec((d, tn), lambda j: (0, j)),
            pl.BlockSpec((1, tn), lambda j: (0, j)),
        ],
        out_specs=pl.BlockSpec((c_pad.shape[0], tn), lambda j: (0, j)),
        compiler_params=pltpu.CompilerParams(
            dimension_semantics=("arbitrary",), vmem_limit_bytes=VMEM_LIMIT_BYTES),
        name="ada",
    )(c_pad, w, b)


_MAIN_WIDTH = 3 * SB_WIDTH + 2 * GLA_QK_WIDTH + GLA_V_WIDTH
_PROJ_GROUPS = (
    (0, 0, SB_WIDTH, SB_HEAD_DIM ** -0.5 * LOG2E),
    (0, SB_WIDTH, SB_WIDTH, 1.0),
    (0, 2 * SB_WIDTH, SB_WIDTH, 1.0),
    (0, 3 * SB_WIDTH, GLA_QK_WIDTH, GLA_KEY_DIM ** -0.5),
    (0, 3 * SB_WIDTH + GLA_QK_WIDTH, GLA_QK_WIDTH, 1.0),
    (0, 3 * SB_WIDTH + 2 * GLA_QK_WIDTH, GLA_V_WIDTH, 1.0),
    (1, 0, GLA_V_WIDTH, 1.0),
    (2, 0, LANES, 1.0),
)


def _stage_w_in(wt_hbm, w_main_ref, w_r_ref, w_lr_ref, stage_ref, sem):
    slots, rows, d = stage_ref.shape
    lr0 = _MAIN_WIDTH
    r0 = lr0 + GLA_GATE_RANK
    jobs = [(c, w_main_ref, c, rows) for c in range(0, lr0, rows)]
    jobs += [(lr0, w_lr_ref, 0, GLA_GATE_RANK)]
    jobs += [(r0 + c, w_r_ref, c, rows) for c in range(0, GLA_V_WIDTH, rows)]
    lane = lax.broadcasted_iota(jnp.int32, (d, rows), 1)

    def copy(n):
        return pltpu.make_async_copy(wt_hbm.at[pl.ds(jobs[n][0], rows), :],
                                     stage_ref.at[n % slots], sem.at[n % slots])

    for n in range(min(slots - 1, len(jobs))):
        copy(n).start()
    for n, (_, dst, col, valid) in enumerate(jobs):
        if n + slots - 1 < len(jobs):
            copy(n + slots - 1).start()
        copy(n).wait()
        blk = stage_ref[n % slots].T
        if valid < rows:
            blk = jnp.where(lane < valid, blk, 0.0)
        dst[:, pl.ds(col, rows)] = blk.astype(BF16)


def _inproj_kernel(layer, x_ref, mod_ref, w_hbm, *refs):
    n_out = len(_PROJ_GROUPS)
    out_refs = refs[:n_out]
    w_main_ref, w_r_ref, w_lr_ref, stage_ref, sem = refs[n_out:]

    @pl.when(pl.program_id(0) == 0)
    def _():
        _stage_w_in(w_hbm.at[layer], w_main_ref, w_r_ref, w_lr_ref, stage_ref, sem)

    shift = mod_ref[0:1, :]
    scale = mod_ref[1:2, :]
    h = (x_ref[...] * (1.0 + scale) + shift).astype(BF16)
    w_refs = (w_main_ref, w_r_ref, w_lr_ref)
    for (src, c0, width, mul), o_ref in zip(_PROJ_GROUPS, out_refs):
        acc = _dot(h, w_refs[src][:, c0:c0 + width])
        if mul != 1.0:
            acc = acc * mul
        o_ref[...] = acc.astype(o_ref.dtype)


def _inproj(x2d, mod, w_in_t, seq, layer):
    t, d = x2d.shape
    tm = INPROJ_TILE
    tiles_per_seq = seq // tm
    widths = [g[2] for g in _PROJ_GROUPS]
    return pl.pallas_call(
        functools.partial(_inproj_kernel, layer),
        out_shape=[jax.ShapeDtypeStruct((t, width), BF16) for width in widths],
        grid=(t // tm,),
        in_specs=[
            pl.BlockSpec((tm, d), lambda i: (i, 0)),
            pl.BlockSpec((None, SUBLANES, d), lambda i: (i // tiles_per_seq, 0, 0)),
            pl.BlockSpec(memory_space=pl.ANY),
        ],
        out_specs=[pl.BlockSpec((tm, width), lambda i: (i, 0)) for width in widths],
        scratch_shapes=[pltpu.VMEM((d, _MAIN_WIDTH), BF16), pltpu.VMEM((d, GLA_V_WIDTH), BF16),
                        pltpu.VMEM((d, LANES), BF16),
                        pltpu.VMEM((INPROJ_STAGE_SLOTS, LANES, d), F32),
                        pltpu.SemaphoreType.DMA((INPROJ_STAGE_SLOTS,))],
        compiler_params=pltpu.CompilerParams(
            dimension_semantics=("arbitrary",), vmem_limit_bytes=VMEM_LIMIT_BYTES),
        name="inproj",
    )(x2d, mod, w_in_t)


def _sb_suffix_sums(sp_blocks, umat):
    m = sp_blocks[0].shape[0]
    res = _dot(jnp.concatenate(sp_blocks, axis=0).astype(BF16), umat)
    cs, nearer = [], None
    for c in reversed(range(len(sp_blocks))):
        part = res[c * m:(c + 1) * m]
        loc, tot = part[:, :SB_TK], part[:, SB_TK:]
        cs.append(loc if nearer is None else loc + nearer)
        nearer = tot if nearer is None else nearer + tot
    return cs[::-1], nearer


def _sb_kernel(q_ref, k_ref, v_ref, g_ref, o_ref):
    tk = SB_TK
    pairs = q_ref.shape[1] // LANES
    nrb = SB_TQ // tk
    nb = SB_WIN_BLOCKS
    lane = lax.broadcasted_iota(jnp.int32, (tk, LANES), 1)
    head0 = lane < SB_HEAD_DIM

    uj = lax.broadcasted_iota(jnp.int32, (tk, 2 * tk), 0)
    us = lax.broadcasted_iota(jnp.int32, (tk, 2 * tk), 1)
    umat = jnp.where((uj >= us) | (us >= tk), 1.0, 0.0).astype(BF16)
    row = lax.broadcasted_iota(jnp.int32, (2 * tk, tk), 0) & (tk - 1)
    col = lax.broadcasted_iota(jnp.int32, (2 * tk, tk), 1)
    causal = col < row
    lanes = [slice(p * LANES, (p + 1) * LANES) for p in range(pairs)]

    first_blk = [pl.program_id(1) * nrb + r - (nb - 1) for r in range(nrb)]
    blk_rows = [[pl.ds(pl.multiple_of(jnp.maximum(first_blk[r] + c, 0) * tk, tk), tk)
                 for c in range(nb)] for r in range(nrb)]
    units = [(r, p) for r in range(nrb) for p in range(pairs)]

    qs, zs, sums, ws, accs = {}, {}, {}, {}, {}

    def stage_logits(u):
        r, p = u
        q = q_ref[r * tk:(r + 1) * tk, lanes[p]]
        zq = jnp.zeros_like(q)
        qs[u] = jnp.concatenate([jnp.where(head0, q, zq), jnp.where(head0, zq, q)], axis=0)
        kw = jnp.concatenate([k_ref[rows, lanes[p]] for rows in blk_rows[r]], axis=0)
        z = _sb_logits(qs[u], kw)
        zs[u] = [z[:, c * tk:(c + 1) * tk] for c in range(nb)]
        zs[u][-1] = jnp.where(causal, zs[u][-1], SB_MASKED_BITS)

    def stage_sums(u):
        sums[u] = _sb_suffix_sums([_softplus2(z) for z in zs[u]], umat)

    def stage_weights(u):
        args = [z - cs for z, cs in zip(zs[u], sums[u][0])]
        ws[u] = jnp.exp2(jnp.concatenate(args, axis=1)).astype(BF16)

    def stage_values(u):
        r, p = u
        vw = [v_ref[rows, lanes[p]] for rows in blk_rows[r]]
        vw = [jnp.where(first_blk[r] + c >= 0, vw[c], jnp.zeros_like(vw[c]))
              for c in range(nb - 1)] + vw[-1:]
        accs[u] = _dot(ws[u], jnp.concatenate(vw, axis=0))

    stages = (stage_logits, stage_sums, stage_weights, stage_values)
    for t in range(len(stages) + len(units) - 1):
        for n, u in enumerate(units):
            if 0 <= t - n < len(stages):
                stages[t - n](u)

    def not_done(cs_):
        m = cs_[0]
        for c in cs_[1:]:
            m = jnp.minimum(m, c)
        return (jnp.min(m) < SB_EXIT_BITS).astype(jnp.int32)

    def cond(s):
        return jnp.logical_and(s[0] >= 0, s[1] > 0)

    def write_out(r, accs_):
        outs = []
        for p in range(pairs):
            o = jnp.where(head0, accs_[p][:tk], accs_[p][tk:])
            sq = o * o
            ss0 = jnp.sum(jnp.where(head0, sq, 0.0), axis=-1, keepdims=True)
            ss1 = jnp.sum(jnp.where(head0, 0.0, sq), axis=-1, keepdims=True)
            ms = jnp.where(head0, ss0, ss1) * (1.0 / SB_HEAD_DIM)
            outs.append(o * lax.rsqrt(ms + RMS_EPS))
        o_ref[r * tk:(r + 1) * tk, :] = (jnp.concatenate(outs, axis=1) * g_ref[...]).astype(o_ref.dtype)

    for r in range(nrb):
        write_out(r, [accs[r, p] for p in range(pairs)])
    for r in range(nrb):
        carries = tuple(sums[r, p][1] for p in range(pairs))
        more = not_done(carries)

        def body(s, r=r):
            j, _, cs_, as_ = s
            rows = pl.ds(pl.multiple_of(j * tk, tk), tk)
            zb = [_sb_logits(qs[r, p], k_ref[rows, lanes[p]]) for p in range(pairs)]
            sb = [_sb_suffix_sums([_softplus2(zb[p])], umat) for p in range(pairs)]
            wb = [jnp.exp2(zb[p] - sb[p][0][0] - cs_[p]).astype(BF16) for p in range(pairs)]
            new_a = [as_[p] + _dot(wb[p], v_ref[rows, lanes[p]]) for p in range(pairs)]
            new_c = [cs_[p] + sb[p][1] for p in range(pairs)]
            return j - 1, not_done(new_c), tuple(new_c), tuple(new_a)

        @pl.when(jnp.logical_and(first_blk[r] - 1 >= 0, more > 0))
        def _(r=r, carries=carries, more=more, body=body):
            _, _, _, accs_ = lax.while_loop(
                cond, body,
                (first_blk[r] - 1, more, carries, tuple(accs[r, p] for p in range(pairs))))
            write_out(r, accs_)


def _sb_attention(q, k, v, g):
    b, s, wdt = q.shape
    return pl.pallas_call(
        _sb_kernel,
        out_shape=jax.ShapeDtypeStruct((b, s, wdt), BF16),
        grid=(b, s // SB_TQ),
        in_specs=[
            pl.BlockSpec((None, SB_TQ, wdt), lambda bi, i: (bi, i, 0)),
            pl.BlockSpec((None, s, wdt), lambda bi, i: (bi, 0, 0)),
            pl.BlockSpec((None, s, wdt), lambda bi, i: (bi, 0, 0)),
            pl.BlockSpec((1, wdt), lambda bi, i: (0, 0)),
        ],
        out_specs=pl.BlockSpec((None, SB_TQ, wdt), lambda bi, i: (bi, i, 0)),
        compiler_params=pltpu.CompilerParams(
            dimension_semantics=("arbitrary", "arbitrary"),
            vmem_limit_bytes=VMEM_LIMIT_BYTES),
        name="sb_attention",
    )(q, k, v, g)


def _gla_kernel(q_ref, k_ref, v_ref, r_ref, lr_ref, wg_ref, bg_ref, g_ref, o_ref, st_ref):
    c = GLA_CHUNK
    dv2 = 2 * GLA_VAL_DIM

    @pl.when(pl.program_id(2) == 0)
    def _():
        st_ref[...] = jnp.zeros_like(st_ref)

    u = _dot(lr_ref[...], wg_ref[...]) + bg_ref[...]
    log_a = (jnp.minimum(u, 0.0) - jnp.log(1.0 + jnp.exp(-jnp.abs(u)))) * (1.0 / GLA_GATE_TEMP)

    ti = lax.broadcasted_iota(jnp.int32, (c, c), 0)
    tj = lax.broadcasted_iota(jnp.int32, (c, c), 1)
    lower = ti >= tj
    tril = jnp.where(lower, 1.0, 0.0).astype(BF16)
    lane = lax.broadcasted_iota(jnp.int32, (c, LANES), 1)
    head0 = lane < GLA_KEY_DIM
    se = lax.broadcasted_iota(jnp.int32, (dv2, LANES), 0)
    sd = lax.broadcasted_iota(jnp.int32, (dv2, LANES), 1)
    st_mask = (se < GLA_VAL_DIM) == (sd < GLA_KEY_DIM)

    chunks = [slice(n * c, (n + 1) * c) for n in range(GLA_STEP // c)]
    la_hi, la_lo = _split_bf16(log_a)
    la_parts = jnp.concatenate([la_hi, la_lo], axis=1)
    bcum, dec = [], []
    for rows in chunks:
        res = _dot(tril, la_parts[rows])
        bcum.append(res[:, :LANES] + res[:, LANES:])
        dec.append(jnp.exp(bcum[-1][c - 1:c, :]))
    q_dec, k_inv, k_end = [], [], []
    for n, rows in enumerate(chunks):
        kf = k_ref[rows, :].astype(F32) * jnp.exp(-bcum[n])
        q_dec.append((q_ref[rows, :].astype(F32) * jnp.exp(bcum[n])).astype(BF16))
        k_inv.append(kf.astype(BF16))
        k_end.append((kf * dec[n]).astype(BF16))
    lower2 = jnp.concatenate([lower, lower], axis=0)
    o_intra = []
    for n, rows in enumerate(chunks):
        zq = jnp.zeros_like(q_dec[n])
        q2 = jnp.concatenate([jnp.where(head0, q_dec[n], zq), jnp.where(head0, zq, q_dec[n])], axis=0)
        a = jnp.where(lower2, _dot_nt(q2, k_inv[n]), 0.0).astype(BF16)
        o_intra.append(jnp.concatenate(
            [_dot(a[:c], v_ref[rows, :GLA_VAL_DIM]), _dot(a[c:], v_ref[rows, GLA_VAL_DIM:])], axis=1))
    kv = [jnp.where(st_mask, _dot_tn(v_ref[rows, :], k_end[n]), 0.0) for n, rows in enumerate(chunks)]
    st = st_ref[...]
    starts = []
    for n in range(len(chunks)):
        starts.append(st.astype(BF16))
        st = st * dec[n] + kv[n]
    st_ref[...] = st
    for n, rows in enumerate(chunks):
        o = o_intra[n] + _dot_nt(q_dec[n], starts[n])
        o0 = o[:, :GLA_VAL_DIM]
        o1 = o[:, GLA_VAL_DIM:]
        n0 = o0 * lax.rsqrt(jnp.mean(o0 * o0, axis=-1, keepdims=True) + RMS_EPS)
        n1 = o1 * lax.rsqrt(jnp.mean(o1 * o1, axis=-1, keepdims=True) + RMS_EPS)
        rf = r_ref[rows, :].astype(F32)
        gate = rf / (1.0 + jnp.exp(-rf))
        o_ref[rows, :] = (jnp.concatenate([n0, n1], axis=1) * g_ref[...] * gate).astype(o_ref.dtype)


def _gla(q, k, v, r, lr, wg, bg, g):
    b, s, _ = q.shape
    pairs = GLA_HEADS // 2
    step = GLA_STEP
    dv2 = 2 * GLA_VAL_DIM
    seq_map = lambda bi, p, t: (bi, t, p)
    return pl.pallas_call(
        _gla_kernel,
        out_shape=jax.ShapeDtypeStruct((b, s, GLA_V_WIDTH), BF16),
        grid=(b, pairs, s // step),
        in_specs=[
            pl.BlockSpec((None, step, LANES), seq_map),
            pl.BlockSpec((None, step, LANES), seq_map),
            pl.BlockSpec((None, step, dv2), seq_map),
            pl.BlockSpec((None, step, dv2), seq_map),
            pl.BlockSpec((None, step, LANES), lambda bi, p, t: (bi, t, 0)),
            pl.BlockSpec((LANES, LANES), lambda bi, p, t: (0, p)),
            pl.BlockSpec((1, LANES), lambda bi, p, t: (0, p)),
            pl.BlockSpec((1, dv2), lambda bi, p, t: (0, p)),
        ],
        out_specs=pl.BlockSpec((None, step, dv2), seq_map),
        scratch_shapes=[pltpu.VMEM((dv2, LANES), F32)],
        compiler_params=pltpu.CompilerParams(
            dimension_semantics=("arbitrary", "arbitrary", "arbitrary"),
            vmem_limit_bytes=VMEM_LIMIT_BYTES),
        name="gla",
    )(q, k, v, r, lr, wg, bg, g)


def _post_kernel(alpha, tiles_per_seq, layer, osb_ref, ogla_ref, x_ref, mod_ref, wo_hbm, g1_ref, b1_ref,
                 wg_hbm, wu_hbm, cw_ref, cb_ref, wd_hbm, g_ref, b_ref,
                 o_ref, halo_ref, act_ref, x1_ref, wo_ref, wg_ref, wu_ref, wd_ref, stage_ref, sem):
    tm = POST_SUB_TILE
    nsub = x_ref.shape[0] // tm
    d_ff = wg_ref.shape[1]
    n_sb = osb_ref.shape[1]
    first = (pl.program_id(0) % tiles_per_seq) == 0

    @pl.when(pl.program_id(0) == 0)
    def _():
        _stage_weights_bf16(((wo_hbm.at[layer], wo_ref), (wg_hbm.at[layer], wg_ref),
                             (wu_hbm.at[layer], wu_ref), (wd_hbm.at[layer], wd_ref)), stage_ref, sem)

    @pl.when(first)
    def _():
        halo_ref[...] = jnp.zeros_like(halo_ref)

    top = lax.broadcasted_iota(jnp.int32, (SUBLANES, FF_TILE), 0)
    hs = []
    for j in range(nsub):
        rows = slice(j * tm, (j + 1) * tm)
        mix = _dot(osb_ref[rows, :], wo_ref[:n_sb, :]) + _dot(ogla_ref[rows, :], wo_ref[n_sb:, :])
        x1_ref[rows, :] = _layer_norm(alpha * x_ref[rows, :] + (1.0 + mod_ref[2:3, :]) * mix,
                                      g1_ref[...], b1_ref[...])
        hs.append((x1_ref[rows, :] * (1.0 + mod_ref[4:5, :]) + mod_ref[3:4, :]).astype(BF16))
    for j in range(nsub):
        rows = slice(j * tm, (j + 1) * tm)
        h = hs[j]
        for f in range(d_ff // FF_TILE):
            cols = slice(f * FF_TILE, (f + 1) * FF_TILE)
            gt = _dot(h, wg_ref[:, cols])
            up = _dot(h, wu_ref[:, cols])
            prev = halo_ref[:, cols]
            halo_ref[:, cols] = gt[tm - SUBLANES:, :]
            r1 = pltpu.roll(gt, 1, 0)
            r2 = pltpu.roll(gt, 2, 0)
            t1 = jnp.where(top < 1, pltpu.roll(prev, 1, 0), r1[:SUBLANES])
            t2 = jnp.where(top < 2, pltpu.roll(prev, 2, 0), r2[:SUBLANES])
            g1 = jnp.concatenate([t1, r1[SUBLANES:]], axis=0)
            g2 = jnp.concatenate([t2, r2[SUBLANES:]], axis=0)
            conv = (g2 * cw_ref[0:1, cols] + g1 * cw_ref[1:2, cols] + gt * cw_ref[2:3, cols]
                    + cb_ref[:, cols])
            act_ref[rows, cols] = (conv / (1.0 + jnp.exp(-conv)) * up).astype(BF16)
        y = _dot(act_ref[rows, :], wd_ref[...])
        o_ref[rows, :] = _layer_norm(alpha * x1_ref[rows, :] + (1.0 + mod_ref[5:6, :]) * y,
                                     g_ref[...], b_ref[...])


def _post(o_sb, o_gla, x2d, mod, w_o, ln1_g, ln1_b, wg, wu, cw, cb, wd, ln_g, ln_b, seq, alpha, layer):
    t, d = x2d.shape
    d_ff = wg.shape[2]
    tm = ROW_TILE
    tiles_per_seq = seq // tm
    row_map = lambda i: (i, 0)
    fixed = lambda i: (0, 0)
    hbm = pl.BlockSpec(memory_space=pl.ANY)
    stage_width = max(w.shape[2] for w in (w_o, wg, wu, wd))
    return pl.pallas_call(
        functools.partial(_post_kernel, alpha, tiles_per_seq, layer),
        out_shape=jax.ShapeDtypeStruct((t, d), F32),
        grid=(t // tm,),
        in_specs=[
            pl.BlockSpec((tm, o_sb.shape[1]), row_map),
            pl.BlockSpec((tm, o_gla.shape[1]), row_map),
            pl.BlockSpec((tm, d), row_map),
            pl.BlockSpec((None, SUBLANES, d), lambda i: (i // tiles_per_seq, 0, 0)),
            hbm,
            pl.BlockSpec((1, d), fixed),
            pl.BlockSpec((1, d), fixed),
            hbm,
            hbm,
            pl.BlockSpec(cw.shape, fixed),
            pl.BlockSpec(cb.shape, fixed),
            hbm,
            pl.BlockSpec((1, d), fixed),
            pl.BlockSpec((1, d), fixed),
        ],
        out_specs=pl.BlockSpec((tm, d), row_map),
        scratch_shapes=[pltpu.VMEM((SUBLANES, d_ff), F32), pltpu.VMEM((tm, d_ff), BF16),
                        pltpu.VMEM((tm, d), F32),
                        pltpu.VMEM(w_o.shape[1:], BF16), pltpu.VMEM(wg.shape[1:], BF16),
                        pltpu.VMEM(wu.shape[1:], BF16), pltpu.VMEM(wd.shape[1:], BF16),
                        pltpu.VMEM((STAGE_SLOTS, STAGE_ROWS, stage_width), F32),
                        pltpu.SemaphoreType.DMA((STAGE_SLOTS,))],
        compiler_params=pltpu.CompilerParams(
            dimension_semantics=("arbitrary",), vmem_limit_bytes=VMEM_LIMIT_BYTES),
        name="post",
    )(o_sb, o_gla, x2d, mod, w_o, ln1_g, ln1_b, wg, wu, cw, cb, wd, ln_g, ln_b)


def kernel(x, c, w_ada, b_ada, w_in, gla_w_gate, gla_b_gate, sb_norm_g, gla_norm_g, w_out,
           ln1_g, ln1_b, w_ff_gate, w_ff_up, conv_w, conv_b, w_down, ln2_g, ln2_b):
    bsz, seq, d = x.shape
    depth = w_ada.shape[0]
    alpha = float((2 * depth) ** 0.25)
    t = bsz * seq

    c_pad = jnp.concatenate([c, jnp.zeros((SUBLANES - bsz, d), c.dtype)], axis=0)
    w_in_t = jnp.swapaxes(w_in, 1, 2)
    xcur = x.reshape(t, d)
    for l in range(depth):
        mod = _ada(c_pad, w_ada[l], b_ada[l][None, :])
        mod = mod[:bsz].reshape(bsz, N_MOD, d)
        mod = jnp.concatenate([mod, jnp.zeros((bsz, SUBLANES - N_MOD, d), F32)], axis=1)

        sb_q, sb_k, sb_v, g_q, g_k, g_v, g_r, g_lr = _inproj(xcur, mod, w_in_t, seq, l)
        as_seq = lambda a: a.reshape(bsz, seq, a.shape[-1])

        o_sb = _sb_attention(as_seq(sb_q), as_seq(sb_k), as_seq(sb_v), sb_norm_g[l][None, :])
        wg_pad = jnp.concatenate(
            [gla_w_gate[l], jnp.zeros((LANES - GLA_GATE_RANK, GLA_QK_WIDTH), F32)], axis=0).astype(BF16)
        o_gla = _gla(as_seq(g_q), as_seq(g_k), as_seq(g_v), as_seq(g_r), as_seq(g_lr),
                     wg_pad, gla_b_gate[l][None, :], gla_norm_g[l][None, :])

        xcur = _post(o_sb.reshape(t, SB_WIDTH), o_gla.reshape(t, GLA_V_WIDTH), xcur, mod,
                     w_out, ln1_g[l][None, :], ln1_b[l][None, :],
                     w_ff_gate, w_ff_up, conv_w[l], conv_b[l][None, :], w_down,
                     ln2_g[l][None, :], ln2_b[l][None, :], seq, alpha, l)
    return xcur.reshape(bsz, seq, d)
```

```python
import functools

import jax
import jax.numpy as jnp
from jax import lax
from jax.experimental import pallas as pl
from jax.experimental.pallas import tpu as pltpu

F32 = jnp.float32
BF16 = jnp.bfloat16

LANES = 128
SUBLANES = 8
VMEM_LIMIT_BYTES = 60 * 1024 * 1024

SB_HEADS = 8
SB_HEAD_DIM = 64
GLA_HEADS = 4
GLA_KEY_DIM = 64
GLA_VAL_DIM = 128
GLA_GATE_RANK = 16
GLA_GATE_TEMP = 16.0
GLA_CHUNK = 64
CONV_WIDTH = 3
N_MOD = 6
LN_EPS = 1e-5
RMS_EPS = 1e-6

SB_WIDTH = SB_HEADS * SB_HEAD_DIM
GLA_QK_WIDTH = GLA_HEADS * GLA_KEY_DIM
GLA_V_WIDTH = GLA_HEADS * GLA_VAL_DIM

SB_EXIT = 88.0
LOG2E = 1.4426950408889634
SB_EXIT_BITS = SB_EXIT * LOG2E
SB_MAX_BITS = 126.0
SB_MASKED_BITS = -1e30
SB_TK = 128
SB_TQ = 8 * SB_TK
SB_WIN_BLOCKS = 3
SB_WIN = SB_WIN_BLOCKS * SB_TK

GLA_STEP = 1024
ADA_K_PARTS = 4
INPROJ_TILE = 1024
ROW_TILE = 1024
POST_SUB_TILE = 512
FF_TILE = 256
STAGE_ROWS = 128
STAGE_SLOTS = 3
INPROJ_STAGE_SLOTS = 6


def _sb_logits(q, k):
    return jnp.minimum(_dot_nt(q, k), SB_MAX_BITS)


def _softplus2(z):
    return jnp.log2(1.0 + jnp.exp2(z))


def _split_bf16(a):
    hi = a.astype(BF16)
    lo = (a - hi.astype(F32)).astype(BF16)
    return hi, lo


def _dot(a, b):
    return jnp.dot(a, b, preferred_element_type=F32)


def _dot_nt(a, b):
    return lax.dot_general(a, b, (((1,), (1,)), ((), ())), preferred_element_type=F32)


def _dot_tn(a, b):
    return lax.dot_general(a, b, (((0,), (0,)), ((), ())), preferred_element_type=F32)


def _stage_weights_bf16(pairs, stage_ref, sem):
    slots, rows = stage_ref.shape[0], stage_ref.shape[1]
    jobs = [(src, dst, r0) for src, dst in pairs for r0 in range(0, src.shape[0], rows)]

    def window(n):
        return stage_ref.at[n % slots, :, pl.ds(0, jobs[n][0].shape[1])]

    def copy(n):
        src, _, r0 = jobs[n]
        return pltpu.make_async_copy(src.at[pl.ds(r0, rows), :], window(n), sem.at[n % slots])

    for n in range(min(slots - 1, len(jobs))):
        copy(n).start()
    for n, (_, dst, r0) in enumerate(jobs):
        if n + slots - 1 < len(jobs):
            copy(n + slots - 1).start()
        copy(n).wait()
        dst[pl.ds(r0, rows), :] = window(n)[...].astype(BF16)


def _layer_norm(v, g, b):
    mu = jnp.mean(v, axis=-1, keepdims=True)
    vc = v - mu
    var = jnp.mean(vc * vc, axis=-1, keepdims=True)
    return vc * lax.rsqrt(var + LN_EPS) * g + b


def _ada_kernel(c_ref, *refs):
    w_refs, b_ref, o_ref = refs[:-2], refs[-2], refs[-1]
    c = c_ref[...]
    ca = c / (1.0 + jnp.exp(-c))
    kp = w_refs[0].shape[0]
    acc = b_ref[...]
    for i, w_ref in enumerate(w_refs):
        ch, cl = _split_bf16(ca[:, i * kp:(i + 1) * kp])
        wh, wl = _split_bf16(w_ref[...])
        acc = acc + (_dot(ch, wh) + _dot(cl, wh) + _dot(ch, wl))
    o_ref[...] = acc


def _ada(c_pad, w, b):
    d, n = w.shape
    tn = 1024
    kp = d // ADA_K_PARTS
    return pl.pallas_call(
        _ada_kernel,
        out_shape=jax.ShapeDtypeStruct((c_pad.shape[0], n), F32),
        grid=(n // tn,),
        in_specs=[pl.BlockSpec((c_pad.shape[0], d), lambda j: (0, 0))]
                 + [pl.BlockSpec((kp, tn), functools.partial(lambda i, j: (i, j), i))
                    for i in range(ADA_K_PARTS)]
                 + [pl.BlockSpec((1, tn), lambda j: (0, j))],
        out_specs=pl.BlockSpec((c_pad.shape[0], tn), lambda j: (0, j)),
        compiler_params=pltpu.CompilerParams(
            dimension_semantics=("arbitrary",), vmem_limit_bytes=VMEM_LIMIT_BYTES),
        name="ada",
    )(c_pad, *([w] * ADA_K_PARTS), b)


_MAIN_WIDTH = 3 * SB_WIDTH + 2 * GLA_QK_WIDTH + GLA_V_WIDTH
_PROJ_GROUPS = (
    (0, 0, SB_WIDTH, SB_HEAD_DIM ** -0.5 * LOG2E),
    (0, SB_WIDTH, SB_WIDTH, 1.0),
    (0, 2 * SB_WIDTH, SB_WIDTH, 1.0),
    (0, 3 * SB_WIDTH, GLA_QK_WIDTH, GLA_KEY_DIM ** -0.5),
    (0, 3 * SB_WIDTH + GLA_QK_WIDTH, GLA_QK_WIDTH, 1.0),
    (0, 3 * SB_WIDTH + 2 * GLA_QK_WIDTH, GLA_V_WIDTH, 1.0),
    (1, 0, GLA_V_WIDTH, 1.0),
    (2, 0, LANES, 1.0),
)


def _stage_w_in(wt_hbm, w_main_ref, w_r_ref, w_lr_ref, stage_ref, sem):
    slots, rows, d = stage_ref.shape
    lr0 = _MAIN_WIDTH
    r0 = lr0 + GLA_GATE_RANK
    jobs = [(c, w_main_ref, c, rows) for c in range(0, lr0, rows)]
    jobs += [(lr0, w_lr_ref, 0, GLA_GATE_RANK)]
    jobs += [(r0 + c, w_r_ref, c, rows) for c in range(0, GLA_V_WIDTH, rows)]
    lane = lax.broadcasted_iota(jnp.int32, (d, rows), 1)

    def copy(n):
        return pltpu.make_async_copy(wt_hbm.at[pl.ds(jobs[n][0], rows), :],
                                     stage_ref.at[n % slots], sem.at[n % slots])

    for n in range(min(slots - 1, len(jobs))):
        copy(n).start()
    for n, (_, dst, col, valid) in enumerate(jobs):
        if n + slots - 1 < len(jobs):
            copy(n + slots - 1).start()
        copy(n).wait()
        blk = stage_ref[n % slots].T
        if valid < rows:
            blk = jnp.where(lane < valid, blk, 0.0)
        dst[:, pl.ds(col, rows)] = blk.astype(BF16)


def _inproj_kernel(layer, x_ref, mod_ref, w_hbm, *refs):
    n_out = len(_PROJ_GROUPS)
    out_refs = refs[:n_out]
    w_main_ref, w_r_ref, w_lr_ref, stage_ref, sem = refs[n_out:]

    @pl.when(pl.program_id(0) == 0)
    def _():
        _stage_w_in(w_hbm.at[layer], w_main_ref, w_r_ref, w_lr_ref, stage_ref, sem)

    shift = mod_ref[0:1, :]
    scale = mod_ref[1:2, :]
    h = (x_ref[...] * (1.0 + scale) + shift).astype(BF16)
    w_refs = (w_main_ref, w_r_ref, w_lr_ref)
    for (src, c0, width, mul), o_ref in zip(_PROJ_GROUPS, out_refs):
        acc = _dot(h, w_refs[src][:, c0:c0 + width])
        if mul != 1.0:
            acc = acc * mul
        o_ref[...] = acc.astype(o_ref.dtype)


def _inproj(x2d, mod, w_in_t, seq, layer):
    t, d = x2d.shape
    tm = INPROJ_TILE
    tiles_per_seq = seq // tm
    widths = [g[2] for g in _PROJ_GROUPS]
    return pl.pallas_call(
        functools.partial(_inproj_kernel, layer),
        out_shape=[jax.ShapeDtypeStruct((t, width), BF16) for width in widths],
        grid=(t // tm,),
        in_specs=[
            pl.BlockSpec((tm, d), lambda i: (i, 0)),
            pl.BlockSpec((None, SUBLANES, d), lambda i: (i // tiles_per_seq, 0, 0)),
            pl.BlockSpec(memory_space=pl.ANY),
        ],
        out_specs=[pl.BlockSpec((tm, width), lambda i: (i, 0)) for width in widths],
        scratch_shapes=[pltpu.VMEM((d, _MAIN_WIDTH), BF16), pltpu.VMEM((d, GLA_V_WIDTH), BF16),
                        pltpu.VMEM((d, LANES), BF16),
                        pltpu.VMEM((INPROJ_STAGE_SLOTS, LANES, d), F32),
                        pltpu.SemaphoreType.DMA((INPROJ_STAGE_SLOTS,))],
        compiler_params=pltpu.CompilerParams(
            dimension_semantics=("arbitrary",), vmem_limit_bytes=VMEM_LIMIT_BYTES),
        name="inproj",
    )(x2d, mod, w_in_t)


def _sb_suffix_sums(sp_blocks, umat):
    m = sp_blocks[0].shape[0]
    res = _dot(jnp.concatenate(sp_blocks, axis=0).astype(BF16), umat)
    cs, nearer = [], None
    for c in reversed(range(len(sp_blocks))):
        part = res[c * m:(c + 1) * m]
        loc, tot = part[:, :SB_TK], part[:, SB_TK:]
        cs.append(loc if nearer is None else loc + nearer)
        nearer = tot if nearer is None else nearer + tot
    return cs[::-1], nearer


def _sb_kernel(q_ref, k_ref, v_ref, g_ref, o_ref):
    tk = SB_TK
    pairs = q_ref.shape[1] // LANES
    nrb = SB_TQ // tk
    nb = SB_WIN_BLOCKS
    lane = lax.broadcasted_iota(jnp.int32, (tk, LANES), 1)
    head0 = lane < SB_HEAD_DIM

    uj = lax.broadcasted_iota(jnp.int32, (tk, 2 * tk), 0)
    us = lax.broadcasted_iota(jnp.int32, (tk, 2 * tk), 1)
    umat = jnp.where((uj >= us) | (us >= tk), 1.0, 0.0).astype(BF16)
    row = lax.broadcasted_iota(jnp.int32, (2 * tk, tk), 0) & (tk - 1)
    col = lax.broadcasted_iota(jnp.int32, (2 * tk, tk), 1)
    causal = col < row
    lanes = [slice(p * LANES, (p + 1) * LANES) for p in range(pairs)]

    first_blk = [pl.program_id(1) * nrb + r - (nb - 1) for r in range(nrb)]
    blk_rows = [[pl.ds(pl.multiple_of(jnp.maximum(first_blk[r] + c, 0) * tk, tk), tk)
                 for c in range(nb)] for r in range(nrb)]
    units = [(r, p) for r in range(nrb) for p in range(pairs)]

    qs, zs, sums, ws, accs = {}, {}, {}, {}, {}

    def stage_logits(u):
        r, p = u
        q = q_ref[r * tk:(r + 1) * tk, lanes[p]]
        zq = jnp.zeros_like(q)
        qs[u] = jnp.concatenate([jnp.where(head0, q, zq), jnp.where(head0, zq, q)], axis=0)
        kw = jnp.concatenate([k_ref[rows, lanes[p]] for rows in blk_rows[r]], axis=0)
        z = _sb_logits(qs[u], kw)
        zs[u] = [z[:, c * tk:(c + 1) * tk] for c in range(nb)]
        zs[u][-1] = jnp.where(causal, zs[u][-1], SB_MASKED_BITS)

    def stage_sums(u):
        sums[u] = _sb_suffix_sums([_softplus2(z) for z in zs[u]], umat)

    def stage_weights(u):
        args = [z - cs for z, cs in zip(zs[u], sums[u][0])]
        ws[u] = jnp.exp2(jnp.concatenate(args, axis=1)).astype(BF16)

    def stage_values(u):
        r, p = u
        vw = [v_ref[rows, lanes[p]] for rows in blk_rows[r]]
        vw = [jnp.where(first_blk[r] + c >= 0, vw[c], jnp.zeros_like(vw[c]))
              for c in range(nb - 1)] + vw[-1:]
        accs[u] = _dot(ws[u], jnp.concatenate(vw, axis=0))

    stages = (stage_logits, stage_sums, stage_weights, stage_values)
    for t in range(len(stages) + len(units) - 1):
        for n, u in enumerate(units):
            if 0 <= t - n < len(stages):
                stages[t - n](u)

    def not_done(cs_):
        m = cs_[0]
        for c in cs_[1:]:
            m = jnp.minimum(m, c)
        return (jnp.min(m) < SB_EXIT_BITS).astype(jnp.int32)

    def cond(s):
        return jnp.logical_and(s[0] >= 0, s[1] > 0)

    def write_out(r, accs_):
        outs = []
        for p in range(pairs):
            o = jnp.where(head0, accs_[p][:tk], accs_[p][tk:])
            sq = o * o
            ss0 = jnp.sum(jnp.where(head0, sq, 0.0), axis=-1, keepdims=True)
            ss1 = jnp.sum(jnp.where(head0, 0.0, sq), axis=-1, keepdims=True)
            ms = jnp.where(head0, ss0, ss1) * (1.0 / SB_HEAD_DIM)
            outs.append(o * lax.rsqrt(ms + RMS_EPS))
        o_ref[r * tk:(r + 1) * tk, :] = (jnp.concatenate(outs, axis=1) * g_ref[...]).astype(o_ref.dtype)

    for r in range(nrb):
        write_out(r, [accs[r, p] for p in range(pairs)])
    for r in range(nrb):
        carries = tuple(sums[r, p][1] for p in range(pairs))
        more = not_done(carries)

        def body(s, r=r):
            j, _, cs_, as_ = s
            rows = pl.ds(pl.multiple_of(j * tk, tk), tk)
            zb = [_sb_logits(qs[r, p], k_ref[rows, lanes[p]]) for p in range(pairs)]
            sb = [_sb_suffix_sums([_softplus2(zb[p])], umat) for p in range(pairs)]
            wb = [jnp.exp2(zb[p] - sb[p][0][0] - cs_[p]).astype(BF16) for p in range(pairs)]
            new_a = [as_[p] + _dot(wb[p], v_ref[rows, lanes[p]]) for p in range(pairs)]
            new_c = [cs_[p] + sb[p][1] for p in range(pairs)]
            return j - 1, not_done(new_c), tuple(new_c), tuple(new_a)

        @pl.when(jnp.logical_and(first_blk[r] - 1 >= 0, more > 0))
        def _(r=r, carries=carries, more=more, body=body):
            _, _, _, accs_ = lax.while_loop(
                cond, body,
                (first_blk[r] - 1, more, carries, tuple(accs[r, p] for p in range(pairs))))
            write_out(r, accs_)


def _sb_attention(q, k, v, g):
    b, s, wdt = q.shape
    return pl.pallas_call(
        _sb_kernel,
        out_shape=jax.ShapeDtypeStruct((b, s, wdt), BF16),
        grid=(b, s // SB_TQ),
        in_specs=[
            pl.BlockSpec((None, SB_TQ, wdt), lambda bi, i: (bi, i, 0)),
            pl.BlockSpec((None, s, wdt), lambda bi, i: (bi, 0, 0)),
            pl.BlockSpec((None, s, wdt), lambda bi, i: (bi, 0, 0)),
            pl.BlockSpec((1, wdt), lambda bi, i: (0, 0)),
        ],
        out_specs=pl.BlockSpec((None, SB_TQ, wdt), lambda bi, i: (bi, i, 0)),
        compiler_params=pltpu.CompilerParams(
            dimension_semantics=("arbitrary", "arbitrary"),
            vmem_limit_bytes=VMEM_LIMIT_BYTES),
        name="sb_attention",
    )(q, k, v, g)


def _gla_kernel(q_ref, k_ref, v_ref, r_ref, lr_ref, wg_ref, bg_ref, g_ref, o_ref, st_ref):
    c = GLA_CHUNK
    dv2 = 2 * GLA_VAL_DIM

    @pl.when(pl.program_id(2) == 0)
    def _():
        st_ref[...] = jnp.zeros_like(st_ref)

    u = _dot(lr_ref[...], wg_ref[...]) + bg_ref[...]
    log_a = (jnp.minimum(u, 0.0) - jnp.log(1.0 + jnp.exp(-jnp.abs(u)))) * (1.0 / GLA_GATE_TEMP)

    ti = lax.broadcasted_iota(jnp.int32, (c, c), 0)
    tj = lax.broadcasted_iota(jnp.int32, (c, c), 1)
    lower = ti >= tj
    tril = jnp.where(lower, 1.0, 0.0).astype(BF16)
    lane = lax.broadcasted_iota(jnp.int32, (c, LANES), 1)
    head0 = lane < GLA_KEY_DIM
    se = lax.broadcasted_iota(jnp.int32, (dv2, LANES), 0)
    sd = lax.broadcasted_iota(jnp.int32, (dv2, LANES), 1)
    st_mask = (se < GLA_VAL_DIM) == (sd < GLA_KEY_DIM)

    chunks = [slice(n * c, (n + 1) * c) for n in range(GLA_STEP // c)]
    la_hi, la_lo = _split_bf16(log_a)
    la_parts = jnp.concatenate([la_hi, la_lo], axis=1)
    bcum, dec = [], []
    for rows in chunks:
        res = _dot(tril, la_parts[rows])
        bcum.append(res[:, :LANES] + res[:, LANES:])
        dec.append(jnp.exp(bcum[-1][c - 1:c, :]))
    q_dec, k_inv, k_end = [], [], []
    for n, rows in enumerate(chunks):
        kf = k_ref[rows, :].astype(F32) * jnp.exp(-bcum[n])
        q_dec.append((q_ref[rows, :].astype(F32) * jnp.exp(bcum[n])).astype(BF16))
        k_inv.append(kf.astype(BF16))
        k_end.append((kf * dec[n]).astype(BF16))
    lower2 = jnp.concatenate([lower, lower], axis=0)
    o_intra = []
    for n, rows in enumerate(chunks):
        zq = jnp.zeros_like(q_dec[n])
        q2 = jnp.concatenate([jnp.where(head0, q_dec[n], zq), jnp.where(head0, zq, q_dec[n])], axis=0)
        a = jnp.where(lower2, _dot_nt(q2, k_inv[n]), 0.0).astype(BF16)
        o_intra.append(jnp.concatenate(
            [_dot(a[:c], v_ref[rows, :GLA_VAL_DIM]), _dot(a[c:], v_ref[rows, GLA_VAL_DIM:])], axis=1))
    kv = [jnp.where(st_mask, _dot_tn(v_ref[rows, :], k_end[n]), 0.0) for n, rows in enumerate(chunks)]
    st = st_ref[...]
    starts = []
    for n in range(len(chunks)):
        starts.append(st.astype(BF16))
        st = st * dec[n] + kv[n]
    st_ref[...] = st
    for n, rows in enumerate(chunks):
        o = o_intra[n] + _dot_nt(q_dec[n], starts[n])
        o0 = o[:, :GLA_VAL_DIM]
        o1 = o[:, GLA_VAL_DIM:]
        n0 = o0 * lax.rsqrt(jnp.mean(o0 * o0, axis=-1, keepdims=True) + RMS_EPS)
        n1 = o1 * lax.rsqrt(jnp.mean(o1 * o1, axis=-1, keepdims=True) + RMS_EPS)
        rf = r_ref[rows, :].astype(F32)
        gate = rf / (1.0 + jnp.exp(-rf))
        o_ref[rows, :] = (jnp.concatenate([n0, n1], axis=1) * g_ref[...] * gate).astype(o_ref.dtype)


def _gla(q, k, v, r, lr, wg, bg, g):
    b, s, _ = q.shape
    pairs = GLA_HEADS // 2
    step = GLA_STEP
    dv2 = 2 * GLA_VAL_DIM
    seq_map = lambda bi, p, t: (bi, t, p)
    return pl.pallas_call(
        _gla_kernel,
        out_shape=jax.ShapeDtypeStruct((b, s, GLA_V_WIDTH), BF16),
        grid=(b, pairs, s // step),
        in_specs=[
            pl.BlockSpec((None, step, LANES), seq_map),
            pl.BlockSpec((None, step, LANES), seq_map),
            pl.BlockSpec((None, step, dv2), seq_map),
            pl.BlockSpec((None, step, dv2), seq_map),
            pl.BlockSpec((None, step, LANES), lambda bi, p, t: (bi, t, 0)),
            pl.BlockSpec((LANES, LANES), lambda bi, p, t: (0, p)),
            pl.BlockSpec((1, LANES), lambda bi, p, t: (0, p)),
            pl.BlockSpec((1, dv2), lambda bi, p, t: (0, p)),
        ],
        out_specs=pl.BlockSpec((None, step, dv2), seq_map),
        scratch_shapes=[pltpu.VMEM((dv2, LANES), F32)],
        compiler_params=pltpu.CompilerParams(
            dimension_semantics=("arbitrary", "arbitrary", "arbitrary"),
            vmem_limit_bytes=VMEM_LIMIT_BYTES),
        name="gla",
    )(q, k, v, r, lr, wg, bg, g)


def _post_kernel(alpha, tiles_per_seq, layer, osb_ref, ogla_ref, x_ref, mod_ref, wo_hbm, g1_ref, b1_ref,
                 wg_hbm, wu_hbm, cw_ref, cb_ref, wd_hbm, g_ref, b_ref,
                 o_ref, halo_ref, act_ref, x1_ref, wo_ref, wg_ref, wu_ref, wd_ref, stage_ref, sem):
    tm = POST_SUB_TILE
    nsub = x_ref.shape[0] // tm
    d_ff = wg_ref.shape[1]
    n_sb = osb_ref.shape[1]
    first = (pl.program_id(0) % tiles_per_seq) == 0

    @pl.when(pl.program_id(0) == 0)
    def _():
        _stage_weights_bf16(((wo_hbm.at[layer], wo_ref), (wg_hbm.at[layer], wg_ref),
                             (wu_hbm.at[layer], wu_ref), (wd_hbm.at[layer], wd_ref)), stage_ref, sem)

    @pl.when(first)
    def _():
        halo_ref[...] = jnp.zeros_like(halo_ref)

    top = lax.broadcasted_iota(jnp.int32, (SUBLANES, FF_TILE), 0)
    hs = []
    for j in range(nsub):
        rows = slice(j * tm, (j + 1) * tm)
        mix = _dot(osb_ref[rows, :], wo_ref[:n_sb, :]) + _dot(ogla_ref[rows, :], wo_ref[n_sb:, :])
        x1_ref[rows, :] = _layer_norm(alpha * x_ref[rows, :] + (1.0 + mod_ref[2:3, :]) * mix,
                                      g1_ref[...], b1_ref[...])
        hs.append((x1_ref[rows, :] * (1.0 + mod_ref[4:5, :]) + mod_ref[3:4, :]).astype(BF16))
    for j in range(nsub):
        rows = slice(j * tm, (j + 1) * tm)
        h = hs[j]
        for f in range(d_ff // FF_TILE):
            cols = slice(f * FF_TILE, (f + 1) * FF_TILE)
            gt = _dot(h, wg_ref[:, cols])
            up = _dot(h, wu_ref[:, cols])
            prev = halo_ref[:, cols]
            halo_ref[:, cols] = gt[tm - SUBLANES:, :]
            r1 = pltpu.roll(gt, 1, 0)
            r2 = pltpu.roll(gt, 2, 0)
            t1 = jnp.where(top < 1, pltpu.roll(prev, 1, 0), r1[:SUBLANES])
            t2 = jnp.where(top < 2, pltpu.roll(prev, 2, 0), r2[:SUBLANES])
            g1 = jnp.concatenate([t1, r1[SUBLANES:]], axis=0)
            g2 = jnp.concatenate([t2, r2[SUBLANES:]], axis=0)
            conv = (g2 * cw_ref[0:1, cols] + g1 * cw_ref[1:2, cols] + gt * cw_ref[2:3, cols]
                    + cb_ref[:, cols])
            act_ref[rows, cols] = (conv / (1.0 + jnp.exp(-conv)) * up).astype(BF16)
        y = _dot(act_ref[rows, :], wd_ref[...])
        o_ref[rows, :] = _layer_norm(alpha * x1_ref[rows, :] + (1.0 + mod_ref[5:6, :]) * y,
                                     g_ref[...], b_ref[...])


def _post(o_sb, o_gla, x2d, mod, w_o, ln1_g, ln1_b, wg, wu, cw, cb, wd, ln_g, ln_b, seq, alpha, layer):
    t, d = x2d.shape
    d_ff = wg.shape[2]
    tm = ROW_TILE
    tiles_per_seq = seq // tm
    row_map = lambda i: (i, 0)
    fixed = lambda i: (0, 0)
    hbm = pl.BlockSpec(memory_space=pl.ANY)
    stage_width = max(w.shape[2] for w in (w_o, wg, wu, wd))
    return pl.pallas_call(
        functools.partial(_post_kernel, alpha, tiles_per_seq, layer),
        out_shape=jax.ShapeDtypeStruct((t, d), F32),
        grid=(t // tm,),
        in_specs=[
            pl.BlockSpec((tm, o_sb.shape[1]), row_map),
            pl.BlockSpec((tm, o_gla.shape[1]), row_map),
            pl.BlockSpec((tm, d), row_map),
            pl.BlockSpec((None, SUBLANES, d), lambda i: (i // tiles_per_seq, 0, 0)),
            hbm,
            pl.BlockSpec((1, d), fixed),
            pl.BlockSpec((1, d), fixed),
            hbm,
            hbm,
            pl.BlockSpec(cw.shape, fixed),
            pl.BlockSpec(cb.shape, fixed),
            hbm,
            pl.BlockSpec((1, d), fixed),
            pl.BlockSpec((1, d), fixed),
        ],
        out_specs=pl.BlockSpec((tm, d), row_map),
        scratch_shapes=[pltpu.VMEM((SUBLANES, d_ff), F32), pltpu.VMEM((tm, d_ff), BF16),
                        pltpu.VMEM((tm, d), F32),
                        pltpu.VMEM(w_o.shape[1:], BF16), pltpu.VMEM(wg.shape[1:], BF16),
                        pltpu.VMEM(wu.shape[1:], BF16), pltpu.VMEM(wd.shape[1:], BF16),
                        pltpu.VMEM((STAGE_SLOTS, STAGE_ROWS, stage_width), F32),
                        pltpu.SemaphoreType.DMA((STAGE_SLOTS,))],
        compiler_params=pltpu.CompilerParams(
            dimension_semantics=("arbitrary",), vmem_limit_bytes=VMEM_LIMIT_BYTES),
        name="post",
    )(o_sb, o_gla, x2d, mod, w_o, ln1_g, ln1_b, wg, wu, cw, cb, wd, ln_g, ln_b)


def kernel(x, c, w_ada, b_ada, w_in, gla_w_gate, gla_b_gate, sb_norm_g, gla_norm_g, w_out,
           ln1_g, ln1_b, w_ff_gate, w_ff_up, conv_w, conv_b, w_down, ln2_g, ln2_b):
    bsz, seq, d = x.shape
    depth = w_ada.shape[0]
    alpha = float((2 * depth) ** 0.25)
    t = bsz * seq

    c_pad = jnp.concatenate([c, jnp.zeros((SUBLANES - bsz, d), c.dtype)], axis=0)
    w_in_t = jnp.swapaxes(w_in, 1, 2)
    xcur = x.reshape(t, d)
    for l in range(depth):
        mod = _ada(c_pad, w_ada[l], b_ada[l][None, :])
        mod = mod[:bsz].reshape(bsz, N_MOD, d)
        mod = jnp.concatenate([mod, jnp.zeros((bsz, SUBLANES - N_MOD, d), F32)], axis=1)

        sb_q, sb_k, sb_v, g_q, g_k, g_v, g_r, g_lr = _inproj(xcur, mod, w_in_t, seq, l)
        as_seq = lambda a: a.reshape(bsz, seq, a.shape[-1])

        o_sb = _sb_attention(as_seq(sb_q), as_seq(sb_k), as_seq(sb_v), sb_norm_g[l][None, :])
        wg_pad = jnp.concatenate(
            [gla_w_gate[l], jnp.zeros((LANES - GLA_GATE_RANK, GLA_QK_WIDTH), F32)], axis=0).astype(BF16)
        o_gla = _gla(as_seq(g_q), as_seq(g_k), as_seq(g_v), as_seq(g_r), as_seq(g_lr),
                     wg_pad, gla_b_gate[l][None, :], gla_norm_g[l][None, :])

        xcur = _post(o_sb.reshape(t, SB_WIDTH), o_gla.reshape(t, GLA_V_WIDTH), xcur, mod,
                     w_out, ln1_g[l][None, :], ln1_b[l][None, :],
                     w_ff_gate, w_ff_up, conv_w[l], conv_b[l][None, :], w_down,
                     ln2_g[l][None, :], ln2_b[l][None, :], seq, alpha, l)
    return xcur.reshape(bsz, seq, d)
```

```python
import functools

import jax
import jax.numpy as jnp
from jax import lax
from jax.experimental import pallas as pl
from jax.experimental.pallas import tpu as pltpu

F32 = jnp.float32
BF16 = jnp.bfloat16

LANES = 128
SUBLANES = 8
VMEM_LIMIT_BYTES = 60 * 1024 * 1024

SB_HEADS = 8
SB_HEAD_DIM = 64
GLA_HEADS = 4
GLA_KEY_DIM = 64
GLA_VAL_DIM = 128
GLA_GATE_RANK = 16
GLA_GATE_TEMP = 16.0
GLA_CHUNK = 64
CONV_WIDTH = 3
N_MOD = 6
LN_EPS = 1e-5
RMS_EPS = 1e-6

SB_WIDTH = SB_HEADS * SB_HEAD_DIM
GLA_QK_WIDTH = GLA_HEADS * GLA_KEY_DIM
GLA_V_WIDTH = GLA_HEADS * GLA_VAL_DIM

SB_EXIT = 88.0
LOG2E = 1.4426950408889634
SB_EXIT_BITS = SB_EXIT * LOG2E
SB_MAX_BITS = 126.0
SB_MASKED_BITS = -1e30
SB_TK = 128
SB_TQ = 8 * SB_TK
SB_WIN_BLOCKS = 3
SB_WIN = SB_WIN_BLOCKS * SB_TK

GLA_STEP = 1024
INPROJ_TILE = 1024
ROW_TILE = 1024
POST_SUB_TILE = 512
FF_TILE = 256
STAGE_ROWS = 128
STAGE_SLOTS = 3
INPROJ_STAGE_SLOTS = 6


def _sb_logits(q, k):
    return jnp.minimum(_dot_nt(q, k), SB_MAX_BITS)


def _softplus2(z):
    return jnp.log2(1.0 + jnp.exp2(z))


def _split_bf16(a):
    hi = a.astype(BF16)
    lo = (a - hi.astype(F32)).astype(BF16)
    return hi, lo


def _dot(a, b):
    return jnp.dot(a, b, preferred_element_type=F32)


def _dot_nt(a, b):
    return lax.dot_general(a, b, (((1,), (1,)), ((), ())), preferred_element_type=F32)


def _dot_tn(a, b):
    return lax.dot_general(a, b, (((0,), (0,)), ((), ())), preferred_element_type=F32)


def _stage_weights_bf16(pairs, stage_ref, sem):
    slots, rows = stage_ref.shape[0], stage_ref.shape[1]
    jobs = [(src, dst, r0) for src, dst in pairs for r0 in range(0, src.shape[0], rows)]

    def window(n):
        return stage_ref.at[n % slots, :, pl.ds(0, jobs[n][0].shape[1])]

    def copy(n):
        src, _, r0 = jobs[n]
        return pltpu.make_async_copy(src.at[pl.ds(r0, rows), :], window(n), sem.at[n % slots])

    for n in range(min(slots - 1, len(jobs))):
        copy(n).start()
    for n, (_, dst, r0) in enumerate(jobs):
        if n + slots - 1 < len(jobs):
            copy(n + slots - 1).start()
        copy(n).wait()
        dst[pl.ds(r0, rows), :] = window(n)[...].astype(BF16)


def _layer_norm(v, g, b):
    mu = jnp.mean(v, axis=-1, keepdims=True)
    vc = v - mu
    var = jnp.mean(vc * vc, axis=-1, keepdims=True)
    return vc * lax.rsqrt(var + LN_EPS) * g + b


def _ada_kernel(c_ref, w_ref, b_ref, o_ref):
    c = c_ref[...]
    ca = c / (1.0 + jnp.exp(-c))
    ch, cl = _split_bf16(ca)
    wh, wl = _split_bf16(w_ref[...])
    o_ref[...] = _dot(ch, wh) + _dot(cl, wh) + _dot(ch, wl) + b_ref[...]


def _ada(c_pad, w, b):
    d, n = w.shape
    tn = 1024
    return pl.pallas_call(
        _ada_kernel,
        out_shape=jax.ShapeDtypeStruct((c_pad.shape[0], n), F32),
        grid=(n // tn,),
        in_specs=[
            pl.BlockSpec((c_pad.shape[0], d), lambda j: (0, 0)),
            pl.BlockSpec((d, tn), lambda j: (0, j)),
            pl.BlockSpec((1, tn), lambda j: (0, j)),
        ],
        out_specs=pl.BlockSpec((c_pad.shape[0], tn), lambda j: (0, j)),
        compiler_params=pltpu.CompilerParams(
            dimension_semantics=("arbitrary",), vmem_limit_bytes=VMEM_LIMIT_BYTES),
        name="ada",
    )(c_pad, w, b)


_MAIN_WIDTH = 3 * SB_WIDTH + 2 * GLA_QK_WIDTH + GLA_V_WIDTH
_PROJ_GROUPS = (
    (0, 0, SB_WIDTH, SB_HEAD_DIM ** -0.5 * LOG2E),
    (0, SB_WIDTH, SB_WIDTH, 1.0),
    (0, 2 * SB_WIDTH, SB_WIDTH, 1.0),
    (0, 3 * SB_WIDTH, GLA_QK_WIDTH, GLA_KEY_DIM ** -0.5),
    (0, 3 * SB_WIDTH + GLA_QK_WIDTH, GLA_QK_WIDTH, 1.0),
    (0, 3 * SB_WIDTH + 2 * GLA_QK_WIDTH, GLA_V_WIDTH, 1.0),
    (1, 0, GLA_V_WIDTH, 1.0),
    (2, 0, LANES, 1.0),
)


def _stage_w_in(wt_hbm, w_main_ref, w_r_ref, w_lr_ref, stage_ref, sem):
    slots, rows, d = stage_ref.shape
    lr0 = _MAIN_WIDTH
    r0 = lr0 + GLA_GATE_RANK
    jobs = [(c, w_main_ref, c, rows) for c in range(0, lr0, rows)]
    jobs += [(lr0, w_lr_ref, 0, GLA_GATE_RANK)]
    jobs += [(r0 + c, w_r_ref, c, rows) for c in range(0, GLA_V_WIDTH, rows)]
    lane = lax.broadcasted_iota(jnp.int32, (d, rows), 1)

    def copy(n):
        return pltpu.make_async_copy(wt_hbm.at[pl.ds(jobs[n][0], rows), :],
                                     stage_ref.at[n % slots], sem.at[n % slots])

    for n in range(min(slots - 1, len(jobs))):
        copy(n).start()
    for n, (_, dst, col, valid) in enumerate(jobs):
        if n + slots - 1 < len(jobs):
            copy(n + slots - 1).start()
        copy(n).wait()
        blk = stage_ref[n % slots].T
        if valid < rows:
            blk = jnp.where(lane < valid, blk, 0.0)
        dst[:, pl.ds(col, rows)] = blk.astype(BF16)


def _inproj_kernel(layer, x_ref, mod_ref, w_hbm, *refs):
    n_out = len(_PROJ_GROUPS)
    out_refs = refs[:n_out]
    w_main_ref, w_r_ref, w_lr_ref, stage_ref, sem = refs[n_out:]

    @pl.when(pl.program_id(0) == 0)
    def _():
        _stage_w_in(w_hbm.at[layer], w_main_ref, w_r_ref, w_lr_ref, stage_ref, sem)

    shift = mod_ref[0:1, :]
    scale = mod_ref[1:2, :]
    h = (x_ref[...] * (1.0 + scale) + shift).astype(BF16)
    w_refs = (w_main_ref, w_r_ref, w_lr_ref)
    for (src, c0, width, mul), o_ref in zip(_PROJ_GROUPS, out_refs):
        acc = _dot(h, w_refs[src][:, c0:c0 + width])
        if mul != 1.0:
            acc = acc * mul
        o_ref[...] = acc.astype(o_ref.dtype)


def _inproj(x2d, mod, w_in_t, seq, layer):
    t, d = x2d.shape
    tm = INPROJ_TILE
    tiles_per_seq = seq // tm
    widths = [g[2] for g in _PROJ_GROUPS]
    return pl.pallas_call(
        functools.partial(_inproj_kernel, layer),
        out_shape=[jax.ShapeDtypeStruct((t, width), BF16) for width in widths],
        grid=(t // tm,),
        in_specs=[
            pl.BlockSpec((tm, d), lambda i: (i, 0)),
            pl.BlockSpec((None, SUBLANES, d), lambda i: (i // tiles_per_seq, 0, 0)),
            pl.BlockSpec(memory_space=pl.ANY),
        ],
        out_specs=[pl.BlockSpec((tm, width), lambda i: (i, 0)) for width in widths],
        scratch_shapes=[pltpu.VMEM((d, _MAIN_WIDTH), BF16), pltpu.VMEM((d, GLA_V_WIDTH), BF16),
                        pltpu.VMEM((d, LANES), BF16),
                        pltpu.VMEM((INPROJ_STAGE_SLOTS, LANES, d), F32),
                        pltpu.SemaphoreType.DMA((INPROJ_STAGE_SLOTS,))],
        compiler_params=pltpu.CompilerParams(
            dimension_semantics=("arbitrary",), vmem_limit_bytes=VMEM_LIMIT_BYTES),
        name="inproj",
    )(x2d, mod, w_in_t)


def _sb_suffix_sums(sp_blocks, umat):
    m = sp_blocks[0].shape[0]
    res = _dot(jnp.concatenate(sp_blocks, axis=0).astype(BF16), umat)
    cs, nearer = [], None
    for c in reversed(range(len(sp_blocks))):
        part = res[c * m:(c + 1) * m]
        loc, tot = part[:, :SB_TK], part[:, SB_TK:]
        cs.append(loc if nearer is None else loc + nearer)
        nearer = tot if nearer is None else nearer + tot
    return cs[::-1], nearer


def _sb_kernel(q_ref, k_ref, v_ref, g_ref, o_ref):
    tk = SB_TK
    pairs = q_ref.shape[1] // LANES
    nrb = SB_TQ // tk
    nb = SB_WIN_BLOCKS
    lane = lax.broadcasted_iota(jnp.int32, (tk, LANES), 1)
    head0 = lane < SB_HEAD_DIM

    uj = lax.broadcasted_iota(jnp.int32, (tk, 2 * tk), 0)
    us = lax.broadcasted_iota(jnp.int32, (tk, 2 * tk), 1)
    umat = jnp.where((uj >= us) | (us >= tk), 1.0, 0.0).astype(BF16)
    row = lax.broadcasted_iota(jnp.int32, (2 * tk, tk), 0) & (tk - 1)
    col = lax.broadcasted_iota(jnp.int32, (2 * tk, tk), 1)
    causal = col < row
    lanes = [slice(p * LANES, (p + 1) * LANES) for p in range(pairs)]

    first_blk = [pl.program_id(1) * nrb + r - (nb - 1) for r in range(nrb)]
    blk_rows = [[pl.ds(pl.multiple_of(jnp.maximum(first_blk[r] + c, 0) * tk, tk), tk)
                 for c in range(nb)] for r in range(nrb)]
    units = [(r, p) for r in range(nrb) for p in range(pairs)]

    qs, zs, sums, ws, accs = {}, {}, {}, {}, {}

    def stage_logits(u):
        r, p = u
        q = q_ref[r * tk:(r + 1) * tk, lanes[p]]
        zq = jnp.zeros_like(q)
        qs[u] = jnp.concatenate([jnp.where(head0, q, zq), jnp.where(head0, zq, q)], axis=0)
        kw = jnp.concatenate([k_ref[rows, lanes[p]] for rows in blk_rows[r]], axis=0)
        z = _sb_logits(qs[u], kw)
        zs[u] = [z[:, c * tk:(c + 1) * tk] for c in range(nb)]
        zs[u][-1] = jnp.where(causal, zs[u][-1], SB_MASKED_BITS)

    def stage_sums(u):
        sums[u] = _sb_suffix_sums([_softplus2(z) for z in zs[u]], umat)

    def stage_weights(u):
        args = [z - cs for z, cs in zip(zs[u], sums[u][0])]
        ws[u] = jnp.exp2(jnp.concatenate(args, axis=1)).astype(BF16)

    def stage_values(u):
        r, p = u
        vw = [v_ref[rows, lanes[p]] for rows in blk_rows[r]]
        vw = [jnp.where(first_blk[r] + c >= 0, vw[c], jnp.zeros_like(vw[c]))
              for c in range(nb - 1)] + vw[-1:]
        accs[u] = _dot(ws[u], jnp.concatenate(vw, axis=0))

    stages = (stage_logits, stage_sums, stage_weights, stage_values)
    for t in range(len(stages) + len(units) - 1):
        for n, u in enumerate(units):
            if 0 <= t - n < len(stages):
                stages[t - n](u)

    def not_done(cs_):
        m = cs_[0]
        for c in cs_[1:]:
            m = jnp.minimum(m, c)
        return (jnp.min(m) < SB_EXIT_BITS).astype(jnp.int32)

    def cond(s):
        return jnp.logical_and(s[0] >= 0, s[1] > 0)

    def write_out(r, accs_):
        outs = []
        for p in range(pairs):
            o = jnp.where(head0, accs_[p][:tk], accs_[p][tk:])
            sq = o * o
            ss0 = jnp.sum(jnp.where(head0, sq, 0.0), axis=-1, keepdims=True)
            ss1 = jnp.sum(jnp.where(head0, 0.0, sq), axis=-1, keepdims=True)
            ms = jnp.where(head0, ss0, ss1) * (1.0 / SB_HEAD_DIM)
            outs.append(o * lax.rsqrt(ms + RMS_EPS))
        o_ref[r * tk:(r + 1) * tk, :] = (jnp.concatenate(outs, axis=1) * g_ref[...]).astype(o_ref.dtype)

    for r in range(nrb):
        write_out(r, [accs[r, p] for p in range(pairs)])
    for r in range(nrb):
        carries = tuple(sums[r, p][1] for p in range(pairs))
        more = not_done(carries)

        def body(s, r=r):
            j, _, cs_, as_ = s
            rows = pl.ds(pl.multiple_of(j * tk, tk), tk)
            zb = [_sb_logits(qs[r, p], k_ref[rows, lanes[p]]) for p in range(pairs)]
            sb = [_sb_suffix_sums([_softplus2(zb[p])], umat) for p in range(pairs)]
            wb = [jnp.exp2(zb[p] - sb[p][0][0] - cs_[p]).astype(BF16) for p in range(pairs)]
            new_a = [as_[p] + _dot(wb[p], v_ref[rows, lanes[p]]) for p in range(pairs)]
            new_c = [cs_[p] + sb[p][1] for p in range(pairs)]
            return j - 1, not_done(new_c), tuple(new_c), tuple(new_a)

        @pl.when(jnp.logical_and(first_blk[r] - 1 >= 0, more > 0))
        def _(r=r, carries=carries, more=more, body=body):
            _, _, _, accs_ = lax.while_loop(
                cond, body,
                (first_blk[r] - 1, more, carries, tuple(accs[r, p] for p in range(pairs))))
            write_out(r, accs_)


def _sb_attention(q, k, v, g):
    b, s, wdt = q.shape
    return pl.pallas_call(
        _sb_kernel,
        out_shape=jax.ShapeDtypeStruct((b, s, wdt), BF16),
        grid=(b, s // SB_TQ),
        in_specs=[
            pl.BlockSpec((None, SB_TQ, wdt), lambda bi, i: (bi, i, 0)),
            pl.BlockSpec((None, s, wdt), lambda bi, i: (bi, 0, 0)),
            pl.BlockSpec((None, s, wdt), lambda bi, i: (bi, 0, 0)),
            pl.BlockSpec((1, wdt), lambda bi, i: (0, 0)),
        ],
        out_specs=pl.BlockSpec((None, SB_TQ, wdt), lambda bi, i: (bi, i, 0)),
        compiler_params=pltpu.CompilerParams(
            dimension_semantics=("arbitrary", "arbitrary"),
            vmem_limit_bytes=VMEM_LIMIT_BYTES),
        name="sb_attention",
    )(q, k, v, g)


def _gla_kernel(q_ref, k_ref, v_ref, r_ref, lr_ref, wg_ref, bg_ref, g_ref, o_ref, st_ref):
    c = GLA_CHUNK

    @pl.when(pl.program_id(2) == 0)
    def _():
        st_ref[...] = jnp.zeros_like(st_ref)

    u = _dot(lr_ref[...], wg_ref[...]) + bg_ref[...]
    log_a = (jnp.minimum(u, 0.0) - jnp.log(1.0 + jnp.exp(-jnp.abs(u)))) * (1.0 / GLA_GATE_TEMP)

    ti = lax.broadcasted_iota(jnp.int32, (c, c), 0)
    tj = lax.broadcasted_iota(jnp.int32, (c, c), 1)
    lower = ti >= tj
    tril = jnp.where(lower, 1.0, 0.0).astype(BF16)
    lane = lax.broadcasted_iota(jnp.int32, (c, LANES), 1)
    head0 = lane < GLA_KEY_DIM

    chunks = [slice(n * c, (n + 1) * c) for n in range(GLA_STEP // c)]
    la_hi, la_lo = _split_bf16(log_a)
    la_parts = jnp.concatenate([la_hi, la_lo], axis=1)
    bcum, dec = [], []
    for rows in chunks:
        res = _dot(tril, la_parts[rows])
        bcum.append(res[:, :LANES] + res[:, LANES:])
        dec.append(jnp.exp(bcum[-1][c - 1:c, :]))
    q2, k_inv, k_end2 = [], [], []
    for n, rows in enumerate(chunks):
        kf = k_ref[rows, :].astype(F32) * jnp.exp(-bcum[n])
        q_dec = (q_ref[rows, :].astype(F32) * jnp.exp(bcum[n])).astype(BF16)
        k_end = (kf * dec[n]).astype(BF16)
        zq = jnp.zeros_like(q_dec)
        q2.append(jnp.concatenate([jnp.where(head0, q_dec, zq), jnp.where(head0, zq, q_dec)], axis=0))
        k_end2.append(jnp.concatenate([jnp.where(head0, k_end, zq), jnp.where(head0, zq, k_end)], axis=0))
        k_inv.append(kf.astype(BF16))
    lower2 = jnp.concatenate([lower, lower], axis=0)
    o_intra = []
    for n, rows in enumerate(chunks):
        a = jnp.where(lower2, _dot_nt(q2[n], k_inv[n]), 0.0).astype(BF16)
        o_intra.append((_dot(a[:c], v_ref[rows, :GLA_VAL_DIM]), _dot(a[c:], v_ref[rows, GLA_VAL_DIM:])))
    kv = [_dot_tn(jnp.concatenate([v_ref[rows, :GLA_VAL_DIM], v_ref[rows, GLA_VAL_DIM:]], axis=0), k_end2[n])
          for n, rows in enumerate(chunks)]
    st = st_ref[...]
    starts = []
    for n in range(len(chunks)):
        starts.append(st.astype(BF16))
        st = st * dec[n] + kv[n]
    st_ref[...] = st
    for n, rows in enumerate(chunks):
        o_inter = _dot_nt(q2[n], starts[n])
        o0 = o_intra[n][0] + o_inter[:c]
        o1 = o_intra[n][1] + o_inter[c:]
        n0 = o0 * lax.rsqrt(jnp.mean(o0 * o0, axis=-1, keepdims=True) + RMS_EPS)
        n1 = o1 * lax.rsqrt(jnp.mean(o1 * o1, axis=-1, keepdims=True) + RMS_EPS)
        rf = r_ref[rows, :].astype(F32)
        gate = rf / (1.0 + jnp.exp(-rf))
        o_ref[rows, :] = (jnp.concatenate([n0, n1], axis=1) * g_ref[...] * gate).astype(o_ref.dtype)


def _gla(q, k, v, r, lr, wg, bg, g):
    b, s, _ = q.shape
    pairs = GLA_HEADS // 2
    step = GLA_STEP
    dv2 = 2 * GLA_VAL_DIM
    seq_map = lambda bi, p, t: (bi, t, p)
    return pl.pallas_call(
        _gla_kernel,
        out_shape=jax.ShapeDtypeStruct((b, s, GLA_V_WIDTH), BF16),
        grid=(b, pairs, s // step),
        in_specs=[
            pl.BlockSpec((None, step, LANES), seq_map),
            pl.BlockSpec((None, step, LANES), seq_map),
            pl.BlockSpec((None, step, dv2), seq_map),
            pl.BlockSpec((None, step, dv2), seq_map),
            pl.BlockSpec((None, step, LANES), lambda bi, p, t: (bi, t, 0)),
            pl.BlockSpec((LANES, LANES), lambda bi, p, t: (0, p)),
            pl.BlockSpec((1, LANES), lambda bi, p, t: (0, p)),
            pl.BlockSpec((1, dv2), lambda bi, p, t: (0, p)),
        ],
        out_specs=pl.BlockSpec((None, step, dv2), seq_map),
        scratch_shapes=[pltpu.VMEM((GLA_VAL_DIM, LANES), F32)],
        compiler_params=pltpu.CompilerParams(
            dimension_semantics=("arbitrary", "arbitrary", "arbitrary"),
            vmem_limit_bytes=VMEM_LIMIT_BYTES),
        name="gla",
    )(q, k, v, r, lr, wg, bg, g)


def _post_kernel(alpha, tiles_per_seq, layer, osb_ref, ogla_ref, x_ref, mod_ref, wo_hbm, g1_ref, b1_ref,
                 wg_hbm, wu_hbm, cw_ref, cb_ref, wd_hbm, g_ref, b_ref,
                 o_ref, halo_ref, act_ref, x1_ref, wo_ref, wg_ref, wu_ref, wd_ref, stage_ref, sem):
    tm = POST_SUB_TILE
    nsub = x_ref.shape[0] // tm
    d_ff = wg_ref.shape[1]
    n_sb = osb_ref.shape[1]
    first = (pl.program_id(0) % tiles_per_seq) == 0

    @pl.when(pl.program_id(0) == 0)
    def _():
        _stage_weights_bf16(((wo_hbm.at[layer], wo_ref), (wg_hbm.at[layer], wg_ref),
                             (wu_hbm.at[layer], wu_ref), (wd_hbm.at[layer], wd_ref)), stage_ref, sem)

    @pl.when(first)
    def _():
        halo_ref[...] = jnp.zeros_like(halo_ref)

    top = lax.broadcasted_iota(jnp.int32, (SUBLANES, FF_TILE), 0)
    hs = []
    for j in range(nsub):
        rows = slice(j * tm, (j + 1) * tm)
        mix = _dot(osb_ref[rows, :], wo_ref[:n_sb, :]) + _dot(ogla_ref[rows, :], wo_ref[n_sb:, :])
        x1_ref[rows, :] = _layer_norm(alpha * x_ref[rows, :] + (1.0 + mod_ref[2:3, :]) * mix,
                                      g1_ref[...], b1_ref[...])
        hs.append((x1_ref[rows, :] * (1.0 + mod_ref[4:5, :]) + mod_ref[3:4, :]).astype(BF16))
    for j in range(nsub):
        rows = slice(j * tm, (j + 1) * tm)
        h = hs[j]
        for f in range(d_ff // FF_TILE):
            cols = slice(f * FF_TILE, (f + 1) * FF_TILE)
            gt = _dot(h, wg_ref[:, cols])
            up = _dot(h, wu_ref[:, cols])
            prev = halo_ref[:, cols]
            halo_ref[:, cols] = gt[tm - SUBLANES:, :]
            r1 = pltpu.roll(gt, 1, 0)
            r2 = pltpu.roll(gt, 2, 0)
            t1 = jnp.where(top < 1, pltpu.roll(prev, 1, 0), r1[:SUBLANES])
            t2 = jnp.where(top < 2, pltpu.roll(prev, 2, 0), r2[:SUBLANES])
            g1 = jnp.concatenate([t1, r1[SUBLANES:]], axis=0)
            g2 = jnp.concatenate([t2, r2[SUBLANES:]], axis=0)
            conv = (g2 * cw_ref[0:1, cols] + g1 * cw_ref[1:2, cols] + gt * cw_ref[2:3, cols]
                    + cb_ref[:, cols])
            act_ref[rows, cols] = (conv / (1.0 + jnp.exp(-conv)) * up).astype(BF16)
        y = _dot(act_ref[rows, :], wd_ref[...])
        o_ref[rows, :] = _layer_norm(alpha * x1_ref[rows, :] + (1.0 + mod_ref[5:6, :]) * y,
                                     g_ref[...], b_ref[...])


def _post(o_sb, o_gla, x2d, mod, w_o, ln1_g, ln1_b, wg, wu, cw, cb, wd, ln_g, ln_b, seq, alpha, layer):
    t, d = x2d.shape
    d_ff = wg.shape[2]
    tm = ROW_TILE
    tiles_per_seq = seq // tm
    row_map = lambda i: (i, 0)
    fixed = lambda i: (0, 0)
    hbm = pl.BlockSpec(memory_space=pl.ANY)
    stage_width = max(w.shape[2] for w in (w_o, wg, wu, wd))
    return pl.pallas_call(
        functools.partial(_post_kernel, alpha, tiles_per_seq, layer),
        out_shape=jax.ShapeDtypeStruct((t, d), F32),
        grid=(t // tm,),
        in_specs=[
            pl.BlockSpec((tm, o_sb.shape[1]), row_map),
            pl.BlockSpec((tm, o_gla.shape[1]), row_map),
            pl.BlockSpec((tm, d), row_map),
            pl.BlockSpec((None, SUBLANES, d), lambda i: (i // tiles_per_seq, 0, 0)),
            hbm,
            pl.BlockSpec((1, d), fixed),
            pl.BlockSpec((1, d), fixed),
            hbm,
            hbm,
            pl.BlockSpec(cw.shape, fixed),
            pl.BlockSpec(cb.shape, fixed),
            hbm,
            pl.BlockSpec((1, d), fixed),
            pl.BlockSpec((1, d), fixed),
        ],
        out_specs=pl.BlockSpec((tm, d), row_map),
        scratch_shapes=[pltpu.VMEM((SUBLANES, d_ff), F32), pltpu.VMEM((tm, d_ff), BF16),
                        pltpu.VMEM((tm, d), F32),
                        pltpu.VMEM(w_o.shape[1:], BF16), pltpu.VMEM(wg.shape[1:], BF16),
                        pltpu.VMEM(wu.shape[1:], BF16), pltpu.VMEM(wd.shape[1:], BF16),
                        pltpu.VMEM((STAGE_SLOTS, STAGE_ROWS, stage_width), F32),
                        pltpu.SemaphoreType.DMA((STAGE_SLOTS,))],
        compiler_params=pltpu.CompilerParams(
            dimension_semantics=("arbitrary",), vmem_limit_bytes=VMEM_LIMIT_BYTES),
        name="post",
    )(o_sb, o_gla, x2d, mod, w_o, ln1_g, ln1_b, wg, wu, cw, cb, wd, ln_g, ln_b)


def kernel(x, c, w_ada, b_ada, w_in, gla_w_gate, gla_b_gate, sb_norm_g, gla_norm_g, w_out,
           ln1_g, ln1_b, w_ff_gate, w_ff_up, conv_w, conv_b, w_down, ln2_g, ln2_b):
    bsz, seq, d = x.shape
    depth = w_ada.shape[0]
    alpha = float((2 * depth) ** 0.25)
    t = bsz * seq

    c_pad = jnp.concatenate([c, jnp.zeros((SUBLANES - bsz, d), c.dtype)], axis=0)
    w_in_t = jnp.swapaxes(w_in, 1, 2)
    xcur = x.reshape(t, d)
    for l in range(depth):
        mod = _ada(c_pad, w_ada[l], b_ada[l][None, :])
        mod = mod[:bsz].reshape(bsz, N_MOD, d)
        mod = jnp.concatenate([mod, jnp.zeros((bsz, SUBLANES - N_MOD, d), F32)], axis=1)

        sb_q, sb_k, sb_v, g_q, g_k, g_v, g_r, g_lr = _inproj(xcur, mod, w_in_t, seq, l)
        as_seq = lambda a: a.reshape(bsz, seq, a.shape[-1])

        o_sb = _sb_attention(as_seq(sb_q), as_seq(sb_k), as_seq(sb_v), sb_norm_g[l][None, :])
        wg_pad = jnp.concatenate(
            [gla_w_gate[l], jnp.zeros((LANES - GLA_GATE_RANK, GLA_QK_WIDTH), F32)], axis=0).astype(BF16)
        o_gla = _gla(as_seq(g_q), as_seq(g_k), as_seq(g_v), as_seq(g_r), as_seq(g_lr),
                     wg_pad, gla_b_gate[l][None, :], gla_norm_g[l][None, :])

        xcur = _post(o_sb.reshape(t, SB_WIDTH), o_gla.reshape(t, GLA_V_WIDTH), xcur, mod,
                     w_out, ln1_g[l][None, :], ln1_b[l][None, :],
                     w_ff_gate, w_ff_up, conv_w[l], conv_b[l][None, :], w_down,
                     ln2_g[l][None, :], ln2_b[l][None, :], seq, alpha, l)
    return xcur.reshape(bsz, seq, d)
```

```python
import functools

import jax
import jax.numpy as jnp
from jax import lax
from jax.experimental import pallas as pl
from jax.experimental.pallas import tpu as pltpu

F32 = jnp.float32
BF16 = jnp.bfloat16

LANES = 128
SUBLANES = 8
VMEM_LIMIT_BYTES = 60 * 1024 * 1024

SB_HEADS = 8
SB_HEAD_DIM = 64
GLA_HEADS = 4
GLA_KEY_DIM = 64
GLA_VAL_DIM = 128
GLA_GATE_RANK = 16
GLA_GATE_TEMP = 16.0
GLA_CHUNK = 64
CONV_WIDTH = 3
N_MOD = 6
LN_EPS = 1e-5
RMS_EPS = 1e-6

SB_WIDTH = SB_HEADS * SB_HEAD_DIM
GLA_QK_WIDTH = GLA_HEADS * GLA_KEY_DIM
GLA_V_WIDTH = GLA_HEADS * GLA_VAL_DIM

SB_EXIT = 88.0
LOG2E = 1.4426950408889634
SB_EXIT_BITS = SB_EXIT * LOG2E
SB_MAX_BITS = 126.0
SB_MASKED_BITS = -1e30
SB_TK = 128
SB_TQ = 8 * SB_TK
SB_WIN_BLOCKS = 3
SB_WIN = SB_WIN_BLOCKS * SB_TK

GLA_STEP = 4096
INPROJ_TILE = 1024
ROW_TILE = 1024
POST_SUB_TILE = 256
FF_TILE = 256
STAGE_ROWS = 128
STAGE_SLOTS = 3
INPROJ_STAGE_SLOTS = 6


def _sb_logits(q, k):
    return jnp.minimum(_dot_nt(q, k), SB_MAX_BITS)


def _softplus2(z):
    return jnp.log2(1.0 + jnp.exp2(z))


def _split_bf16(a):
    hi = a.astype(BF16)
    lo = (a - hi.astype(F32)).astype(BF16)
    return hi, lo


def _dot(a, b):
    return jnp.dot(a, b, preferred_element_type=F32)


def _dot_nt(a, b):
    return lax.dot_general(a, b, (((1,), (1,)), ((), ())), preferred_element_type=F32)


def _dot_tn(a, b):
    return lax.dot_general(a, b, (((0,), (0,)), ((), ())), preferred_element_type=F32)


def _stage_weights_bf16(pairs, stage_ref, sem):
    slots, rows = stage_ref.shape[0], stage_ref.shape[1]
    jobs = [(src, dst, r0) for src, dst in pairs for r0 in range(0, src.shape[0], rows)]

    def window(n):
        return stage_ref.at[n % slots, :, pl.ds(0, jobs[n][0].shape[1])]

    def copy(n):
        src, _, r0 = jobs[n]
        return pltpu.make_async_copy(src.at[pl.ds(r0, rows), :], window(n), sem.at[n % slots])

    for n in range(min(slots - 1, len(jobs))):
        copy(n).start()
    for n, (_, dst, r0) in enumerate(jobs):
        if n + slots - 1 < len(jobs):
            copy(n + slots - 1).start()
        copy(n).wait()
        dst[pl.ds(r0, rows), :] = window(n)[...].astype(BF16)


def _layer_norm(v, g, b):
    mu = jnp.mean(v, axis=-1, keepdims=True)
    vc = v - mu
    var = jnp.mean(vc * vc, axis=-1, keepdims=True)
    return vc * lax.rsqrt(var + LN_EPS) * g + b


def _ada_kernel(c_ref, w_ref, b_ref, o_ref):
    c = c_ref[...]
    ca = c / (1.0 + jnp.exp(-c))
    ch, cl = _split_bf16(ca)
    wh, wl = _split_bf16(w_ref[...])
    o_ref[...] = _dot(ch, wh) + _dot(cl, wh) + _dot(ch, wl) + b_ref[...]


def _ada(c_pad, w, b):
    d, n = w.shape
    tn = 1024
    return pl.pallas_call(
        _ada_kernel,
        out_shape=jax.ShapeDtypeStruct((c_pad.shape[0], n), F32),
        grid=(n // tn,),
        in_specs=[
            pl.BlockSpec((c_pad.shape[0], d), lambda j: (0, 0)),
            pl.BlockSpec((d, tn), lambda j: (0, j)),
            pl.BlockSpec((1, tn), lambda j: (0, j)),
        ],
        out_specs=pl.BlockSpec((c_pad.shape[0], tn), lambda j: (0, j)),
        compiler_params=pltpu.CompilerParams(
            dimension_semantics=("arbitrary",), vmem_limit_bytes=VMEM_LIMIT_BYTES),
        name="ada",
    )(c_pad, w, b)


_MAIN_WIDTH = 3 * SB_WIDTH + 2 * GLA_QK_WIDTH + GLA_V_WIDTH
_PROJ_GROUPS = (
    (0, 0, SB_WIDTH, SB_HEAD_DIM ** -0.5 * LOG2E),
    (0, SB_WIDTH, SB_WIDTH, 1.0),
    (0, 2 * SB_WIDTH, SB_WIDTH, 1.0),
    (0, 3 * SB_WIDTH, GLA_QK_WIDTH, GLA_KEY_DIM ** -0.5),
    (0, 3 * SB_WIDTH + GLA_QK_WIDTH, GLA_QK_WIDTH, 1.0),
    (0, 3 * SB_WIDTH + 2 * GLA_QK_WIDTH, GLA_V_WIDTH, 1.0),
    (1, 0, GLA_V_WIDTH, 1.0),
    (2, 0, LANES, 1.0),
)


def _stage_w_in(wt_hbm, w_main_ref, w_r_ref, w_lr_ref, stage_ref, sem):
    slots, rows, d = stage_ref.shape
    lr0 = _MAIN_WIDTH
    r0 = lr0 + GLA_GATE_RANK
    jobs = [(c, w_main_ref, c, rows) for c in range(0, lr0, rows)]
    jobs += [(lr0, w_lr_ref, 0, GLA_GATE_RANK)]
    jobs += [(r0 + c, w_r_ref, c, rows) for c in range(0, GLA_V_WIDTH, rows)]
    lane = lax.broadcasted_iota(jnp.int32, (d, rows), 1)

    def copy(n):
        return pltpu.make_async_copy(wt_hbm.at[pl.ds(jobs[n][0], rows), :],
                                     stage_ref.at[n % slots], sem.at[n % slots])

    for n in range(min(slots - 1, len(jobs))):
        copy(n).start()
    for n, (_, dst, col, valid) in enumerate(jobs):
        if n + slots - 1 < len(jobs):
            copy(n + slots - 1).start()
        copy(n).wait()
        blk = stage_ref[n % slots].T
        if valid < rows:
            blk = jnp.where(lane < valid, blk, 0.0)
        dst[:, pl.ds(col, rows)] = blk.astype(BF16)


def _inproj_kernel(layer, x_ref, mod_ref, w_hbm, *refs):
    n_out = len(_PROJ_GROUPS)
    out_refs = refs[:n_out]
    w_main_ref, w_r_ref, w_lr_ref, stage_ref, sem = refs[n_out:]

    @pl.when(pl.program_id(0) == 0)
    def _():
        _stage_w_in(w_hbm.at[layer], w_main_ref, w_r_ref, w_lr_ref, stage_ref, sem)

    shift = mod_ref[0:1, :]
    scale = mod_ref[1:2, :]
    h = (x_ref[...] * (1.0 + scale) + shift).astype(BF16)
    w_refs = (w_main_ref, w_r_ref, w_lr_ref)
    for (src, c0, width, mul), o_ref in zip(_PROJ_GROUPS, out_refs):
        acc = _dot(h, w_refs[src][:, c0:c0 + width])
        if mul != 1.0:
            acc = acc * mul
        o_ref[...] = acc.astype(o_ref.dtype)


def _inproj(x2d, mod, w_in_t, seq, layer):
    t, d = x2d.shape
    tm = INPROJ_TILE
    tiles_per_seq = seq // tm
    widths = [g[2] for g in _PROJ_GROUPS]
    return pl.pallas_call(
        functools.partial(_inproj_kernel, layer),
        out_shape=[jax.ShapeDtypeStruct((t, width), BF16) for width in widths],
        grid=(t // tm,),
        in_specs=[
            pl.BlockSpec((tm, d), lambda i: (i, 0)),
            pl.BlockSpec((None, SUBLANES, d), lambda i: (i // tiles_per_seq, 0, 0)),
            pl.BlockSpec(memory_space=pl.ANY),
        ],
        out_specs=[pl.BlockSpec((tm, width), lambda i: (i, 0)) for width in widths],
        scratch_shapes=[pltpu.VMEM((d, _MAIN_WIDTH), BF16), pltpu.VMEM((d, GLA_V_WIDTH), BF16),
                        pltpu.VMEM((d, LANES), BF16),
                        pltpu.VMEM((INPROJ_STAGE_SLOTS, LANES, d), F32),
                        pltpu.SemaphoreType.DMA((INPROJ_STAGE_SLOTS,))],
        compiler_params=pltpu.CompilerParams(
            dimension_semantics=("arbitrary",), vmem_limit_bytes=VMEM_LIMIT_BYTES),
        name="inproj",
    )(x2d, mod, w_in_t)


def _sb_suffix_sums(sp_blocks, umat):
    m = sp_blocks[0].shape[0]
    res = _dot(jnp.concatenate(sp_blocks, axis=0).astype(BF16), umat)
    cs, nearer = [], None
    for c in reversed(range(len(sp_blocks))):
        part = res[c * m:(c + 1) * m]
        loc, tot = part[:, :SB_TK], part[:, SB_TK:]
        cs.append(loc if nearer is None else loc + nearer)
        nearer = tot if nearer is None else nearer + tot
    return cs[::-1], nearer


def _sb_kernel(q_ref, k_ref, v_ref, g_ref, o_ref):
    tk = SB_TK
    pairs = q_ref.shape[1] // LANES
    nrb = SB_TQ // tk
    nb = SB_WIN_BLOCKS
    lane = lax.broadcasted_iota(jnp.int32, (tk, LANES), 1)
    head0 = lane < SB_HEAD_DIM

    uj = lax.broadcasted_iota(jnp.int32, (tk, 2 * tk), 0)
    us = lax.broadcasted_iota(jnp.int32, (tk, 2 * tk), 1)
    umat = jnp.where((uj >= us) | (us >= tk), 1.0, 0.0).astype(BF16)
    row = lax.broadcasted_iota(jnp.int32, (2 * tk, tk), 0) & (tk - 1)
    col = lax.broadcasted_iota(jnp.int32, (2 * tk, tk), 1)
    causal = col < row
    lanes = [slice(p * LANES, (p + 1) * LANES) for p in range(pairs)]

    first_blk = [pl.program_id(1) * nrb + r - (nb - 1) for r in range(nrb)]
    blk_rows = [[pl.ds(pl.multiple_of(jnp.maximum(first_blk[r] + c, 0) * tk, tk), tk)
                 for c in range(nb)] for r in range(nrb)]
    units = [(r, p) for r in range(nrb) for p in range(pairs)]

    qs, zs, sums, ws, accs = {}, {}, {}, {}, {}

    def stage_logits(u):
        r, p = u
        q = q_ref[r * tk:(r + 1) * tk, lanes[p]]
        zq = jnp.zeros_like(q)
        qs[u] = jnp.concatenate([jnp.where(head0, q, zq), jnp.where(head0, zq, q)], axis=0)
        kw = jnp.concatenate([k_ref[rows, lanes[p]] for rows in blk_rows[r]], axis=0)
        z = _sb_logits(qs[u], kw)
        zs[u] = [z[:, c * tk:(c + 1) * tk] for c in range(nb)]
        zs[u][-1] = jnp.where(causal, zs[u][-1], SB_MASKED_BITS)

    def stage_sums(u):
        sums[u] = _sb_suffix_sums([_softplus2(z) for z in zs[u]], umat)

    def stage_weights(u):
        args = [z - cs for z, cs in zip(zs[u], sums[u][0])]
        ws[u] = jnp.exp2(jnp.concatenate(args, axis=1)).astype(BF16)

    def stage_values(u):
        r, p = u
        vw = [v_ref[rows, lanes[p]] for rows in blk_rows[r]]
        vw = [jnp.where(first_blk[r] + c >= 0, vw[c], jnp.zeros_like(vw[c]))
              for c in range(nb - 1)] + vw[-1:]
        accs[u] = _dot(ws[u], jnp.concatenate(vw, axis=0))

    stages = (stage_logits, stage_sums, stage_weights, stage_values)
    for t in range(len(stages) + len(units) - 1):
        for n, u in enumerate(units):
            if 0 <= t - n < len(stages):
                stages[t - n](u)

    def not_done(cs_):
        m = cs_[0]
        for c in cs_[1:]:
            m = jnp.minimum(m, c)
        return (jnp.min(m) < SB_EXIT_BITS).astype(jnp.int32)

    def cond(s):
        return jnp.logical_and(s[0] >= 0, s[1] > 0)

    def write_out(r, accs_):
        outs = []
        for p in range(pairs):
            o = jnp.where(head0, accs_[p][:tk], accs_[p][tk:])
            sq = o * o
            ss0 = jnp.sum(jnp.where(head0, sq, 0.0), axis=-1, keepdims=True)
            ss1 = jnp.sum(jnp.where(head0, 0.0, sq), axis=-1, keepdims=True)
            ms = jnp.where(head0, ss0, ss1) * (1.0 / SB_HEAD_DIM)
            outs.append(o * lax.rsqrt(ms + RMS_EPS))
        o_ref[r * tk:(r + 1) * tk, :] = (jnp.concatenate(outs, axis=1) * g_ref[...]).astype(o_ref.dtype)

    for r in range(nrb):
        write_out(r, [accs[r, p] for p in range(pairs)])
    for r in range(nrb):
        carries = tuple(sums[r, p][1] for p in range(pairs))
        more = not_done(carries)

        def body(s, r=r):
            j, _, cs_, as_ = s
            rows = pl.ds(pl.multiple_of(j * tk, tk), tk)
            zb = [_sb_logits(qs[r, p], k_ref[rows, lanes[p]]) for p in range(pairs)]
            sb = [_sb_suffix_sums([_softplus2(zb[p])], umat) for p in range(pairs)]
            wb = [jnp.exp2(zb[p] - sb[p][0][0] - cs_[p]).astype(BF16) for p in range(pairs)]
            new_a = [as_[p] + _dot(wb[p], v_ref[rows, lanes[p]]) for p in range(pairs)]
            new_c = [cs_[p] + sb[p][1] for p in range(pairs)]
            return j - 1, not_done(new_c), tuple(new_c), tuple(new_a)

        @pl.when(jnp.logical_and(first_blk[r] - 1 >= 0, more > 0))
        def _(r=r, carries=carries, more=more, body=body):
            _, _, _, accs_ = lax.while_loop(
                cond, body,
                (first_blk[r] - 1, more, carries, tuple(accs[r, p] for p in range(pairs))))
            write_out(r, accs_)


def _sb_attention(q, k, v, g):
    b, s, wdt = q.shape
    return pl.pallas_call(
        _sb_kernel,
        out_shape=jax.ShapeDtypeStruct((b, s, wdt), BF16),
        grid=(b, s // SB_TQ),
        in_specs=[
            pl.BlockSpec((None, SB_TQ, wdt), lambda bi, i: (bi, i, 0)),
            pl.BlockSpec((None, s, wdt), lambda bi, i: (bi, 0, 0)),
            pl.BlockSpec((None, s, wdt), lambda bi, i: (bi, 0, 0)),
            pl.BlockSpec((1, wdt), lambda bi, i: (0, 0)),
        ],
        out_specs=pl.BlockSpec((None, SB_TQ, wdt), lambda bi, i: (bi, i, 0)),
        compiler_params=pltpu.CompilerParams(
            dimension_semantics=("arbitrary", "arbitrary"),
            vmem_limit_bytes=VMEM_LIMIT_BYTES),
        name="sb_attention",
    )(q, k, v, g)


def _gla_kernel(q_ref, k_ref, v_ref, r_ref, lr_ref, wg_ref, bg_ref, g_ref, o_ref, st_ref):
    c = GLA_CHUNK

    @pl.when(pl.program_id(2) == 0)
    def _():
        st_ref[...] = jnp.zeros_like(st_ref)

    u = _dot(lr_ref[...], wg_ref[...]) + bg_ref[...]
    log_a = (jnp.minimum(u, 0.0) - jnp.log(1.0 + jnp.exp(-jnp.abs(u)))) * (1.0 / GLA_GATE_TEMP)

    ti = lax.broadcasted_iota(jnp.int32, (c, c), 0)
    tj = lax.broadcasted_iota(jnp.int32, (c, c), 1)
    lower = ti >= tj
    tril = jnp.where(lower, 1.0, 0.0).astype(BF16)
    lane = lax.broadcasted_iota(jnp.int32, (c, LANES), 1)
    head0 = lane < GLA_KEY_DIM

    chunks = [slice(n * c, (n + 1) * c) for n in range(GLA_STEP // c)]
    la_hi, la_lo = _split_bf16(log_a)
    la_parts = jnp.concatenate([la_hi, la_lo], axis=1)
    bcum, dec = [], []
    for rows in chunks:
        res = _dot(tril, la_parts[rows])
        bcum.append(res[:, :LANES] + res[:, LANES:])
        dec.append(jnp.exp(bcum[-1][c - 1:c, :]))
    q2, k_inv, k_end2 = [], [], []
    for n, rows in enumerate(chunks):
        kf = k_ref[rows, :].astype(F32) * jnp.exp(-bcum[n])
        q_dec = (q_ref[rows, :].astype(F32) * jnp.exp(bcum[n])).astype(BF16)
        k_end = (kf * dec[n]).astype(BF16)
        zq = jnp.zeros_like(q_dec)
        q2.append(jnp.concatenate([jnp.where(head0, q_dec, zq), jnp.where(head0, zq, q_dec)], axis=0))
        k_end2.append(jnp.concatenate([jnp.where(head0, k_end, zq), jnp.where(head0, zq, k_end)], axis=0))
        k_inv.append(kf.astype(BF16))
    lower2 = jnp.concatenate([lower, lower], axis=0)
    o_intra = []
    for n, rows in enumerate(chunks):
        a = jnp.where(lower2, _dot_nt(q2[n], k_inv[n]), 0.0).astype(BF16)
        o_intra.append((_dot(a[:c], v_ref[rows, :GLA_VAL_DIM]), _dot(a[c:], v_ref[rows, GLA_VAL_DIM:])))
    kv = [_dot_tn(jnp.concatenate([v_ref[rows, :GLA_VAL_DIM], v_ref[rows, GLA_VAL_DIM:]], axis=0), k_end2[n])
          for n, rows in enumerate(chunks)]
    st = st_ref[...]
    starts = []
    for n in range(len(chunks)):
        starts.append(st.astype(BF16))
        st = st * dec[n] + kv[n]
    st_ref[...] = st
    for n, rows in enumerate(chunks):
        o_inter = _dot_nt(q2[n], starts[n])
        o0 = o_intra[n][0] + o_inter[:c]
        o1 = o_intra[n][1] + o_inter[c:]
        n0 = o0 * lax.rsqrt(jnp.mean(o0 * o0, axis=-1, keepdims=True) + RMS_EPS)
        n1 = o1 * lax.rsqrt(jnp.mean(o1 * o1, axis=-1, keepdims=True) + RMS_EPS)
        rf = r_ref[rows, :].astype(F32)
        gate = rf / (1.0 + jnp.exp(-rf))
        o_ref[rows, :] = (jnp.concatenate([n0, n1], axis=1) * g_ref[...] * gate).astype(o_ref.dtype)


def _gla(q, k, v, r, lr, wg, bg, g):
    b, s, _ = q.shape
    pairs = GLA_HEADS // 2
    step = GLA_STEP
    dv2 = 2 * GLA_VAL_DIM
    seq_map = lambda bi, p, t: (bi, t, p)
    return pl.pallas_call(
        _gla_kernel,
        out_shape=jax.ShapeDtypeStruct((b, s, GLA_V_WIDTH), BF16),
        grid=(b, pairs, s // step),
        in_specs=[
            pl.BlockSpec((None, step, LANES), seq_map),
            pl.BlockSpec((None, step, LANES), seq_map),
            pl.BlockSpec((None, step, dv2), seq_map),
            pl.BlockSpec((None, step, dv2), seq_map),
            pl.BlockSpec((None, step, LANES), lambda bi, p, t: (bi, t, 0)),
            pl.BlockSpec((LANES, LANES), lambda bi, p, t: (0, p)),
            pl.BlockSpec((1, LANES), lambda bi, p, t: (0, p)),
            pl.BlockSpec((1, dv2), lambda bi, p, t: (0, p)),
        ],
        out_specs=pl.BlockSpec((None, step, dv2), seq_map),
        scratch_shapes=[pltpu.VMEM((GLA_VAL_DIM, LANES), F32)],
        compiler_params=pltpu.CompilerParams(
            dimension_semantics=("arbitrary", "arbitrary", "arbitrary"),
            vmem_limit_bytes=VMEM_LIMIT_BYTES),
        name="gla",
    )(q, k, v, r, lr, wg, bg, g)


def _post_kernel(alpha, tiles_per_seq, layer, osb_ref, ogla_ref, x_ref, mod_ref, wo_hbm, g1_ref, b1_ref,
                 wg_hbm, wu_hbm, cw_ref, cb_ref, wd_hbm, g_ref, b_ref,
                 o_ref, halo_ref, act_ref, x1_ref, wo_ref, wg_ref, wu_ref, wd_ref, stage_ref, sem):
    tm = POST_SUB_TILE
    nsub = x_ref.shape[0] // tm
    d_ff = wg_ref.shape[1]
    n_sb = osb_ref.shape[1]
    first = (pl.program_id(0) % tiles_per_seq) == 0

    @pl.when(pl.program_id(0) == 0)
    def _():
        _stage_weights_bf16(((wo_hbm.at[layer], wo_ref), (wg_hbm.at[layer], wg_ref),
                             (wu_hbm.at[layer], wu_ref), (wd_hbm.at[layer], wd_ref)), stage_ref, sem)

    @pl.when(first)
    def _():
        halo_ref[...] = jnp.zeros_like(halo_ref)

    top = lax.broadcasted_iota(jnp.int32, (SUBLANES, FF_TILE), 0)
    hs = []
    for j in range(nsub):
        rows = slice(j * tm, (j + 1) * tm)
        mix = _dot(osb_ref[rows, :], wo_ref[:n_sb, :]) + _dot(ogla_ref[rows, :], wo_ref[n_sb:, :])
        x1_ref[rows, :] = _layer_norm(alpha * x_ref[rows, :] + (1.0 + mod_ref[2:3, :]) * mix,
                                      g1_ref[...], b1_ref[...])
        hs.append((x1_ref[rows, :] * (1.0 + mod_ref[4:5, :]) + mod_ref[3:4, :]).astype(BF16))
    for j in range(nsub):
        rows = slice(j * tm, (j + 1) * tm)
        h = hs[j]
        for f in range(d_ff // FF_TILE):
            cols = slice(f * FF_TILE, (f + 1) * FF_TILE)
            gt = _dot(h, wg_ref[:, cols])
            up = _dot(h, wu_ref[:, cols])
            prev = halo_ref[:, cols]
            halo_ref[:, cols] = gt[tm - SUBLANES:, :]
            r1 = pltpu.roll(gt, 1, 0)
            r2 = pltpu.roll(gt, 2, 0)
            t1 = jnp.where(top < 1, pltpu.roll(prev, 1, 0), r1[:SUBLANES])
            t2 = jnp.where(top < 2, pltpu.roll(prev, 2, 0), r2[:SUBLANES])
            g1 = jnp.concatenate([t1, r1[SUBLANES:]], axis=0)
            g2 = jnp.concatenate([t2, r2[SUBLANES:]], axis=0)
            conv = (g2 * cw_ref[0:1, cols] + g1 * cw_ref[1:2, cols] + gt * cw_ref[2:3, cols]
                    + cb_ref[:, cols])
            act_ref[rows, cols] = (conv / (1.0 + jnp.exp(-conv)) * up).astype(BF16)
        y = _dot(act_ref[rows, :], wd_ref[...])
        o_ref[rows, :] = _layer_norm(alpha * x1_ref[rows, :] + (1.0 + mod_ref[5:6, :]) * y,
                                     g_ref[...], b_ref[...])


def _post(o_sb, o_gla, x2d, mod, w_o, ln1_g, ln1_b, wg, wu, cw, cb, wd, ln_g, ln_b, seq, alpha, layer):
    t, d = x2d.shape
    d_ff = wg.shape[2]
    tm = ROW_TILE
    tiles_per_seq = seq // tm
    row_map = lambda i: (i, 0)
    fixed = lambda i: (0, 0)
    hbm = pl.BlockSpec(memory_space=pl.ANY)
    stage_width = max(w.shape[2] for w in (w_o, wg, wu, wd))
    return pl.pallas_call(
        functools.partial(_post_kernel, alpha, tiles_per_seq, layer),
        out_shape=jax.ShapeDtypeStruct((t, d), F32),
        grid=(t // tm,),
        in_specs=[
            pl.BlockSpec((tm, o_sb.shape[1]), row_map),
            pl.BlockSpec((tm, o_gla.shape[1]), row_map),
            pl.BlockSpec((tm, d), row_map),
            pl.BlockSpec((None, SUBLANES, d), lambda i: (i // tiles_per_seq, 0, 0)),
            hbm,
            pl.BlockSpec((1, d), fixed),
            pl.BlockSpec((1, d), fixed),
            hbm,
            hbm,
            pl.BlockSpec(cw.shape, fixed),
            pl.BlockSpec(cb.shape, fixed),
            hbm,
            pl.BlockSpec((1, d), fixed),
            pl.BlockSpec((1, d), fixed),
        ],
        out_specs=pl.BlockSpec((tm, d), row_map),
        scratch_shapes=[pltpu.VMEM((SUBLANES, d_ff), F32), pltpu.VMEM((tm, d_ff), BF16),
                        pltpu.VMEM((tm, d), F32),
                        pltpu.VMEM(w_o.shape[1:], BF16), pltpu.VMEM(wg.shape[1:], BF16),
                        pltpu.VMEM(wu.shape[1:], BF16), pltpu.VMEM(wd.shape[1:], BF16),
                        pltpu.VMEM((STAGE_SLOTS, STAGE_ROWS, stage_width), F32),
                        pltpu.SemaphoreType.DMA((STAGE_SLOTS,))],
        compiler_params=pltpu.CompilerParams(
            dimension_semantics=("arbitrary",), vmem_limit_bytes=VMEM_LIMIT_BYTES),
        name="post",
    )(o_sb, o_gla, x2d, mod, w_o, ln1_g, ln1_b, wg, wu, cw, cb, wd, ln_g, ln_b)


def kernel(x, c, w_ada, b_ada, w_in, gla_w_gate, gla_b_gate, sb_norm_g, gla_norm_g, w_out,
           ln1_g, ln1_b, w_ff_gate, w_ff_up, conv_w, conv_b, w_down, ln2_g, ln2_b):
    bsz, seq, d = x.shape
    depth = w_ada.shape[0]
    alpha = float((2 * depth) ** 0.25)
    t = bsz * seq

    c_pad = jnp.concatenate([c, jnp.zeros((SUBLANES - bsz, d), c.dtype)], axis=0)
    w_in_t = jnp.swapaxes(w_in, 1, 2)
    xcur = x.reshape(t, d)
    for l in range(depth):
        mod = _ada(c_pad, w_ada[l], b_ada[l][None, :])
        mod = mod[:bsz].reshape(bsz, N_MOD, d)
        mod = jnp.concatenate([mod, jnp.zeros((bsz, SUBLANES - N_MOD, d), F32)], axis=1)

        sb_q, sb_k, sb_v, g_q, g_k, g_v, g_r, g_lr = _inproj(xcur, mod, w_in_t, seq, l)
        as_seq = lambda a: a.reshape(bsz, seq, a.shape[-1])

        o_sb = _sb_attention(as_seq(sb_q), as_seq(sb_k), as_seq(sb_v), sb_norm_g[l][None, :])
        wg_pad = jnp.concatenate(
            [gla_w_gate[l], jnp.zeros((LANES - GLA_GATE_RANK, GLA_QK_WIDTH), F32)], axis=0).astype(BF16)
        o_gla = _gla(as_seq(g_q), as_seq(g_k), as_seq(g_v), as_seq(g_r), as_seq(g_lr),
                     wg_pad, gla_b_gate[l][None, :], gla_norm_g[l][None, :])

        xcur = _post(o_sb.reshape(t, SB_WIDTH), o_gla.reshape(t, GLA_V_WIDTH), xcur, mod,
                     w_out, ln1_g[l][None, :], ln1_b[l][None, :],
                     w_ff_gate, w_ff_up, conv_w[l], conv_b[l][None, :], w_down,
                     ln2_g[l][None, :], ln2_b[l][None, :], seq, alpha, l)
    return xcur.reshape(bsz, seq, d)
```

```python
import functools

import jax
import jax.numpy as jnp
from jax import lax
from jax.experimental import pallas as pl
from jax.experimental.pallas import tpu as pltpu

F32 = jnp.float32
BF16 = jnp.bfloat16

LANES = 128
SUBLANES = 8
VMEM_LIMIT_BYTES = 60 * 1024 * 1024

SB_HEADS = 8
SB_HEAD_DIM = 64
GLA_HEADS = 4
GLA_KEY_DIM = 64
GLA_VAL_DIM = 128
GLA_GATE_RANK = 16
GLA_GATE_TEMP = 16.0
GLA_CHUNK = 64
CONV_WIDTH = 3
N_MOD = 6
LN_EPS = 1e-5
RMS_EPS = 1e-6

SB_WIDTH = SB_HEADS * SB_HEAD_DIM
GLA_QK_WIDTH = GLA_HEADS * GLA_KEY_DIM
GLA_V_WIDTH = GLA_HEADS * GLA_VAL_DIM

SB_EXIT = 88.0
LOG2E = 1.4426950408889634
SB_EXIT_BITS = SB_EXIT * LOG2E
SB_MAX_BITS = 126.0
SB_MASKED_BITS = -1e30
SB_TK = 128
SB_TQ = 16 * SB_TK
SB_WIN_BLOCKS = 3
SB_WIN = SB_WIN_BLOCKS * SB_TK

GLA_STEP = 4096
INPROJ_TILE = 1024
ROW_TILE = 1024
POST_SUB_TILE = 256
FF_TILE = 256
STAGE_ROWS = 128
STAGE_SLOTS = 3
INPROJ_STAGE_SLOTS = 6


def _sb_logits(q, k):
    return jnp.minimum(_dot_nt(q, k), SB_MAX_BITS)


def _softplus2(z):
    return jnp.log2(1.0 + jnp.exp2(z))


def _split_bf16(a):
    hi = a.astype(BF16)
    lo = (a - hi.astype(F32)).astype(BF16)
    return hi, lo


def _dot(a, b):
    return jnp.dot(a, b, preferred_element_type=F32)


def _dot_nt(a, b):
    return lax.dot_general(a, b, (((1,), (1,)), ((), ())), preferred_element_type=F32)


def _dot_tn(a, b):
    return lax.dot_general(a, b, (((0,), (0,)), ((), ())), preferred_element_type=F32)


def _stage_weights_bf16(pairs, stage_ref, sem):
    slots, rows = stage_ref.shape[0], stage_ref.shape[1]
    jobs = [(src, dst, r0) for src, dst in pairs for r0 in range(0, src.shape[0], rows)]

    def window(n):
        return stage_ref.at[n % slots, :, pl.ds(0, jobs[n][0].shape[1])]

    def copy(n):
        src, _, r0 = jobs[n]
        return pltpu.make_async_copy(src.at[pl.ds(r0, rows), :], window(n), sem.at[n % slots])

    for n in range(min(slots - 1, len(jobs))):
        copy(n).start()
    for n, (_, dst, r0) in enumerate(jobs):
        if n + slots - 1 < len(jobs):
            copy(n + slots - 1).start()
        copy(n).wait()
        dst[pl.ds(r0, rows), :] = window(n)[...].astype(BF16)


def _layer_norm(v, g, b):
    mu = jnp.mean(v, axis=-1, keepdims=True)
    vc = v - mu
    var = jnp.mean(vc * vc, axis=-1, keepdims=True)
    return vc * lax.rsqrt(var + LN_EPS) * g + b


def _ada_kernel(c_ref, w_ref, b_ref, o_ref):
    c = c_ref[...]
    ca = c / (1.0 + jnp.exp(-c))
    ch, cl = _split_bf16(ca)
    wh, wl = _split_bf16(w_ref[...])
    o_ref[...] = _dot(ch, wh) + _dot(cl, wh) + _dot(ch, wl) + b_ref[...]


def _ada(c_pad, w, b):
    d, n = w.shape
    tn = 1024
    return pl.pallas_call(
        _ada_kernel,
        out_shape=jax.ShapeDtypeStruct((c_pad.shape[0], n), F32),
        grid=(n // tn,),
        in_specs=[
            pl.BlockSpec((c_pad.shape[0], d), lambda j: (0, 0)),
            pl.BlockSpec((d, tn), lambda j: (0, j)),
            pl.BlockSpec((1, tn), lambda j: (0, j)),
        ],
        out_specs=pl.BlockSpec((c_pad.shape[0], tn), lambda j: (0, j)),
        compiler_params=pltpu.CompilerParams(
            dimension_semantics=("arbitrary",), vmem_limit_bytes=VMEM_LIMIT_BYTES),
        name="ada",
    )(c_pad, w, b)


_MAIN_WIDTH = 3 * SB_WIDTH + 2 * GLA_QK_WIDTH + GLA_V_WIDTH
_PROJ_GROUPS = (
    (0, 0, SB_WIDTH, SB_HEAD_DIM ** -0.5 * LOG2E),
    (0, SB_WIDTH, SB_WIDTH, 1.0),
    (0, 2 * SB_WIDTH, SB_WIDTH, 1.0),
    (0, 3 * SB_WIDTH, GLA_QK_WIDTH, GLA_KEY_DIM ** -0.5),
    (0, 3 * SB_WIDTH + GLA_QK_WIDTH, GLA_QK_WIDTH, 1.0),
    (0, 3 * SB_WIDTH + 2 * GLA_QK_WIDTH, GLA_V_WIDTH, 1.0),
    (1, 0, GLA_V_WIDTH, 1.0),
    (2, 0, LANES, 1.0),
)


def _stage_w_in(wt_hbm, w_main_ref, w_r_ref, w_lr_ref, stage_ref, sem):
    slots, rows, d = stage_ref.shape
    lr0 = _MAIN_WIDTH
    r0 = lr0 + GLA_GATE_RANK
    jobs = [(c, w_main_ref, c, rows) for c in range(0, lr0, rows)]
    jobs += [(lr0, w_lr_ref, 0, GLA_GATE_RANK)]
    jobs += [(r0 + c, w_r_ref, c, rows) for c in range(0, GLA_V_WIDTH, rows)]
    lane = lax.broadcasted_iota(jnp.int32, (d, rows), 1)

    def copy(n):
        return pltpu.make_async_copy(wt_hbm.at[pl.ds(jobs[n][0], rows), :],
                                     stage_ref.at[n % slots], sem.at[n % slots])

    for n in range(min(slots - 1, len(jobs))):
        copy(n).start()
    for n, (_, dst, col, valid) in enumerate(jobs):
        if n + slots - 1 < len(jobs):
            copy(n + slots - 1).start()
        copy(n).wait()
        blk = stage_ref[n % slots].T
        if valid < rows:
            blk = jnp.where(lane < valid, blk, 0.0)
        dst[:, pl.ds(col, rows)] = blk.astype(BF16)


def _inproj_kernel(layer, x_ref, mod_ref, w_hbm, *refs):
    n_out = len(_PROJ_GROUPS)
    out_refs = refs[:n_out]
    w_main_ref, w_r_ref, w_lr_ref, stage_ref, sem = refs[n_out:]

    @pl.when(pl.program_id(0) == 0)
    def _():
        _stage_w_in(w_hbm.at[layer], w_main_ref, w_r_ref, w_lr_ref, stage_ref, sem)

    shift = mod_ref[0:1, :]
    scale = mod_ref[1:2, :]
    h = (x_ref[...] * (1.0 + scale) + shift).astype(BF16)
    w_refs = (w_main_ref, w_r_ref, w_lr_ref)
    for (src, c0, width, mul), o_ref in zip(_PROJ_GROUPS, out_refs):
        acc = _dot(h, w_refs[src][:, c0:c0 + width])
        if mul != 1.0:
            acc = acc * mul
        o_ref[...] = acc.astype(o_ref.dtype)


def _inproj(x2d, mod, w_in_t, seq, layer):
    t, d = x2d.shape
    tm = INPROJ_TILE
    tiles_per_seq = seq // tm
    widths = [g[2] for g in _PROJ_GROUPS]
    return pl.pallas_call(
        functools.partial(_inproj_kernel, layer),
        out_shape=[jax.ShapeDtypeStruct((t, width), BF16) for width in widths],
        grid=(t // tm,),
        in_specs=[
            pl.BlockSpec((tm, d), lambda i: (i, 0)),
            pl.BlockSpec((None, SUBLANES, d), lambda i: (i // tiles_per_seq, 0, 0)),
            pl.BlockSpec(memory_space=pl.ANY),
        ],
        out_specs=[pl.BlockSpec((tm, width), lambda i: (i, 0)) for width in widths],
        scratch_shapes=[pltpu.VMEM((d, _MAIN_WIDTH), BF16), pltpu.VMEM((d, GLA_V_WIDTH), BF16),
                        pltpu.VMEM((d, LANES), BF16),
                        pltpu.VMEM((INPROJ_STAGE_SLOTS, LANES, d), F32),
                        pltpu.SemaphoreType.DMA((INPROJ_STAGE_SLOTS,))],
        compiler_params=pltpu.CompilerParams(
            dimension_semantics=("arbitrary",), vmem_limit_bytes=VMEM_LIMIT_BYTES),
        name="inproj",
    )(x2d, mod, w_in_t)


def _sb_suffix_sums(sp_blocks, umat):
    m = sp_blocks[0].shape[0]
    res = _dot(jnp.concatenate(sp_blocks, axis=0).astype(BF16), umat)
    cs, nearer = [], None
    for c in reversed(range(len(sp_blocks))):
        part = res[c * m:(c + 1) * m]
        loc, tot = part[:, :SB_TK], part[:, SB_TK:]
        cs.append(loc if nearer is None else loc + nearer)
        nearer = tot if nearer is None else nearer + tot
    return cs[::-1], nearer


def _sb_kernel(q_ref, k_ref, v_ref, g_ref, o_ref):
    tk = SB_TK
    pairs = q_ref.shape[1] // LANES
    nrb = SB_TQ // tk
    nb = SB_WIN_BLOCKS
    lane = lax.broadcasted_iota(jnp.int32, (tk, LANES), 1)
    head0 = lane < SB_HEAD_DIM

    uj = lax.broadcasted_iota(jnp.int32, (tk, 2 * tk), 0)
    us = lax.broadcasted_iota(jnp.int32, (tk, 2 * tk), 1)
    umat = jnp.where((uj >= us) | (us >= tk), 1.0, 0.0).astype(BF16)
    row = lax.broadcasted_iota(jnp.int32, (2 * tk, tk), 0) & (tk - 1)
    col = lax.broadcasted_iota(jnp.int32, (2 * tk, tk), 1)
    causal = col < row
    lanes = [slice(p * LANES, (p + 1) * LANES) for p in range(pairs)]

    first_blk = [pl.program_id(1) * nrb + r - (nb - 1) for r in range(nrb)]
    blk_rows = [[pl.ds(pl.multiple_of(jnp.maximum(first_blk[r] + c, 0) * tk, tk), tk)
                 for c in range(nb)] for r in range(nrb)]
    units = [(r, p) for r in range(nrb) for p in range(pairs)]

    qs, zs, sums, ws, accs = {}, {}, {}, {}, {}

    def stage_logits(u):
        r, p = u
        q = q_ref[r * tk:(r + 1) * tk, lanes[p]]
        zq = jnp.zeros_like(q)
        qs[u] = jnp.concatenate([jnp.where(head0, q, zq), jnp.where(head0, zq, q)], axis=0)
        kw = jnp.concatenate([k_ref[rows, lanes[p]] for rows in blk_rows[r]], axis=0)
        z = _sb_logits(qs[u], kw)
        zs[u] = [z[:, c * tk:(c + 1) * tk] for c in range(nb)]
        zs[u][-1] = jnp.where(causal, zs[u][-1], SB_MASKED_BITS)

    def stage_sums(u):
        sums[u] = _sb_suffix_sums([_softplus2(z) for z in zs[u]], umat)

    def stage_weights(u):
        args = [z - cs for z, cs in zip(zs[u], sums[u][0])]
        ws[u] = jnp.exp2(jnp.concatenate(args, axis=1)).astype(BF16)

    def stage_values(u):
        r, p = u
        vw = [v_ref[rows, lanes[p]] for rows in blk_rows[r]]
        vw = [jnp.where(first_blk[r] + c >= 0, vw[c], jnp.zeros_like(vw[c]))
              for c in range(nb - 1)] + vw[-1:]
        accs[u] = _dot(ws[u], jnp.concatenate(vw, axis=0))

    stages = (stage_logits, stage_sums, stage_weights, stage_values)
    for t in range(len(stages) + len(units) - 1):
        for n, u in enumerate(units):
            if 0 <= t - n < len(stages):
                stages[t - n](u)

    def not_done(cs_):
        m = cs_[0]
        for c in cs_[1:]:
            m = jnp.minimum(m, c)
        return (jnp.min(m) < SB_EXIT_BITS).astype(jnp.int32)

    def cond(s):
        return jnp.logical_and(s[0] >= 0, s[1] > 0)

    def write_out(r, accs_):
        outs = []
        for p in range(pairs):
            o = jnp.where(head0, accs_[p][:tk], accs_[p][tk:])
            sq = o * o
            ss0 = jnp.sum(jnp.where(head0, sq, 0.0), axis=-1, keepdims=True)
            ss1 = jnp.sum(jnp.where(head0, 0.0, sq), axis=-1, keepdims=True)
            ms = jnp.where(head0, ss0, ss1) * (1.0 / SB_HEAD_DIM)
            outs.append(o * lax.rsqrt(ms + RMS_EPS))
        o_ref[r * tk:(r + 1) * tk, :] = (jnp.concatenate(outs, axis=1) * g_ref[...]).astype(o_ref.dtype)

    for r in range(nrb):
        write_out(r, [accs[r, p] for p in range(pairs)])
    for r in range(nrb):
        carries = tuple(sums[r, p][1] for p in range(pairs))
        more = not_done(carries)

        def body(s, r=r):
            j, _, cs_, as_ = s
            rows = pl.ds(pl.multiple_of(j * tk, tk), tk)
            zb = [_sb_logits(qs[r, p], k_ref[rows, lanes[p]]) for p in range(pairs)]
            sb = [_sb_suffix_sums([_softplus2(zb[p])], umat) for p in range(pairs)]
            wb = [jnp.exp2(zb[p] - sb[p][0][0] - cs_[p]).astype(BF16) for p in range(pairs)]
            new_a = [as_[p] + _dot(wb[p], v_ref[rows, lanes[p]]) for p in range(pairs)]
            new_c = [cs_[p] + sb[p][1] for p in range(pairs)]
            return j - 1, not_done(new_c), tuple(new_c), tuple(new_a)

        @pl.when(jnp.logical_and(first_blk[r] - 1 >= 0, more > 0))
        def _(r=r, carries=carries, more=more, body=body):
            _, _, _, accs_ = lax.while_loop(
                cond, body,
                (first_blk[r] - 1, more, carries, tuple(accs[r, p] for p in range(pairs))))
            write_out(r, accs_)


def _sb_attention(q, k, v, g):
    b, s, wdt = q.shape
    return pl.pallas_call(
        _sb_kernel,
        out_shape=jax.ShapeDtypeStruct((b, s, wdt), BF16),
        grid=(b, s // SB_TQ),
        in_specs=[
            pl.BlockSpec((None, SB_TQ, wdt), lambda bi, i: (bi, i, 0)),
            pl.BlockSpec((None, s, wdt), lambda bi, i: (bi, 0, 0)),
            pl.BlockSpec((None, s, wdt), lambda bi, i: (bi, 0, 0)),
            pl.BlockSpec((1, wdt), lambda bi, i: (0, 0)),
        ],
        out_specs=pl.BlockSpec((None, SB_TQ, wdt), lambda bi, i: (bi, i, 0)),
        compiler_params=pltpu.CompilerParams(
            dimension_semantics=("arbitrary", "arbitrary"),
            vmem_limit_bytes=VMEM_LIMIT_BYTES),
        name="sb_attention",
    )(q, k, v, g)


def _gla_kernel(q_ref, k_ref, v_ref, r_ref, lr_ref, wg_ref, bg_ref, g_ref, o_ref, st_ref):
    c = GLA_CHUNK

    @pl.when(pl.program_id(2) == 0)
    def _():
        st_ref[...] = jnp.zeros_like(st_ref)

    u = _dot(lr_ref[...], wg_ref[...]) + bg_ref[...]
    log_a = (jnp.minimum(u, 0.0) - jnp.log(1.0 + jnp.exp(-jnp.abs(u)))) * (1.0 / GLA_GATE_TEMP)

    ti = lax.broadcasted_iota(jnp.int32, (c, c), 0)
    tj = lax.broadcasted_iota(jnp.int32, (c, c), 1)
    lower = ti >= tj
    tril = jnp.where(lower, 1.0, 0.0).astype(BF16)
    lane = lax.broadcasted_iota(jnp.int32, (c, LANES), 1)
    head0 = lane < GLA_KEY_DIM

    chunks = [slice(n * c, (n + 1) * c) for n in range(GLA_STEP // c)]
    la_hi, la_lo = _split_bf16(log_a)
    la_parts = jnp.concatenate([la_hi, la_lo], axis=1)
    bcum, dec = [], []
    for rows in chunks:
        res = _dot(tril, la_parts[rows])
        bcum.append(res[:, :LANES] + res[:, LANES:])
        dec.append(jnp.exp(bcum[-1][c - 1:c, :]))
    q2, k_inv, k_end2 = [], [], []
    for n, rows in enumerate(chunks):
        kf = k_ref[rows, :].astype(F32) * jnp.exp(-bcum[n])
        q_dec = (q_ref[rows, :].astype(F32) * jnp.exp(bcum[n])).astype(BF16)
        k_end = (kf * dec[n]).astype(BF16)
        zq = jnp.zeros_like(q_dec)
        q2.append(jnp.concatenate([jnp.where(head0, q_dec, zq), jnp.where(head0, zq, q_dec)], axis=0))
        k_end2.append(jnp.concatenate([jnp.where(head0, k_end, zq), jnp.where(head0, zq, k_end)], axis=0))
        k_inv.append(kf.astype(BF16))
    lower2 = jnp.concatenate([lower, lower], axis=0)
    o_intra = []
    for n, rows in enumerate(chunks):
        a = jnp.where(lower2, _dot_nt(q2[n], k_inv[n]), 0.0).astype(BF16)
        o_intra.append((_dot(a[:c], v_ref[rows, :GLA_VAL_DIM]), _dot(a[c:], v_ref[rows, GLA_VAL_DIM:])))
    kv = [_dot_tn(jnp.concatenate([v_ref[rows, :GLA_VAL_DIM], v_ref[rows, GLA_VAL_DIM:]], axis=0), k_end2[n])
          for n, rows in enumerate(chunks)]
    st = st_ref[...]
    starts = []
    for n in range(len(chunks)):
        starts.append(st.astype(BF16))
        st = st * dec[n] + kv[n]
    st_ref[...] = st
    for n, rows in enumerate(chunks):
        o_inter = _dot_nt(q2[n], starts[n])
        o0 = o_intra[n][0] + o_inter[:c]
        o1 = o_intra[n][1] + o_inter[c:]
        n0 = o0 * lax.rsqrt(jnp.mean(o0 * o0, axis=-1, keepdims=True) + RMS_EPS)
        n1 = o1 * lax.rsqrt(jnp.mean(o1 * o1, axis=-1, keepdims=True) + RMS_EPS)
        rf = r_ref[rows, :].astype(F32)
        gate = rf / (1.0 + jnp.exp(-rf))
        o_ref[rows, :] = (jnp.concatenate([n0, n1], axis=1) * g_ref[...] * gate).astype(o_ref.dtype)


def _gla(q, k, v, r, lr, wg, bg, g):
    b, s, _ = q.shape
    pairs = GLA_HEADS // 2
    step = GLA_STEP
    dv2 = 2 * GLA_VAL_DIM
    seq_map = lambda bi, p, t: (bi, t, p)
    return pl.pallas_call(
        _gla_kernel,
        out_shape=jax.ShapeDtypeStruct((b, s, GLA_V_WIDTH), BF16),
        grid=(b, pairs, s // step),
        in_specs=[
            pl.BlockSpec((None, step, LANES), seq_map),
            pl.BlockSpec((None, step, LANES), seq_map),
            pl.BlockSpec((None, step, dv2), seq_map),
            pl.BlockSpec((None, step, dv2), seq_map),
            pl.BlockSpec((None, step, LANES), lambda bi, p, t: (bi, t, 0)),
            pl.BlockSpec((LANES, LANES), lambda bi, p, t: (0, p)),
            pl.BlockSpec((1, LANES), lambda bi, p, t: (0, p)),
            pl.BlockSpec((1, dv2), lambda bi, p, t: (0, p)),
        ],
        out_specs=pl.BlockSpec((None, step, dv2), seq_map),
        scratch_shapes=[pltpu.VMEM((GLA_VAL_DIM, LANES), F32)],
        compiler_params=pltpu.CompilerParams(
            dimension_semantics=("arbitrary", "arbitrary", "arbitrary"),
            vmem_limit_bytes=VMEM_LIMIT_BYTES),
        name="gla",
    )(q, k, v, r, lr, wg, bg, g)


def _post_kernel(alpha, tiles_per_seq, layer, osb_ref, ogla_ref, x_ref, mod_ref, wo_hbm, g1_ref, b1_ref,
                 wg_hbm, wu_hbm, cw_ref, cb_ref, wd_hbm, g_ref, b_ref,
                 o_ref, halo_ref, act_ref, x1_ref, wo_ref, wg_ref, wu_ref, wd_ref, stage_ref, sem):
    tm = POST_SUB_TILE
    nsub = x_ref.shape[0] // tm
    d_ff = wg_ref.shape[1]
    n_sb = osb_ref.shape[1]
    first = (pl.program_id(0) % tiles_per_seq) == 0

    @pl.when(pl.program_id(0) == 0)
    def _():
        _stage_weights_bf16(((wo_hbm.at[layer], wo_ref), (wg_hbm.at[layer], wg_ref),
                             (wu_hbm.at[layer], wu_ref), (wd_hbm.at[layer], wd_ref)), stage_ref, sem)

    @pl.when(first)
    def _():
        halo_ref[...] = jnp.zeros_like(halo_ref)

    top = lax.broadcasted_iota(jnp.int32, (SUBLANES, FF_TILE), 0)
    hs = []
    for j in range(nsub):
        rows = slice(j * tm, (j + 1) * tm)
        mix = _dot(osb_ref[rows, :], wo_ref[:n_sb, :]) + _dot(ogla_ref[rows, :], wo_ref[n_sb:, :])
        x1_ref[rows, :] = _layer_norm(alpha * x_ref[rows, :] + (1.0 + mod_ref[2:3, :]) * mix,
                                      g1_ref[...], b1_ref[...])
        hs.append((x1_ref[rows, :] * (1.0 + mod_ref[4:5, :]) + mod_ref[3:4, :]).astype(BF16))
    for j in range(nsub):
        rows = slice(j * tm, (j + 1) * tm)
        h = hs[j]
        for f in range(d_ff // FF_TILE):
            cols = slice(f * FF_TILE, (f + 1) * FF_TILE)
            gt = _dot(h, wg_ref[:, cols])
            up = _dot(h, wu_ref[:, cols])
            prev = halo_ref[:, cols]
            halo_ref[:, cols] = gt[tm - SUBLANES:, :]
            r1 = pltpu.roll(gt, 1, 0)
            r2 = pltpu.roll(gt, 2, 0)
            t1 = jnp.where(top < 1, pltpu.roll(prev, 1, 0), r1[:SUBLANES])
            t2 = jnp.where(top < 2, pltpu.roll(prev, 2, 0), r2[:SUBLANES])
            g1 = jnp.concatenate([t1, r1[SUBLANES:]], axis=0)
            g2 = jnp.concatenate([t2, r2[SUBLANES:]], axis=0)
            conv = (g2 * cw_ref[0:1, cols] + g1 * cw_ref[1:2, cols] + gt * cw_ref[2:3, cols]
                    + cb_ref[:, cols])
            act_ref[rows, cols] = (conv / (1.0 + jnp.exp(-conv)) * up).astype(BF16)
        y = _dot(act_ref[rows, :], wd_ref[...])
        o_ref[rows, :] = _layer_norm(alpha * x1_ref[rows, :] + (1.0 + mod_ref[5:6, :]) * y,
                                     g_ref[...], b_ref[...])


def _post(o_sb, o_gla, x2d, mod, w_o, ln1_g, ln1_b, wg, wu, cw, cb, wd, ln_g, ln_b, seq, alpha, layer):
    t, d = x2d.shape
    d_ff = wg.shape[2]
    tm = ROW_TILE
    tiles_per_seq = seq // tm
    row_map = lambda i: (i, 0)
    fixed = lambda i: (0, 0)
    hbm = pl.BlockSpec(memory_space=pl.ANY)
    stage_width = max(w.shape[2] for w in (w_o, wg, wu, wd))
    return pl.pallas_call(
        functools.partial(_post_kernel, alpha, tiles_per_seq, layer),
        out_shape=jax.ShapeDtypeStruct((t, d), F32),
        grid=(t // tm,),
        in_specs=[
            pl.BlockSpec((tm, o_sb.shape[1]), row_map),
            pl.BlockSpec((tm, o_gla.shape[1]), row_map),
            pl.BlockSpec((tm, d), row_map),
            pl.BlockSpec((None, SUBLANES, d), lambda i: (i // tiles_per_seq, 0, 0)),
            hbm,
            pl.BlockSpec((1, d), fixed),
            pl.BlockSpec((1, d), fixed),
            hbm,
            hbm,
            pl.BlockSpec(cw.shape, fixed),
            pl.BlockSpec(cb.shape, fixed),
            hbm,
            pl.BlockSpec((1, d), fixed),
            pl.BlockSpec((1, d), fixed),
        ],
        out_specs=pl.BlockSpec((tm, d), row_map),
        scratch_shapes=[pltpu.VMEM((SUBLANES, d_ff), F32), pltpu.VMEM((tm, d_ff), BF16),
                        pltpu.VMEM((tm, d), F32),
                        pltpu.VMEM(w_o.shape[1:], BF16), pltpu.VMEM(wg.shape[1:], BF16),
                        pltpu.VMEM(wu.shape[1:], BF16), pltpu.VMEM(wd.shape[1:], BF16),
                        pltpu.VMEM((STAGE_SLOTS, STAGE_ROWS, stage_width), F32),
                        pltpu.SemaphoreType.DMA((STAGE_SLOTS,))],
        compiler_params=pltpu.CompilerParams(
            dimension_semantics=("arbitrary",), vmem_limit_bytes=VMEM_LIMIT_BYTES),
        name="post",
    )(o_sb, o_gla, x2d, mod, w_o, ln1_g, ln1_b, wg, wu, cw, cb, wd, ln_g, ln_b)


def kernel(x, c, w_ada, b_ada, w_in, gla_w_gate, gla_b_gate, sb_norm_g, gla_norm_g, w_out,
           ln1_g, ln1_b, w_ff_gate, w_ff_up, conv_w, conv_b, w_down, ln2_g, ln2_b):
    bsz, seq, d = x.shape
    depth = w_ada.shape[0]
    alpha = float((2 * depth) ** 0.25)
    t = bsz * seq

    c_pad = jnp.concatenate([c, jnp.zeros((SUBLANES - bsz, d), c.dtype)], axis=0)
    w_in_t = jnp.swapaxes(w_in, 1, 2)
    xcur = x.reshape(t, d)
    for l in range(depth):
        mod = _ada(c_pad, w_ada[l], b_ada[l][None, :])
        mod = mod[:bsz].reshape(bsz, N_MOD, d)
        mod = jnp.concatenate([mod, jnp.zeros((bsz, SUBLANES - N_MOD, d), F32)], axis=1)

        sb_q, sb_k, sb_v, g_q, g_k, g_v, g_r, g_lr = _inproj(xcur, mod, w_in_t, seq, l)
        as_seq = lambda a: a.reshape(bsz, seq, a.shape[-1])

        o_sb = _sb_attention(as_seq(sb_q), as_seq(sb_k), as_seq(sb_v), sb_norm_g[l][None, :])
        wg_pad = jnp.concatenate(
            [gla_w_gate[l], jnp.zeros((LANES - GLA_GATE_RANK, GLA_QK_WIDTH), F32)], axis=0).astype(BF16)
        o_gla = _gla(as_seq(g_q), as_seq(g_k), as_seq(g_v), as_seq(g_r), as_seq(g_lr),
                     wg_pad, gla_b_gate[l][None, :], gla_norm_g[l][None, :])

        xcur = _post(o_sb.reshape(t, SB_WIDTH), o_gla.reshape(t, GLA_V_WIDTH), xcur, mod,
                     w_out, ln1_g[l][None, :], ln1_b[l][None, :],
                     w_ff_gate, w_ff_up, conv_w[l], conv_b[l][None, :], w_down,
                     ln2_g[l][None, :], ln2_b[l][None, :], seq, alpha, l)
    return xcur.reshape(bsz, seq, d)
```

```python
import functools

import jax
import jax.numpy as jnp
from jax import lax
from jax.experimental import pallas as pl
from jax.experimental.pallas import tpu as pltpu

F32 = jnp.float32
BF16 = jnp.bfloat16

LANES = 128
SUBLANES = 8
VMEM_LIMIT_BYTES = 60 * 1024 * 1024

SB_HEADS = 8
SB_HEAD_DIM = 64
GLA_HEADS = 4
GLA_KEY_DIM = 64
GLA_VAL_DIM = 128
GLA_GATE_RANK = 16
GLA_GATE_TEMP = 16.0
GLA_CHUNK = 64
CONV_WIDTH = 3
N_MOD = 6
LN_EPS = 1e-5
RMS_EPS = 1e-6

SB_WIDTH = SB_HEADS * SB_HEAD_DIM
GLA_QK_WIDTH = GLA_HEADS * GLA_KEY_DIM
GLA_V_WIDTH = GLA_HEADS * GLA_VAL_DIM

SB_EXIT = 88.0
LOG2E = 1.4426950408889634
SB_EXIT_BITS = SB_EXIT * LOG2E
SB_MAX_BITS = 126.0
SB_MASKED_BITS = -1e30
SB_TK = 128
SB_TQ = 8 * SB_TK
SB_WIN_BLOCKS = 3
SB_WIN = SB_WIN_BLOCKS * SB_TK

GLA_STEP = 4096
INPROJ_TILE = 1024
ROW_TILE = 1024
POST_SUB_TILE = 256
FF_TILE = 256
STAGE_ROWS = 128
STAGE_SLOTS = 3
INPROJ_STAGE_SLOTS = 6


def _sb_logits(q, k):
    return jnp.minimum(_dot_nt(q, k), SB_MAX_BITS)


def _softplus2(z):
    return jnp.log2(1.0 + jnp.exp2(z))


def _split_bf16(a):
    hi = a.astype(BF16)
    lo = (a - hi.astype(F32)).astype(BF16)
    return hi, lo


def _dot(a, b):
    return jnp.dot(a, b, preferred_element_type=F32)


def _dot_nt(a, b):
    return lax.dot_general(a, b, (((1,), (1,)), ((), ())), preferred_element_type=F32)


def _dot_tn(a, b):
    return lax.dot_general(a, b, (((0,), (0,)), ((), ())), preferred_element_type=F32)


def _stage_weights_bf16(pairs, stage_ref, sem):
    slots, rows = stage_ref.shape[0], stage_ref.shape[1]
    jobs = [(src, dst, r0) for src, dst in pairs for r0 in range(0, src.shape[0], rows)]

    def window(n):
        return stage_ref.at[n % slots, :, pl.ds(0, jobs[n][0].shape[1])]

    def copy(n):
        src, _, r0 = jobs[n]
        return pltpu.make_async_copy(src.at[pl.ds(r0, rows), :], window(n), sem.at[n % slots])

    for n in range(min(slots - 1, len(jobs))):
        copy(n).start()
    for n, (_, dst, r0) in enumerate(jobs):
        if n + slots - 1 < len(jobs):
            copy(n + slots - 1).start()
        copy(n).wait()
        dst[pl.ds(r0, rows), :] = window(n)[...].astype(BF16)


def _layer_norm(v, g, b):
    mu = jnp.mean(v, axis=-1, keepdims=True)
    vc = v - mu
    var = jnp.mean(vc * vc, axis=-1, keepdims=True)
    return vc * lax.rsqrt(var + LN_EPS) * g + b


def _ada_kernel(c_ref, w_ref, b_ref, o_ref):
    c = c_ref[...]
    ca = c / (1.0 + jnp.exp(-c))
    o_ref[...] = _dot(ca.astype(BF16), w_ref[...].astype(BF16)) + b_ref[...]


def _ada(c_pad, w, b):
    d, n = w.shape
    tn = 1024
    return pl.pallas_call(
        _ada_kernel,
        out_shape=jax.ShapeDtypeStruct((c_pad.shape[0], n), F32),
        grid=(n // tn,),
        in_specs=[
            pl.BlockSpec((c_pad.shape[0], d), lambda j: (0, 0)),
            pl.BlockSpec((d, tn), lambda j: (0, j)),
            pl.BlockSpec((1, tn), lambda j: (0, j)),
        ],
        out_specs=pl.BlockSpec((c_pad.shape[0], tn), lambda j: (0, j)),
        compiler_params=pltpu.CompilerParams(
            dimension_semantics=("arbitrary",), vmem_limit_bytes=VMEM_LIMIT_BYTES),
        name="ada",
    )(c_pad, w, b)


_MAIN_WIDTH = 3 * SB_WIDTH + 2 * GLA_QK_WIDTH + GLA_V_WIDTH
_PROJ_GROUPS = (
    (0, 0, SB_WIDTH, SB_HEAD_DIM ** -0.5 * LOG2E),
    (0, SB_WIDTH, SB_WIDTH, 1.0),
    (0, 2 * SB_WIDTH, SB_WIDTH, 1.0),
    (0, 3 * SB_WIDTH, GLA_QK_WIDTH, GLA_KEY_DIM ** -0.5),
    (0, 3 * SB_WIDTH + GLA_QK_WIDTH, GLA_QK_WIDTH, 1.0),
    (0, 3 * SB_WIDTH + 2 * GLA_QK_WIDTH, GLA_V_WIDTH, 1.0),
    (1, 0, GLA_V_WIDTH, 1.0),
    (2, 0, LANES, 1.0),
)


def _stage_w_in(wt_hbm, w_main_ref, w_r_ref, w_lr_ref, stage_ref, sem):
    slots, rows, d = stage_ref.shape
    lr0 = _MAIN_WIDTH
    r0 = lr0 + GLA_GATE_RANK
    jobs = [(c, w_main_ref, c, rows) for c in range(0, lr0, rows)]
    jobs += [(lr0, w_lr_ref, 0, GLA_GATE_RANK)]
    jobs += [(r0 + c, w_r_ref, c, rows) for c in range(0, GLA_V_WIDTH, rows)]
    lane = lax.broadcasted_iota(jnp.int32, (d, rows), 1)

    def copy(n):
        return pltpu.make_async_copy(wt_hbm.at[pl.ds(jobs[n][0], rows), :],
                                     stage_ref.at[n % slots], sem.at[n % slots])

    for n in range(min(slots - 1, len(jobs))):
        copy(n).start()
    for n, (_, dst, col, valid) in enumerate(jobs):
        if n + slots - 1 < len(jobs):
            copy(n + slots - 1).start()
        copy(n).wait()
        blk = stage_ref[n % slots].T
        if valid < rows:
            blk = jnp.where(lane < valid, blk, 0.0)
        dst[:, pl.ds(col, rows)] = blk.astype(BF16)


def _inproj_kernel(layer, x_ref, mod_ref, w_hbm, *refs):
    n_out = len(_PROJ_GROUPS)
    out_refs = refs[:n_out]
    w_main_ref, w_r_ref, w_lr_ref, stage_ref, sem = refs[n_out:]

    @pl.when(pl.program_id(0) == 0)
    def _():
        _stage_w_in(w_hbm.at[layer], w_main_ref, w_r_ref, w_lr_ref, stage_ref, sem)

    shift = mod_ref[0:1, :]
    scale = mod_ref[1:2, :]
    h = (x_ref[...] * (1.0 + scale) + shift).astype(BF16)
    w_refs = (w_main_ref, w_r_ref, w_lr_ref)
    for (src, c0, width, mul), o_ref in zip(_PROJ_GROUPS, out_refs):
        acc = _dot(h, w_refs[src][:, c0:c0 + width])
        if mul != 1.0:
            acc = acc * mul
        o_ref[...] = acc.astype(o_ref.dtype)


def _inproj(x2d, mod, w_in_t, seq, layer):
    t, d = x2d.shape
    tm = INPROJ_TILE
    tiles_per_seq = seq // tm
    widths = [g[2] for g in _PROJ_GROUPS]
    return pl.pallas_call(
        functools.partial(_inproj_kernel, layer),
        out_shape=[jax.ShapeDtypeStruct((t, width), BF16) for width in widths],
        grid=(t // tm,),
        in_specs=[
            pl.BlockSpec((tm, d), lambda i: (i, 0)),
            pl.BlockSpec((None, SUBLANES, d), lambda i: (i // tiles_per_seq, 0, 0)),
            pl.BlockSpec(memory_space=pl.ANY),
        ],
        out_specs=[pl.BlockSpec((tm, width), lambda i: (i, 0)) for width in widths],
        scratch_shapes=[pltpu.VMEM((d, _MAIN_WIDTH), BF16), pltpu.VMEM((d, GLA_V_WIDTH), BF16),
                        pltpu.VMEM((d, LANES), BF16),
                        pltpu.VMEM((INPROJ_STAGE_SLOTS, LANES, d), F32),
                        pltpu.SemaphoreType.DMA((INPROJ_STAGE_SLOTS,))],
        compiler_params=pltpu.CompilerParams(
            dimension_semantics=("arbitrary",), vmem_limit_bytes=VMEM_LIMIT_BYTES),
        name="inproj",
    )(x2d, mod, w_in_t)


def _sb_suffix_sums(sp_blocks, umat):
    m = sp_blocks[0].shape[0]
    res = _dot(jnp.concatenate(sp_blocks, axis=0).astype(BF16), umat)
    cs, nearer = [], None
    for c in reversed(range(len(sp_blocks))):
        part = res[c * m:(c + 1) * m]
        loc, tot = part[:, :SB_TK], part[:, SB_TK:]
        cs.append(loc if nearer is None else loc + nearer)
        nearer = tot if nearer is None else nearer + tot
    return cs[::-1], nearer


def _sb_kernel(q_ref, k_ref, v_ref, g_ref, o_ref):
    tk = SB_TK
    pairs = q_ref.shape[1] // LANES
    nrb = SB_TQ // tk
    nb = SB_WIN_BLOCKS
    lane = lax.broadcasted_iota(jnp.int32, (tk, LANES), 1)
    head0 = lane < SB_HEAD_DIM

    uj = lax.broadcasted_iota(jnp.int32, (tk, 2 * tk), 0)
    us = lax.broadcasted_iota(jnp.int32, (tk, 2 * tk), 1)
    umat = jnp.where((uj >= us) | (us >= tk), 1.0, 0.0).astype(BF16)
    row = lax.broadcasted_iota(jnp.int32, (2 * tk, tk), 0) & (tk - 1)
    col = lax.broadcasted_iota(jnp.int32, (2 * tk, tk), 1)
    causal = col < row
    lanes = [slice(p * LANES, (p + 1) * LANES) for p in range(pairs)]

    first_blk = [pl.program_id(1) * nrb + r - (nb - 1) for r in range(nrb)]
    blk_rows = [[pl.ds(pl.multiple_of(jnp.maximum(first_blk[r] + c, 0) * tk, tk), tk)
                 for c in range(nb)] for r in range(nrb)]
    units = [(r, p) for r in range(nrb) for p in range(pairs)]

    qs, zs, sums, ws, accs = {}, {}, {}, {}, {}

    def stage_logits(u):
        r, p = u
        q = q_ref[r * tk:(r + 1) * tk, lanes[p]]
        zq = jnp.zeros_like(q)
        qs[u] = jnp.concatenate([jnp.where(head0, q, zq), jnp.where(head0, zq, q)], axis=0)
        kw = jnp.concatenate([k_ref[rows, lanes[p]] for rows in blk_rows[r]], axis=0)
        z = _sb_logits(qs[u], kw)
        zs[u] = [z[:, c * tk:(c + 1) * tk] for c in range(nb)]
        zs[u][-1] = jnp.where(causal, zs[u][-1], SB_MASKED_BITS)

    def stage_sums(u):
        sums[u] = _sb_suffix_sums([_softplus2(z) for z in zs[u]], umat)

    def stage_weights(u):
        args = [z - cs for z, cs in zip(zs[u], sums[u][0])]
        ws[u] = jnp.exp2(jnp.concatenate(args, axis=1)).astype(BF16)

    def stage_values(u):
        r, p = u
        vw = [v_ref[rows, lanes[p]] for rows in blk_rows[r]]
        vw = [jnp.where(first_blk[r] + c >= 0, vw[c], jnp.zeros_like(vw[c]))
              for c in range(nb - 1)] + vw[-1:]
        accs[u] = _dot(ws[u], jnp.concatenate(vw, axis=0))

    stages = (stage_logits, stage_sums, stage_weights, stage_values)
    for t in range(len(stages) + len(units) - 1):
        for n, u in enumerate(units):
            if 0 <= t - n < len(stages):
                stages[t - n](u)

    def not_done(cs_):
        m = cs_[0]
        for c in cs_[1:]:
            m = jnp.minimum(m, c)
        return (jnp.min(m) < SB_EXIT_BITS).astype(jnp.int32)

    def cond(s):
        return jnp.logical_and(s[0] >= 0, s[1] > 0)

    def write_out(r, accs_):
        outs = []
        for p in range(pairs):
            o = jnp.where(head0, accs_[p][:tk], accs_[p][tk:])
            sq = o * o
            ss0 = jnp.sum(jnp.where(head0, sq, 0.0), axis=-1, keepdims=True)
            ss1 = jnp.sum(jnp.where(head0, 0.0, sq), axis=-1, keepdims=True)
            ms = jnp.where(head0, ss0, ss1) * (1.0 / SB_HEAD_DIM)
            outs.append(o * lax.rsqrt(ms + RMS_EPS))
        o_ref[r * tk:(r + 1) * tk, :] = (jnp.concatenate(outs, axis=1) * g_ref[...]).astype(o_ref.dtype)

    for r in range(nrb):
        write_out(r, [accs[r, p] for p in range(pairs)])
    for r in range(nrb):
        carries = tuple(sums[r, p][1] for p in range(pairs))
        more = not_done(carries)

        def body(s, r=r):
            j, _, cs_, as_ = s
            rows = pl.ds(pl.multiple_of(j * tk, tk), tk)
            zb = [_sb_logits(qs[r, p], k_ref[rows, lanes[p]]) for p in range(pairs)]
            sb = [_sb_suffix_sums([_softplus2(zb[p])], umat) for p in range(pairs)]
            wb = [jnp.exp2(zb[p] - sb[p][0][0] - cs_[p]).astype(BF16) for p in range(pairs)]
            new_a = [as_[p] + _dot(wb[p], v_ref[rows, lanes[p]]) for p in range(pairs)]
            new_c = [cs_[p] + sb[p][1] for p in range(pairs)]
            return j - 1, not_done(new_c), tuple(new_c), tuple(new_a)

        @pl.when(jnp.logical_and(first_blk[r] - 1 >= 0, more > 0))
        def _(r=r, carries=carries, more=more, body=body):
            _, _, _, accs_ = lax.while_loop(
                cond, body,
                (first_blk[r] - 1, more, carries, tuple(accs[r, p] for p in range(pairs))))
            write_out(r, accs_)


def _sb_attention(q, k, v, g):
    b, s, wdt = q.shape
    return pl.pallas_call(
        _sb_kernel,
        out_shape=jax.ShapeDtypeStruct((b, s, wdt), BF16),
        grid=(b, s // SB_TQ),
        in_specs=[
            pl.BlockSpec((None, SB_TQ, wdt), lambda bi, i: (bi, i, 0)),
            pl.BlockSpec((None, s, wdt), lambda bi, i: (bi, 0, 0)),
            pl.BlockSpec((None, s, wdt), lambda bi, i: (bi, 0, 0)),
            pl.BlockSpec((1, wdt), lambda bi, i: (0, 0)),
        ],
        out_specs=pl.BlockSpec((None, SB_TQ, wdt), lambda bi, i: (bi, i, 0)),
        compiler_params=pltpu.CompilerParams(
            dimension_semantics=("arbitrary", "arbitrary"),
            vmem_limit_bytes=VMEM_LIMIT_BYTES),
        name="sb_attention",
    )(q, k, v, g)


def _gla_kernel(q_ref, k_ref, v_ref, r_ref, lr_ref, wg_ref, bg_ref, g_ref, o_ref, st_ref):
    c = GLA_CHUNK

    @pl.when(pl.program_id(2) == 0)
    def _():
        st_ref[...] = jnp.zeros_like(st_ref)

    u = _dot(lr_ref[...], wg_ref[...]) + bg_ref[...]
    log_a = (jnp.minimum(u, 0.0) - jnp.log(1.0 + jnp.exp(-jnp.abs(u)))) * (1.0 / GLA_GATE_TEMP)

    ti = lax.broadcasted_iota(jnp.int32, (c, c), 0)
    tj = lax.broadcasted_iota(jnp.int32, (c, c), 1)
    lower = ti >= tj
    tril = jnp.where(lower, 1.0, 0.0).astype(BF16)
    lane = lax.broadcasted_iota(jnp.int32, (c, LANES), 1)
    head0 = lane < GLA_KEY_DIM

    chunks = [slice(n * c, (n + 1) * c) for n in range(q_ref.shape[0] // c)]
    la_hi, la_lo = _split_bf16(log_a)
    la_parts = jnp.concatenate([la_hi, la_lo], axis=1)
    bcum, dec = [], []
    for rows in chunks:
        res = _dot(tril, la_parts[rows])
        bcum.append(res[:, :LANES] + res[:, LANES:])
        dec.append(jnp.exp(bcum[-1][c - 1:c, :]))
    q2, k_inv, k_end2 = [], [], []
    for n, rows in enumerate(chunks):
        kf = k_ref[rows, :].astype(F32) * jnp.exp(-bcum[n])
        q_dec = (q_ref[rows, :].astype(F32) * jnp.exp(bcum[n])).astype(BF16)
        k_end = (kf * dec[n]).astype(BF16)
        zq = jnp.zeros_like(q_dec)
        q2.append(jnp.concatenate([jnp.where(head0, q_dec, zq), jnp.where(head0, zq, q_dec)], axis=0))
        k_end2.append(jnp.concatenate([jnp.where(head0, k_end, zq), jnp.where(head0, zq, k_end)], axis=0))
        k_inv.append(kf.astype(BF16))
    lower2 = jnp.concatenate([lower, lower], axis=0)
    o_intra = []
    for n, rows in enumerate(chunks):
        a = jnp.where(lower2, _dot_nt(q2[n], k_inv[n]), 0.0).astype(BF16)
        o_intra.append((_dot(a[:c], v_ref[rows, :GLA_VAL_DIM]), _dot(a[c:], v_ref[rows, GLA_VAL_DIM:])))
    kv = [_dot_tn(jnp.concatenate([v_ref[rows, :GLA_VAL_DIM], v_ref[rows, GLA_VAL_DIM:]], axis=0), k_end2[n])
          for n, rows in enumerate(chunks)]
    st = st_ref[...]
    starts = []
    for n in range(len(chunks)):
        starts.append(st.astype(BF16))
        st = st * dec[n] + kv[n]
    st_ref[...] = st
    for n, rows in enumerate(chunks):
        o_inter = _dot_nt(q2[n], starts[n])
        o0 = o_intra[n][0] + o_inter[:c]
        o1 = o_intra[n][1] + o_inter[c:]
        n0 = o0 * lax.rsqrt(jnp.mean(o0 * o0, axis=-1, keepdims=True) + RMS_EPS)
        n1 = o1 * lax.rsqrt(jnp.mean(o1 * o1, axis=-1, keepdims=True) + RMS_EPS)
        rf = r_ref[rows, :].astype(F32)
        gate = rf / (1.0 + jnp.exp(-rf))
        o_ref[rows, :] = (jnp.concatenate([n0, n1], axis=1) * g_ref[...] * gate).astype(o_ref.dtype)


def _gla(q, k, v, r, lr, wg, bg, g):
    b, s, _ = q.shape
    pairs = GLA_HEADS // 2
    step = min(GLA_STEP, s)
    dv2 = 2 * GLA_VAL_DIM
    seq_map = lambda bi, p, t: (bi, t, p)
    return pl.pallas_call(
        _gla_kernel,
        out_shape=jax.ShapeDtypeStruct((b, s, GLA_V_WIDTH), BF16),
        grid=(b, pairs, s // step),
        in_specs=[
            pl.BlockSpec((None, step, LANES), seq_map),
            pl.BlockSpec((None, step, LANES), seq_map),
            pl.BlockSpec((None, step, dv2), seq_map),
            pl.BlockSpec((None, step, dv2), seq_map),
            pl.BlockSpec((None, step, LANES), lambda bi, p, t: (bi, t, 0)),
            pl.BlockSpec((LANES, LANES), lambda bi, p, t: (0, p)),
            pl.BlockSpec((1, LANES), lambda bi, p, t: (0, p)),
            pl.BlockSpec((1, dv2), lambda bi, p, t: (0, p)),
        ],
        out_specs=pl.BlockSpec((None, step, dv2), seq_map),
        scratch_shapes=[pltpu.VMEM((GLA_VAL_DIM, LANES), F32)],
        compiler_params=pltpu.CompilerParams(
            dimension_semantics=("arbitrary", "arbitrary", "arbitrary"),
            vmem_limit_bytes=VMEM_LIMIT_BYTES),
        name="gla",
    )(q, k, v, r, lr, wg, bg, g)


def _post_kernel(alpha, tiles_per_seq, layer, osb_ref, ogla_ref, x_ref, mod_ref, wo_hbm, g1_ref, b1_ref,
                 wg_hbm, wu_hbm, cw_ref, cb_ref, wd_hbm, g_ref, b_ref,
                 o_ref, halo_ref, act_ref, x1_ref, wo_ref, wg_ref, wu_ref, wd_ref, stage_ref, sem):
    tm = POST_SUB_TILE
    nsub = x_ref.shape[0] // tm
    d_ff = wg_ref.shape[1]
    n_sb = osb_ref.shape[1]
    first = (pl.program_id(0) % tiles_per_seq) == 0

    @pl.when(pl.program_id(0) == 0)
    def _():
        _stage_weights_bf16(((wo_hbm.at[layer], wo_ref), (wg_hbm.at[layer], wg_ref),
                             (wu_hbm.at[layer], wu_ref), (wd_hbm.at[layer], wd_ref)), stage_ref, sem)

    @pl.when(first)
    def _():
        halo_ref[...] = jnp.zeros_like(halo_ref)

    top = lax.broadcasted_iota(jnp.int32, (SUBLANES, FF_TILE), 0)
    hs = []
    for j in range(nsub):
        rows = slice(j * tm, (j + 1) * tm)
        mix = _dot(osb_ref[rows, :], wo_ref[:n_sb, :]) + _dot(ogla_ref[rows, :], wo_ref[n_sb:, :])
        x1_ref[rows, :] = _layer_norm(alpha * x_ref[rows, :] + (1.0 + mod_ref[2:3, :]) * mix,
                                      g1_ref[...], b1_ref[...])
        hs.append((x1_ref[rows, :] * (1.0 + mod_ref[4:5, :]) + mod_ref[3:4, :]).astype(BF16))
    for j in range(nsub):
        rows = slice(j * tm, (j + 1) * tm)
        h = hs[j]
        for f in range(d_ff // FF_TILE):
            cols = slice(f * FF_TILE, (f + 1) * FF_TILE)
            gt = _dot(h, wg_ref[:, cols])
            up = _dot(h, wu_ref[:, cols])
            prev = halo_ref[:, cols]
            halo_ref[:, cols] = gt[tm - SUBLANES:, :]
            r1 = pltpu.roll(gt, 1, 0)
            r2 = pltpu.roll(gt, 2, 0)
            t1 = jnp.where(top < 1, pltpu.roll(prev, 1, 0), r1[:SUBLANES])
            t2 = jnp.where(top < 2, pltpu.roll(prev, 2, 0), r2[:SUBLANES])
            g1 = jnp.concatenate([t1, r1[SUBLANES:]], axis=0)
            g2 = jnp.concatenate([t2, r2[SUBLANES:]], axis=0)
            conv = (g2 * cw_ref[0:1, cols] + g1 * cw_ref[1:2, cols] + gt * cw_ref[2:3, cols]
                    + cb_ref[:, cols])
            act_ref[rows, cols] = (conv / (1.0 + jnp.exp(-conv)) * up).astype(BF16)
        y = _dot(act_ref[rows, :], wd_ref[...])
        o_ref[rows, :] = _layer_norm(alpha * x1_ref[rows, :] + (1.0 + mod_ref[5:6, :]) * y,
                                     g_ref[...], b_ref[...])


def _post(o_sb, o_gla, x2d, mod, w_o, ln1_g, ln1_b, wg, wu, cw, cb, wd, ln_g, ln_b, seq, alpha, layer):
    t, d = x2d.shape
    d_ff = wg.shape[2]
    tm = ROW_TILE
    tiles_per_seq = seq // tm
    row_map = lambda i: (i, 0)
    fixed = lambda i: (0, 0)
    hbm = pl.BlockSpec(memory_space=pl.ANY)
    stage_width = max(w.shape[2] for w in (w_o, wg, wu, wd))
    return pl.pallas_call(
        functools.partial(_post_kernel, alpha, tiles_per_seq, layer),
        out_shape=jax.ShapeDtypeStruct((t, d), F32),
        grid=(t // tm,),
        in_specs=[
            pl.BlockSpec((tm, o_sb.shape[1]), row_map),
            pl.BlockSpec((tm, o_gla.shape[1]), row_map),
            pl.BlockSpec((tm, d), row_map),
            pl.BlockSpec((None, SUBLANES, d), lambda i: (i // tiles_per_seq, 0, 0)),
            hbm,
            pl.BlockSpec((1, d), fixed),
            pl.BlockSpec((1, d), fixed),
            hbm,
            hbm,
            pl.BlockSpec(cw.shape, fixed),
            pl.BlockSpec(cb.shape, fixed),
            hbm,
            pl.BlockSpec((1, d), fixed),
            pl.BlockSpec((1, d), fixed),
        ],
        out_specs=pl.BlockSpec((tm, d), row_map),
        scratch_shapes=[pltpu.VMEM((SUBLANES, d_ff), F32), pltpu.VMEM((tm, d_ff), BF16),
                        pltpu.VMEM((tm, d), F32),
                        pltpu.VMEM(w_o.shape[1:], BF16), pltpu.VMEM(wg.shape[1:], BF16),
                        pltpu.VMEM(wu.shape[1:], BF16), pltpu.VMEM(wd.shape[1:], BF16),
                        pltpu.VMEM((STAGE_SLOTS, STAGE_ROWS, stage_width), F32),
                        pltpu.SemaphoreType.DMA((STAGE_SLOTS,))],
        compiler_params=pltpu.CompilerParams(
            dimension_semantics=("arbitrary",), vmem_limit_bytes=VMEM_LIMIT_BYTES),
        name="post",
    )(o_sb, o_gla, x2d, mod, w_o, ln1_g, ln1_b, wg, wu, cw, cb, wd, ln_g, ln_b)


def kernel(x, c, w_ada, b_ada, w_in, gla_w_gate, gla_b_gate, sb_norm_g, gla_norm_g, w_out,
           ln1_g, ln1_b, w_ff_gate, w_ff_up, conv_w, conv_b, w_down, ln2_g, ln2_b):
    bsz, seq, d = x.shape
    depth = w_ada.shape[0]
    alpha = float((2 * depth) ** 0.25)
    t = bsz * seq

    c_pad = jnp.concatenate([c, jnp.zeros((SUBLANES - bsz, d), c.dtype)], axis=0)
    w_in_t = jnp.swapaxes(w_in, 1, 2)
    xcur = x.reshape(t, d)
    for l in range(depth):
        mod = _ada(c_pad, w_ada[l], b_ada[l][None, :])
        mod = mod[:bsz].reshape(bsz, N_MOD, d)
        mod = jnp.concatenate([mod, jnp.zeros((bsz, SUBLANES - N_MOD, d), F32)], axis=1)

        sb_q, sb_k, sb_v, g_q, g_k, g_v, g_r, g_lr = _inproj(xcur, mod, w_in_t, seq, l)
        as_seq = lambda a: a.reshape(bsz, seq, a.shape[-1])

        o_sb = _sb_attention(as_seq(sb_q), as_seq(sb_k), as_seq(sb_v), sb_norm_g[l][None, :])
        wg_pad = jnp.concatenate(
            [gla_w_gate[l], jnp.zeros((LANES - GLA_GATE_RANK, GLA_QK_WIDTH), F32)], axis=0).astype(BF16)
        o_gla = _gla(as_seq(g_q), as_seq(g_k), as_seq(g_v), as_seq(g_r), as_seq(g_lr),
                     wg_pad, gla_b_gate[l][None, :], gla_norm_g[l][None, :])

        xcur = _post(o_sb.reshape(t, SB_WIDTH), o_gla.reshape(t, GLA_V_WIDTH), xcur, mod,
                     w_out, ln1_g[l][None, :], ln1_b[l][None, :],
                     w_ff_gate, w_ff_up, conv_w[l], conv_b[l][None, :], w_down,
                     ln2_g[l][None, :], ln2_b[l][None, :], seq, alpha, l)
    return xcur.reshape(bsz, seq, d)
```

```python
import functools

import jax
import jax.numpy as jnp
from jax import lax
from jax.experimental import pallas as pl
from jax.experimental.pallas import tpu as pltpu

F32 = jnp.float32
BF16 = jnp.bfloat16

LANES = 128
SUBLANES = 8
VMEM_LIMIT_BYTES = 60 * 1024 * 1024

SB_HEADS = 8
SB_HEAD_DIM = 64
GLA_HEADS = 4
GLA_KEY_DIM = 64
GLA_VAL_DIM = 128
GLA_GATE_RANK = 16
GLA_GATE_TEMP = 16.0
GLA_CHUNK = 64
CONV_WIDTH = 3
N_MOD = 6
LN_EPS = 1e-5
RMS_EPS = 1e-6

SB_WIDTH = SB_HEADS * SB_HEAD_DIM
GLA_QK_WIDTH = GLA_HEADS * GLA_KEY_DIM
GLA_V_WIDTH = GLA_HEADS * GLA_VAL_DIM

SB_EXIT = 88.0
LOG2E = 1.4426950408889634
SB_EXIT_BITS = SB_EXIT * LOG2E
SB_MAX_BITS = 126.0
SB_MASKED_BITS = -1e30
SB_TK = 128
SB_TQ = 8 * SB_TK
SB_WIN_BLOCKS = 3
SB_WIN = SB_WIN_BLOCKS * SB_TK

GLA_STEP = 4096
ADA_ROWS = 256
INPROJ_TILE = 1024
ROW_TILE = 1024
POST_SUB_TILE = 256
FF_TILE = 256
STAGE_ROWS = 128
STAGE_SLOTS = 3
INPROJ_STAGE_SLOTS = 6


def _sb_logits(q, k):
    return jnp.minimum(_dot_nt(q, k), SB_MAX_BITS)


def _softplus2(z):
    return jnp.log2(1.0 + jnp.exp2(z))


def _split_bf16(a):
    hi = a.astype(BF16)
    lo = (a - hi.astype(F32)).astype(BF16)
    return hi, lo


def _dot(a, b):
    return jnp.dot(a, b, preferred_element_type=F32)


def _dot_nt(a, b):
    return lax.dot_general(a, b, (((1,), (1,)), ((), ())), preferred_element_type=F32)


def _dot_tn(a, b):
    return lax.dot_general(a, b, (((0,), (0,)), ((), ())), preferred_element_type=F32)


def _stage_weights_bf16(pairs, stage_ref, sem):
    slots, rows = stage_ref.shape[0], stage_ref.shape[1]
    jobs = [(src, dst, r0) for src, dst in pairs for r0 in range(0, src.shape[0], rows)]

    def window(n):
        return stage_ref.at[n % slots, :, pl.ds(0, jobs[n][0].shape[1])]

    def copy(n):
        src, _, r0 = jobs[n]
        return pltpu.make_async_copy(src.at[pl.ds(r0, rows), :], window(n), sem.at[n % slots])

    for n in range(min(slots - 1, len(jobs))):
        copy(n).start()
    for n, (_, dst, r0) in enumerate(jobs):
        if n + slots - 1 < len(jobs):
            copy(n + slots - 1).start()
        copy(n).wait()
        dst[pl.ds(r0, rows), :] = window(n)[...].astype(BF16)


def _layer_norm(v, g, b):
    mu = jnp.mean(v, axis=-1, keepdims=True)
    vc = v - mu
    var = jnp.mean(vc * vc, axis=-1, keepdims=True)
    return vc * lax.rsqrt(var + LN_EPS) * g + b


def _ada_kernel(c_ref, w_ref, b_ref, o_ref):
    c = c_ref[...]
    ca = c / (1.0 + jnp.exp(-c))
    part = _dot(ca.astype(BF16), w_ref[...].astype(BF16))

    @pl.when(pl.program_id(0) == 0)
    def _():
        o_ref[...] = part + b_ref[...]

    @pl.when(pl.program_id(0) > 0)
    def _():
        o_ref[...] += part


def _ada(c_pad, w, b):
    d, n = w.shape
    tk = ADA_ROWS
    return pl.pallas_call(
        _ada_kernel,
        out_shape=jax.ShapeDtypeStruct((c_pad.shape[0], n), F32),
        grid=(d // tk,),
        in_specs=[
            pl.BlockSpec((c_pad.shape[0], tk), lambda k: (0, k)),
            pl.BlockSpec((tk, n), lambda k: (k, 0)),
            pl.BlockSpec((1, n), lambda k: (0, 0)),
        ],
        out_specs=pl.BlockSpec((c_pad.shape[0], n), lambda k: (0, 0)),
        compiler_params=pltpu.CompilerParams(
            dimension_semantics=("arbitrary",), vmem_limit_bytes=VMEM_LIMIT_BYTES),
        name="ada",
    )(c_pad, w, b)


_MAIN_WIDTH = 3 * SB_WIDTH + 2 * GLA_QK_WIDTH + GLA_V_WIDTH
_PROJ_GROUPS = (
    (0, 0, SB_WIDTH, SB_HEAD_DIM ** -0.5 * LOG2E),
    (0, SB_WIDTH, SB_WIDTH, 1.0),
    (0, 2 * SB_WIDTH, SB_WIDTH, 1.0),
    (0, 3 * SB_WIDTH, GLA_QK_WIDTH, GLA_KEY_DIM ** -0.5),
    (0, 3 * SB_WIDTH + GLA_QK_WIDTH, GLA_QK_WIDTH, 1.0),
    (0, 3 * SB_WIDTH + 2 * GLA_QK_WIDTH, GLA_V_WIDTH, 1.0),
    (1, 0, GLA_V_WIDTH, 1.0),
    (2, 0, LANES, 1.0),
)


def _stage_w_in(wt_hbm, w_main_ref, w_r_ref, w_lr_ref, stage_ref, sem):
    slots, rows, d = stage_ref.shape
    lr0 = _MAIN_WIDTH
    r0 = lr0 + GLA_GATE_RANK
    jobs = [(c, w_main_ref, c, rows) for c in range(0, lr0, rows)]
    jobs += [(lr0, w_lr_ref, 0, GLA_GATE_RANK)]
    jobs += [(r0 + c, w_r_ref, c, rows) for c in range(0, GLA_V_WIDTH, rows)]
    lane = lax.broadcasted_iota(jnp.int32, (d, rows), 1)

    def copy(n):
        return pltpu.make_async_copy(wt_hbm.at[pl.ds(jobs[n][0], rows), :],
                                     stage_ref.at[n % slots], sem.at[n % slots])

    for n in range(min(slots - 1, len(jobs))):
        copy(n).start()
    for n, (_, dst, col, valid) in enumerate(jobs):
        if n + slots - 1 < len(jobs):
            copy(n + slots - 1).start()
        copy(n).wait()
        blk = stage_ref[n % slots].T
        if valid < rows:
            blk = jnp.where(lane < valid, blk, 0.0)
        dst[:, pl.ds(col, rows)] = blk.astype(BF16)


def _inproj_kernel(layer, x_ref, mod_ref, w_hbm, *refs):
    n_out = len(_PROJ_GROUPS)
    out_refs = refs[:n_out]
    w_main_ref, w_r_ref, w_lr_ref, stage_ref, sem = refs[n_out:]

    @pl.when(pl.program_id(0) == 0)
    def _():
        _stage_w_in(w_hbm.at[layer], w_main_ref, w_r_ref, w_lr_ref, stage_ref, sem)

    shift = mod_ref[0:1, :]
    scale = mod_ref[1:2, :]
    h = (x_ref[...] * (1.0 + scale) + shift).astype(BF16)
    w_refs = (w_main_ref, w_r_ref, w_lr_ref)
    for (src, c0, width, mul), o_ref in zip(_PROJ_GROUPS, out_refs):
        acc = _dot(h, w_refs[src][:, c0:c0 + width])
        if mul != 1.0:
            acc = acc * mul
        o_ref[...] = acc.astype(o_ref.dtype)


def _inproj(x2d, mod, w_in_t, seq, layer):
    t, d = x2d.shape
    tm = INPROJ_TILE
    tiles_per_seq = seq // tm
    widths = [g[2] for g in _PROJ_GROUPS]
    return pl.pallas_call(
        functools.partial(_inproj_kernel, layer),
        out_shape=[jax.ShapeDtypeStruct((t, width), BF16) for width in widths],
        grid=(t // tm,),
        in_specs=[
            pl.BlockSpec((tm, d), lambda i: (i, 0)),
            pl.BlockSpec((None, SUBLANES, d), lambda i: (i // tiles_per_seq, 0, 0)),
            pl.BlockSpec(memory_space=pl.ANY),
        ],
        out_specs=[pl.BlockSpec((tm, width), lambda i: (i, 0)) for width in widths],
        scratch_shapes=[pltpu.VMEM((d, _MAIN_WIDTH), BF16), pltpu.VMEM((d, GLA_V_WIDTH), BF16),
                        pltpu.VMEM((d, LANES), BF16),
                        pltpu.VMEM((INPROJ_STAGE_SLOTS, LANES, d), F32),
                        pltpu.SemaphoreType.DMA((INPROJ_STAGE_SLOTS,))],
        compiler_params=pltpu.CompilerParams(
            dimension_semantics=("arbitrary",), vmem_limit_bytes=VMEM_LIMIT_BYTES),
        name="inproj",
    )(x2d, mod, w_in_t)


def _sb_suffix_sums(sp_blocks, umat):
    m = sp_blocks[0].shape[0]
    res = _dot(jnp.concatenate(sp_blocks, axis=0).astype(BF16), umat)
    cs, nearer = [], None
    for c in reversed(range(len(sp_blocks))):
        part = res[c * m:(c + 1) * m]
        loc, tot = part[:, :SB_TK], part[:, SB_TK:]
        cs.append(loc if nearer is None else loc + nearer)
        nearer = tot if nearer is None else nearer + tot
    return cs[::-1], nearer


def _sb_kernel(q_ref, k_ref, v_ref, g_ref, o_ref):
    tk = SB_TK
    pairs = q_ref.shape[1] // LANES
    nrb = SB_TQ // tk
    nb = SB_WIN_BLOCKS
    lane = lax.broadcasted_iota(jnp.int32, (tk, LANES), 1)
    head0 = lane < SB_HEAD_DIM

    uj = lax.broadcasted_iota(jnp.int32, (tk, 2 * tk), 0)
    us = lax.broadcasted_iota(jnp.int32, (tk, 2 * tk), 1)
    umat = jnp.where((uj >= us) | (us >= tk), 1.0, 0.0).astype(BF16)
    row = lax.broadcasted_iota(jnp.int32, (2 * tk, tk), 0) & (tk - 1)
    col = lax.broadcasted_iota(jnp.int32, (2 * tk, tk), 1)
    causal = col < row
    lanes = [slice(p * LANES, (p + 1) * LANES) for p in range(pairs)]

    first_blk = [pl.program_id(1) * nrb + r - (nb - 1) for r in range(nrb)]
    blk_rows = [[pl.ds(pl.multiple_of(jnp.maximum(first_blk[r] + c, 0) * tk, tk), tk)
                 for c in range(nb)] for r in range(nrb)]
    units = [(r, p) for r in range(nrb) for p in range(pairs)]

    qs, zs, sums, ws, accs = {}, {}, {}, {}, {}

    def stage_logits(u):
        r, p = u
        q = q_ref[r * tk:(r + 1) * tk, lanes[p]]
        zq = jnp.zeros_like(q)
        qs[u] = jnp.concatenate([jnp.where(head0, q, zq), jnp.where(head0, zq, q)], axis=0)
        kw = jnp.concatenate([k_ref[rows, lanes[p]] for rows in blk_rows[r]], axis=0)
        z = _sb_logits(qs[u], kw)
        zs[u] = [z[:, c * tk:(c + 1) * tk] for c in range(nb)]
        zs[u][-1] = jnp.where(causal, zs[u][-1], SB_MASKED_BITS)

    def stage_sums(u):
        sums[u] = _sb_suffix_sums([_softplus2(z) for z in zs[u]], umat)

    def stage_weights(u):
        args = [z - cs for z, cs in zip(zs[u], sums[u][0])]
        ws[u] = jnp.exp2(jnp.concatenate(args, axis=1)).astype(BF16)

    def stage_values(u):
        r, p = u
        vw = [v_ref[rows, lanes[p]] for rows in blk_rows[r]]
        vw = [jnp.where(first_blk[r] + c >= 0, vw[c], jnp.zeros_like(vw[c]))
              for c in range(nb - 1)] + vw[-1:]
        accs[u] = _dot(ws[u], jnp.concatenate(vw, axis=0))

    stages = (stage_logits, stage_sums, stage_weights, stage_values)
    order = sorted((stage + n // 2, n, stage)
                   for n in range(len(units)) for stage in range(len(stages)))
    for _, n, stage in order:
        stages[stage](units[n])

    def not_done(cs_):
        m = cs_[0]
        for c in cs_[1:]:
            m = jnp.minimum(m, c)
        return (jnp.min(m) < SB_EXIT_BITS).astype(jnp.int32)

    def cond(s):
        return jnp.logical_and(s[0] >= 0, s[1] > 0)

    def write_out(r, accs_):
        outs = []
        for p in range(pairs):
            o = jnp.where(head0, accs_[p][:tk], accs_[p][tk:])
            sq = o * o
            ss0 = jnp.sum(jnp.where(head0, sq, 0.0), axis=-1, keepdims=True)
            ss1 = jnp.sum(jnp.where(head0, 0.0, sq), axis=-1, keepdims=True)
            ms = jnp.where(head0, ss0, ss1) * (1.0 / SB_HEAD_DIM)
            outs.append(o * lax.rsqrt(ms + RMS_EPS))
        o_ref[r * tk:(r + 1) * tk, :] = (jnp.concatenate(outs, axis=1) * g_ref[...]).astype(o_ref.dtype)

    for r in range(nrb):
        write_out(r, [accs[r, p] for p in range(pairs)])
    for r in range(nrb):
        carries = tuple(sums[r, p][1] for p in range(pairs))
        more = not_done(carries)

        def body(s, r=r):
            j, _, cs_, as_ = s
            rows = pl.ds(pl.multiple_of(j * tk, tk), tk)
            zb = [_sb_logits(qs[r, p], k_ref[rows, lanes[p]]) for p in range(pairs)]
            sb = [_sb_suffix_sums([_softplus2(zb[p])], umat) for p in range(pairs)]
            wb = [jnp.exp2(zb[p] - sb[p][0][0] - cs_[p]).astype(BF16) for p in range(pairs)]
            new_a = [as_[p] + _dot(wb[p], v_ref[rows, lanes[p]]) for p in range(pairs)]
            new_c = [cs_[p] + sb[p][1] for p in range(pairs)]
            return j - 1, not_done(new_c), tuple(new_c), tuple(new_a)

        @pl.when(jnp.logical_and(first_blk[r] - 1 >= 0, more > 0))
        def _(r=r, carries=carries, more=more, body=body):
            _, _, _, accs_ = lax.while_loop(
                cond, body,
                (first_blk[r] - 1, more, carries, tuple(accs[r, p] for p in range(pairs))))
            write_out(r, accs_)


def _sb_attention(q, k, v, g):
    b, s, wdt = q.shape
    return pl.pallas_call(
        _sb_kernel,
        out_shape=jax.ShapeDtypeStruct((b, s, wdt), BF16),
        grid=(b, s // SB_TQ),
        in_specs=[
            pl.BlockSpec((None, SB_TQ, wdt), lambda bi, i: (bi, i, 0)),
            pl.BlockSpec((None, s, wdt), lambda bi, i: (bi, 0, 0)),
            pl.BlockSpec((None, s, wdt), lambda bi, i: (bi, 0, 0)),
            pl.BlockSpec((1, wdt), lambda bi, i: (0, 0)),
        ],
        out_specs=pl.BlockSpec((None, SB_TQ, wdt), lambda bi, i: (bi, i, 0)),
        compiler_params=pltpu.CompilerParams(
            dimension_semantics=("arbitrary", "arbitrary"),
            vmem_limit_bytes=VMEM_LIMIT_BYTES),
        name="sb_attention",
    )(q, k, v, g)


def _gla_kernel(q_ref, k_ref, v_ref, r_ref, lr_ref, wg_ref, bg_ref, g_ref, o_ref, st_ref):
    c = GLA_CHUNK

    @pl.when(pl.program_id(2) == 0)
    def _():
        st_ref[...] = jnp.zeros_like(st_ref)

    u = _dot(lr_ref[...], wg_ref[...]) + bg_ref[...]
    log_a = (jnp.minimum(u, 0.0) - jnp.log(1.0 + jnp.exp(-jnp.abs(u)))) * (1.0 / GLA_GATE_TEMP)

    ti = lax.broadcasted_iota(jnp.int32, (c, c), 0)
    tj = lax.broadcasted_iota(jnp.int32, (c, c), 1)
    lower = ti >= tj
    tril = jnp.where(lower, 1.0, 0.0).astype(BF16)
    lane = lax.broadcasted_iota(jnp.int32, (c, LANES), 1)
    head0 = lane < GLA_KEY_DIM

    chunks = [slice(n * c, (n + 1) * c) for n in range(q_ref.shape[0] // c)]
    la_hi, la_lo = _split_bf16(log_a)
    la_parts = jnp.concatenate([la_hi, la_lo], axis=1)
    bcum, dec = [], []
    for rows in chunks:
        res = _dot(tril, la_parts[rows])
        bcum.append(res[:, :LANES] + res[:, LANES:])
        dec.append(jnp.exp(bcum[-1][c - 1:c, :]))
    q2, k_inv, k_end2 = [], [], []
    for n, rows in enumerate(chunks):
        kf = k_ref[rows, :].astype(F32) * jnp.exp(-bcum[n])
        q_dec = (q_ref[rows, :].astype(F32) * jnp.exp(bcum[n])).astype(BF16)
        k_end = (kf * dec[n]).astype(BF16)
        zq = jnp.zeros_like(q_dec)
        q2.append(jnp.concatenate([jnp.where(head0, q_dec, zq), jnp.where(head0, zq, q_dec)], axis=0))
        k_end2.append(jnp.concatenate([jnp.where(head0, k_end, zq), jnp.where(head0, zq, k_end)], axis=0))
        k_inv.append(kf.astype(BF16))
    lower2 = jnp.concatenate([lower, lower], axis=0)
    o_intra = []
    for n, rows in enumerate(chunks):
        a = jnp.where(lower2, _dot_nt(q2[n], k_inv[n]), 0.0).astype(BF16)
        o_intra.append((_dot(a[:c], v_ref[rows, :GLA_VAL_DIM]), _dot(a[c:], v_ref[rows, GLA_VAL_DIM:])))
    kv = [_dot_tn(jnp.concatenate([v_ref[rows, :GLA_VAL_DIM], v_ref[rows, GLA_VAL_DIM:]], axis=0), k_end2[n])
          for n, rows in enumerate(chunks)]
    st = st_ref[...]
    starts = []
    for n in range(len(chunks)):
        starts.append(st.astype(BF16))
        st = st * dec[n] + kv[n]
    st_ref[...] = st
    for n, rows in enumerate(chunks):
        o_inter = _dot_nt(q2[n], starts[n])
        o0 = o_intra[n][0] + o_inter[:c]
        o1 = o_intra[n][1] + o_inter[c:]
        n0 = o0 * lax.rsqrt(jnp.mean(o0 * o0, axis=-1, keepdims=True) + RMS_EPS)
        n1 = o1 * lax.rsqrt(jnp.mean(o1 * o1, axis=-1, keepdims=True) + RMS_EPS)
        rf = r_ref[rows, :].astype(F32)
        gate = rf / (1.0 + jnp.exp(-rf))
        o_ref[rows, :] = (jnp.concatenate([n0, n1], axis=1) * g_ref[...] * gate).astype(o_ref.dtype)


def _gla(q, k, v, r, lr, wg, bg, g):
    b, s, _ = q.shape
    pairs = GLA_HEADS // 2
    step = min(GLA_STEP, s)
    dv2 = 2 * GLA_VAL_DIM
    seq_map = lambda bi, p, t: (bi, t, p)
    return pl.pallas_call(
        _gla_kernel,
        out_shape=jax.ShapeDtypeStruct((b, s, GLA_V_WIDTH), BF16),
        grid=(b, pairs, s // step),
        in_specs=[
            pl.BlockSpec((None, step, LANES), seq_map),
            pl.BlockSpec((None, step, LANES), seq_map),
            pl.BlockSpec((None, step, dv2), seq_map),
            pl.BlockSpec((None, step, dv2), seq_map),
            pl.BlockSpec((None, step, LANES), lambda bi, p, t: (bi, t, 0)),
            pl.BlockSpec((LANES, LANES), lambda bi, p, t: (0, p)),
            pl.BlockSpec((1, LANES), lambda bi, p, t: (0, p)),
            pl.BlockSpec((1, dv2), lambda bi, p, t: (0, p)),
        ],
        out_specs=pl.BlockSpec((None, step, dv2), seq_map),
        scratch_shapes=[pltpu.VMEM((GLA_VAL_DIM, LANES), F32)],
        compiler_params=pltpu.CompilerParams(
            dimension_semantics=("arbitrary", "arbitrary", "arbitrary"),
            vmem_limit_bytes=VMEM_LIMIT_BYTES),
        name="gla",
    )(q, k, v, r, lr, wg, bg, g)


def _post_kernel(alpha, tiles_per_seq, layer, osb_ref, ogla_ref, x_ref, mod_ref, wo_hbm, g1_ref, b1_ref,
                 wg_hbm, wu_hbm, cw_ref, cb_ref, wd_hbm, g_ref, b_ref,
                 o_ref, halo_ref, act_ref, x1_ref, wo_ref, wg_ref, wu_ref, wd_ref, stage_ref, sem):
    tm = POST_SUB_TILE
    nsub = x_ref.shape[0] // tm
    d_ff = wg_ref.shape[1]
    n_sb = osb_ref.shape[1]
    first = (pl.program_id(0) % tiles_per_seq) == 0

    @pl.when(pl.program_id(0) == 0)
    def _():
        _stage_weights_bf16(((wo_hbm.at[layer], wo_ref), (wg_hbm.at[layer], wg_ref),
                             (wu_hbm.at[layer], wu_ref), (wd_hbm.at[layer], wd_ref)), stage_ref, sem)

    @pl.when(first)
    def _():
        halo_ref[...] = jnp.zeros_like(halo_ref)

    top = lax.broadcasted_iota(jnp.int32, (SUBLANES, FF_TILE), 0)
    hs = []
    for j in range(nsub):
        rows = slice(j * tm, (j + 1) * tm)
        mix = _dot(osb_ref[rows, :], wo_ref[:n_sb, :]) + _dot(ogla_ref[rows, :], wo_ref[n_sb:, :])
        x1_ref[rows, :] = _layer_norm(alpha * x_ref[rows, :] + (1.0 + mod_ref[2:3, :]) * mix,
                                      g1_ref[...], b1_ref[...])
        hs.append((x1_ref[rows, :] * (1.0 + mod_ref[4:5, :]) + mod_ref[3:4, :]).astype(BF16))
    for j in range(nsub):
        rows = slice(j * tm, (j + 1) * tm)
        h = hs[j]
        for f in range(d_ff // FF_TILE):
            cols = slice(f * FF_TILE, (f + 1) * FF_TILE)
            gt = _dot(h, wg_ref[:, cols])
            up = _dot(h, wu_ref[:, cols])
            prev = halo_ref[:, cols]
            halo_ref[:, cols] = gt[tm - SUBLANES:, :]
            r1 = pltpu.roll(gt, 1, 0)
            r2 = pltpu.roll(gt, 2, 0)
            t1 = jnp.where(top < 1, pltpu.roll(prev, 1, 0), r1[:SUBLANES])
            t2 = jnp.where(top < 2, pltpu.roll(prev, 2, 0), r2[:SUBLANES])
            g1 = jnp.concatenate([t1, r1[SUBLANES:]], axis=0)
            g2 = jnp.concatenate([t2, r2[SUBLANES:]], axis=0)
            conv = (g2 * cw_ref[0:1, cols] + g1 * cw_ref[1:2, cols] + gt * cw_ref[2:3, cols]
                    + cb_ref[:, cols])
            act_ref[rows, cols] = (conv / (1.0 + jnp.exp(-conv)) * up).astype(BF16)
        y = _dot(act_ref[rows, :], wd_ref[...])
        o_ref[rows, :] = _layer_norm(alpha * x1_ref[rows, :] + (1.0 + mod_ref[5:6, :]) * y,
                                     g_ref[...], b_ref[...])


def _post(o_sb, o_gla, x2d, mod, w_o, ln1_g, ln1_b, wg, wu, cw, cb, wd, ln_g, ln_b, seq, alpha, layer):
    t, d = x2d.shape
    d_ff = wg.shape[2]
    tm = ROW_TILE
    tiles_per_seq = seq // tm
    row_map = lambda i: (i, 0)
    fixed = lambda i: (0, 0)
    hbm = pl.BlockSpec(memory_space=pl.ANY)
    stage_width = max(w.shape[2] for w in (w_o, wg, wu, wd))
    return pl.pallas_call(
        functools.partial(_post_kernel, alpha, tiles_per_seq, layer),
        out_shape=jax.ShapeDtypeStruct((t, d), F32),
        grid=(t // tm,),
        in_specs=[
            pl.BlockSpec((tm, o_sb.shape[1]), row_map),
            pl.BlockSpec((tm, o_gla.shape[1]), row_map),
            pl.BlockSpec((tm, d), row_map),
            pl.BlockSpec((None, SUBLANES, d), lambda i: (i // tiles_per_seq, 0, 0)),
            hbm,
            pl.BlockSpec((1, d), fixed),
            pl.BlockSpec((1, d), fixed),
            hbm,
            hbm,
            pl.BlockSpec(cw.shape, fixed),
            pl.BlockSpec(cb.shape, fixed),
            hbm,
            pl.BlockSpec((1, d), fixed),
            pl.BlockSpec((1, d), fixed),
        ],
        out_specs=pl.BlockSpec((tm, d), row_map),
        scratch_shapes=[pltpu.VMEM((SUBLANES, d_ff), F32), pltpu.VMEM((tm, d_ff), BF16),
                        pltpu.VMEM((tm, d), F32),
                        pltpu.VMEM(w_o.shape[1:], BF16), pltpu.VMEM(wg.shape[1:], BF16),
                        pltpu.VMEM(wu.shape[1:], BF16), pltpu.VMEM(wd.shape[1:], BF16),
                        pltpu.VMEM((STAGE_SLOTS, STAGE_ROWS, stage_width), F32),
                        pltpu.SemaphoreType.DMA((STAGE_SLOTS,))],
        compiler_params=pltpu.CompilerParams(
            dimension_semantics=("arbitrary",), vmem_limit_bytes=VMEM_LIMIT_BYTES),
        name="post",
    )(o_sb, o_gla, x2d, mod, w_o, ln1_g, ln1_b, wg, wu, cw, cb, wd, ln_g, ln_b)


def kernel(x, c, w_ada, b_ada, w_in, gla_w_gate, gla_b_gate, sb_norm_g, gla_norm_g, w_out,
           ln1_g, ln1_b, w_ff_gate, w_ff_up, conv_w, conv_b, w_down, ln2_g, ln2_b):
    bsz, seq, d = x.shape
    depth = w_ada.shape[0]
    alpha = float((2 * depth) ** 0.25)
    t = bsz * seq

    c_pad = jnp.concatenate([c, jnp.zeros((SUBLANES - bsz, d), c.dtype)], axis=0)
    w_in_t = jnp.swapaxes(w_in, 1, 2)
    xcur = x.reshape(t, d)
    for l in range(depth):
        mod = _ada(c_pad, w_ada[l], b_ada[l][None, :])
        mod = mod[:bsz].reshape(bsz, N_MOD, d)
        mod = jnp.concatenate([mod, jnp.zeros((bsz, SUBLANES - N_MOD, d), F32)], axis=1)

        sb_q, sb_k, sb_v, g_q, g_k, g_v, g_r, g_lr = _inproj(xcur, mod, w_in_t, seq, l)
        as_seq = lambda a: a.reshape(bsz, seq, a.shape[-1])

        o_sb = _sb_attention(as_seq(sb_q), as_seq(sb_k), as_seq(sb_v), sb_norm_g[l][None, :])
        wg_pad = jnp.concatenate(
            [gla_w_gate[l], jnp.zeros((LANES - GLA_GATE_RANK, GLA_QK_WIDTH), F32)], axis=0).astype(BF16)
        o_gla = _gla(as_seq(g_q), as_seq(g_k), as_seq(g_v), as_seq(g_r), as_seq(g_lr),
                     wg_pad, gla_b_gate[l][None, :], gla_norm_g[l][None, :])

        xcur = _post(o_sb.reshape(t, SB_WIDTH), o_gla.reshape(t, GLA_V_WIDTH), xcur, mod,
                     w_out, ln1_g[l][None, :], ln1_b[l][None, :],
                     w_ff_gate, w_ff_up, conv_w[l], conv_b[l][None, :], w_down,
                     ln2_g[l][None, :], ln2_b[l][None, :], seq, alpha, l)
    return xcur.reshape(bsz, seq, d)
```

```python
import functools

import jax
import jax.numpy as jnp
from jax import lax
from jax.experimental import pallas as pl
from jax.experimental.pallas import tpu as pltpu

F32 = jnp.float32
BF16 = jnp.bfloat16

LANES = 128
SUBLANES = 8
VMEM_LIMIT_BYTES = 60 * 1024 * 1024

SB_HEADS = 8
SB_HEAD_DIM = 64
GLA_HEADS = 4
GLA_KEY_DIM = 64
GLA_VAL_DIM = 128
GLA_GATE_RANK = 16
GLA_GATE_TEMP = 16.0
GLA_CHUNK = 64
CONV_WIDTH = 3
N_MOD = 6
LN_EPS = 1e-5
RMS_EPS = 1e-6

SB_WIDTH = SB_HEADS * SB_HEAD_DIM
GLA_QK_WIDTH = GLA_HEADS * GLA_KEY_DIM
GLA_V_WIDTH = GLA_HEADS * GLA_VAL_DIM

SB_EXIT = 88.0
LOG2E = 1.4426950408889634
SB_EXIT_BITS = SB_EXIT * LOG2E
SB_MAX_BITS = 126.0
SB_MASKED_BITS = -1e30
SB_TK = 128
SB_TQ = 8 * SB_TK
SB_WIN_BLOCKS = 3
SB_WIN = SB_WIN_BLOCKS * SB_TK

GLA_STEP = 4096
ADA_ROWS = 256
INPROJ_TILE = 1024
ROW_TILE = 1024
POST_SUB_TILE = 256
FF_TILE = 256
STAGE_ROWS = 128
STAGE_SLOTS = 3
INPROJ_STAGE_SLOTS = 6


def _sb_logits(q, k):
    return jnp.minimum(_dot_nt(q, k), SB_MAX_BITS)


def _softplus2(z):
    return jnp.log2(1.0 + jnp.exp2(z))


def _split_bf16(a):
    hi = a.astype(BF16)
    lo = (a - hi.astype(F32)).astype(BF16)
    return hi, lo


def _dot(a, b):
    return jnp.dot(a, b, preferred_element_type=F32)


def _dot_nt(a, b):
    return lax.dot_general(a, b, (((1,), (1,)), ((), ())), preferred_element_type=F32)


def _dot_tn(a, b):
    return lax.dot_general(a, b, (((0,), (0,)), ((), ())), preferred_element_type=F32)


def _stage_weights_bf16(pairs, stage_ref, sem):
    slots, rows = stage_ref.shape[0], stage_ref.shape[1]
    jobs = [(src, dst, r0) for src, dst in pairs for r0 in range(0, src.shape[0], rows)]

    def window(n):
        return stage_ref.at[n % slots, :, pl.ds(0, jobs[n][0].shape[1])]

    def copy(n):
        src, _, r0 = jobs[n]
        return pltpu.make_async_copy(src.at[pl.ds(r0, rows), :], window(n), sem.at[n % slots])

    for n in range(min(slots - 1, len(jobs))):
        copy(n).start()
    for n, (_, dst, r0) in enumerate(jobs):
        if n + slots - 1 < len(jobs):
            copy(n + slots - 1).start()
        copy(n).wait()
        dst[pl.ds(r0, rows), :] = window(n)[...].astype(BF16)


def _layer_norm(v, g, b):
    mu = jnp.mean(v, axis=-1, keepdims=True)
    vc = v - mu
    var = jnp.mean(vc * vc, axis=-1, keepdims=True)
    return vc * lax.rsqrt(var + LN_EPS) * g + b


def _ada_kernel(c_ref, w_ref, b_ref, o_ref):
    c = c_ref[...]
    ca = c / (1.0 + jnp.exp(-c))
    part = _dot(ca.astype(BF16), w_ref[...].astype(BF16))

    @pl.when(pl.program_id(0) == 0)
    def _():
        o_ref[...] = part + b_ref[...]

    @pl.when(pl.program_id(0) > 0)
    def _():
        o_ref[...] += part


def _ada(c_pad, w, b):
    d, n = w.shape
    tk = ADA_ROWS
    return pl.pallas_call(
        _ada_kernel,
        out_shape=jax.ShapeDtypeStruct((c_pad.shape[0], n), F32),
        grid=(d // tk,),
        in_specs=[
            pl.BlockSpec((c_pad.shape[0], tk), lambda k: (0, k)),
            pl.BlockSpec((tk, n), lambda k: (k, 0)),
            pl.BlockSpec((1, n), lambda k: (0, 0)),
        ],
        out_specs=pl.BlockSpec((c_pad.shape[0], n), lambda k: (0, 0)),
        compiler_params=pltpu.CompilerParams(
            dimension_semantics=("arbitrary",), vmem_limit_bytes=VMEM_LIMIT_BYTES),
        name="ada",
    )(c_pad, w, b)


_MAIN_WIDTH = 3 * SB_WIDTH + 2 * GLA_QK_WIDTH + GLA_V_WIDTH
_PROJ_GROUPS = (
    (0, 0, SB_WIDTH, SB_HEAD_DIM ** -0.5 * LOG2E),
    (0, SB_WIDTH, SB_WIDTH, 1.0),
    (0, 2 * SB_WIDTH, SB_WIDTH, 1.0),
    (0, 3 * SB_WIDTH, GLA_QK_WIDTH, GLA_KEY_DIM ** -0.5),
    (0, 3 * SB_WIDTH + GLA_QK_WIDTH, GLA_QK_WIDTH, 1.0),
    (0, 3 * SB_WIDTH + 2 * GLA_QK_WIDTH, GLA_V_WIDTH, 1.0),
    (1, 0, GLA_V_WIDTH, 1.0),
    (2, 0, LANES, 1.0),
)


def _stage_w_in(wt_hbm, w_main_ref, w_r_ref, w_lr_ref, stage_ref, sem):
    slots, rows, d = stage_ref.shape
    lr0 = _MAIN_WIDTH
    r0 = lr0 + GLA_GATE_RANK
    jobs = [(c, w_main_ref, c, rows) for c in range(0, lr0, rows)]
    jobs += [(lr0, w_lr_ref, 0, GLA_GATE_RANK)]
    jobs += [(r0 + c, w_r_ref, c, rows) for c in range(0, GLA_V_WIDTH, rows)]
    lane = lax.broadcasted_iota(jnp.int32, (d, rows), 1)

    def copy(n):
        return pltpu.make_async_copy(wt_hbm.at[pl.ds(jobs[n][0], rows), :],
                                     stage_ref.at[n % slots], sem.at[n % slots])

    for n in range(min(slots - 1, len(jobs))):
        copy(n).start()
    for n, (_, dst, col, valid) in enumerate(jobs):
        if n + slots - 1 < len(jobs):
            copy(n + slots - 1).start()
        copy(n).wait()
        blk = stage_ref[n % slots].T
        if valid < rows:
            blk = jnp.where(lane < valid, blk, 0.0)
        dst[:, pl.ds(col, rows)] = blk.astype(BF16)


def _inproj_kernel(layer, x_ref, mod_ref, w_hbm, *refs):
    n_out = len(_PROJ_GROUPS)
    out_refs = refs[:n_out]
    w_main_ref, w_r_ref, w_lr_ref, stage_ref, sem = refs[n_out:]

    @pl.when(pl.program_id(0) == 0)
    def _():
        _stage_w_in(w_hbm.at[layer], w_main_ref, w_r_ref, w_lr_ref, stage_ref, sem)

    shift = mod_ref[0:1, :]
    scale = mod_ref[1:2, :]
    h = (x_ref[...] * (1.0 + scale) + shift).astype(BF16)
    w_refs = (w_main_ref, w_r_ref, w_lr_ref)
    for (src, c0, width, mul), o_ref in zip(_PROJ_GROUPS, out_refs):
        acc = _dot(h, w_refs[src][:, c0:c0 + width])
        if mul != 1.0:
            acc = acc * mul
        o_ref[...] = acc.astype(o_ref.dtype)


def _inproj(x2d, mod, w_in_t, seq, layer):
    t, d = x2d.shape
    tm = INPROJ_TILE
    tiles_per_seq = seq // tm
    widths = [g[2] for g in _PROJ_GROUPS]
    return pl.pallas_call(
        functools.partial(_inproj_kernel, layer),
        out_shape=[jax.ShapeDtypeStruct((t, width), BF16) for width in widths],
        grid=(t // tm,),
        in_specs=[
            pl.BlockSpec((tm, d), lambda i: (i, 0)),
            pl.BlockSpec((None, SUBLANES, d), lambda i: (i // tiles_per_seq, 0, 0)),
            pl.BlockSpec(memory_space=pl.ANY),
        ],
        out_specs=[pl.BlockSpec((tm, width), lambda i: (i, 0)) for width in widths],
        scratch_shapes=[pltpu.VMEM((d, _MAIN_WIDTH), BF16), pltpu.VMEM((d, GLA_V_WIDTH), BF16),
                        pltpu.VMEM((d, LANES), BF16),
                        pltpu.VMEM((INPROJ_STAGE_SLOTS, LANES, d), F32),
                        pltpu.SemaphoreType.DMA((INPROJ_STAGE_SLOTS,))],
        compiler_params=pltpu.CompilerParams(
            dimension_semantics=("arbitrary",), vmem_limit_bytes=VMEM_LIMIT_BYTES),
        name="inproj",
    )(x2d, mod, w_in_t)


def _sb_suffix_sums(sp_blocks, umat):
    m = sp_blocks[0].shape[0]
    res = _dot(jnp.concatenate(sp_blocks, axis=0).astype(BF16), umat)
    cs, nearer = [], None
    for c in reversed(range(len(sp_blocks))):
        part = res[c * m:(c + 1) * m]
        loc, tot = part[:, :SB_TK], part[:, SB_TK:]
        cs.append(loc if nearer is None else loc + nearer)
        nearer = tot if nearer is None else nearer + tot
    return cs[::-1], nearer


def _sb_kernel(q_ref, k_ref, v_ref, g_ref, o_ref):
    tk = SB_TK
    pairs = q_ref.shape[1] // LANES
    nrb = SB_TQ // tk
    nb = SB_WIN_BLOCKS
    lane = lax.broadcasted_iota(jnp.int32, (tk, LANES), 1)
    head0 = lane < SB_HEAD_DIM

    uj = lax.broadcasted_iota(jnp.int32, (tk, 2 * tk), 0)
    us = lax.broadcasted_iota(jnp.int32, (tk, 2 * tk), 1)
    umat = jnp.where((uj >= us) | (us >= tk), 1.0, 0.0).astype(BF16)
    row = lax.broadcasted_iota(jnp.int32, (2 * tk, tk), 0) & (tk - 1)
    col = lax.broadcasted_iota(jnp.int32, (2 * tk, tk), 1)
    causal = col < row
    lanes = [slice(p * LANES, (p + 1) * LANES) for p in range(pairs)]

    first_blk = [pl.program_id(1) * nrb + r - (nb - 1) for r in range(nrb)]
    blk_rows = [[pl.ds(pl.multiple_of(jnp.maximum(first_blk[r] + c, 0) * tk, tk), tk)
                 for c in range(nb)] for r in range(nrb)]
    units = [(r, p) for r in range(nrb) for p in range(pairs)]

    qs, zs, sums, ws, accs = {}, {}, {}, {}, {}

    def stage_logits(u):
        r, p = u
        q = q_ref[r * tk:(r + 1) * tk, lanes[p]]
        zq = jnp.zeros_like(q)
        qs[u] = jnp.concatenate([jnp.where(head0, q, zq), jnp.where(head0, zq, q)], axis=0)
        kw = jnp.concatenate([k_ref[rows, lanes[p]] for rows in blk_rows[r]], axis=0)
        z = _sb_logits(qs[u], kw)
        zs[u] = [z[:, c * tk:(c + 1) * tk] for c in range(nb)]
        zs[u][-1] = jnp.where(causal, zs[u][-1], SB_MASKED_BITS)

    def stage_sums(u):
        sums[u] = _sb_suffix_sums([_softplus2(z) for z in zs[u]], umat)

    def stage_weights(u):
        args = [z - cs for z, cs in zip(zs[u], sums[u][0])]
        ws[u] = jnp.exp2(jnp.concatenate(args, axis=1)).astype(BF16)

    def stage_values(u):
        r, p = u
        vw = [v_ref[rows, lanes[p]] for rows in blk_rows[r]]
        vw = [jnp.where(first_blk[r] + c >= 0, vw[c], jnp.zeros_like(vw[c]))
              for c in range(nb - 1)] + vw[-1:]
        accs[u] = _dot(ws[u], jnp.concatenate(vw, axis=0))

    stages = (stage_logits, stage_sums, stage_weights, stage_values)
    order = sorted((stage + n // 2, n, stage)
                   for n in range(len(units)) for stage in range(len(stages)))
    for _, n, stage in order:
        stages[stage](units[n])

    def not_done(cs_):
        m = cs_[0]
        for c in cs_[1:]:
            m = jnp.minimum(m, c)
        return (jnp.min(m) < SB_EXIT_BITS).astype(jnp.int32)

    def cond(s):
        return jnp.logical_and(s[0] >= 0, s[1] > 0)

    def write_out(r, accs_):
        outs = []
        for p in range(pairs):
            o = jnp.where(head0, accs_[p][:tk], accs_[p][tk:])
            sq = o * o
            ss0 = jnp.sum(jnp.where(head0, sq, 0.0), axis=-1, keepdims=True)
            ss1 = jnp.sum(jnp.where(head0, 0.0, sq), axis=-1, keepdims=True)
            ms = jnp.where(head0, ss0, ss1) * (1.0 / SB_HEAD_DIM)
            outs.append(o * lax.rsqrt(ms + RMS_EPS))
        o_ref[r * tk:(r + 1) * tk, :] = (jnp.concatenate(outs, axis=1) * g_ref[...]).astype(o_ref.dtype)

    for r in range(nrb):
        write_out(r, [accs[r, p] for p in range(pairs)])
    carries = [tuple(sums[r, p][1] for p in range(pairs)) for r in range(nrb)]
    more = [jnp.logical_and(first_blk[r] - 1 >= 0, not_done(carries[r]) > 0) for r in range(nrb)]
    any_more = functools.reduce(jnp.logical_or, more)

    @pl.when(any_more)
    def _():
        for r in range(nrb):
            def body(s, r=r):
                j, _, cs_, as_ = s
                rows = pl.ds(pl.multiple_of(j * tk, tk), tk)
                zb = [_sb_logits(qs[r, p], k_ref[rows, lanes[p]]) for p in range(pairs)]
                sb = [_sb_suffix_sums([_softplus2(zb[p])], umat) for p in range(pairs)]
                wb = [jnp.exp2(zb[p] - sb[p][0][0] - cs_[p]).astype(BF16) for p in range(pairs)]
                new_a = [as_[p] + _dot(wb[p], v_ref[rows, lanes[p]]) for p in range(pairs)]
                new_c = [cs_[p] + sb[p][1] for p in range(pairs)]
                return j - 1, not_done(new_c), tuple(new_c), tuple(new_a)

            @pl.when(more[r])
            def _(r=r, body=body):
                _, _, _, accs_ = lax.while_loop(
                    cond, body,
                    (first_blk[r] - 1, jnp.int32(1), carries[r],
                     tuple(accs[r, p] for p in range(pairs))))
                write_out(r, accs_)


def _sb_attention(q, k, v, g):
    b, s, wdt = q.shape
    return pl.pallas_call(
        _sb_kernel,
        out_shape=jax.ShapeDtypeStruct((b, s, wdt), BF16),
        grid=(b, s // SB_TQ),
        in_specs=[
            pl.BlockSpec((None, SB_TQ, wdt), lambda bi, i: (bi, i, 0)),
            pl.BlockSpec((None, s, wdt), lambda bi, i: (bi, 0, 0)),
            pl.BlockSpec((None, s, wdt), lambda bi, i: (bi, 0, 0)),
            pl.BlockSpec((1, wdt), lambda bi, i: (0, 0)),
        ],
        out_specs=pl.BlockSpec((None, SB_TQ, wdt), lambda bi, i: (bi, i, 0)),
        compiler_params=pltpu.CompilerParams(
            dimension_semantics=("arbitrary", "arbitrary"),
            vmem_limit_bytes=VMEM_LIMIT_BYTES),
        name="sb_attention",
    )(q, k, v, g)


def _gla_kernel(q_ref, k_ref, v_ref, r_ref, lr_ref, wg_ref, bg_ref, g_ref, o_ref, st_ref):
    c = GLA_CHUNK

    @pl.when(pl.program_id(2) == 0)
    def _():
        st_ref[...] = jnp.zeros_like(st_ref)

    u = _dot(lr_ref[...], wg_ref[...]) + bg_ref[...]
    log_a = (jnp.minimum(u, 0.0) - jnp.log(1.0 + jnp.exp(-jnp.abs(u)))) * (1.0 / GLA_GATE_TEMP)

    ti = lax.broadcasted_iota(jnp.int32, (c, c), 0)
    tj = lax.broadcasted_iota(jnp.int32, (c, c), 1)
    lower = ti >= tj
    tril = jnp.where(lower, 1.0, 0.0).astype(BF16)
    lane = lax.broadcasted_iota(jnp.int32, (c, LANES), 1)
    head0 = lane < GLA_KEY_DIM

    chunks = [slice(n * c, (n + 1) * c) for n in range(q_ref.shape[0] // c)]
    la_hi, la_lo = _split_bf16(log_a)
    la_parts = jnp.concatenate([la_hi, la_lo], axis=1)
    bcum, dec = [], []
    for rows in chunks:
        res = _dot(tril, la_parts[rows])
        bcum.append(res[:, :LANES] + res[:, LANES:])
        dec.append(jnp.exp(bcum[-1][c - 1:c, :]))
    q2, k_inv, k_end2 = [], [], []
    for n, rows in enumerate(chunks):
        kf = k_ref[rows, :].astype(F32) * jnp.exp(-bcum[n])
        q_dec = (q_ref[rows, :].astype(F32) * jnp.exp(bcum[n])).astype(BF16)
        k_end = (kf * dec[n]).astype(BF16)
        zq = jnp.zeros_like(q_dec)
        q2.append(jnp.concatenate([jnp.where(head0, q_dec, zq), jnp.where(head0, zq, q_dec)], axis=0))
        k_end2.append(jnp.concatenate([jnp.where(head0, k_end, zq), jnp.where(head0, zq, k_end)], axis=0))
        k_inv.append(kf.astype(BF16))
    lower2 = jnp.concatenate([lower, lower], axis=0)
    o_intra = []
    for n, rows in enumerate(chunks):
        a = jnp.where(lower2, _dot_nt(q2[n], k_inv[n]), 0.0).astype(BF16)
        o_intra.append((_dot(a[:c], v_ref[rows, :GLA_VAL_DIM]), _dot(a[c:], v_ref[rows, GLA_VAL_DIM:])))
    kv = [_dot_tn(jnp.concatenate([v_ref[rows, :GLA_VAL_DIM], v_ref[rows, GLA_VAL_DIM:]], axis=0), k_end2[n])
          for n, rows in enumerate(chunks)]
    st = st_ref[...]
    starts = []
    for n in range(len(chunks)):
        starts.append(st.astype(BF16))
        st = st * dec[n] + kv[n]
    st_ref[...] = st
    for n, rows in enumerate(chunks):
        o_inter = _dot_nt(q2[n], starts[n])
        o0 = o_intra[n][0] + o_inter[:c]
        o1 = o_intra[n][1] + o_inter[c:]
        n0 = o0 * lax.rsqrt(jnp.mean(o0 * o0, axis=-1, keepdims=True) + RMS_EPS)
        n1 = o1 * lax.rsqrt(jnp.mean(o1 * o1, axis=-1, keepdims=True) + RMS_EPS)
        rf = r_ref[rows, :].astype(F32)
        gate = rf / (1.0 + jnp.exp(-rf))
        o_ref[rows, :] = (jnp.concatenate([n0, n1], axis=1) * g_ref[...] * gate).astype(o_ref.dtype)


def _gla(q, k, v, r, lr, wg, bg, g):
    b, s, _ = q.shape
    pairs = GLA_HEADS // 2
    step = min(GLA_STEP, s)
    dv2 = 2 * GLA_VAL_DIM
    seq_map = lambda bi, p, t: (bi, t, p)
    return pl.pallas_call(
        _gla_kernel,
        out_shape=jax.ShapeDtypeStruct((b, s, GLA_V_WIDTH), BF16),
        grid=(b, pairs, s // step),
        in_specs=[
            pl.BlockSpec((None, step, LANES), seq_map),
            pl.BlockSpec((None, step, LANES), seq_map),
            pl.BlockSpec((None, step, dv2), seq_map),
            pl.BlockSpec((None, step, dv2), seq_map),
            pl.BlockSpec((None, step, LANES), lambda bi, p, t: (bi, t, 0)),
            pl.BlockSpec((LANES, LANES), lambda bi, p, t: (0, p)),
            pl.BlockSpec((1, LANES), lambda bi, p, t: (0, p)),
            pl.BlockSpec((1, dv2), lambda bi, p, t: (0, p)),
        ],
        out_specs=pl.BlockSpec((None, step, dv2), seq_map),
        scratch_shapes=[pltpu.VMEM((GLA_VAL_DIM, LANES), F32)],
        compiler_params=pltpu.CompilerParams(
            dimension_semantics=("arbitrary", "arbitrary", "arbitrary"),
            vmem_limit_bytes=VMEM_LIMIT_BYTES),
        name="gla",
    )(q, k, v, r, lr, wg, bg, g)


def _post_kernel(alpha, tiles_per_seq, layer, osb_ref, ogla_ref, x_ref, mod_ref, wo_hbm, g1_ref, b1_ref,
                 wg_hbm, wu_hbm, cw_ref, cb_ref, wd_hbm, g_ref, b_ref,
                 o_ref, halo_ref, act_ref, x1_ref, wo_ref, wg_ref, wu_ref, wd_ref, stage_ref, sem):
    tm = POST_SUB_TILE
    nsub = x_ref.shape[0] // tm
    d_ff = wg_ref.shape[1]
    n_sb = osb_ref.shape[1]
    first = (pl.program_id(0) % tiles_per_seq) == 0

    @pl.when(pl.program_id(0) == 0)
    def _():
        _stage_weights_bf16(((wo_hbm.at[layer], wo_ref), (wg_hbm.at[layer], wg_ref),
                             (wu_hbm.at[layer], wu_ref), (wd_hbm.at[layer], wd_ref)), stage_ref, sem)

    @pl.when(first)
    def _():
        halo_ref[...] = jnp.zeros_like(halo_ref)

    top = lax.broadcasted_iota(jnp.int32, (SUBLANES, FF_TILE), 0)
    hs = []
    for j in range(nsub):
        rows = slice(j * tm, (j + 1) * tm)
        mix = _dot(osb_ref[rows, :], wo_ref[:n_sb, :]) + _dot(ogla_ref[rows, :], wo_ref[n_sb:, :])
        x1_ref[rows, :] = _layer_norm(alpha * x_ref[rows, :] + (1.0 + mod_ref[2:3, :]) * mix,
                                      g1_ref[...], b1_ref[...])
        hs.append((x1_ref[rows, :] * (1.0 + mod_ref[4:5, :]) + mod_ref[3:4, :]).astype(BF16))
    for j in range(nsub):
        rows = slice(j * tm, (j + 1) * tm)
        h = hs[j]
        for f in range(d_ff // FF_TILE):
            cols = slice(f * FF_TILE, (f + 1) * FF_TILE)
            gt = _dot(h, wg_ref[:, cols])
            up = _dot(h, wu_ref[:, cols])
            prev = halo_ref[:, cols]
            halo_ref[:, cols] = gt[tm - SUBLANES:, :]
            r1 = pltpu.roll(gt, 1, 0)
            r2 = pltpu.roll(gt, 2, 0)
            t1 = jnp.where(top < 1, pltpu.roll(prev, 1, 0), r1[:SUBLANES])
            t2 = jnp.where(top < 2, pltpu.roll(prev, 2, 0), r2[:SUBLANES])
            g1 = jnp.concatenate([t1, r1[SUBLANES:]], axis=0)
            g2 = jnp.concatenate([t2, r2[SUBLANES:]], axis=0)
            conv = (g2 * cw_ref[0:1, cols] + g1 * cw_ref[1:2, cols] + gt * cw_ref[2:3, cols]
                    + cb_ref[:, cols])
            act_ref[rows, cols] = (conv / (1.0 + jnp.exp(-conv)) * up).astype(BF16)
        y = _dot(act_ref[rows, :], wd_ref[...])
        o_ref[rows, :] = _layer_norm(alpha * x1_ref[rows, :] + (1.0 + mod_ref[5:6, :]) * y,
                                     g_ref[...], b_ref[...])


def _post(o_sb, o_gla, x2d, mod, w_o, ln1_g, ln1_b, wg, wu, cw, cb, wd, ln_g, ln_b, seq, alpha, layer):
    t, d = x2d.shape
    d_ff = wg.shape[2]
    tm = ROW_TILE
    tiles_per_seq = seq // tm
    row_map = lambda i: (i, 0)
    fixed = lambda i: (0, 0)
    hbm = pl.BlockSpec(memory_space=pl.ANY)
    stage_width = max(w.shape[2] for w in (w_o, wg, wu, wd))
    return pl.pallas_call(
        functools.partial(_post_kernel, alpha, tiles_per_seq, layer),
        out_shape=jax.ShapeDtypeStruct((t, d), F32),
        grid=(t // tm,),
        in_specs=[
            pl.BlockSpec((tm, o_sb.shape[1]), row_map),
            pl.BlockSpec((tm, o_gla.shape[1]), row_map),
            pl.BlockSpec((tm, d), row_map),
            pl.BlockSpec((None, SUBLANES, d), lambda i: (i // tiles_per_seq, 0, 0)),
            hbm,
            pl.BlockSpec((1, d), fixed),
            pl.BlockSpec((1, d), fixed),
            hbm,
            hbm,
            pl.BlockSpec(cw.shape, fixed),
            pl.BlockSpec(cb.shape, fixed),
            hbm,
            pl.BlockSpec((1, d), fixed),
            pl.BlockSpec((1, d), fixed),
        ],
        out_specs=pl.BlockSpec((tm, d), row_map),
        scratch_shapes=[pltpu.VMEM((SUBLANES, d_ff), F32), pltpu.VMEM((tm, d_ff), BF16),
                        pltpu.VMEM((tm, d), F32),
                        pltpu.VMEM(w_o.shape[1:], BF16), pltpu.VMEM(wg.shape[1:], BF16),
                        pltpu.VMEM(wu.shape[1:], BF16), pltpu.VMEM(wd.shape[1:], BF16),
                        pltpu.VMEM((STAGE_SLOTS, STAGE_ROWS, stage_width), F32),
                        pltpu.SemaphoreType.DMA((STAGE_SLOTS,))],
        compiler_params=pltpu.CompilerParams(
            dimension_semantics=("arbitrary",), vmem_limit_bytes=VMEM_LIMIT_BYTES),
        name="post",
    )(o_sb, o_gla, x2d, mod, w_o, ln1_g, ln1_b, wg, wu, cw, cb, wd, ln_g, ln_b)


def kernel(x, c, w_ada, b_ada, w_in, gla_w_gate, gla_b_gate, sb_norm_g, gla_norm_g, w_out,
           ln1_g, ln1_b, w_ff_gate, w_ff_up, conv_w, conv_b, w_down, ln2_g, ln2_b):
    bsz, seq, d = x.shape
    depth = w_ada.shape[0]
    alpha = float((2 * depth) ** 0.25)
    t = bsz * seq

    c_pad = jnp.concatenate([c, jnp.zeros((SUBLANES - bsz, d), c.dtype)], axis=0)
    w_in_t = jnp.swapaxes(w_in, 1, 2)
    xcur = x.reshape(t, d)
    for l in range(depth):
        mod = _ada(c_pad, w_ada[l], b_ada[l][None, :])
        mod = mod[:bsz].reshape(bsz, N_MOD, d)
        mod = jnp.concatenate([mod, jnp.zeros((bsz, SUBLANES - N_MOD, d), F32)], axis=1)

        sb_q, sb_k, sb_v, g_q, g_k, g_v, g_r, g_lr = _inproj(xcur, mod, w_in_t, seq, l)
        as_seq = lambda a: a.reshape(bsz, seq, a.shape[-1])

        o_sb = _sb_attention(as_seq(sb_q), as_seq(sb_k), as_seq(sb_v), sb_norm_g[l][None, :])
        wg_pad = jnp.concatenate(
            [gla_w_gate[l], jnp.zeros((LANES - GLA_GATE_RANK, GLA_QK_WIDTH), F32)], axis=0).astype(BF16)
        o_gla = _gla(as_seq(g_q), as_seq(g_k), as_seq(g_v), as_seq(g_r), as_seq(g_lr),
                     wg_pad, gla_b_gate[l][None, :], gla_norm_g[l][None, :])

        xcur = _post(o_sb.reshape(t, SB_WIDTH), o_gla.reshape(t, GLA_V_WIDTH), xcur, mod,
                     w_out, ln1_g[l][None, :], ln1_b[l][None, :],
                     w_ff_gate, w_ff_up, conv_w[l], conv_b[l][None, :], w_down,
                     ln2_g[l][None, :], ln2_b[l][None, :], seq, alpha, l)
    return xcur.reshape(bsz, seq, d)
```

```python
import functools

import jax
import jax.numpy as jnp
from jax import lax
from jax.experimental import pallas as pl
from jax.experimental.pallas import tpu as pltpu

F32 = jnp.float32
BF16 = jnp.bfloat16

LANES = 128
SUBLANES = 8
VMEM_LIMIT_BYTES = 60 * 1024 * 1024

SB_HEADS = 8
SB_HEAD_DIM = 64
GLA_HEADS = 4
GLA_KEY_DIM = 64
GLA_VAL_DIM = 128
GLA_GATE_RANK = 16
GLA_GATE_TEMP = 16.0
GLA_CHUNK = 64
CONV_WIDTH = 3
N_MOD = 6
LN_EPS = 1e-5
RMS_EPS = 1e-6

SB_WIDTH = SB_HEADS * SB_HEAD_DIM
GLA_QK_WIDTH = GLA_HEADS * GLA_KEY_DIM
GLA_V_WIDTH = GLA_HEADS * GLA_VAL_DIM

SB_EXIT = 88.0
LOG2E = 1.4426950408889634
SB_EXIT_BITS = SB_EXIT * LOG2E
SB_MAX_BITS = 126.0
SB_MASKED_BITS = -1e30
SB_TK = 128
SB_TQ = 8 * SB_TK
SB_WIN_BLOCKS = 3
SB_WIN = SB_WIN_BLOCKS * SB_TK

GLA_STEP = 4096
ADA_ROWS = 256
INPROJ_TILE = 1024
ROW_TILE = 1024
POST_SUB_TILE = 256
FF_TILE = 256
STAGE_ROWS = 128
STAGE_SLOTS = 3
INPROJ_STAGE_SLOTS = 6


def _sb_logits(q, k):
    return jnp.minimum(_dot_nt(q, k), SB_MAX_BITS)


def _softplus2(z):
    return jnp.log2(1.0 + jnp.exp2(z))


def _split_bf16(a):
    hi = a.astype(BF16)
    lo = (a - hi.astype(F32)).astype(BF16)
    return hi, lo


def _dot(a, b):
    return jnp.dot(a, b, preferred_element_type=F32)


def _dot_nt(a, b):
    return lax.dot_general(a, b, (((1,), (1,)), ((), ())), preferred_element_type=F32)


def _dot_tn(a, b):
    return lax.dot_general(a, b, (((0,), (0,)), ((), ())), preferred_element_type=F32)


def _stage_weights_bf16(pairs, stage_ref, sem):
    slots, rows = stage_ref.shape[0], stage_ref.shape[1]
    jobs = [(src, dst, r0) for src, dst in pairs for r0 in range(0, src.shape[0], rows)]

    def window(n):
        return stage_ref.at[n % slots, :, pl.ds(0, jobs[n][0].shape[1])]

    def copy(n):
        src, _, r0 = jobs[n]
        return pltpu.make_async_copy(src.at[pl.ds(r0, rows), :], window(n), sem.at[n % slots])

    for n in range(min(slots - 1, len(jobs))):
        copy(n).start()
    for n, (_, dst, r0) in enumerate(jobs):
        if n + slots - 1 < len(jobs):
            copy(n + slots - 1).start()
        copy(n).wait()
        dst[pl.ds(r0, rows), :] = window(n)[...].astype(BF16)


def _layer_norm(v, g, b):
    mu = jnp.mean(v, axis=-1, keepdims=True)
    vc = v - mu
    var = jnp.mean(vc * vc, axis=-1, keepdims=True)
    return vc * lax.rsqrt(var + LN_EPS) * g + b


def _ada_kernel(c_ref, w_ref, b_ref, o_ref):
    c = c_ref[...]
    ca = c / (1.0 + jnp.exp(-c))
    part = _dot(ca.astype(BF16), w_ref[...].astype(BF16))

    @pl.when(pl.program_id(0) == 0)
    def _():
        o_ref[...] = part + b_ref[...]

    @pl.when(pl.program_id(0) > 0)
    def _():
        o_ref[...] += part


def _ada(c_pad, w, b):
    d, n = w.shape
    tk = ADA_ROWS
    return pl.pallas_call(
        _ada_kernel,
        out_shape=jax.ShapeDtypeStruct((c_pad.shape[0], n), F32),
        grid=(d // tk,),
        in_specs=[
            pl.BlockSpec((c_pad.shape[0], tk), lambda k: (0, k)),
            pl.BlockSpec((tk, n), lambda k: (k, 0)),
            pl.BlockSpec((1, n), lambda k: (0, 0)),
        ],
        out_specs=pl.BlockSpec((c_pad.shape[0], n), lambda k: (0, 0)),
        compiler_params=pltpu.CompilerParams(
            dimension_semantics=("arbitrary",), vmem_limit_bytes=VMEM_LIMIT_BYTES),
        name="ada",
    )(c_pad, w, b)


_MAIN_WIDTH = 3 * SB_WIDTH + 2 * GLA_QK_WIDTH + GLA_V_WIDTH
_PROJ_GROUPS = (
    (0, 0, SB_WIDTH, SB_HEAD_DIM ** -0.5 * LOG2E),
    (0, SB_WIDTH, SB_WIDTH, 1.0),
    (0, 2 * SB_WIDTH, SB_WIDTH, 1.0),
    (0, 3 * SB_WIDTH, GLA_QK_WIDTH, GLA_KEY_DIM ** -0.5),
    (0, 3 * SB_WIDTH + GLA_QK_WIDTH, GLA_QK_WIDTH, 1.0),
    (0, 3 * SB_WIDTH + 2 * GLA_QK_WIDTH, GLA_V_WIDTH, 1.0),
    (1, 0, GLA_V_WIDTH, 1.0),
    (2, 0, LANES, 1.0),
)


def _stage_w_in(wt_hbm, w_main_ref, w_r_ref, w_lr_ref, stage_ref, sem):
    slots, rows, d = stage_ref.shape
    lr0 = _MAIN_WIDTH
    r0 = lr0 + GLA_GATE_RANK
    jobs = [(c, w_main_ref, c, rows) for c in range(0, lr0, rows)]
    jobs += [(lr0, w_lr_ref, 0, GLA_GATE_RANK)]
    jobs += [(r0 + c, w_r_ref, c, rows) for c in range(0, GLA_V_WIDTH, rows)]
    lane = lax.broadcasted_iota(jnp.int32, (d, rows), 1)

    def copy(n):
        return pltpu.make_async_copy(wt_hbm.at[pl.ds(jobs[n][0], rows), :],
                                     stage_ref.at[n % slots], sem.at[n % slots])

    for n in range(min(slots - 1, len(jobs))):
        copy(n).start()
    for n, (_, dst, col, valid) in enumerate(jobs):
        if n + slots - 1 < len(jobs):
            copy(n + slots - 1).start()
        copy(n).wait()
        blk = stage_ref[n % slots].T
        if valid < rows:
            blk = jnp.where(lane < valid, blk, 0.0)
        dst[:, pl.ds(col, rows)] = blk.astype(BF16)


def _inproj_kernel(layer, x_ref, mod_ref, w_hbm, *refs):
    n_out = len(_PROJ_GROUPS)
    out_refs = refs[:n_out]
    w_main_ref, w_r_ref, w_lr_ref, stage_ref, sem = refs[n_out:]

    @pl.when(pl.program_id(0) == 0)
    def _():
        _stage_w_in(w_hbm.at[layer], w_main_ref, w_r_ref, w_lr_ref, stage_ref, sem)

    shift = mod_ref[0:1, :]
    scale = mod_ref[1:2, :]
    h = (x_ref[...] * (1.0 + scale) + shift).astype(BF16)
    w_refs = (w_main_ref, w_r_ref, w_lr_ref)
    for (src, c0, width, mul), o_ref in zip(_PROJ_GROUPS, out_refs):
        acc = _dot(h, w_refs[src][:, c0:c0 + width])
        if mul != 1.0:
            acc = acc * mul
        o_ref[...] = acc.astype(o_ref.dtype)


def _inproj(x2d, mod, w_in_t, seq, layer):
    t, d = x2d.shape
    tm = INPROJ_TILE
    tiles_per_seq = seq // tm
    widths = [g[2] for g in _PROJ_GROUPS]
    return pl.pallas_call(
        functools.partial(_inproj_kernel, layer),
        out_shape=[jax.ShapeDtypeStruct((t, width), BF16) for width in widths],
        grid=(t // tm,),
        in_specs=[
            pl.BlockSpec((tm, d), lambda i: (i, 0)),
            pl.BlockSpec((None, SUBLANES, d), lambda i: (i // tiles_per_seq, 0, 0)),
            pl.BlockSpec(memory_space=pl.ANY),
        ],
        out_specs=[pl.BlockSpec((tm, width), lambda i: (i, 0)) for width in widths],
        scratch_shapes=[pltpu.VMEM((d, _MAIN_WIDTH), BF16), pltpu.VMEM((d, GLA_V_WIDTH), BF16),
                        pltpu.VMEM((d, LANES), BF16),
                        pltpu.VMEM((INPROJ_STAGE_SLOTS, LANES, d), F32),
                        pltpu.SemaphoreType.DMA((INPROJ_STAGE_SLOTS,))],
        compiler_params=pltpu.CompilerParams(
            dimension_semantics=("arbitrary",), vmem_limit_bytes=VMEM_LIMIT_BYTES),
        name="inproj",
    )(x2d, mod, w_in_t)


def _sb_suffix_sums(sp_blocks, umat):
    m = sp_blocks[0].shape[0]
    res = _dot(jnp.concatenate(sp_blocks, axis=0).astype(BF16), umat)
    cs, nearer = [], None
    for c in reversed(range(len(sp_blocks))):
        part = res[c * m:(c + 1) * m]
        loc, tot = part[:, :SB_TK], part[:, SB_TK:]
        cs.append(loc if nearer is None else loc + nearer)
        nearer = tot if nearer is None else nearer + tot
    return cs[::-1], nearer


def _sb_kernel(q_ref, k_ref, v_ref, g_ref, o_ref):
    tk = SB_TK
    pairs = q_ref.shape[1] // LANES
    nrb = SB_TQ // tk
    nb = SB_WIN_BLOCKS
    lane = lax.broadcasted_iota(jnp.int32, (tk, LANES), 1)
    head0 = lane < SB_HEAD_DIM

    uj = lax.broadcasted_iota(jnp.int32, (tk, 2 * tk), 0)
    us = lax.broadcasted_iota(jnp.int32, (tk, 2 * tk), 1)
    umat = jnp.where((uj >= us) | (us >= tk), 1.0, 0.0).astype(BF16)
    row = lax.broadcasted_iota(jnp.int32, (2 * tk, tk), 0) & (tk - 1)
    col = lax.broadcasted_iota(jnp.int32, (2 * tk, tk), 1)
    causal = col < row
    lanes = [slice(p * LANES, (p + 1) * LANES) for p in range(pairs)]

    first_blk = [pl.program_id(1) * nrb + r - (nb - 1) for r in range(nrb)]
    blk_rows = [[pl.ds(pl.multiple_of(jnp.maximum(first_blk[r] + c, 0) * tk, tk), tk)
                 for c in range(nb)] for r in range(nrb)]
    units = [(r, p) for r in range(nrb) for p in range(pairs)]

    qs, zs, sums, ws, accs = {}, {}, {}, {}, {}

    def stage_logits(u):
        r, p = u
        q = q_ref[r * tk:(r + 1) * tk, lanes[p]]
        zq = jnp.zeros_like(q)
        qs[u] = jnp.concatenate([jnp.where(head0, q, zq), jnp.where(head0, zq, q)], axis=0)
        kw = jnp.concatenate([k_ref[rows, lanes[p]] for rows in blk_rows[r]], axis=0)
        z = _sb_logits(qs[u], kw)
        zs[u] = [z[:, c * tk:(c + 1) * tk] for c in range(nb)]
        zs[u][-1] = jnp.where(causal, zs[u][-1], SB_MASKED_BITS)

    def stage_sums(u):
        sums[u] = _sb_suffix_sums([_softplus2(z) for z in zs[u]], umat)

    def stage_weights(u):
        args = [z - cs for z, cs in zip(zs[u], sums[u][0])]
        ws[u] = jnp.exp2(jnp.concatenate(args, axis=1)).astype(BF16)

    def stage_values(u):
        r, p = u
        vw = [v_ref[rows, lanes[p]] for rows in blk_rows[r]]
        vw = [jnp.where(first_blk[r] + c >= 0, vw[c], jnp.zeros_like(vw[c]))
              for c in range(nb - 1)] + vw[-1:]
        accs[u] = _dot(ws[u], jnp.concatenate(vw, axis=0))

    stages = (stage_logits, stage_sums, stage_weights, stage_values)
    order = sorted((stage + n // 2, n, stage)
                   for n in range(len(units)) for stage in range(len(stages)))
    for _, n, stage in order:
        stages[stage](units[n])

    def not_done(cs_):
        m = cs_[0]
        for c in cs_[1:]:
            m = jnp.minimum(m, c)
        return (jnp.min(m) < SB_EXIT_BITS).astype(jnp.int32)

    def cond(s):
        return jnp.logical_and(s[0] >= 0, s[1] > 0)

    def write_out(r, accs_):
        outs = []
        for p in range(pairs):
            o = jnp.where(head0, accs_[p][:tk], accs_[p][tk:])
            sq = o * o
            ss0 = jnp.sum(jnp.where(head0, sq, 0.0), axis=-1, keepdims=True)
            ss1 = jnp.sum(jnp.where(head0, 0.0, sq), axis=-1, keepdims=True)
            ms = jnp.where(head0, ss0, ss1) * (1.0 / SB_HEAD_DIM)
            outs.append(o * lax.rsqrt(ms + RMS_EPS))
        o_ref[r * tk:(r + 1) * tk, :] = (jnp.concatenate(outs, axis=1) * g_ref[...]).astype(o_ref.dtype)

    for r in range(nrb):
        write_out(r, [accs[r, p] for p in range(pairs)])
    carries = [tuple(sums[r, p][1] for p in range(pairs)) for r in range(nrb)]
    more = [jnp.logical_and(first_blk[r] - 1 >= 0, not_done(carries[r]) > 0) for r in range(nrb)]
    any_more = functools.reduce(jnp.logical_or, more)

    @pl.when(any_more)
    def _():
        for r in range(nrb):
            def body(s, r=r):
                j, _, cs_, as_ = s
                rows = pl.ds(pl.multiple_of(j * tk, tk), tk)
                zb = [_sb_logits(qs[r, p], k_ref[rows, lanes[p]]) for p in range(pairs)]
                sb = [_sb_suffix_sums([_softplus2(zb[p])], umat) for p in range(pairs)]
                wb = [jnp.exp2(zb[p] - sb[p][0][0] - cs_[p]).astype(BF16) for p in range(pairs)]
                new_a = [as_[p] + _dot(wb[p], v_ref[rows, lanes[p]]) for p in range(pairs)]
                new_c = [cs_[p] + sb[p][1] for p in range(pairs)]
                return j - 1, not_done(new_c), tuple(new_c), tuple(new_a)

            @pl.when(more[r])
            def _(r=r, body=body):
                _, _, _, accs_ = lax.while_loop(
                    cond, body,
                    (first_blk[r] - 1, jnp.int32(1), carries[r],
                     tuple(accs[r, p] for p in range(pairs))))
                write_out(r, accs_)


def _sb_attention(q, k, v, g):
    b, s, wdt = q.shape
    return pl.pallas_call(
        _sb_kernel,
        out_shape=jax.ShapeDtypeStruct((b, s, wdt), BF16),
        grid=(b, s // SB_TQ),
        in_specs=[
            pl.BlockSpec((None, SB_TQ, wdt), lambda bi, i: (bi, i, 0)),
            pl.BlockSpec((None, s, wdt), lambda bi, i: (bi, 0, 0)),
            pl.BlockSpec((None, s, wdt), lambda bi, i: (bi, 0, 0)),
            pl.BlockSpec((1, wdt), lambda bi, i: (0, 0)),
        ],
        out_specs=pl.BlockSpec((None, SB_TQ, wdt), lambda bi, i: (bi, i, 0)),
        compiler_params=pltpu.CompilerParams(
            dimension_semantics=("arbitrary", "arbitrary"),
            vmem_limit_bytes=VMEM_LIMIT_BYTES),
        name="sb_attention",
    )(q, k, v, g)


def _gla_kernel(q_ref, k_ref, v_ref, r_ref, lr_ref, wg_ref, bg_ref, g_ref, o_ref, st_ref):
    c = GLA_CHUNK

    @pl.when(pl.program_id(2) == 0)
    def _():
        st_ref[...] = jnp.zeros_like(st_ref)

    u = _dot(lr_ref[...], wg_ref[...]) + bg_ref[...]
    log_a = (jnp.minimum(u, 0.0) - jnp.log(1.0 + jnp.exp(-jnp.abs(u)))) * (1.0 / GLA_GATE_TEMP)

    ti = lax.broadcasted_iota(jnp.int32, (c, c), 0)
    tj = lax.broadcasted_iota(jnp.int32, (c, c), 1)
    lower = ti >= tj
    tril = jnp.where(lower, 1.0, 0.0).astype(BF16)
    lane = lax.broadcasted_iota(jnp.int32, (c, LANES), 1)
    head0 = lane < GLA_KEY_DIM

    chunks = [slice(n * c, (n + 1) * c) for n in range(q_ref.shape[0] // c)]
    la_hi, la_lo = _split_bf16(log_a)
    la_parts = jnp.concatenate([la_hi, la_lo], axis=1)
    bcum, dec = [], []
    for rows in chunks:
        res = _dot(tril, la_parts[rows])
        bcum.append(res[:, :LANES] + res[:, LANES:])
        dec.append(jnp.exp(bcum[-1][c - 1:c, :]))
    q2, k_inv, k_end2 = [], [], []
    for n, rows in enumerate(chunks):
        kf = k_ref[rows, :].astype(F32) * jnp.exp(-bcum[n])
        q_dec = (q_ref[rows, :].astype(F32) * jnp.exp(bcum[n])).astype(BF16)
        k_end = (kf * dec[n]).astype(BF16)
        zq = jnp.zeros_like(q_dec)
        q2.append(jnp.concatenate([jnp.where(head0, q_dec, zq), jnp.where(head0, zq, q_dec)], axis=0))
        k_end2.append(jnp.concatenate([jnp.where(head0, k_end, zq), jnp.where(head0, zq, k_end)], axis=0))
        k_inv.append(kf.astype(BF16))
    lower2 = jnp.concatenate([lower, lower], axis=0)
    o_intra = []
    for n, rows in enumerate(chunks):
        a = jnp.where(lower2, _dot_nt(q2[n], k_inv[n]), 0.0).astype(BF16)
        o_intra.append((_dot(a[:c], v_ref[rows, :GLA_VAL_DIM]), _dot(a[c:], v_ref[rows, GLA_VAL_DIM:])))
    kv = [_dot_tn(jnp.concatenate([v_ref[rows, :GLA_VAL_DIM], v_ref[rows, GLA_VAL_DIM:]], axis=0), k_end2[n])
          for n, rows in enumerate(chunks)]
    st = st_ref[...]
    starts = []
    for n in range(len(chunks)):
        starts.append(st.astype(BF16))
        st = st * dec[n] + kv[n]
    st_ref[...] = st
    for n, rows in enumerate(chunks):
        o_inter = _dot_nt(q2[n], starts[n])
        o0 = o_intra[n][0] + o_inter[:c]
        o1 = o_intra[n][1] + o_inter[c:]
        n0 = o0 * lax.rsqrt(jnp.mean(o0 * o0, axis=-1, keepdims=True) + RMS_EPS)
        n1 = o1 * lax.rsqrt(jnp.mean(o1 * o1, axis=-1, keepdims=True) + RMS_EPS)
        rf = r_ref[rows, :].astype(F32)
        gate = rf / (1.0 + jnp.exp(-rf))
        o_ref[rows, :] = (jnp.concatenate([n0, n1], axis=1) * g_ref[...] * gate).astype(o_ref.dtype)


def _gla(q, k, v, r, lr, wg, bg, g):
    b, s, _ = q.shape
    pairs = GLA_HEADS // 2
    step = min(GLA_STEP, s)
    dv2 = 2 * GLA_VAL_DIM
    seq_map = lambda bi, p, t: (bi, t, p)
    return pl.pallas_call(
        _gla_kernel,
        out_shape=jax.ShapeDtypeStruct((b, s, GLA_V_WIDTH), BF16),
        grid=(b, pairs, s // step),
        in_specs=[
            pl.BlockSpec((None, step, LANES), seq_map),
            pl.BlockSpec((None, step, LANES), seq_map),
            pl.BlockSpec((None, step, dv2), seq_map),
            pl.BlockSpec((None, step, dv2), seq_map),
            pl.BlockSpec((None, step, LANES), lambda bi, p, t: (bi, t, 0)),
            pl.BlockSpec((LANES, LANES), lambda bi, p, t: (0, p)),
            pl.BlockSpec((1, LANES), lambda bi, p, t: (0, p)),
            pl.BlockSpec((1, dv2), lambda bi, p, t: (0, p)),
        ],
        out_specs=pl.BlockSpec((None, step, dv2), seq_map),
        scratch_shapes=[pltpu.VMEM((GLA_VAL_DIM, LANES), F32)],
        compiler_params=pltpu.CompilerParams(
            dimension_semantics=("arbitrary", "arbitrary", "arbitrary"),
            vmem_limit_bytes=VMEM_LIMIT_BYTES),
        name="gla",
    )(q, k, v, r, lr, wg, bg, g)


def _post_kernel(alpha, tiles_per_seq, layer, osb_ref, ogla_ref, x_ref, mod_ref, wo_hbm, g1_ref, b1_ref,
                 wg_hbm, wu_hbm, cw_ref, cb_ref, wd_hbm, g_ref, b_ref,
                 o_ref, halo_ref, act_ref, x1_ref, wo_ref, wg_ref, wu_ref, wd_ref, stage_ref, sem):
    tm = POST_SUB_TILE
    nsub = x_ref.shape[0] // tm
    d_ff = wg_ref.shape[1]
    n_sb = osb_ref.shape[1]
    first = (pl.program_id(0) % tiles_per_seq) == 0

    @pl.when(pl.program_id(0) == 0)
    def _():
        _stage_weights_bf16(((wo_hbm.at[layer], wo_ref), (wg_hbm.at[layer], wg_ref),
                             (wu_hbm.at[layer], wu_ref), (wd_hbm.at[layer], wd_ref)), stage_ref, sem)

    @pl.when(first)
    def _():
        halo_ref[...] = jnp.zeros_like(halo_ref)

    top = lax.broadcasted_iota(jnp.int32, (SUBLANES, FF_TILE), 0)
    hs = []
    for j in range(nsub):
        rows = slice(j * tm, (j + 1) * tm)
        mix = _dot(osb_ref[rows, :], wo_ref[:n_sb, :]) + _dot(ogla_ref[rows, :], wo_ref[n_sb:, :])
        x1_ref[rows, :] = _layer_norm(alpha * x_ref[rows, :] + (1.0 + mod_ref[2:3, :]) * mix,
                                      g1_ref[...], b1_ref[...])
        hs.append((x1_ref[rows, :] * (1.0 + mod_ref[4:5, :]) + mod_ref[3:4, :]).astype(BF16))
    for j in range(nsub):
        rows = slice(j * tm, (j + 1) * tm)
        h = hs[j]
        for f in range(d_ff // FF_TILE):
            cols = slice(f * FF_TILE, (f + 1) * FF_TILE)
            gt = _dot(h, wg_ref[:, cols])
            up = _dot(h, wu_ref[:, cols])
            prev = halo_ref[:, cols]
            halo_ref[:, cols] = gt[tm - SUBLANES:, :]
            r1 = pltpu.roll(gt, 1, 0)
            r2 = pltpu.roll(gt, 2, 0)
            t1 = jnp.where(top < 1, pltpu.roll(prev, 1, 0), r1[:SUBLANES])
            t2 = jnp.where(top < 2, pltpu.roll(prev, 2, 0), r2[:SUBLANES])
            g1 = jnp.concatenate([t1, r1[SUBLANES:]], axis=0)
            g2 = jnp.concatenate([t2, r2[SUBLANES:]], axis=0)
            conv = (g2 * cw_ref[0:1, cols] + g1 * cw_ref[1:2, cols] + gt * cw_ref[2:3, cols]
                    + cb_ref[:, cols])
            act_ref[rows, cols] = (conv / (1.0 + jnp.exp(-conv)) * up).astype(BF16)
        y = _dot(act_ref[rows, :], wd_ref[...])
        o_ref[rows, :] = _layer_norm(alpha * x1_ref[rows, :] + (1.0 + mod_ref[5:6, :]) * y,
                                     g_ref[...], b_ref[...])


def _post(o_sb, o_gla, x2d, mod, w_o, ln1_g, ln1_b, wg, wu, cw, cb, wd, ln_g, ln_b, seq, alpha, layer):
    t, d = x2d.shape
    d_ff = wg.shape[2]
    assert cw.shape[0] == CONV_WIDTH
    tm = ROW_TILE
    tiles_per_seq = seq // tm
    row_map = lambda i: (i, 0)
    fixed = lambda i: (0, 0)
    hbm = pl.BlockSpec(memory_space=pl.ANY)
    stage_width = max(w.shape[2] for w in (w_o, wg, wu, wd))
    return pl.pallas_call(
        functools.partial(_post_kernel, alpha, tiles_per_seq, layer),
        out_shape=jax.ShapeDtypeStruct((t, d), F32),
        grid=(t // tm,),
        in_specs=[
            pl.BlockSpec((tm, o_sb.shape[1]), row_map),
            pl.BlockSpec((tm, o_gla.shape[1]), row_map),
            pl.BlockSpec((tm, d), row_map),
            pl.BlockSpec((None, SUBLANES, d), lambda i: (i // tiles_per_seq, 0, 0)),
            hbm,
            pl.BlockSpec((1, d), fixed),
            pl.BlockSpec((1, d), fixed),
            hbm,
            hbm,
            pl.BlockSpec(cw.shape, fixed),
            pl.BlockSpec(cb.shape, fixed),
            hbm,
            pl.BlockSpec((1, d), fixed),
            pl.BlockSpec((1, d), fixed),
        ],
        out_specs=pl.BlockSpec((tm, d), row_map),
        scratch_shapes=[pltpu.VMEM((SUBLANES, d_ff), F32), pltpu.VMEM((tm, d_ff), BF16),
                        pltpu.VMEM((tm, d), F32),
                        pltpu.VMEM(w_o.shape[1:], BF16), pltpu.VMEM(wg.shape[1:], BF16),
                        pltpu.VMEM(wu.shape[1:], BF16), pltpu.VMEM(wd.shape[1:], BF16),
                        pltpu.VMEM((STAGE_SLOTS, STAGE_ROWS, stage_width), F32),
                        pltpu.SemaphoreType.DMA((STAGE_SLOTS,))],
        compiler_params=pltpu.CompilerParams(
            dimension_semantics=("arbitrary",), vmem_limit_bytes=VMEM_LIMIT_BYTES),
        name="post",
    )(o_sb, o_gla, x2d, mod, w_o, ln1_g, ln1_b, wg, wu, cw, cb, wd, ln_g, ln_b)


def kernel(x, c, w_ada, b_ada, w_in, gla_w_gate, gla_b_gate, sb_norm_g, gla_norm_g, w_out,
           ln1_g, ln1_b, w_ff_gate, w_ff_up, conv_w, conv_b, w_down, ln2_g, ln2_b):
    bsz, seq, d = x.shape
    depth = w_ada.shape[0]
    alpha = float((2 * depth) ** 0.25)
    t = bsz * seq

    c_pad = jnp.concatenate([c, jnp.zeros((SUBLANES - bsz, d), c.dtype)], axis=0)
    w_in_t = jnp.swapaxes(w_in, 1, 2)
    xcur = x.reshape(t, d)
    for l in range(depth):
        mod = _ada(c_pad, w_ada[l], b_ada[l][None, :])
        mod = mod[:bsz].reshape(bsz, N_MOD, d)
        mod = jnp.concatenate([mod, jnp.zeros((bsz, SUBLANES - N_MOD, d), F32)], axis=1)

        sb_q, sb_k, sb_v, g_q, g_k, g_v, g_r, g_lr = _inproj(xcur, mod, w_in_t, seq, l)
        as_seq = lambda a: a.reshape(bsz, seq, a.shape[-1])

        o_sb = _sb_attention(as_seq(sb_q), as_seq(sb_k), as_seq(sb_v), sb_norm_g[l][None, :])
        wg_pad = jnp.concatenate(
            [gla_w_gate[l], jnp.zeros((LANES - GLA_GATE_RANK, GLA_QK_WIDTH), F32)], axis=0).astype(BF16)
        o_gla = _gla(as_seq(g_q), as_seq(g_k), as_seq(g_v), as_seq(g_r), as_seq(g_lr),
                     wg_pad, gla_b_gate[l][None, :], gla_norm_g[l][None, :])

        xcur = _post(o_sb.reshape(t, SB_WIDTH), o_gla.reshape(t, GLA_V_WIDTH), xcur, mod,
                     w_out, ln1_g[l][None, :], ln1_b[l][None, :],
                     w_ff_gate, w_ff_up, conv_w[l], conv_b[l][None, :], w_down,
                     ln2_g[l][None, :], ln2_b[l][None, :], seq, alpha, l)
    return xcur.reshape(bsz, seq, d)
```

```python
import functools

import jax
import jax.numpy as jnp
from jax import lax
from jax.experimental import pallas as pl
from jax.experimental.pallas import tpu as pltpu

F32 = jnp.float32
BF16 = jnp.bfloat16

LANES = 128
SUBLANES = 8
VMEM_LIMIT_BYTES = 60 * 1024 * 1024

SB_HEADS = 8
SB_HEAD_DIM = 64
GLA_HEADS = 4
GLA_KEY_DIM = 64
GLA_VAL_DIM = 128
GLA_GATE_RANK = 16
GLA_GATE_TEMP = 16.0
GLA_CHUNK = 64
CONV_WIDTH = 3
N_MOD = 6
LN_EPS = 1e-5
RMS_EPS = 1e-6

SB_WIDTH = SB_HEADS * SB_HEAD_DIM
GLA_QK_WIDTH = GLA_HEADS * GLA_KEY_DIM
GLA_V_WIDTH = GLA_HEADS * GLA_VAL_DIM

SB_EXIT = 88.0
LOG2E = 1.4426950408889634
SB_EXIT_BITS = SB_EXIT * LOG2E
SB_MAX_BITS = 126.0
SB_MASKED_BITS = -1e30
SB_TK = 128
SB_TQ = 8 * SB_TK
SB_WIN_BLOCKS = 3
SB_WIN = SB_WIN_BLOCKS * SB_TK

GLA_STEP = 4096
ADA_ROWS = 256
INPROJ_TILE = 1024
INPROJ_SUB_TILE = 256
ROW_TILE = 1024
POST_SUB_TILE = 256
FF_TILE = 256
STAGE_ROWS = 128
STAGE_SLOTS = 3
INPROJ_STAGE_SLOTS = 6


def _sb_logits(q, k):
    return jnp.minimum(_dot_nt(q, k), SB_MAX_BITS)


def _softplus2(z):
    return jnp.log2(1.0 + jnp.exp2(z))


def _split_bf16(a):
    hi = a.astype(BF16)
    lo = (a - hi.astype(F32)).astype(BF16)
    return hi, lo


def _dot(a, b):
    return jnp.dot(a, b, preferred_element_type=F32)


def _dot_nt(a, b):
    return lax.dot_general(a, b, (((1,), (1,)), ((), ())), preferred_element_type=F32)


def _dot_tn(a, b):
    return lax.dot_general(a, b, (((0,), (0,)), ((), ())), preferred_element_type=F32)


def _stage_weights_bf16(pairs, stage_ref, sem):
    slots, rows = stage_ref.shape[0], stage_ref.shape[1]
    jobs = [(src, dst, r0) for src, dst in pairs for r0 in range(0, src.shape[0], rows)]

    def window(n):
        return stage_ref.at[n % slots, :, pl.ds(0, jobs[n][0].shape[1])]

    def copy(n):
        src, _, r0 = jobs[n]
        return pltpu.make_async_copy(src.at[pl.ds(r0, rows), :], window(n), sem.at[n % slots])

    for n in range(min(slots - 1, len(jobs))):
        copy(n).start()
    for n, (_, dst, r0) in enumerate(jobs):
        if n + slots - 1 < len(jobs):
            copy(n + slots - 1).start()
        copy(n).wait()
        dst[pl.ds(r0, rows), :] = window(n)[...].astype(BF16)


def _layer_norm(v, g, b):
    mu = jnp.mean(v, axis=-1, keepdims=True)
    vc = v - mu
    var = jnp.mean(vc * vc, axis=-1, keepdims=True)
    return vc * lax.rsqrt(var + LN_EPS) * g + b


def _ada_kernel(c_ref, w_ref, b_ref, o_ref):
    c = c_ref[...]
    ca = c / (1.0 + jnp.exp(-c))
    part = _dot(ca.astype(BF16), w_ref[...].astype(BF16))

    @pl.when(pl.program_id(0) == 0)
    def _():
        o_ref[...] = part + b_ref[...]

    @pl.when(pl.program_id(0) > 0)
    def _():
        o_ref[...] += part


def _ada(c_pad, w, b):
    d, n = w.shape
    tk = ADA_ROWS
    return pl.pallas_call(
        _ada_kernel,
        out_shape=jax.ShapeDtypeStruct((c_pad.shape[0], n), F32),
        grid=(d // tk,),
        in_specs=[
            pl.BlockSpec((c_pad.shape[0], tk), lambda k: (0, k)),
            pl.BlockSpec((tk, n), lambda k: (k, 0)),
            pl.BlockSpec((1, n), lambda k: (0, 0)),
        ],
        out_specs=pl.BlockSpec((c_pad.shape[0], n), lambda k: (0, 0)),
        compiler_params=pltpu.CompilerParams(
            dimension_semantics=("arbitrary",), vmem_limit_bytes=VMEM_LIMIT_BYTES),
        name="ada",
    )(c_pad, w, b)


_MAIN_WIDTH = 3 * SB_WIDTH + 2 * GLA_QK_WIDTH + GLA_V_WIDTH
_PROJ_GROUPS = (
    (0, 0, SB_WIDTH, SB_HEAD_DIM ** -0.5 * LOG2E),
    (0, SB_WIDTH, SB_WIDTH, 1.0),
    (0, 2 * SB_WIDTH, SB_WIDTH, 1.0),
    (0, 3 * SB_WIDTH, GLA_QK_WIDTH, GLA_KEY_DIM ** -0.5),
    (0, 3 * SB_WIDTH + GLA_QK_WIDTH, GLA_QK_WIDTH, 1.0),
    (0, 3 * SB_WIDTH + 2 * GLA_QK_WIDTH, GLA_V_WIDTH, 1.0),
    (1, 0, GLA_V_WIDTH, 1.0),
    (2, 0, LANES, 1.0),
)


def _stage_w_in(wt_hbm, w_main_ref, w_r_ref, w_lr_ref, stage_ref, sem):
    slots, rows, d = stage_ref.shape
    lr0 = _MAIN_WIDTH
    r0 = lr0 + GLA_GATE_RANK
    jobs = [(c, w_main_ref, c, rows) for c in range(0, lr0, rows)]
    jobs += [(lr0, w_lr_ref, 0, GLA_GATE_RANK)]
    jobs += [(r0 + c, w_r_ref, c, rows) for c in range(0, GLA_V_WIDTH, rows)]
    lane = lax.broadcasted_iota(jnp.int32, (d, rows), 1)

    def copy(n):
        return pltpu.make_async_copy(wt_hbm.at[pl.ds(jobs[n][0], rows), :],
                                     stage_ref.at[n % slots], sem.at[n % slots])

    for n in range(min(slots - 1, len(jobs))):
        copy(n).start()
    for n, (_, dst, col, valid) in enumerate(jobs):
        if n + slots - 1 < len(jobs):
            copy(n + slots - 1).start()
        copy(n).wait()
        blk = stage_ref[n % slots].T
        if valid < rows:
            blk = jnp.where(lane < valid, blk, 0.0)
        dst[:, pl.ds(col, rows)] = blk.astype(BF16)


def _inproj_kernel(layer, x_ref, mod_ref, w_hbm, *refs):
    n_out = len(_PROJ_GROUPS)
    out_refs = refs[:n_out]
    w_main_ref, w_r_ref, w_lr_ref, stage_ref, sem = refs[n_out:]

    @pl.when(pl.program_id(0) == 0)
    def _():
        _stage_w_in(w_hbm.at[layer], w_main_ref, w_r_ref, w_lr_ref, stage_ref, sem)

    shift = mod_ref[0:1, :]
    scale = mod_ref[1:2, :]
    w_refs = (w_main_ref, w_r_ref, w_lr_ref)
    for r0 in range(0, x_ref.shape[0], INPROJ_SUB_TILE):
        rows = slice(r0, r0 + INPROJ_SUB_TILE)
        h = (x_ref[rows, :] * (1.0 + scale) + shift).astype(BF16)
        for (src, c0, width, mul), o_ref in zip(_PROJ_GROUPS, out_refs):
            acc = _dot(h, w_refs[src][:, c0:c0 + width])
            if mul != 1.0:
                acc = acc * mul
            o_ref[rows, :] = acc.astype(o_ref.dtype)


def _inproj(x2d, mod, w_in_t, seq, layer):
    t, d = x2d.shape
    tm = INPROJ_TILE
    tiles_per_seq = seq // tm
    widths = [g[2] for g in _PROJ_GROUPS]
    return pl.pallas_call(
        functools.partial(_inproj_kernel, layer),
        out_shape=[jax.ShapeDtypeStruct((t, width), BF16) for width in widths],
        grid=(t // tm,),
        in_specs=[
            pl.BlockSpec((tm, d), lambda i: (i, 0)),
            pl.BlockSpec((None, SUBLANES, d), lambda i: (i // tiles_per_seq, 0, 0)),
            pl.BlockSpec(memory_space=pl.ANY),
        ],
        out_specs=[pl.BlockSpec((tm, width), lambda i: (i, 0)) for width in widths],
        scratch_shapes=[pltpu.VMEM((d, _MAIN_WIDTH), BF16), pltpu.VMEM((d, GLA_V_WIDTH), BF16),
                        pltpu.VMEM((d, LANES), BF16),
                        pltpu.VMEM((INPROJ_STAGE_SLOTS, LANES, d), F32),
                        pltpu.SemaphoreType.DMA((INPROJ_STAGE_SLOTS,))],
        compiler_params=pltpu.CompilerParams(
            dimension_semantics=("arbitrary",), vmem_limit_bytes=VMEM_LIMIT_BYTES),
        name="inproj",
    )(x2d, mod, w_in_t)


def _sb_suffix_sums(sp_blocks, umat):
    m = sp_blocks[0].shape[0]
    res = _dot(jnp.concatenate(sp_blocks, axis=0).astype(BF16), umat)
    cs, nearer = [], None
    for c in reversed(range(len(sp_blocks))):
        part = res[c * m:(c + 1) * m]
        loc, tot = part[:, :SB_TK], part[:, SB_TK:]
        cs.append(loc if nearer is None else loc + nearer)
        nearer = tot if nearer is None else nearer + tot
    return cs[::-1], nearer


def _sb_kernel(q_ref, k_ref, v_ref, g_ref, o_ref):
    tk = SB_TK
    pairs = q_ref.shape[1] // LANES
    nrb = SB_TQ // tk
    nb = SB_WIN_BLOCKS
    lane = lax.broadcasted_iota(jnp.int32, (tk, LANES), 1)
    head0 = lane < SB_HEAD_DIM

    uj = lax.broadcasted_iota(jnp.int32, (tk, 2 * tk), 0)
    us = lax.broadcasted_iota(jnp.int32, (tk, 2 * tk), 1)
    umat = jnp.where((uj >= us) | (us >= tk), 1.0, 0.0).astype(BF16)
    row = lax.broadcasted_iota(jnp.int32, (2 * tk, tk), 0) & (tk - 1)
    col = lax.broadcasted_iota(jnp.int32, (2 * tk, tk), 1)
    causal = col < row
    lanes = [slice(p * LANES, (p + 1) * LANES) for p in range(pairs)]

    first_blk = [pl.program_id(1) * nrb + r - (nb - 1) for r in range(nrb)]
    blk_rows = [[pl.ds(pl.multiple_of(jnp.maximum(first_blk[r] + c, 0) * tk, tk), tk)
                 for c in range(nb)] for r in range(nrb)]
    units = [(r, p) for r in range(nrb) for p in range(pairs)]

    qs, zs, sums, ws, accs = {}, {}, {}, {}, {}

    def stage_logits(u):
        r, p = u
        q = q_ref[r * tk:(r + 1) * tk, lanes[p]]
        zq = jnp.zeros_like(q)
        qs[u] = jnp.concatenate([jnp.where(head0, q, zq), jnp.where(head0, zq, q)], axis=0)
        kw = jnp.concatenate([k_ref[rows, lanes[p]] for rows in blk_rows[r]], axis=0)
        z = _sb_logits(qs[u], kw)
        zs[u] = [z[:, c * tk:(c + 1) * tk] for c in range(nb)]
        zs[u][-1] = jnp.where(causal, zs[u][-1], SB_MASKED_BITS)

    def stage_sums(u):
        sums[u] = _sb_suffix_sums([_softplus2(z) for z in zs[u]], umat)

    def stage_weights(u):
        args = [z - cs for z, cs in zip(zs[u], sums[u][0])]
        ws[u] = jnp.exp2(jnp.concatenate(args, axis=1)).astype(BF16)

    def stage_values(u):
        r, p = u
        vw = [v_ref[rows, lanes[p]] for rows in blk_rows[r]]
        vw = [jnp.where(first_blk[r] + c >= 0, vw[c], jnp.zeros_like(vw[c]))
              for c in range(nb - 1)] + vw[-1:]
        accs[u] = _dot(ws[u], jnp.concatenate(vw, axis=0))

    stages = (stage_logits, stage_sums, stage_weights, stage_values)
    order = sorted((stage + n // 2, n, stage)
                   for n in range(len(units)) for stage in range(len(stages)))
    for _, n, stage in order:
        stages[stage](units[n])

    def not_done(cs_):
        m = cs_[0]
        for c in cs_[1:]:
            m = jnp.minimum(m, c)
        return (jnp.min(m) < SB_EXIT_BITS).astype(jnp.int32)

    def cond(s):
        return jnp.logical_and(s[0] >= 0, s[1] > 0)

    def write_out(r, accs_):
        outs = []
        for p in range(pairs):
            o = jnp.where(head0, accs_[p][:tk], accs_[p][tk:])
            sq = o * o
            ss0 = jnp.sum(jnp.where(head0, sq, 0.0), axis=-1, keepdims=True)
            ss1 = jnp.sum(jnp.where(head0, 0.0, sq), axis=-1, keepdims=True)
            ms = jnp.where(head0, ss0, ss1) * (1.0 / SB_HEAD_DIM)
            outs.append(o * lax.rsqrt(ms + RMS_EPS))
        o_ref[r * tk:(r + 1) * tk, :] = (jnp.concatenate(outs, axis=1) * g_ref[...]).astype(o_ref.dtype)

    for r in range(nrb):
        write_out(r, [accs[r, p] for p in range(pairs)])
    carries = [tuple(sums[r, p][1] for p in range(pairs)) for r in range(nrb)]
    more = [jnp.logical_and(first_blk[r] - 1 >= 0, not_done(carries[r]) > 0) for r in range(nrb)]
    any_more = functools.reduce(jnp.logical_or, more)

    @pl.when(any_more)
    def _():
        for r in range(nrb):
            def body(s, r=r):
                j, _, cs_, as_ = s
                rows = pl.ds(pl.multiple_of(j * tk, tk), tk)
                zb = [_sb_logits(qs[r, p], k_ref[rows, lanes[p]]) for p in range(pairs)]
                sb = [_sb_suffix_sums([_softplus2(zb[p])], umat) for p in range(pairs)]
                wb = [jnp.exp2(zb[p] - sb[p][0][0] - cs_[p]).astype(BF16) for p in range(pairs)]
                new_a = [as_[p] + _dot(wb[p], v_ref[rows, lanes[p]]) for p in range(pairs)]
                new_c = [cs_[p] + sb[p][1] for p in range(pairs)]
                return j - 1, not_done(new_c), tuple(new_c), tuple(new_a)

            @pl.when(more[r])
            def _(r=r, body=body):
                _, _, _, accs_ = lax.while_loop(
                    cond, body,
                    (first_blk[r] - 1, jnp.int32(1), carries[r],
                     tuple(accs[r, p] for p in range(pairs))))
                write_out(r, accs_)


def _sb_attention(q, k, v, g):
    b, s, wdt = q.shape
    return pl.pallas_call(
        _sb_kernel,
        out_shape=jax.ShapeDtypeStruct((b, s, wdt), BF16),
        grid=(b, s // SB_TQ),
        in_specs=[
            pl.BlockSpec((None, SB_TQ, wdt), lambda bi, i: (bi, i, 0)),
            pl.BlockSpec((None, s, wdt), lambda bi, i: (bi, 0, 0)),
            pl.BlockSpec((None, s, wdt), lambda bi, i: (bi, 0, 0)),
            pl.BlockSpec((1, wdt), lambda bi, i: (0, 0)),
        ],
        out_specs=pl.BlockSpec((None, SB_TQ, wdt), lambda bi, i: (bi, i, 0)),
        compiler_params=pltpu.CompilerParams(
            dimension_semantics=("arbitrary", "arbitrary"),
            vmem_limit_bytes=VMEM_LIMIT_BYTES),
        name="sb_attention",
    )(q, k, v, g)


def _gla_kernel(q_ref, k_ref, v_ref, r_ref, lr_ref, wg_ref, bg_ref, g_ref, o_ref, st_ref):
    c = GLA_CHUNK

    @pl.when(pl.program_id(2) == 0)
    def _():
        st_ref[...] = jnp.zeros_like(st_ref)

    u = _dot(lr_ref[...], wg_ref[...]) + bg_ref[...]
    log_a = (jnp.minimum(u, 0.0) - jnp.log(1.0 + jnp.exp(-jnp.abs(u)))) * (1.0 / GLA_GATE_TEMP)

    ti = lax.broadcasted_iota(jnp.int32, (c, c), 0)
    tj = lax.broadcasted_iota(jnp.int32, (c, c), 1)
    lower = ti >= tj
    tril = jnp.where(lower, 1.0, 0.0).astype(BF16)
    lane = lax.broadcasted_iota(jnp.int32, (c, LANES), 1)
    head0 = lane < GLA_KEY_DIM

    chunks = [slice(n * c, (n + 1) * c) for n in range(q_ref.shape[0] // c)]
    la_hi, la_lo = _split_bf16(log_a)
    la_parts = jnp.concatenate([la_hi, la_lo], axis=1)
    bcum, dec = [], []
    for rows in chunks:
        res = _dot(tril, la_parts[rows])
        bcum.append(res[:, :LANES] + res[:, LANES:])
        dec.append(jnp.exp(bcum[-1][c - 1:c, :]))
    q2, k_inv, k_end2 = [], [], []
    for n, rows in enumerate(chunks):
        kf = k_ref[rows, :].astype(F32) * jnp.exp(-bcum[n])
        q_dec = (q_ref[rows, :].astype(F32) * jnp.exp(bcum[n])).astype(BF16)
        k_end = (kf * dec[n]).astype(BF16)
        zq = jnp.zeros_like(q_dec)
        q2.append(jnp.concatenate([jnp.where(head0, q_dec, zq), jnp.where(head0, zq, q_dec)], axis=0))
        k_end2.append(jnp.concatenate([jnp.where(head0, k_end, zq), jnp.where(head0, zq, k_end)], axis=0))
        k_inv.append(kf.astype(BF16))
    lower2 = jnp.concatenate([lower, lower], axis=0)
    o_intra = []
    for n, rows in enumerate(chunks):
        a = jnp.where(lower2, _dot_nt(q2[n], k_inv[n]), 0.0).astype(BF16)
        o_intra.append((_dot(a[:c], v_ref[rows, :GLA_VAL_DIM]), _dot(a[c:], v_ref[rows, GLA_VAL_DIM:])))
    kv = [_dot_tn(jnp.concatenate([v_ref[rows, :GLA_VAL_DIM], v_ref[rows, GLA_VAL_DIM:]], axis=0), k_end2[n])
          for n, rows in enumerate(chunks)]
    st = st_ref[...]
    starts = []
    for n in range(len(chunks)):
        starts.append(st.astype(BF16))
        st = st * dec[n] + kv[n]
    st_ref[...] = st
    for n, rows in enumerate(chunks):
        o_inter = _dot_nt(q2[n], starts[n])
        o0 = o_intra[n][0] + o_inter[:c]
        o1 = o_intra[n][1] + o_inter[c:]
        n0 = o0 * lax.rsqrt(jnp.mean(o0 * o0, axis=-1, keepdims=True) + RMS_EPS)
        n1 = o1 * lax.rsqrt(jnp.mean(o1 * o1, axis=-1, keepdims=True) + RMS_EPS)
        rf = r_ref[rows, :].astype(F32)
        gate = rf / (1.0 + jnp.exp(-rf))
        o_ref[rows, :] = (jnp.concatenate([n0, n1], axis=1) * g_ref[...] * gate).astype(o_ref.dtype)


def _gla(q, k, v, r, lr, wg, bg, g):
    b, s, _ = q.shape
    pairs = GLA_HEADS // 2
    step = min(GLA_STEP, s)
    dv2 = 2 * GLA_VAL_DIM
    seq_map = lambda bi, p, t: (bi, t, p)
    return pl.pallas_call(
        _gla_kernel,
        out_shape=jax.ShapeDtypeStruct((b, s, GLA_V_WIDTH), BF16),
        grid=(b, pairs, s // step),
        in_specs=[
            pl.BlockSpec((None, step, LANES), seq_map),
            pl.BlockSpec((None, step, LANES), seq_map),
            pl.BlockSpec((None, step, dv2), seq_map),
            pl.BlockSpec((None, step, dv2), seq_map),
            pl.BlockSpec((None, step, LANES), lambda bi, p, t: (bi, t, 0)),
            pl.BlockSpec((LANES, LANES), lambda bi, p, t: (0, p)),
            pl.BlockSpec((1, LANES), lambda bi, p, t: (0, p)),
            pl.BlockSpec((1, dv2), lambda bi, p, t: (0, p)),
        ],
        out_specs=pl.BlockSpec((None, step, dv2), seq_map),
        scratch_shapes=[pltpu.VMEM((GLA_VAL_DIM, LANES), F32)],
        compiler_params=pltpu.CompilerParams(
            dimension_semantics=("arbitrary", "arbitrary", "arbitrary"),
            vmem_limit_bytes=VMEM_LIMIT_BYTES),
        name="gla",
    )(q, k, v, r, lr, wg, bg, g)


def _post_kernel(alpha, tiles_per_seq, layer, osb_ref, ogla_ref, x_ref, mod_ref, wo_hbm, g1_ref, b1_ref,
                 wg_hbm, wu_hbm, cw_ref, cb_ref, wd_hbm, g_ref, b_ref,
                 o_ref, halo_ref, act_ref, x1_ref, wo_ref, wg_ref, wu_ref, wd_ref, stage_ref, sem):
    tm = POST_SUB_TILE
    nsub = x_ref.shape[0] // tm
    d_ff = wg_ref.shape[1]
    n_sb = osb_ref.shape[1]
    first = (pl.program_id(0) % tiles_per_seq) == 0

    @pl.when(pl.program_id(0) == 0)
    def _():
        _stage_weights_bf16(((wo_hbm.at[layer], wo_ref), (wg_hbm.at[layer], wg_ref),
                             (wu_hbm.at[layer], wu_ref), (wd_hbm.at[layer], wd_ref)), stage_ref, sem)

    @pl.when(first)
    def _():
        halo_ref[...] = jnp.zeros_like(halo_ref)

    top = lax.broadcasted_iota(jnp.int32, (SUBLANES, FF_TILE), 0)
    hs = []
    for j in range(nsub):
        rows = slice(j * tm, (j + 1) * tm)
        mix = _dot(osb_ref[rows, :], wo_ref[:n_sb, :]) + _dot(ogla_ref[rows, :], wo_ref[n_sb:, :])
        x1_ref[rows, :] = _layer_norm(alpha * x_ref[rows, :] + (1.0 + mod_ref[2:3, :]) * mix,
                                      g1_ref[...], b1_ref[...])
        hs.append((x1_ref[rows, :] * (1.0 + mod_ref[4:5, :]) + mod_ref[3:4, :]).astype(BF16))
    for j in range(nsub):
        rows = slice(j * tm, (j + 1) * tm)
        h = hs[j]
        for f in range(d_ff // FF_TILE):
            cols = slice(f * FF_TILE, (f + 1) * FF_TILE)
            gt = _dot(h, wg_ref[:, cols])
            up = _dot(h, wu_ref[:, cols])
            prev = halo_ref[:, cols]
            halo_ref[:, cols] = gt[tm - SUBLANES:, :]
            r1 = pltpu.roll(gt, 1, 0)
            r2 = pltpu.roll(gt, 2, 0)
            t1 = jnp.where(top < 1, pltpu.roll(prev, 1, 0), r1[:SUBLANES])
            t2 = jnp.where(top < 2, pltpu.roll(prev, 2, 0), r2[:SUBLANES])
            g1 = jnp.concatenate([t1, r1[SUBLANES:]], axis=0)
            g2 = jnp.concatenate([t2, r2[SUBLANES:]], axis=0)
            conv = (g2 * cw_ref[0:1, cols] + g1 * cw_ref[1:2, cols] + gt * cw_ref[2:3, cols]
                    + cb_ref[:, cols])
            act_ref[rows, cols] = (conv / (1.0 + jnp.exp(-conv)) * up).astype(BF16)
        y = _dot(act_ref[rows, :], wd_ref[...])
        o_ref[rows, :] = _layer_norm(alpha * x1_ref[rows, :] + (1.0 + mod_ref[5:6, :]) * y,
                                     g_ref[...], b_ref[...])


def _post(o_sb, o_gla, x2d, mod, w_o, ln1_g, ln1_b, wg, wu, cw, cb, wd, ln_g, ln_b, seq, alpha, layer):
    t, d = x2d.shape
    d_ff = wg.shape[2]
    assert cw.shape[0] == CONV_WIDTH
    tm = ROW_TILE
    tiles_per_seq = seq // tm
    row_map = lambda i: (i, 0)
    fixed = lambda i: (0, 0)
    hbm = pl.BlockSpec(memory_space=pl.ANY)
    stage_width = max(w.shape[2] for w in (w_o, wg, wu, wd))
    return pl.pallas_call(
        functools.partial(_post_kernel, alpha, tiles_per_seq, layer),
        out_shape=jax.ShapeDtypeStruct((t, d), F32),
        grid=(t // tm,),
        in_specs=[
            pl.BlockSpec((tm, o_sb.shape[1]), row_map),
            pl.BlockSpec((tm, o_gla.shape[1]), row_map),
            pl.BlockSpec((tm, d), row_map),
            pl.BlockSpec((None, SUBLANES, d), lambda i: (i // tiles_per_seq, 0, 0)),
            hbm,
            pl.BlockSpec((1, d), fixed),
            pl.BlockSpec((1, d), fixed),
            hbm,
            hbm,
            pl.BlockSpec(cw.shape, fixed),
            pl.BlockSpec(cb.shape, fixed),
            hbm,
            pl.BlockSpec((1, d), fixed),
            pl.BlockSpec((1, d), fixed),
        ],
        out_specs=pl.BlockSpec((tm, d), row_map),
        scratch_shapes=[pltpu.VMEM((SUBLANES, d_ff), F32), pltpu.VMEM((tm, d_ff), BF16),
                        pltpu.VMEM((tm, d), F32),
                        pltpu.VMEM(w_o.shape[1:], BF16), pltpu.VMEM(wg.shape[1:], BF16),
                        pltpu.VMEM(wu.shape[1:], BF16), pltpu.VMEM(wd.shape[1:], BF16),
                        pltpu.VMEM((STAGE_SLOTS, STAGE_ROWS, stage_width), F32),
                        pltpu.SemaphoreType.DMA((STAGE_SLOTS,))],
        compiler_params=pltpu.CompilerParams(
            dimension_semantics=("arbitrary",), vmem_limit_bytes=VMEM_LIMIT_BYTES),
        name="post",
    )(o_sb, o_gla, x2d, mod, w_o, ln1_g, ln1_b, wg, wu, cw, cb, wd, ln_g, ln_b)


def kernel(x, c, w_ada, b_ada, w_in, gla_w_gate, gla_b_gate, sb_norm_g, gla_norm_g, w_out,
           ln1_g, ln1_b, w_ff_gate, w_ff_up, conv_w, conv_b, w_down, ln2_g, ln2_b):
    bsz, seq, d = x.shape
    depth = w_ada.shape[0]
    alpha = float((2 * depth) ** 0.25)
    t = bsz * seq

    c_pad = jnp.concatenate([c, jnp.zeros((SUBLANES - bsz, d), c.dtype)], axis=0)
    w_in_t = jnp.swapaxes(w_in, 1, 2)
    xcur = x.reshape(t, d)
    for l in range(depth):
        mod = _ada(c_pad, w_ada[l], b_ada[l][None, :])
        mod = mod[:bsz].reshape(bsz, N_MOD, d)
        mod = jnp.concatenate([mod, jnp.zeros((bsz, SUBLANES - N_MOD, d), F32)], axis=1)

        sb_q, sb_k, sb_v, g_q, g_k, g_v, g_r, g_lr = _inproj(xcur, mod, w_in_t, seq, l)
        as_seq = lambda a: a.reshape(bsz, seq, a.shape[-1])

        o_sb = _sb_attention(as_seq(sb_q), as_seq(sb_k), as_seq(sb_v), sb_norm_g[l][None, :])
        wg_pad = jnp.concatenate(
            [gla_w_gate[l], jnp.zeros((LANES - GLA_GATE_RANK, GLA_QK_WIDTH), F32)], axis=0).astype(BF16)
        o_gla = _gla(as_seq(g_q), as_seq(g_k), as_seq(g_v), as_seq(g_r), as_seq(g_lr),
                     wg_pad, gla_b_gate[l][None, :], gla_norm_g[l][None, :])

        xcur = _post(o_sb.reshape(t, SB_WIDTH), o_gla.reshape(t, GLA_V_WIDTH), xcur, mod,
                     w_out, ln1_g[l][None, :], ln1_b[l][None, :],
                     w_ff_gate, w_ff_up, conv_w[l], conv_b[l][None, :], w_down,
                     ln2_g[l][None, :], ln2_b[l][None, :], seq, alpha, l)
    return xcur.reshape(bsz, seq, d)
```

```python
import functools

import jax
import jax.numpy as jnp
from jax import lax
from jax.experimental import pallas as pl
from jax.experimental.pallas import tpu as pltpu

F32 = jnp.float32
BF16 = jnp.bfloat16

LANES = 128
SUBLANES = 8
VMEM_LIMIT_BYTES = 60 * 1024 * 1024

SB_HEADS = 8
SB_HEAD_DIM = 64
GLA_HEADS = 4
GLA_KEY_DIM = 64
GLA_VAL_DIM = 128
GLA_GATE_RANK = 16
GLA_GATE_TEMP = 16.0
GLA_CHUNK = 64
CONV_WIDTH = 3
N_MOD = 6
LN_EPS = 1e-5
RMS_EPS = 1e-6

SB_WIDTH = SB_HEADS * SB_HEAD_DIM
GLA_QK_WIDTH = GLA_HEADS * GLA_KEY_DIM
GLA_V_WIDTH = GLA_HEADS * GLA_VAL_DIM

SB_EXIT = 88.0
LOG2E = 1.4426950408889634
SB_EXIT_BITS = SB_EXIT * LOG2E
SB_MAX_BITS = 126.0
SB_MASKED_BITS = -1e30
SB_TK = 128
SB_TQ = 8 * SB_TK
SB_WIN_BLOCKS = 3
SB_WIN = SB_WIN_BLOCKS * SB_TK

GLA_STEP = 4096
ADA_ROWS = 256
INPROJ_TILE = 1024
ROW_TILE = 512
POST_SUB_TILE = 256
FF_TILE = 256
STAGE_ROWS = 128
STAGE_SLOTS = 3
INPROJ_STAGE_SLOTS = 6


def _sb_logits(q, k):
    return jnp.minimum(_dot_nt(q, k), SB_MAX_BITS)


def _softplus2(z):
    return jnp.log2(1.0 + jnp.exp2(z))


def _split_bf16(a):
    hi = a.astype(BF16)
    lo = (a - hi.astype(F32)).astype(BF16)
    return hi, lo


def _dot(a, b):
    return jnp.dot(a, b, preferred_element_type=F32)


def _dot_nt(a, b):
    return lax.dot_general(a, b, (((1,), (1,)), ((), ())), preferred_element_type=F32)


def _dot_tn(a, b):
    return lax.dot_general(a, b, (((0,), (0,)), ((), ())), preferred_element_type=F32)


def _stage_weights_bf16(pairs, stage_ref, sem):
    slots, rows = stage_ref.shape[0], stage_ref.shape[1]
    jobs = [(src, dst, r0) for src, dst in pairs for r0 in range(0, src.shape[0], rows)]

    def window(n):
        return stage_ref.at[n % slots, :, pl.ds(0, jobs[n][0].shape[1])]

    def copy(n):
        src, _, r0 = jobs[n]
        return pltpu.make_async_copy(src.at[pl.ds(r0, rows), :], window(n), sem.at[n % slots])

    for n in range(min(slots - 1, len(jobs))):
        copy(n).start()
    for n, (_, dst, r0) in enumerate(jobs):
        if n + slots - 1 < len(jobs):
            copy(n + slots - 1).start()
        copy(n).wait()
        dst[pl.ds(r0, rows), :] = window(n)[...].astype(BF16)


def _layer_norm(v, g, b):
    mu = jnp.mean(v, axis=-1, keepdims=True)
    vc = v - mu
    var = jnp.mean(vc * vc, axis=-1, keepdims=True)
    return vc * lax.rsqrt(var + LN_EPS) * g + b


def _ada_kernel(c_ref, w_ref, b_ref, o_ref):
    c = c_ref[...]
    ca = c / (1.0 + jnp.exp(-c))
    part = _dot(ca.astype(BF16), w_ref[...].astype(BF16))

    @pl.when(pl.program_id(0) == 0)
    def _():
        o_ref[...] = part + b_ref[...]

    @pl.when(pl.program_id(0) > 0)
    def _():
        o_ref[...] += part


def _ada(c_pad, w, b):
    d, n = w.shape
    tk = ADA_ROWS
    return pl.pallas_call(
        _ada_kernel,
        out_shape=jax.ShapeDtypeStruct((c_pad.shape[0], n), F32),
        grid=(d // tk,),
        in_specs=[
            pl.BlockSpec((c_pad.shape[0], tk), lambda k: (0, k)),
            pl.BlockSpec((tk, n), lambda k: (k, 0)),
            pl.BlockSpec((1, n), lambda k: (0, 0)),
        ],
        out_specs=pl.BlockSpec((c_pad.shape[0], n), lambda k: (0, 0)),
        compiler_params=pltpu.CompilerParams(
            dimension_semantics=("arbitrary",), vmem_limit_bytes=VMEM_LIMIT_BYTES),
        name="ada",
    )(c_pad, w, b)


_MAIN_WIDTH = 3 * SB_WIDTH + 2 * GLA_QK_WIDTH + GLA_V_WIDTH
_PROJ_GROUPS = (
    (0, 0, SB_WIDTH, SB_HEAD_DIM ** -0.5 * LOG2E),
    (0, SB_WIDTH, SB_WIDTH, 1.0),
    (0, 2 * SB_WIDTH, SB_WIDTH, 1.0),
    (0, 3 * SB_WIDTH, GLA_QK_WIDTH, GLA_KEY_DIM ** -0.5),
    (0, 3 * SB_WIDTH + GLA_QK_WIDTH, GLA_QK_WIDTH, 1.0),
    (0, 3 * SB_WIDTH + 2 * GLA_QK_WIDTH, GLA_V_WIDTH, 1.0),
    (1, 0, GLA_V_WIDTH, 1.0),
    (2, 0, LANES, 1.0),
)


def _stage_w_in(wt_hbm, w_main_ref, w_r_ref, w_lr_ref, stage_ref, sem):
    slots, rows, d = stage_ref.shape
    lr0 = _MAIN_WIDTH
    r0 = lr0 + GLA_GATE_RANK
    jobs = [(c, w_main_ref, c, rows) for c in range(0, lr0, rows)]
    jobs += [(lr0, w_lr_ref, 0, GLA_GATE_RANK)]
    jobs += [(r0 + c, w_r_ref, c, rows) for c in range(0, GLA_V_WIDTH, rows)]
    lane = lax.broadcasted_iota(jnp.int32, (d, rows), 1)

    def copy(n):
        return pltpu.make_async_copy(wt_hbm.at[pl.ds(jobs[n][0], rows), :],
                                     stage_ref.at[n % slots], sem.at[n % slots])

    for n in range(min(slots - 1, len(jobs))):
        copy(n).start()
    for n, (_, dst, col, valid) in enumerate(jobs):
        if n + slots - 1 < len(jobs):
            copy(n + slots - 1).start()
        copy(n).wait()
        blk = stage_ref[n % slots].T
        if valid < rows:
            blk = jnp.where(lane < valid, blk, 0.0)
        dst[:, pl.ds(col, rows)] = blk.astype(BF16)


def _inproj_kernel(layer, x_ref, mod_ref, w_hbm, *refs):
    n_out = len(_PROJ_GROUPS)
    out_refs = refs[:n_out]
    w_main_ref, w_r_ref, w_lr_ref, stage_ref, sem = refs[n_out:]

    @pl.when(pl.program_id(0) == 0)
    def _():
        _stage_w_in(w_hbm.at[layer], w_main_ref, w_r_ref, w_lr_ref, stage_ref, sem)

    shift = mod_ref[0:1, :]
    scale = mod_ref[1:2, :]
    h = (x_ref[...] * (1.0 + scale) + shift).astype(BF16)
    w_refs = (w_main_ref, w_r_ref, w_lr_ref)
    for (src, c0, width, mul), o_ref in zip(_PROJ_GROUPS, out_refs):
        acc = _dot(h, w_refs[src][:, c0:c0 + width])
        if mul != 1.0:
            acc = acc * mul
        o_ref[...] = acc.astype(o_ref.dtype)


def _inproj(x2d, mod, w_in_t, seq, layer):
    t, d = x2d.shape
    tm = INPROJ_TILE
    tiles_per_seq = seq // tm
    widths = [g[2] for g in _PROJ_GROUPS]
    return pl.pallas_call(
        functools.partial(_inproj_kernel, layer),
        out_shape=[jax.ShapeDtypeStruct((t, width), BF16) for width in widths],
        grid=(t // tm,),
        in_specs=[
            pl.BlockSpec((tm, d), lambda i: (i, 0)),
            pl.BlockSpec((None, SUBLANES, d), lambda i: (i // tiles_per_seq, 0, 0)),
            pl.BlockSpec(memory_space=pl.ANY),
        ],
        out_specs=[pl.BlockSpec((tm, width), lambda i: (i, 0)) for width in widths],
        scratch_shapes=[pltpu.VMEM((d, _MAIN_WIDTH), BF16), pltpu.VMEM((d, GLA_V_WIDTH), BF16),
                        pltpu.VMEM((d, LANES), BF16),
                        pltpu.VMEM((INPROJ_STAGE_SLOTS, LANES, d), F32),
                        pltpu.SemaphoreType.DMA((INPROJ_STAGE_SLOTS,))],
        compiler_params=pltpu.CompilerParams(
            dimension_semantics=("arbitrary",), vmem_limit_bytes=VMEM_LIMIT_BYTES),
        name="inproj",
    )(x2d, mod, w_in_t)


def _sb_suffix_sums(sp_blocks, umat):
    m = sp_blocks[0].shape[0]
    res = _dot(jnp.concatenate(sp_blocks, axis=0).astype(BF16), umat)
    cs, nearer = [], None
    for c in reversed(range(len(sp_blocks))):
        part = res[c * m:(c + 1) * m]
        loc, tot = part[:, :SB_TK], part[:, SB_TK:]
        cs.append(loc if nearer is None else loc + nearer)
        nearer = tot if nearer is None else nearer + tot
    return cs[::-1], nearer


def _sb_kernel(q_ref, k_ref, v_ref, g_ref, o_ref):
    tk = SB_TK
    pairs = q_ref.shape[1] // LANES
    nrb = SB_TQ // tk
    nb = SB_WIN_BLOCKS
    lane = lax.broadcasted_iota(jnp.int32, (tk, LANES), 1)
    head0 = lane < SB_HEAD_DIM

    uj = lax.broadcasted_iota(jnp.int32, (tk, 2 * tk), 0)
    us = lax.broadcasted_iota(jnp.int32, (tk, 2 * tk), 1)
    umat = jnp.where((uj >= us) | (us >= tk), 1.0, 0.0).astype(BF16)
    row = lax.broadcasted_iota(jnp.int32, (2 * tk, tk), 0) & (tk - 1)
    col = lax.broadcasted_iota(jnp.int32, (2 * tk, tk), 1)
    causal = col < row
    lanes = [slice(p * LANES, (p + 1) * LANES) for p in range(pairs)]

    first_blk = [pl.program_id(1) * nrb + r - (nb - 1) for r in range(nrb)]
    blk_rows = [[pl.ds(pl.multiple_of(jnp.maximum(first_blk[r] + c, 0) * tk, tk), tk)
                 for c in range(nb)] for r in range(nrb)]
    units = [(r, p) for r in range(nrb) for p in range(pairs)]

    qs, zs, sums, ws, accs = {}, {}, {}, {}, {}

    def stage_logits(u):
        r, p = u
        q = q_ref[r * tk:(r + 1) * tk, lanes[p]]
        zq = jnp.zeros_like(q)
        qs[u] = jnp.concatenate([jnp.where(head0, q, zq), jnp.where(head0, zq, q)], axis=0)
        kw = jnp.concatenate([k_ref[rows, lanes[p]] for rows in blk_rows[r]], axis=0)
        z = _sb_logits(qs[u], kw)
        zs[u] = [z[:, c * tk:(c + 1) * tk] for c in range(nb)]
        zs[u][-1] = jnp.where(causal, zs[u][-1], SB_MASKED_BITS)

    def stage_sums(u):
        sums[u] = _sb_suffix_sums([_softplus2(z) for z in zs[u]], umat)

    def stage_weights(u):
        args = [z - cs for z, cs in zip(zs[u], sums[u][0])]
        ws[u] = jnp.exp2(jnp.concatenate(args, axis=1)).astype(BF16)

    def stage_values(u):
        r, p = u
        vw = [v_ref[rows, lanes[p]] for rows in blk_rows[r]]
        vw = [jnp.where(first_blk[r] + c >= 0, vw[c], jnp.zeros_like(vw[c]))
              for c in range(nb - 1)] + vw[-1:]
        accs[u] = _dot(ws[u], jnp.concatenate(vw, axis=0))

    stages = (stage_logits, stage_sums, stage_weights, stage_values)
    order = sorted((stage + n // 2, n, stage)
                   for n in range(len(units)) for stage in range(len(stages)))
    for _, n, stage in order:
        stages[stage](units[n])

    def not_done(cs_):
        m = cs_[0]
        for c in cs_[1:]:
            m = jnp.minimum(m, c)
        return (jnp.min(m) < SB_EXIT_BITS).astype(jnp.int32)

    def cond(s):
        return jnp.logical_and(s[0] >= 0, s[1] > 0)

    def write_out(r, accs_):
        outs = []
        for p in range(pairs):
            o = jnp.where(head0, accs_[p][:tk], accs_[p][tk:])
            sq = o * o
            ss0 = jnp.sum(jnp.where(head0, sq, 0.0), axis=-1, keepdims=True)
            ss1 = jnp.sum(jnp.where(head0, 0.0, sq), axis=-1, keepdims=True)
            ms = jnp.where(head0, ss0, ss1) * (1.0 / SB_HEAD_DIM)
            outs.append(o * lax.rsqrt(ms + RMS_EPS))
        o_ref[r * tk:(r + 1) * tk, :] = (jnp.concatenate(outs, axis=1) * g_ref[...]).astype(o_ref.dtype)

    for r in range(nrb):
        write_out(r, [accs[r, p] for p in range(pairs)])
    carries = [tuple(sums[r, p][1] for p in range(pairs)) for r in range(nrb)]
    more = [jnp.logical_and(first_blk[r] - 1 >= 0, not_done(carries[r]) > 0) for r in range(nrb)]
    any_more = functools.reduce(jnp.logical_or, more)

    @pl.when(any_more)
    def _():
        for r in range(nrb):
            def body(s, r=r):
                j, _, cs_, as_ = s
                rows = pl.ds(pl.multiple_of(j * tk, tk), tk)
                zb = [_sb_logits(qs[r, p], k_ref[rows, lanes[p]]) for p in range(pairs)]
                sb = [_sb_suffix_sums([_softplus2(zb[p])], umat) for p in range(pairs)]
                wb = [jnp.exp2(zb[p] - sb[p][0][0] - cs_[p]).astype(BF16) for p in range(pairs)]
                new_a = [as_[p] + _dot(wb[p], v_ref[rows, lanes[p]]) for p in range(pairs)]
                new_c = [cs_[p] + sb[p][1] for p in range(pairs)]
                return j - 1, not_done(new_c), tuple(new_c), tuple(new_a)

            @pl.when(more[r])
            def _(r=r, body=body):
                _, _, _, accs_ = lax.while_loop(
                    cond, body,
                    (first_blk[r] - 1, jnp.int32(1), carries[r],
                     tuple(accs[r, p] for p in range(pairs))))
                write_out(r, accs_)


def _sb_attention(q, k, v, g):
    b, s, wdt = q.shape
    return pl.pallas_call(
        _sb_kernel,
        out_shape=jax.ShapeDtypeStruct((b, s, wdt), BF16),
        grid=(b, s // SB_TQ),
        in_specs=[
            pl.BlockSpec((None, SB_TQ, wdt), lambda bi, i: (bi, i, 0)),
            pl.BlockSpec((None, s, wdt), lambda bi, i: (bi, 0, 0)),
            pl.BlockSpec((None, s, wdt), lambda bi, i: (bi, 0, 0)),
            pl.BlockSpec((1, wdt), lambda bi, i: (0, 0)),
        ],
        out_specs=pl.BlockSpec((None, SB_TQ, wdt), lambda bi, i: (bi, i, 0)),
        compiler_params=pltpu.CompilerParams(
            dimension_semantics=("arbitrary", "arbitrary"),
            vmem_limit_bytes=VMEM_LIMIT_BYTES),
        name="sb_attention",
    )(q, k, v, g)


def _gla_kernel(q_ref, k_ref, v_ref, r_ref, lr_ref, wg_ref, bg_ref, g_ref, o_ref, st_ref):
    c = GLA_CHUNK

    @pl.when(pl.program_id(2) == 0)
    def _():
        st_ref[...] = jnp.zeros_like(st_ref)

    u = _dot(lr_ref[...], wg_ref[...]) + bg_ref[...]
    log_a = (jnp.minimum(u, 0.0) - jnp.log(1.0 + jnp.exp(-jnp.abs(u)))) * (1.0 / GLA_GATE_TEMP)

    ti = lax.broadcasted_iota(jnp.int32, (c, c), 0)
    tj = lax.broadcasted_iota(jnp.int32, (c, c), 1)
    lower = ti >= tj
    tril = jnp.where(lower, 1.0, 0.0).astype(BF16)
    lane = lax.broadcasted_iota(jnp.int32, (c, LANES), 1)
    head0 = lane < GLA_KEY_DIM

    chunks = [slice(n * c, (n + 1) * c) for n in range(q_ref.shape[0] // c)]
    la_hi, la_lo = _split_bf16(log_a)
    la_parts = jnp.concatenate([la_hi, la_lo], axis=1)
    bcum, dec = [], []
    for rows in chunks:
        res = _dot(tril, la_parts[rows])
        bcum.append(res[:, :LANES] + res[:, LANES:])
        dec.append(jnp.exp(bcum[-1][c - 1:c, :]))
    q2, k_inv, k_end2 = [], [], []
    for n, rows in enumerate(chunks):
        kf = k_ref[rows, :].astype(F32) * jnp.exp(-bcum[n])
        q_dec = (q_ref[rows, :].astype(F32) * jnp.exp(bcum[n])).astype(BF16)
        k_end = (kf * dec[n]).astype(BF16)
        zq = jnp.zeros_like(q_dec)
        q2.append(jnp.concatenate([jnp.where(head0, q_dec, zq), jnp.where(head0, zq, q_dec)], axis=0))
        k_end2.append(jnp.concatenate([jnp.where(head0, k_end, zq), jnp.where(head0, zq, k_end)], axis=0))
        k_inv.append(kf.astype(BF16))
    lower2 = jnp.concatenate([lower, lower], axis=0)
    o_intra = []
    for n, rows in enumerate(chunks):
        a = jnp.where(lower2, _dot_nt(q2[n], k_inv[n]), 0.0).astype(BF16)
        o_intra.append((_dot(a[:c], v_ref[rows, :GLA_VAL_DIM]), _dot(a[c:], v_ref[rows, GLA_VAL_DIM:])))
    kv = [_dot_tn(jnp.concatenate([v_ref[rows, :GLA_VAL_DIM], v_ref[rows, GLA_VAL_DIM:]], axis=0), k_end2[n])
          for n, rows in enumerate(chunks)]
    st = st_ref[...]
    starts = []
    for n in range(len(chunks)):
        starts.append(st.astype(BF16))
        st = st * dec[n] + kv[n]
    st_ref[...] = st
    for n, rows in enumerate(chunks):
        o_inter = _dot_nt(q2[n], starts[n])
        o0 = o_intra[n][0] + o_inter[:c]
        o1 = o_intra[n][1] + o_inter[c:]
        n0 = o0 * lax.rsqrt(jnp.mean(o0 * o0, axis=-1, keepdims=True) + RMS_EPS)
        n1 = o1 * lax.rsqrt(jnp.mean(o1 * o1, axis=-1, keepdims=True) + RMS_EPS)
        rf = r_ref[rows, :].astype(F32)
        gate = rf / (1.0 + jnp.exp(-rf))
        o_ref[rows, :] = (jnp.concatenate([n0, n1], axis=1) * g_ref[...] * gate).astype(o_ref.dtype)


def _gla(q, k, v, r, lr, wg, bg, g):
    b, s, _ = q.shape
    pairs = GLA_HEADS // 2
    step = min(GLA_STEP, s)
    dv2 = 2 * GLA_VAL_DIM
    seq_map = lambda bi, p, t: (bi, t, p)
    return pl.pallas_call(
        _gla_kernel,
        out_shape=jax.ShapeDtypeStruct((b, s, GLA_V_WIDTH), BF16),
        grid=(b, pairs, s // step),
        in_specs=[
            pl.BlockSpec((None, step, LANES), seq_map),
            pl.BlockSpec((None, step, LANES), seq_map),
            pl.BlockSpec((None, step, dv2), seq_map),
            pl.BlockSpec((None, step, dv2), seq_map),
            pl.BlockSpec((None, step, LANES), lambda bi, p, t: (bi, t, 0)),
            pl.BlockSpec((LANES, LANES), lambda bi, p, t: (0, p)),
            pl.BlockSpec((1, LANES), lambda bi, p, t: (0, p)),
            pl.BlockSpec((1, dv2), lambda bi, p, t: (0, p)),
        ],
        out_specs=pl.BlockSpec((None, step, dv2), seq_map),
        scratch_shapes=[pltpu.VMEM((GLA_VAL_DIM, LANES), F32)],
        compiler_params=pltpu.CompilerParams(
            dimension_semantics=("arbitrary", "arbitrary", "arbitrary"),
            vmem_limit_bytes=VMEM_LIMIT_BYTES),
        name="gla",
    )(q, k, v, r, lr, wg, bg, g)


def _post_kernel(alpha, tiles_per_seq, layer, osb_ref, ogla_ref, x_ref, mod_ref, wo_hbm, g1_ref, b1_ref,
                 wg_hbm, wu_hbm, cw_ref, cb_ref, wd_hbm, g_ref, b_ref,
                 o_ref, halo_ref, act_ref, x1_ref, wo_ref, wg_ref, wu_ref, wd_ref, stage_ref, sem):
    tm = POST_SUB_TILE
    nsub = x_ref.shape[0] // tm
    d_ff = wg_ref.shape[1]
    n_sb = osb_ref.shape[1]
    first = (pl.program_id(0) % tiles_per_seq) == 0

    @pl.when(pl.program_id(0) == 0)
    def _():
        _stage_weights_bf16(((wo_hbm.at[layer], wo_ref), (wg_hbm.at[layer], wg_ref),
                             (wu_hbm.at[layer], wu_ref), (wd_hbm.at[layer], wd_ref)), stage_ref, sem)

    @pl.when(first)
    def _():
        halo_ref[...] = jnp.zeros_like(halo_ref)

    top = lax.broadcasted_iota(jnp.int32, (SUBLANES, FF_TILE), 0)
    hs = []
    for j in range(nsub):
        rows = slice(j * tm, (j + 1) * tm)
        mix = _dot(osb_ref[rows, :], wo_ref[:n_sb, :]) + _dot(ogla_ref[rows, :], wo_ref[n_sb:, :])
        x1_ref[rows, :] = _layer_norm(alpha * x_ref[rows, :] + (1.0 + mod_ref[2:3, :]) * mix,
                                      g1_ref[...], b1_ref[...])
        hs.append((x1_ref[rows, :] * (1.0 + mod_ref[4:5, :]) + mod_ref[3:4, :]).astype(BF16))
    for j in range(nsub):
        rows = slice(j * tm, (j + 1) * tm)
        h = hs[j]
        for f in range(d_ff // FF_TILE):
            cols = slice(f * FF_TILE, (f + 1) * FF_TILE)
            gt = _dot(h, wg_ref[:, cols])
            up = _dot(h, wu_ref[:, cols])
            prev = halo_ref[:, cols]
            halo_ref[:, cols] = gt[tm - SUBLANES:, :]
            r1 = pltpu.roll(gt, 1, 0)
            r2 = pltpu.roll(gt, 2, 0)
            t1 = jnp.where(top < 1, pltpu.roll(prev, 1, 0), r1[:SUBLANES])
            t2 = jnp.where(top < 2, pltpu.roll(prev, 2, 0), r2[:SUBLANES])
            g1 = jnp.concatenate([t1, r1[SUBLANES:]], axis=0)
            g2 = jnp.concatenate([t2, r2[SUBLANES:]], axis=0)
            conv = (g2 * cw_ref[0:1, cols] + g1 * cw_ref[1:2, cols] + gt * cw_ref[2:3, cols]
                    + cb_ref[:, cols])
            act_ref[rows, cols] = (conv / (1.0 + jnp.exp(-conv)) * up).astype(BF16)
        y = _dot(act_ref[rows, :], wd_ref[...])
        o_ref[rows, :] = _layer_norm(alpha * x1_ref[rows, :] + (1.0 + mod_ref[5:6, :]) * y,
                                     g_ref[...], b_ref[...])


def _post(o_sb, o_gla, x2d, mod, w_o, ln1_g, ln1_b, wg, wu, cw, cb, wd, ln_g, ln_b, seq, alpha, layer):
    t, d = x2d.shape
    d_ff = wg.shape[2]
    assert cw.shape[0] == CONV_WIDTH
    tm = ROW_TILE
    tiles_per_seq = seq // tm
    row_map = lambda i: (i, 0)
    fixed = lambda i: (0, 0)
    hbm = pl.BlockSpec(memory_space=pl.ANY)
    stage_width = max(w.shape[2] for w in (w_o, wg, wu, wd))
    return pl.pallas_call(
        functools.partial(_post_kernel, alpha, tiles_per_seq, layer),
        out_shape=jax.ShapeDtypeStruct((t, d), F32),
        grid=(t // tm,),
        in_specs=[
            pl.BlockSpec((tm, o_sb.shape[1]), row_map),
            pl.BlockSpec((tm, o_gla.shape[1]), row_map),
            pl.BlockSpec((tm, d), row_map),
            pl.BlockSpec((None, SUBLANES, d), lambda i: (i // tiles_per_seq, 0, 0)),
            hbm,
            pl.BlockSpec((1, d), fixed),
            pl.BlockSpec((1, d), fixed),
            hbm,
            hbm,
            pl.BlockSpec(cw.shape, fixed),
            pl.BlockSpec(cb.shape, fixed),
            hbm,
            pl.BlockSpec((1, d), fixed),
            pl.BlockSpec((1, d), fixed),
        ],
        out_specs=pl.BlockSpec((tm, d), row_map),
        scratch_shapes=[pltpu.VMEM((SUBLANES, d_ff), F32), pltpu.VMEM((tm, d_ff), BF16),
                        pltpu.VMEM((tm, d), F32),
                        pltpu.VMEM(w_o.shape[1:], BF16), pltpu.VMEM(wg.shape[1:], BF16),
                        pltpu.VMEM(wu.shape[1:], BF16), pltpu.VMEM(wd.shape[1:], BF16),
                        pltpu.VMEM((STAGE_SLOTS, STAGE_ROWS, stage_width), F32),
                        pltpu.SemaphoreType.DMA((STAGE_SLOTS,))],
        compiler_params=pltpu.CompilerParams(
            dimension_semantics=("arbitrary",), vmem_limit_bytes=VMEM_LIMIT_BYTES),
        name="post",
    )(o_sb, o_gla, x2d, mod, w_o, ln1_g, ln1_b, wg, wu, cw, cb, wd, ln_g, ln_b)


def kernel(x, c, w_ada, b_ada, w_in, gla_w_gate, gla_b_gate, sb_norm_g, gla_norm_g, w_out,
           ln1_g, ln1_b, w_ff_gate, w_ff_up, conv_w, conv_b, w_down, ln2_g, ln2_b):
    bsz, seq, d = x.shape
    depth = w_ada.shape[0]
    alpha = float((2 * depth) ** 0.25)
    t = bsz * seq

    c_pad = jnp.concatenate([c, jnp.zeros((SUBLANES - bsz, d), c.dtype)], axis=0)
    w_in_t = jnp.swapaxes(w_in, 1, 2)
    xcur = x.reshape(t, d)
    for l in range(depth):
        mod = _ada(c_pad, w_ada[l], b_ada[l][None, :])
        mod = mod[:bsz].reshape(bsz, N_MOD, d)
        mod = jnp.concatenate([mod, jnp.zeros((bsz, SUBLANES - N_MOD, d), F32)], axis=1)

        sb_q, sb_k, sb_v, g_q, g_k, g_v, g_r, g_lr = _inproj(xcur, mod, w_in_t, seq, l)
        as_seq = lambda a: a.reshape(bsz, seq, a.shape[-1])

        o_sb = _sb_attention(as_seq(sb_q), as_seq(sb_k), as_seq(sb_v), sb_norm_g[l][None, :])
        wg_pad = jnp.concatenate(
            [gla_w_gate[l], jnp.zeros((LANES - GLA_GATE_RANK, GLA_QK_WIDTH), F32)], axis=0).astype(BF16)
        o_gla = _gla(as_seq(g_q), as_seq(g_k), as_seq(g_v), as_seq(g_r), as_seq(g_lr),
                     wg_pad, gla_b_gate[l][None, :], gla_norm_g[l][None, :])

        xcur = _post(o_sb.reshape(t, SB_WIDTH), o_gla.reshape(t, GLA_V_WIDTH), xcur, mod,
                     w_out, ln1_g[l][None, :], ln1_b[l][None, :],
                     w_ff_gate, w_ff_up, conv_w[l], conv_b[l][None, :], w_down,
                     ln2_g[l][None, :], ln2_b[l][None, :], seq, alpha, l)
    return xcur.reshape(bsz, seq, d)
```

```python
import functools

import jax
import jax.numpy as jnp
from jax import lax
from jax.experimental import pallas as pl
from jax.experimental.pallas import tpu as pltpu

F32 = jnp.float32
BF16 = jnp.bfloat16

LANES = 128
SUBLANES = 8
VMEM_LIMIT_BYTES = 60 * 1024 * 1024

SB_HEADS = 8
SB_HEAD_DIM = 64
GLA_HEADS = 4
GLA_KEY_DIM = 64
GLA_VAL_DIM = 128
GLA_GATE_RANK = 16
GLA_GATE_TEMP = 16.0
GLA_CHUNK = 64
CONV_WIDTH = 3
N_MOD = 6
LN_EPS = 1e-5
RMS_EPS = 1e-6

SB_WIDTH = SB_HEADS * SB_HEAD_DIM
GLA_QK_WIDTH = GLA_HEADS * GLA_KEY_DIM
GLA_V_WIDTH = GLA_HEADS * GLA_VAL_DIM

SB_EXIT = 88.0
LOG2E = 1.4426950408889634
SB_EXIT_BITS = SB_EXIT * LOG2E
SB_MAX_BITS = 126.0
SB_MASKED_BITS = -1e30
SB_TK = 128
SB_TQ = 8 * SB_TK
SB_WIN_BLOCKS = 3
SB_WIN = SB_WIN_BLOCKS * SB_TK

GLA_STEP = 4096
ADA_ROWS = 128
INPROJ_TILE = 1024
ROW_TILE = 1024
POST_SUB_TILE = 256
FF_TILE = 256
STAGE_ROWS = 128
STAGE_SLOTS = 3
INPROJ_STAGE_SLOTS = 6


def _sb_logits(q, k):
    return jnp.minimum(_dot_nt(q, k), SB_MAX_BITS)


def _softplus2(z):
    return jnp.log2(1.0 + jnp.exp2(z))


def _split_bf16(a):
    hi = a.astype(BF16)
    lo = (a - hi.astype(F32)).astype(BF16)
    return hi, lo


def _dot(a, b):
    return jnp.dot(a, b, preferred_element_type=F32)


def _dot_nt(a, b):
    return lax.dot_general(a, b, (((1,), (1,)), ((), ())), preferred_element_type=F32)


def _dot_tn(a, b):
    return lax.dot_general(a, b, (((0,), (0,)), ((), ())), preferred_element_type=F32)


def _stage_weights_bf16(pairs, stage_ref, sem):
    slots, rows = stage_ref.shape[0], stage_ref.shape[1]
    jobs = [(src, dst, r0) for src, dst in pairs for r0 in range(0, src.shape[0], rows)]

    def window(n):
        return stage_ref.at[n % slots, :, pl.ds(0, jobs[n][0].shape[1])]

    def copy(n):
        src, _, r0 = jobs[n]
        return pltpu.make_async_copy(src.at[pl.ds(r0, rows), :], window(n), sem.at[n % slots])

    for n in range(min(slots - 1, len(jobs))):
        copy(n).start()
    for n, (_, dst, r0) in enumerate(jobs):
        if n + slots - 1 < len(jobs):
            copy(n + slots - 1).start()
        copy(n).wait()
        dst[pl.ds(r0, rows), :] = window(n)[...].astype(BF16)


def _layer_norm(v, g, b):
    mu = jnp.mean(v, axis=-1, keepdims=True)
    vc = v - mu
    var = jnp.mean(vc * vc, axis=-1, keepdims=True)
    return vc * lax.rsqrt(var + LN_EPS) * g + b


_MAIN_WIDTH = 3 * SB_WIDTH + 2 * GLA_QK_WIDTH + GLA_V_WIDTH
_PROJ_GROUPS = (
    (0, 0, SB_WIDTH, SB_HEAD_DIM ** -0.5 * LOG2E),
    (0, SB_WIDTH, SB_WIDTH, 1.0),
    (0, 2 * SB_WIDTH, SB_WIDTH, 1.0),
    (0, 3 * SB_WIDTH, GLA_QK_WIDTH, GLA_KEY_DIM ** -0.5),
    (0, 3 * SB_WIDTH + GLA_QK_WIDTH, GLA_QK_WIDTH, 1.0),
    (0, 3 * SB_WIDTH + 2 * GLA_QK_WIDTH, GLA_V_WIDTH, 1.0),
    (1, 0, GLA_V_WIDTH, 1.0),
    (2, 0, LANES, 1.0),
)


def _stage_w_in(wt_hbm, w_main_ref, w_r_ref, w_lr_ref, stage_ref, sem):
    slots, rows, d = stage_ref.shape
    lr0 = _MAIN_WIDTH
    r0 = lr0 + GLA_GATE_RANK
    jobs = [(c, w_main_ref, c, rows) for c in range(0, lr0, rows)]
    jobs += [(lr0, w_lr_ref, 0, GLA_GATE_RANK)]
    jobs += [(r0 + c, w_r_ref, c, rows) for c in range(0, GLA_V_WIDTH, rows)]
    lane = lax.broadcasted_iota(jnp.int32, (d, rows), 1)

    def copy(n):
        return pltpu.make_async_copy(wt_hbm.at[pl.ds(jobs[n][0], rows), :],
                                     stage_ref.at[n % slots], sem.at[n % slots])

    for n in range(min(slots - 1, len(jobs))):
        copy(n).start()
    for n, (_, dst, col, valid) in enumerate(jobs):
        if n + slots - 1 < len(jobs):
            copy(n + slots - 1).start()
        copy(n).wait()
        blk = stage_ref[n % slots].T
        if valid < rows:
            blk = jnp.where(lane < valid, blk, 0.0)
        dst[:, pl.ds(col, rows)] = blk.astype(BF16)


def _modulation(c_ref, wada_hbm, bada_ref, c8_ref, mod_ref, mod_out_ref, ada_stage_ref, ada_sem,
                between):
    bsz, d = c_ref.shape
    slots, rows = ada_stage_ref.shape[0], ada_stage_ref.shape[1]
    n_jobs = d // rows

    def copy(n):
        return pltpu.make_async_copy(wada_hbm.at[pl.ds(n * rows, rows), :],
                                     ada_stage_ref.at[n % slots], ada_sem.at[n % slots])

    for n in range(min(slots - 1, n_jobs)):
        copy(n).start()
    between()
    c8_ref[...] = jnp.zeros_like(c8_ref)
    c8_ref[0:bsz, :] = c_ref[...]
    c = c8_ref[...]
    ca = (c / (1.0 + jnp.exp(-c))).astype(BF16)
    acc = jnp.zeros(mod_ref.shape, F32) + bada_ref[...]
    for n in range(n_jobs):
        if n + slots - 1 < n_jobs:
            copy(n + slots - 1).start()
        copy(n).wait()
        acc = acc + _dot(ca[:, n * rows:(n + 1) * rows], ada_stage_ref[n % slots].astype(BF16))
    mod_ref[...] = acc
    mod_out_ref[...] = jnp.zeros_like(mod_out_ref)
    for b in range(bsz):
        for m in range(N_MOD):
            mod_out_ref[b, m:m + 1, :] = mod_ref[b:b + 1, m * d:(m + 1) * d]


def _inproj_kernel(layer, tiles_per_seq, x_ref, c_ref, wada_hbm, bada_ref, w_hbm, *refs):
    n_out = len(_PROJ_GROUPS)
    out_refs, mod_out_ref = refs[:n_out], refs[n_out]
    (w_main_ref, w_r_ref, w_lr_ref, stage_ref, sem,
     c8_ref, mod_ref, ada_stage_ref, ada_sem) = refs[n_out + 1:]
    d = x_ref.shape[1]

    @pl.when(pl.program_id(0) == 0)
    def _():
        _modulation(c_ref, wada_hbm.at[layer], bada_ref, c8_ref, mod_ref, mod_out_ref,
                    ada_stage_ref, ada_sem,
                    lambda: _stage_w_in(w_hbm.at[layer], w_main_ref, w_r_ref, w_lr_ref, stage_ref, sem))

    b = pl.program_id(0) // tiles_per_seq
    shift = mod_ref[pl.ds(b, 1), 0:d]
    scale = mod_ref[pl.ds(b, 1), d:2 * d]
    h = (x_ref[...] * (1.0 + scale) + shift).astype(BF16)
    w_refs = (w_main_ref, w_r_ref, w_lr_ref)
    for (src, c0, width, mul), o_ref in zip(_PROJ_GROUPS, out_refs):
        acc = _dot(h, w_refs[src][:, c0:c0 + width])
        if mul != 1.0:
            acc = acc * mul
        o_ref[...] = acc.astype(o_ref.dtype)


def _inproj(x2d, c, w_ada, b_ada, w_in_t, seq, layer):
    t, d = x2d.shape
    bsz = c.shape[0]
    n_mod = w_ada.shape[2]
    tm = INPROJ_TILE
    tiles_per_seq = seq // tm
    widths = [g[2] for g in _PROJ_GROUPS]
    fixed = lambda i: (0, 0)
    hbm = pl.BlockSpec(memory_space=pl.ANY)
    return pl.pallas_call(
        functools.partial(_inproj_kernel, layer, tiles_per_seq),
        out_shape=[jax.ShapeDtypeStruct((t, width), BF16) for width in widths]
                  + [jax.ShapeDtypeStruct((bsz, SUBLANES, d), F32)],
        grid=(t // tm,),
        in_specs=[
            pl.BlockSpec((tm, d), lambda i: (i, 0)),
            pl.BlockSpec((bsz, d), fixed),
            hbm,
            pl.BlockSpec((1, n_mod), fixed),
            hbm,
        ],
        out_specs=[pl.BlockSpec((tm, width), lambda i: (i, 0)) for width in widths]
                  + [pl.BlockSpec((bsz, SUBLANES, d), lambda i: (0, 0, 0))],
        scratch_shapes=[pltpu.VMEM((d, _MAIN_WIDTH), BF16), pltpu.VMEM((d, GLA_V_WIDTH), BF16),
                        pltpu.VMEM((d, LANES), BF16),
                        pltpu.VMEM((INPROJ_STAGE_SLOTS, LANES, d), F32),
                        pltpu.SemaphoreType.DMA((INPROJ_STAGE_SLOTS,)),
                        pltpu.VMEM((SUBLANES, d), F32), pltpu.VMEM((SUBLANES, n_mod), F32),
                        pltpu.VMEM((STAGE_SLOTS, ADA_ROWS, n_mod), F32),
                        pltpu.SemaphoreType.DMA((STAGE_SLOTS,))],
        compiler_params=pltpu.CompilerParams(
            dimension_semantics=("arbitrary",), vmem_limit_bytes=VMEM_LIMIT_BYTES),
        name="inproj",
    )(x2d, c, w_ada, b_ada, w_in_t)


def _sb_suffix_sums(sp_blocks, umat):
    m = sp_blocks[0].shape[0]
    res = _dot(jnp.concatenate(sp_blocks, axis=0).astype(BF16), umat)
    cs, nearer = [], None
    for c in reversed(range(len(sp_blocks))):
        part = res[c * m:(c + 1) * m]
        loc, tot = part[:, :SB_TK], part[:, SB_TK:]
        cs.append(loc if nearer is None else loc + nearer)
        nearer = tot if nearer is None else nearer + tot
    return cs[::-1], nearer


def _sb_kernel(q_ref, k_ref, v_ref, g_ref, o_ref):
    tk = SB_TK
    pairs = q_ref.shape[1] // LANES
    nrb = SB_TQ // tk
    nb = SB_WIN_BLOCKS
    lane = lax.broadcasted_iota(jnp.int32, (tk, LANES), 1)
    head0 = lane < SB_HEAD_DIM

    uj = lax.broadcasted_iota(jnp.int32, (tk, 2 * tk), 0)
    us = lax.broadcasted_iota(jnp.int32, (tk, 2 * tk), 1)
    umat = jnp.where((uj >= us) | (us >= tk), 1.0, 0.0).astype(BF16)
    row = lax.broadcasted_iota(jnp.int32, (2 * tk, tk), 0) & (tk - 1)
    col = lax.broadcasted_iota(jnp.int32, (2 * tk, tk), 1)
    causal = col < row
    lanes = [slice(p * LANES, (p + 1) * LANES) for p in range(pairs)]

    first_blk = [pl.program_id(1) * nrb + r - (nb - 1) for r in range(nrb)]
    blk_rows = [[pl.ds(pl.multiple_of(jnp.maximum(first_blk[r] + c, 0) * tk, tk), tk)
                 for c in range(nb)] for r in range(nrb)]
    units = [(r, p) for r in range(nrb) for p in range(pairs)]

    qs, zs, sums, ws, accs = {}, {}, {}, {}, {}

    def stage_logits(u):
        r, p = u
        q = q_ref[r * tk:(r + 1) * tk, lanes[p]]
        zq = jnp.zeros_like(q)
        qs[u] = jnp.concatenate([jnp.where(head0, q, zq), jnp.where(head0, zq, q)], axis=0)
        kw = jnp.concatenate([k_ref[rows, lanes[p]] for rows in blk_rows[r]], axis=0)
        z = _sb_logits(qs[u], kw)
        zs[u] = [z[:, c * tk:(c + 1) * tk] for c in range(nb)]
        zs[u][-1] = jnp.where(causal, zs[u][-1], SB_MASKED_BITS)

    def stage_sums(u):
        sums[u] = _sb_suffix_sums([_softplus2(z) for z in zs[u]], umat)

    def stage_weights(u):
        args = [z - cs for z, cs in zip(zs[u], sums[u][0])]
        ws[u] = jnp.exp2(jnp.concatenate(args, axis=1)).astype(BF16)

    def stage_values(u):
        r, p = u
        vw = [v_ref[rows, lanes[p]] for rows in blk_rows[r]]
        vw = [jnp.where(first_blk[r] + c >= 0, vw[c], jnp.zeros_like(vw[c]))
              for c in range(nb - 1)] + vw[-1:]
        accs[u] = _dot(ws[u], jnp.concatenate(vw, axis=0))

    stages = (stage_logits, stage_sums, stage_weights, stage_values)
    order = sorted((stage + n // 2, n, stage)
                   for n in range(len(units)) for stage in range(len(stages)))
    for _, n, stage in order:
        stages[stage](units[n])

    def not_done(cs_):
        m = cs_[0]
        for c in cs_[1:]:
            m = jnp.minimum(m, c)
        return (jnp.min(m) < SB_EXIT_BITS).astype(jnp.int32)

    def cond(s):
        return jnp.logical_and(s[0] >= 0, s[1] > 0)

    def write_out(r, accs_):
        outs = []
        for p in range(pairs):
            o = jnp.where(head0, accs_[p][:tk], accs_[p][tk:])
            sq = o * o
            ss0 = jnp.sum(jnp.where(head0, sq, 0.0), axis=-1, keepdims=True)
            ss1 = jnp.sum(jnp.where(head0, 0.0, sq), axis=-1, keepdims=True)
            ms = jnp.where(head0, ss0, ss1) * (1.0 / SB_HEAD_DIM)
            outs.append(o * lax.rsqrt(ms + RMS_EPS))
        o_ref[r * tk:(r + 1) * tk, :] = (jnp.concatenate(outs, axis=1) * g_ref[...]).astype(o_ref.dtype)

    for r in range(nrb):
        write_out(r, [accs[r, p] for p in range(pairs)])
    carries = [tuple(sums[r, p][1] for p in range(pairs)) for r in range(nrb)]
    more = [jnp.logical_and(first_blk[r] - 1 >= 0, not_done(carries[r]) > 0) for r in range(nrb)]
    any_more = functools.reduce(jnp.logical_or, more)

    @pl.when(any_more)
    def _():
        for r in range(nrb):
            def body(s, r=r):
                j, _, cs_, as_ = s
                rows = pl.ds(pl.multiple_of(j * tk, tk), tk)
                zb = [_sb_logits(qs[r, p], k_ref[rows, lanes[p]]) for p in range(pairs)]
                sb = [_sb_suffix_sums([_softplus2(zb[p])], umat) for p in range(pairs)]
                wb = [jnp.exp2(zb[p] - sb[p][0][0] - cs_[p]).astype(BF16) for p in range(pairs)]
                new_a = [as_[p] + _dot(wb[p], v_ref[rows, lanes[p]]) for p in range(pairs)]
                new_c = [cs_[p] + sb[p][1] for p in range(pairs)]
                return j - 1, not_done(new_c), tuple(new_c), tuple(new_a)

            @pl.when(more[r])
            def _(r=r, body=body):
                _, _, _, accs_ = lax.while_loop(
                    cond, body,
                    (first_blk[r] - 1, jnp.int32(1), carries[r],
                     tuple(accs[r, p] for p in range(pairs))))
                write_out(r, accs_)


def _sb_attention(q, k, v, g):
    b, s, wdt = q.shape
    return pl.pallas_call(
        _sb_kernel,
        out_shape=jax.ShapeDtypeStruct((b, s, wdt), BF16),
        grid=(b, s // SB_TQ),
        in_specs=[
            pl.BlockSpec((None, SB_TQ, wdt), lambda bi, i: (bi, i, 0)),
            pl.BlockSpec((None, s, wdt), lambda bi, i: (bi, 0, 0)),
            pl.BlockSpec((None, s, wdt), lambda bi, i: (bi, 0, 0)),
            pl.BlockSpec((1, wdt), lambda bi, i: (0, 0)),
        ],
        out_specs=pl.BlockSpec((None, SB_TQ, wdt), lambda bi, i: (bi, i, 0)),
        compiler_params=pltpu.CompilerParams(
            dimension_semantics=("arbitrary", "arbitrary"),
            vmem_limit_bytes=VMEM_LIMIT_BYTES),
        name="sb_attention",
    )(q, k, v, g)


def _gla_kernel(q_ref, k_ref, v_ref, r_ref, lr_ref, wg_ref, bg_ref, g_ref, o_ref, st_ref):
    c = GLA_CHUNK

    @pl.when(pl.program_id(2) == 0)
    def _():
        st_ref[...] = jnp.zeros_like(st_ref)

    u = _dot(lr_ref[...], wg_ref[...]) + bg_ref[...]
    log_a = (jnp.minimum(u, 0.0) - jnp.log(1.0 + jnp.exp(-jnp.abs(u)))) * (1.0 / GLA_GATE_TEMP)

    ti = lax.broadcasted_iota(jnp.int32, (c, c), 0)
    tj = lax.broadcasted_iota(jnp.int32, (c, c), 1)
    lower = ti >= tj
    tril = jnp.where(lower, 1.0, 0.0).astype(BF16)
    lane = lax.broadcasted_iota(jnp.int32, (c, LANES), 1)
    head0 = lane < GLA_KEY_DIM

    chunks = [slice(n * c, (n + 1) * c) for n in range(q_ref.shape[0] // c)]
    la_hi, la_lo = _split_bf16(log_a)
    la_parts = jnp.concatenate([la_hi, la_lo], axis=1)
    bcum, dec = [], []
    for rows in chunks:
        res = _dot(tril, la_parts[rows])
        bcum.append(res[:, :LANES] + res[:, LANES:])
        dec.append(jnp.exp(bcum[-1][c - 1:c, :]))
    q2, k_inv, k_end2 = [], [], []
    for n, rows in enumerate(chunks):
        kf = k_ref[rows, :].astype(F32) * jnp.exp(-bcum[n])
        q_dec = (q_ref[rows, :].astype(F32) * jnp.exp(bcum[n])).astype(BF16)
        k_end = (kf * dec[n]).astype(BF16)
        zq = jnp.zeros_like(q_dec)
        q2.append(jnp.concatenate([jnp.where(head0, q_dec, zq), jnp.where(head0, zq, q_dec)], axis=0))
        k_end2.append(jnp.concatenate([jnp.where(head0, k_end, zq), jnp.where(head0, zq, k_end)], axis=0))
        k_inv.append(kf.astype(BF16))
    lower2 = jnp.concatenate([lower, lower], axis=0)
    o_intra = []
    for n, rows in enumerate(chunks):
        a = jnp.where(lower2, _dot_nt(q2[n], k_inv[n]), 0.0).astype(BF16)
        o_intra.append((_dot(a[:c], v_ref[rows, :GLA_VAL_DIM]), _dot(a[c:], v_ref[rows, GLA_VAL_DIM:])))
    kv = [_dot_tn(jnp.concatenate([v_ref[rows, :GLA_VAL_DIM], v_ref[rows, GLA_VAL_DIM:]], axis=0), k_end2[n])
          for n, rows in enumerate(chunks)]
    st = st_ref[...]
    starts = []
    for n in range(len(chunks)):
        starts.append(st.astype(BF16))
        st = st * dec[n] + kv[n]
    st_ref[...] = st
    for n, rows in enumerate(chunks):
        o_inter = _dot_nt(q2[n], starts[n])
        o0 = o_intra[n][0] + o_inter[:c]
        o1 = o_intra[n][1] + o_inter[c:]
        n0 = o0 * lax.rsqrt(jnp.mean(o0 * o0, axis=-1, keepdims=True) + RMS_EPS)
        n1 = o1 * lax.rsqrt(jnp.mean(o1 * o1, axis=-1, keepdims=True) + RMS_EPS)
        rf = r_ref[rows, :].astype(F32)
        gate = rf / (1.0 + jnp.exp(-rf))
        o_ref[rows, :] = (jnp.concatenate([n0, n1], axis=1) * g_ref[...] * gate).astype(o_ref.dtype)


def _gla(q, k, v, r, lr, wg, bg, g):
    b, s, _ = q.shape
    pairs = GLA_HEADS // 2
    step = min(GLA_STEP, s)
    dv2 = 2 * GLA_VAL_DIM
    seq_map = lambda bi, p, t: (bi, t, p)
    return pl.pallas_call(
        _gla_kernel,
        out_shape=jax.ShapeDtypeStruct((b, s, GLA_V_WIDTH), BF16),
        grid=(b, pairs, s // step),
        in_specs=[
            pl.BlockSpec((None, step, LANES), seq_map),
            pl.BlockSpec((None, step, LANES), seq_map),
            pl.BlockSpec((None, step, dv2), seq_map),
            pl.BlockSpec((None, step, dv2), seq_map),
            pl.BlockSpec((None, step, LANES), lambda bi, p, t: (bi, t, 0)),
            pl.BlockSpec((LANES, LANES), lambda bi, p, t: (0, p)),
            pl.BlockSpec((1, LANES), lambda bi, p, t: (0, p)),
            pl.BlockSpec((1, dv2), lambda bi, p, t: (0, p)),
        ],
        out_specs=pl.BlockSpec((None, step, dv2), seq_map),
        scratch_shapes=[pltpu.VMEM((GLA_VAL_DIM, LANES), F32)],
        compiler_params=pltpu.CompilerParams(
            dimension_semantics=("arbitrary", "arbitrary", "arbitrary"),
            vmem_limit_bytes=VMEM_LIMIT_BYTES),
        name="gla",
    )(q, k, v, r, lr, wg, bg, g)


def _post_kernel(alpha, tiles_per_seq, layer, osb_ref, ogla_ref, x_ref, mod_ref, wo_hbm, g1_ref, b1_ref,
                 wg_hbm, wu_hbm, cw_ref, cb_ref, wd_hbm, g_ref, b_ref,
                 o_ref, halo_ref, act_ref, x1_ref, wo_ref, wg_ref, wu_ref, wd_ref, stage_ref, sem):
    tm = POST_SUB_TILE
    nsub = x_ref.shape[0] // tm
    d_ff = wg_ref.shape[1]
    n_sb = osb_ref.shape[1]
    first = (pl.program_id(0) % tiles_per_seq) == 0

    @pl.when(pl.program_id(0) == 0)
    def _():
        _stage_weights_bf16(((wo_hbm.at[layer], wo_ref), (wg_hbm.at[layer], wg_ref),
                             (wu_hbm.at[layer], wu_ref), (wd_hbm.at[layer], wd_ref)), stage_ref, sem)

    @pl.when(first)
    def _():
        halo_ref[...] = jnp.zeros_like(halo_ref)

    top = lax.broadcasted_iota(jnp.int32, (SUBLANES, FF_TILE), 0)
    hs = []
    for j in range(nsub):
        rows = slice(j * tm, (j + 1) * tm)
        mix = _dot(osb_ref[rows, :], wo_ref[:n_sb, :]) + _dot(ogla_ref[rows, :], wo_ref[n_sb:, :])
        x1_ref[rows, :] = _layer_norm(alpha * x_ref[rows, :] + (1.0 + mod_ref[2:3, :]) * mix,
                                      g1_ref[...], b1_ref[...])
        hs.append((x1_ref[rows, :] * (1.0 + mod_ref[4:5, :]) + mod_ref[3:4, :]).astype(BF16))
    for j in range(nsub):
        rows = slice(j * tm, (j + 1) * tm)
        h = hs[j]
        for f in range(d_ff // FF_TILE):
            cols = slice(f * FF_TILE, (f + 1) * FF_TILE)
            gt = _dot(h, wg_ref[:, cols])
            up = _dot(h, wu_ref[:, cols])
            prev = halo_ref[:, cols]
            halo_ref[:, cols] = gt[tm - SUBLANES:, :]
            r1 = pltpu.roll(gt, 1, 0)
            r2 = pltpu.roll(gt, 2, 0)
            t1 = jnp.where(top < 1, pltpu.roll(prev, 1, 0), r1[:SUBLANES])
            t2 = jnp.where(top < 2, pltpu.roll(prev, 2, 0), r2[:SUBLANES])
            g1 = jnp.concatenate([t1, r1[SUBLANES:]], axis=0)
            g2 = jnp.concatenate([t2, r2[SUBLANES:]], axis=0)
            conv = (g2 * cw_ref[0:1, cols] + g1 * cw_ref[1:2, cols] + gt * cw_ref[2:3, cols]
                    + cb_ref[:, cols])
            act_ref[rows, cols] = (conv / (1.0 + jnp.exp(-conv)) * up).astype(BF16)
        y = _dot(act_ref[rows, :], wd_ref[...])
        o_ref[rows, :] = _layer_norm(alpha * x1_ref[rows, :] + (1.0 + mod_ref[5:6, :]) * y,
                                     g_ref[...], b_ref[...])


def _post(o_sb, o_gla, x2d, mod, w_o, ln1_g, ln1_b, wg, wu, cw, cb, wd, ln_g, ln_b, seq, alpha, layer):
    t, d = x2d.shape
    d_ff = wg.shape[2]
    assert cw.shape[0] == CONV_WIDTH
    tm = ROW_TILE
    tiles_per_seq = seq // tm
    row_map = lambda i: (i, 0)
    fixed = lambda i: (0, 0)
    hbm = pl.BlockSpec(memory_space=pl.ANY)
    stage_width = max(w.shape[2] for w in (w_o, wg, wu, wd))
    return pl.pallas_call(
        functools.partial(_post_kernel, alpha, tiles_per_seq, layer),
        out_shape=jax.ShapeDtypeStruct((t, d), F32),
        grid=(t // tm,),
        in_specs=[
            pl.BlockSpec((tm, o_sb.shape[1]), row_map),
            pl.BlockSpec((tm, o_gla.shape[1]), row_map),
            pl.BlockSpec((tm, d), row_map),
            pl.BlockSpec((None, SUBLANES, d), lambda i: (i // tiles_per_seq, 0, 0)),
            hbm,
            pl.BlockSpec((1, d), fixed),
            pl.BlockSpec((1, d), fixed),
            hbm,
            hbm,
            pl.BlockSpec(cw.shape, fixed),
            pl.BlockSpec(cb.shape, fixed),
            hbm,
            pl.BlockSpec((1, d), fixed),
            pl.BlockSpec((1, d), fixed),
        ],
        out_specs=pl.BlockSpec((tm, d), row_map),
        scratch_shapes=[pltpu.VMEM((SUBLANES, d_ff), F32), pltpu.VMEM((tm, d_ff), BF16),
                        pltpu.VMEM((tm, d), F32),
                        pltpu.VMEM(w_o.shape[1:], BF16), pltpu.VMEM(wg.shape[1:], BF16),
                        pltpu.VMEM(wu.shape[1:], BF16), pltpu.VMEM(wd.shape[1:], BF16),
                        pltpu.VMEM((STAGE_SLOTS, STAGE_ROWS, stage_width), F32),
                        pltpu.SemaphoreType.DMA((STAGE_SLOTS,))],
        compiler_params=pltpu.CompilerParams(
            dimension_semantics=("arbitrary",), vmem_limit_bytes=VMEM_LIMIT_BYTES),
        name="post",
    )(o_sb, o_gla, x2d, mod, w_o, ln1_g, ln1_b, wg, wu, cw, cb, wd, ln_g, ln_b)


def kernel(x, c, w_ada, b_ada, w_in, gla_w_gate, gla_b_gate, sb_norm_g, gla_norm_g, w_out,
           ln1_g, ln1_b, w_ff_gate, w_ff_up, conv_w, conv_b, w_down, ln2_g, ln2_b):
    bsz, seq, d = x.shape
    depth = w_ada.shape[0]
    alpha = float((2 * depth) ** 0.25)
    t = bsz * seq

    w_in_t = jnp.swapaxes(w_in, 1, 2)
    xcur = x.reshape(t, d)
    for l in range(depth):
        sb_q, sb_k, sb_v, g_q, g_k, g_v, g_r, g_lr, mod = _inproj(
            xcur, c, w_ada, b_ada[l][None, :], w_in_t, seq, l)
        as_seq = lambda a: a.reshape(bsz, seq, a.shape[-1])

        o_sb = _sb_attention(as_seq(sb_q), as_seq(sb_k), as_seq(sb_v), sb_norm_g[l][None, :])
        wg_pad = jnp.concatenate(
            [gla_w_gate[l], jnp.zeros((LANES - GLA_GATE_RANK, GLA_QK_WIDTH), F32)], axis=0).astype(BF16)
        o_gla = _gla(as_seq(g_q), as_seq(g_k), as_seq(g_v), as_seq(g_r), as_seq(g_lr),
                     wg_pad, gla_b_gate[l][None, :], gla_norm_g[l][None, :])

        xcur = _post(o_sb.reshape(t, SB_WIDTH), o_gla.reshape(t, GLA_V_WIDTH), xcur, mod,
                     w_out, ln1_g[l][None, :], ln1_b[l][None, :],
                     w_ff_gate, w_ff_up, conv_w[l], conv_b[l][None, :], w_down,
                     ln2_g[l][None, :], ln2_b[l][None, :], seq, alpha, l)
    return xcur.reshape(bsz, seq, d)
```
